```python
import jax, jax.numpy as jnp
from jax import lax
import numpy as np

D_MODEL = 2048
BATCH = 4
SEQ = 2048
DEPTH = 2

A_WIDTH = D_MODEL // 2
A_GROUPS = 8
A_GROUP_DIM = A_WIDTH // A_GROUPS
A_CHUNK = 128
B_WIDTH = D_MODEL // 2
B_HEAD_DIM = 64
B_HEADS = B_WIDTH // B_HEAD_DIM
B_DECAY_LORA = 64
B_AAA_LORA = 64
B_GATE_LORA = 160
B_GN_EPS = 64e-5
B_PROJ = 3 * B_WIDTH + B_DECAY_LORA + B_AAA_LORA + B_GATE_LORA
AB_PROJ = 2 * A_WIDTH + B_PROJ
C_HEADS = 16
C_HEAD_DIM = D_MODEL // C_HEADS
MOBA_BLOCK = 256
MOBA_TOPK = 3
MOBA_QCHUNK = 16
FFN_HIDDEN = -(-8 * D_MODEL // (3 * 256)) * 256
N_EVEN = (DEPTH + 1) // 2
N_ODD = DEPTH // 2
NORM_EPS = 1e-6

kernel_name = "hybrid_gmlp_rwkv7_moba_trunk"


def rmsnorm(x, gain):
    xf = x.astype(jnp.float32)
    xf = xf * lax.rsqrt(jnp.mean(xf * xf, axis=-1, keepdims=True) + NORM_EPS)
    return (xf * gain.astype(jnp.float32)).astype(x.dtype)


def layernorm(x, gain, bias, eps=1e-5):
    xf = x.astype(jnp.float32)
    mu = jnp.mean(xf, axis=-1, keepdims=True)
    var = jnp.mean(jnp.square(xf - mu), axis=-1, keepdims=True)
    y = (xf - mu) * lax.rsqrt(var + eps)
    return (y * gain.astype(jnp.float32) + bias.astype(jnp.float32)).astype(x.dtype)


def token_shift(z, mu):
    z_prev = jnp.pad(z, ((0, 0), (1, 0), (0, 0)))[:, :-1]
    return z + mu * (z_prev - z)


def chunked_spatial_gating(z, v_gain, v_bias, w_s, b_s):
    bsz, s, _ = z.shape
    z = jax.nn.gelu(z)
    u, v = jnp.split(z, 2, axis=-1)
    v = layernorm(v, v_gain, v_bias)
    v = v.reshape(bsz, s // A_CHUNK, A_CHUNK, A_GROUPS, A_GROUP_DIM)
    causal = jnp.tril(jnp.ones((A_CHUNK, A_CHUNK), dtype=bool))
    w = jnp.where(causal[None], w_s, 0.0)
    sv = jnp.einsum('gij,bcjgd->bcigd', w, v) + b_s.T[None, None, :, :, None]
    return u * sv.reshape(bsz, s, A_WIDTH)


def rwkv7_time_mix(z, mu, w0, w2, a0, a2, g2, k_k, k_a, r_k, lnx_gain, lnx_bias):
    bsz, s, _ = z.shape
    f32 = jnp.float32
    z = token_shift(z, mu)
    cuts = [B_WIDTH, 2 * B_WIDTH, 3 * B_WIDTH, 3 * B_WIDTH + B_DECAY_LORA,
            3 * B_WIDTH + B_DECAY_LORA + B_AAA_LORA]
    r, k, v, zw, za, zg = jnp.split(z, cuts, axis=-1)
    w_log = -jax.nn.softplus(-(w0 + jnp.tanh(zw) @ w2)) - 0.5
    decay = jnp.exp(-jnp.exp(w_log.astype(f32)))
    a = jax.nn.sigmoid(a0 + za @ a2)
    g = jax.nn.sigmoid(zg) @ g2

    def heads(t):
        return t.reshape(bsz, s, B_HEADS, B_HEAD_DIM)

    kk = heads(k * k_k).astype(f32)
    kk = kk / jnp.maximum(jnp.sqrt(jnp.sum(kk * kk, axis=-1, keepdims=True)), 1e-12)
    k = k * (1 + (a - 1) * k_a)
    r_h, k_h, v_h, a_h = heads(r), heads(k), heads(v), heads(a)
    w_h = heads(decay)
    vec_a = -kk
    vec_b = kk * a_h.astype(f32)

    def step(state, inp):
        r_t, w_t, k_t, v_t, a_t, b_t = inp
        sa = jnp.einsum('bhij,bhj->bhi', state, a_t)
        state = (state * w_t[:, :, None, :] + sa[..., None] * b_t[:, :, None, :]
                 + v_t[..., None] * k_t[:, :, None, :])
        y_t = jnp.einsum('bhij,bhj->bhi', state, r_t)
        return state, y_t

    xs = tuple(jnp.moveaxis(t.astype(f32), 1, 0) for t in (r_h, w_h, k_h, v_h, vec_a, vec_b))
    state0 = jnp.zeros((bsz, B_HEADS, B_HEAD_DIM, B_HEAD_DIM), f32)
    _, y = lax.scan(step, state0, xs)
    y = jnp.moveaxis(y, 0, 1)
    mean = jnp.mean(y, axis=-1, keepdims=True)
    var = jnp.mean(jnp.square(y - mean), axis=-1, keepdims=True)
    y = ((y - mean) * lax.rsqrt(var + B_GN_EPS)).reshape(bsz, s, B_WIDTH)
    y = y * lnx_gain.astype(f32) + lnx_bias.astype(f32)
    bonus = jnp.sum((r_h * k_h * r_k).astype(f32), axis=-1, keepdims=True) * v_h.astype(f32)
    y = (y + bonus.reshape(bsz, s, B_WIDTH)) * g.astype(f32)
    return y.astype(z.dtype)


def alibi_slopes(n_heads):
    return 2.0 ** (-8.0 * jnp.arange(1, n_heads + 1, dtype=jnp.float32) / n_heads)


def moba_attention(q, k, v):
    bsz, nh, s, dh = q.shape
    f32 = jnp.float32
    s_pad = -(-s // MOBA_BLOCK) * MOBA_BLOCK
    pad = ((0, 0), (0, 0), (0, s_pad - s), (0, 0))
    qf, kf, vf = (jnp.pad(t.astype(f32), pad) for t in (q, k, v))
    nb = s_pad // MOBA_BLOCK
    topk = min(MOBA_TOPK, nb)
    kb = kf.reshape(bsz, nh, nb, MOBA_BLOCK, dh)
    vb = vf.reshape(bsz, nh, nb, MOBA_BLOCK, dh)
    k_mean = jnp.mean(kb, axis=3)
    q_block = jnp.arange(s_pad) // MOBA_BLOCK
    gate = jnp.einsum('bhtd,bhnd->bhtn', qf, k_mean)
    fully_past = jnp.arange(nb)[None, :] < q_block[:, None]
    gate = jnp.where(fully_past, gate, -jnp.inf)
    _, sel = lax.top_k(gate, topk)
    sel_valid = jnp.arange(topk)[None, :] < q_block[:, None]

    slopes = alibi_slopes(nh)[None, :, None, None]
    scale = dh ** -0.5
    b_ix = jnp.arange(bsz)[:, None, None, None]
    h_ix = jnp.arange(nh)[None, :, None, None]
    blk_off = jnp.arange(MOBA_BLOCK)

    def chunk(ci):
        t0 = ci * MOBA_QCHUNK
        q_c = lax.dynamic_slice_in_dim(qf, t0, MOBA_QCHUNK, axis=2)
        t_c = t0 + jnp.arange(MOBA_QCHUNK)
        own0 = (t0 // MOBA_BLOCK) * MOBA_BLOCK
        k_own = lax.dynamic_slice_in_dim(kf, own0, MOBA_BLOCK, axis=2)
        v_own = lax.dynamic_slice_in_dim(vf, own0, MOBA_BLOCK, axis=2)
        dist_own = (t_c[:, None] - (own0 + blk_off)[None, :]).astype(f32)
        s_own = jnp.einsum('bhqd,bhsd->bhqs', q_c, k_own) * scale - slopes * jnp.abs(dist_own)
        s_own = jnp.where(dist_own >= 0, s_own, -jnp.inf)
        sel_c = lax.dynamic_slice_in_dim(sel, t0, MOBA_QCHUNK, axis=2)
        val_c = lax.dynamic_slice_in_dim(sel_valid, t0, MOBA_QCHUNK, axis=0)
        k_sel = kb[b_ix, h_ix, sel_c]
        v_sel = vb[b_ix, h_ix, sel_c]
        sel_pos = sel_c[..., None] * MOBA_BLOCK + blk_off
        dist_sel = (t_c[None, None, :, None, None] - sel_pos).astype(f32)
        s_sel = (jnp.einsum('bhqd,bhqksd->bhqks', q_c, k_sel) * scale
                 - slopes[..., None] * jnp.abs(dist_sel))
        s_sel = jnp.where(val_c[None, None, :, :, None], s_sel, -jnp.inf)
        s_all = jnp.concatenate(
            [s_own, s_sel.reshape(bsz, nh, MOBA_QCHUNK, topk * MOBA_BLOCK)], axis=-1)
        p = jax.nn.softmax(s_all, axis=-1)
        p_own = p[..., :MOBA_BLOCK]
        p_sel = p[..., MOBA_BLOCK:].reshape(bsz, nh, MOBA_QCHUNK, topk, MOBA_BLOCK)
        return (jnp.einsum('bhqs,bhsd->bhqd', p_own, v_own)
                + jnp.einsum('bhqks,bhqksd->bhqd', p_sel, v_sel))

    out = lax.map(chunk, jnp.arange(s_pad // MOBA_QCHUNK))
    out = jnp.moveaxis(out, 0, 2).reshape(bsz, nh, s_pad, dh)[:, :, :s]
    return out.astype(q.dtype)


def setup_inputs(seed: int = 0) -> dict:
    key = jax.random.key(seed)
    ks = jax.random.split(key, 32)
    f32 = jnp.float32
    D, F = D_MODEL, FFN_HIDDEN

    def nrm(k, shape, std):
        return jax.random.normal(k, shape, f32) * std

    def gain(k, shape):
        return 1.0 + 0.05 * jax.random.normal(k, shape, f32)

    return {
        "x": nrm(ks[0], (BATCH, SEQ, D), 1.0),
        "c": nrm(ks[1], (BATCH, D), 1.0),
        "w_ada": nrm(ks[2], (DEPTH, D, 6 * D), 0.5 * D ** -0.5),
        "b_ada": nrm(ks[3], (DEPTH, 6 * D), 0.01),
        "g_pre_mix": gain(ks[4], (DEPTH, D)),
        "g_post_mix": gain(ks[5], (DEPTH, D)),
        "g_pre_ffn": gain(ks[6], (DEPTH, D)),
        "g_post_ffn": gain(ks[7], (DEPTH, D)),
        "w_ffn_in": nrm(ks[8], (DEPTH, D, 2 * F), D ** -0.5),
        "w_ffn_out": nrm(ks[9], (DEPTH, F, D), F ** -0.5),
        "w_in_ab": nrm(ks[10], (N_EVEN, D, AB_PROJ), D ** -0.5),
        "w_out_ab": nrm(ks[11], (N_EVEN, A_WIDTH + B_WIDTH, D), (A_WIDTH + B_WIDTH) ** -0.5),
        "a_v_gain": gain(ks[12], (N_EVEN, A_WIDTH)),
        "a_v_bias": nrm(ks[13], (N_EVEN, A_WIDTH), 0.02),
        "a_w_s": nrm(ks[14], (N_EVEN, A_GROUPS, A_CHUNK, A_CHUNK), A_CHUNK ** -0.5),
        "a_b_s": gain(ks[15], (N_EVEN, A_GROUPS, A_CHUNK)),
        "b_mu": jax.random.uniform(ks[16], (N_EVEN, B_PROJ), f32),
        "b_w0": jax.random.uniform(ks[17], (N_EVEN, B_WIDTH), f32, -5.0, 0.5),
        "b_w2": nrm(ks[18], (N_EVEN, B_DECAY_LORA, B_WIDTH), 0.5 * B_DECAY_LORA ** -0.5),
        "b_a0": nrm(ks[19], (N_EVEN, B_WIDTH), 0.1),
        "b_a2": nrm(ks[20], (N_EVEN, B_AAA_LORA, B_WIDTH), 0.5 * B_AAA_LORA ** -0.5),
        "b_g2": nrm(ks[21], (N_EVEN, B_GATE_LORA, B_WIDTH), B_GATE_LORA ** -0.5),
        "b_k_k": 0.85 + 0.05 * jax.random.normal(ks[22], (N_EVEN, B_WIDTH), f32),
        "b_k_a": gain(ks[23], (N_EVEN, B_WIDTH)),
        "b_r_k": nrm(ks[24], (N_EVEN, B_HEADS, B_HEAD_DIM), 0.1),
        "b_lnx_gain": gain(ks[25], (N_EVEN, B_WIDTH)),
        "b_lnx_bias": nrm(ks[26], (N_EVEN, B_WIDTH), 0.02),
        "w_qkv": nrm(ks[27], (N_ODD, D, 3 * D), D ** -0.5),
        "w_o": nrm(ks[28], (N_ODD, D, D), D ** -0.5),
    }


def reference(x, c, w_ada, b_ada, g_pre_mix, g_post_mix, g_pre_ffn, g_post_ffn,
              w_ffn_in, w_ffn_out, w_in_ab, w_out_ab, a_v_gain, a_v_bias, a_w_s, a_b_s,
              b_mu, b_w0, b_w2, b_a0, b_a2, b_g2, b_k_k, b_k_a, b_r_k, b_lnx_gain,
              b_lnx_bias, w_qkv, w_o):
    bsz, s, d = x.shape
    cond = jax.nn.silu(c)
    for layer in range(DEPTH):
        mod = cond @ w_ada[layer] + b_ada[layer]
        sh_m, sc_m, gt_m, sh_f, sc_f, gt_f = [m[:, None, :] for m in jnp.split(mod, 6, axis=-1)]
        i = layer // 2
        h = rmsnorm(x, g_pre_mix[layer]) * (1 + sc_m) + sh_m
        if layer % 2 == 0:
            z = h @ w_in_ab[i]
            y_a = chunked_spatial_gating(z[..., :2 * A_WIDTH], a_v_gain[i], a_v_bias[i],
                                         a_w_s[i], a_b_s[i])
            y_b = rwkv7_time_mix(z[..., 2 * A_WIDTH:], b_mu[i], b_w0[i], b_w2[i], b_a0[i],
                                 b_a2[i], b_g2[i], b_k_k[i], b_k_a[i], b_r_k[i],
                                 b_lnx_gain[i], b_lnx_bias[i])
            y = jnp.concatenate([y_a, y_b], axis=-1) @ w_out_ab[i]
        else:
            q, k, v = jnp.split(h @ w_qkv[i], 3, axis=-1)
            q, k, v = (t.reshape(bsz, s, C_HEADS, C_HEAD_DIM).transpose(0, 2, 1, 3)
                       for t in (q, k, v))
            o = moba_attention(q, k, v)
            y = o.transpose(0, 2, 1, 3).reshape(bsz, s, d) @ w_o[i]
        x = x + gt_m * rmsnorm(y, g_post_mix[layer])
        h = rmsnorm(x, g_pre_ffn[layer]) * (1 + sc_f) + sh_f
        gate, up = jnp.split(h @ w_ffn_in[layer], 2, axis=-1)
        y = (jax.nn.silu(gate) * up) @ w_ffn_out[layer]
        x = x + gt_f * rmsnorm(y, g_post_ffn[layer])
    return x
```

```python
import functools

import jax
import jax.numpy as jnp
from jax import lax
from jax.experimental import pallas as pl
from jax.experimental.pallas import tpu as pltpu

F32 = jnp.float32
BF16 = jnp.bfloat16
HI = lax.Precision.HIGHEST

NORM_EPS = 1e-6
LN_EPS = 1e-5
GN_EPS = 64e-5

LANE = 128
A_GROUPS = 8
A_CHUNK = 128
RW_HEAD = 64
RW_CHUNK = 64
RW_PACK = 2
MOBA_BLOCK = 256
MOBA_TOPK = 3
ATT_HEAD = 128

NT_DIMS = (((1,), (1,)), ((), ()))
TN_DIMS = (((0,), (0,)), ((), ()))

VMEM_LIMIT = 56 * 1024 * 1024


def _params(sem):
    return pltpu.CompilerParams(dimension_semantics=sem, vmem_limit_bytes=VMEM_LIMIT)


def _rms(x, gain):
    ms = jnp.mean(x * x, axis=-1, keepdims=True)
    return x * lax.rsqrt(ms + NORM_EPS) * gain


def _ada_kernel(c_ref, w_ref, b_ref, o_ref):
    c = c_ref[...]
    cond = (c * jax.nn.sigmoid(c)).astype(BF16)
    o_ref[0] = jnp.dot(cond, w_ref[0].astype(BF16), preferred_element_type=F32) + b_ref[0]


def _ada_mod(c, w_ada, b_ada):
    depth, d, n = w_ada.shape
    bsz = c.shape[0]
    bp = 8
    c_p = jnp.pad(c, ((0, bp - bsz), (0, 0)))
    tn = 1024
    out = pl.pallas_call(
        _ada_kernel,
        grid=(depth, n // tn),
        in_specs=[pl.BlockSpec((bp, d), lambda l, j: (0, 0)),
                  pl.BlockSpec((1, d, tn), lambda l, j: (l, 0, j)),
                  pl.BlockSpec((1, 1, tn), lambda l, j: (l, 0, j))],
        out_specs=pl.BlockSpec((1, bp, tn), lambda l, j: (l, 0, j)),
        out_shape=jax.ShapeDtypeStruct((depth, bp, n), F32),
        compiler_params=_params(("parallel", "parallel")),
        name="ada_mod",
    )(c_p, w_ada, b_ada.reshape(depth, 1, n))
    return out[:, :bsz]


def _norm_mm_kernel(x_ref, g_ref, sc_ref, sh_ref, w_ref, o_ref, h_scr):
    @pl.when(pl.program_id(2) == 0)
    def _():
        h = _rms(x_ref[0], g_ref[...])
        h_scr[...] = (h * (1.0 + sc_ref[0]) + sh_ref[0]).astype(BF16)

    o_ref[0] = jnp.dot(h_scr[...], w_ref[...], preferred_element_type=F32).astype(o_ref.dtype)


def _norm_mm(x, gain, mod3, sc_idx, sh_idx, w, out_dtype, tm=512, tn=512, name="norm_mm"):
    bsz, s, d = x.shape
    n = w.shape[1]
    return pl.pallas_call(
        _norm_mm_kernel,
        grid=(bsz, s // tm, n // tn),
        in_specs=[pl.BlockSpec((1, tm, d), lambda b, m, j: (b, m, 0)),
                  pl.BlockSpec((1, d), lambda b, m, j: (0, 0)),
                  pl.BlockSpec((1, 1, d), lambda b, m, j: (b, 0, sc_idx)),
                  pl.BlockSpec((1, 1, d), lambda b, m, j: (b, 0, sh_idx)),
                  pl.BlockSpec((d, tn), lambda b, m, j: (0, j))],
        out_specs=pl.BlockSpec((1, tm, tn), lambda b, m, j: (b, m, j)),
        out_shape=jax.ShapeDtypeStruct((bsz, s, n), out_dtype),
        scratch_shapes=[pltpu.VMEM((tm, d), BF16)],
        compiler_params=_params(("parallel", "parallel", "arbitrary")),
        name=name,
    )(x, gain.reshape(1, d), mod3, mod3, w)


def _ffn_kernel(x_ref, gpre_ref, sc_ref, sh_ref, gt_ref, gpost_ref, wg_ref, wu_ref, wo_ref,
                o_ref, h_scr, acc_scr):
    f = pl.program_id(2)

    @pl.when(f == 0)
    def _():
        h = _rms(x_ref[0], gpre_ref[...])
        h_scr[...] = (h * (1.0 + sc_ref[0]) + sh_ref[0]).astype(BF16)

    h = h_scr[...]
    g = jnp.dot(h, wg_ref[...], preferred_element_type=F32)
    u = jnp.dot(h, wu_ref[...], preferred_element_type=F32)
    a = (g * jax.nn.sigmoid(g) * u).astype(BF16)
    y = jnp.dot(a, wo_ref[...], preferred_element_type=F32)

    @pl.when(f == 0)
    def _():
        acc_scr[...] = y

    @pl.when(f > 0)
    def _():
        acc_scr[...] += y

    @pl.when(f == pl.num_programs(2) - 1)
    def _():
        o_ref[0] = x_ref[0] + gt_ref[0] * _rms(acc_scr[...], gpost_ref[...])


def _ffn(x, gpre, gpost, mod3, w_in, w_out, layer, tm=512, tf=512):
    bsz, s, d = x.shape
    fh = w_out.shape[1]
    nf = fh // tf
    return pl.pallas_call(
        _ffn_kernel,
        grid=(bsz, s // tm, nf),
        in_specs=[pl.BlockSpec((1, tm, d), lambda b, m, f: (b, m, 0)),
                  pl.BlockSpec((1, d), lambda b, m, f: (0, 0)),
                  pl.BlockSpec((1, 1, d), lambda b, m, f: (b, 0, 4)),
                  pl.BlockSpec((1, 1, d), lambda b, m, f: (b, 0, 3)),
                  pl.BlockSpec((1, 1, d), lambda b, m, f: (b, 0, 5)),
                  pl.BlockSpec((1, d), lambda b, m, f: (0, 0)),
                  pl.BlockSpec((None, d, tf), lambda b, m, f: (layer, 0, f)),
                  pl.BlockSpec((None, d, tf), lambda b, m, f: (layer, 0, nf + f)),
                  pl.BlockSpec((None, tf, d), lambda b, m, f: (layer, f, 0))],
        out_specs=pl.BlockSpec((1, tm, d), lambda b, m, f: (b, m, 0)),
        out_shape=jax.ShapeDtypeStruct((bsz, s, d), F32),
        scratch_shapes=[pltpu.VMEM((tm, d), BF16), pltpu.VMEM((tm, d), F32)],
        compiler_params=_params(("parallel", "parallel", "arbitrary")),
        name="ffn",
    )(x, gpre.reshape(1, d), mod3, mod3, mod3, gpost.reshape(1, d), w_in, w_in, w_out)


def _out_proj_kernel(a0_ref, a1_ref, w0_ref, w1_ref, x_ref, gt_ref, gpost_ref, o_ref):
    y = (jnp.dot(a0_ref[0], w0_ref[...], preferred_element_type=F32)
         + jnp.dot(a1_ref[0], w1_ref[...], preferred_element_type=F32))
    o_ref[0] = x_ref[0] + gt_ref[0] * _rms(y, gpost_ref[...])


def _out_proj(a0, a1, col0, col1, w, x, mod3, gpost, tm=256):
    bsz, s, d = x.shape
    kh = w.shape[0] // 2
    return pl.pallas_call(
        _out_proj_kernel,
        grid=(bsz, s // tm),
        in_specs=[pl.BlockSpec((1, tm, kh), lambda b, m: (b, m, col0)),
                  pl.BlockSpec((1, tm, kh), lambda b, m: (b, m, col1)),
                  pl.BlockSpec((kh, d), lambda b, m: (0, 0)),
                  pl.BlockSpec((kh, d), lambda b, m: (1, 0)),
                  pl.BlockSpec((1, tm, d), lambda b, m: (b, m, 0)),
                  pl.BlockSpec((1, 1, d), lambda b, m: (b, 0, 2)),
                  pl.BlockSpec((1, d), lambda b, m: (0, 0))],
        out_specs=pl.BlockSpec((1, tm, d), lambda b, m: (b, m, 0)),
        out_shape=jax.ShapeDtypeStruct((bsz, s, d), F32),
        compiler_params=_params(("parallel", "parallel")),
        name="out_proj",
    )(a0, a1, w, w, x, mod3, gpost.reshape(1, d))


def _mixer_a_kernel(z_ref, vg_ref, vb_ref, ws_ref, bst_ref, o_ref):
    z = jax.nn.gelu(z_ref[0].astype(F32))
    wdt = z.shape[1] // 2
    u = z[:, :wdt]
    v = z[:, wdt:]
    mu = jnp.mean(v, axis=-1, keepdims=True)
    dv = v - mu
    var = jnp.mean(dv * dv, axis=-1, keepdims=True)
    vn = (dv * lax.rsqrt(var + LN_EPS) * vg_ref[...] + vb_ref[...]).astype(BF16)
    ch = z.shape[0]
    causal = (lax.broadcasted_iota(jnp.int32, (ch, ch), 0)
              >= lax.broadcasted_iota(jnp.int32, (ch, ch), 1))
    gd = wdt // A_GROUPS
    for g in range(A_GROUPS):
        w = jnp.where(causal, ws_ref[g], 0.0).astype(BF16)
        sv = jnp.dot(w, vn[:, g * gd:(g + 1) * gd], preferred_element_type=F32)
        sv = sv + bst_ref[:, g:g + 1]
        o_ref[0, :, g * gd:(g + 1) * gd] = (u[:, g * gd:(g + 1) * gd] * sv).astype(o_ref.dtype)


def _mixer_a(z, v_gain, v_bias, w_s, b_s, width):
    bsz, s, _ = z.shape
    ch = A_CHUNK
    return pl.pallas_call(
        _mixer_a_kernel,
        grid=(bsz, s // ch),
        in_specs=[pl.BlockSpec((1, ch, 2 * width), lambda b, c: (b, c, 0)),
                  pl.BlockSpec((1, width), lambda b, c: (0, 0)),
                  pl.BlockSpec((1, width), lambda b, c: (0, 0)),
                  pl.BlockSpec((A_GROUPS, ch, ch), lambda b, c: (0, 0, 0)),
                  pl.BlockSpec((ch, A_GROUPS), lambda b, c: (0, 0))],
        out_specs=pl.BlockSpec((1, ch, width), lambda b, c: (b, c, 0)),
        out_shape=jax.ShapeDtypeStruct((bsz, s, width), BF16),
        compiler_params=_params(("parallel", "parallel")),
        name="mixer_a",
    )(z, v_gain.reshape(1, width), v_bias.reshape(1, width), w_s, b_s.T)


_PV_MU_R, _PV_MU_K, _PV_MU_V, _PV_W0, _PV_A0, _PV_KK, _PV_KA, _PV_RK, _PV_LG, _PV_LB = range(10)
_PV_ROWS = 16


def _shift_lerp(x, prev_row, mu):
    rolled = pltpu.roll(x, 1, axis=0)
    first = lax.broadcasted_iota(jnp.int32, x.shape, 0) == 0
    xp = jnp.where(first, prev_row, rolled)
    return x + mu * (xp - x)


def _rwkv_kernel(zr_ref, zk_ref, zv_ref, zl_ref, pv_ref, mul_ref, w2_ref, a2_ref, g2_ref, o_ref,
                 s_scr, prev_scr, prevl_scr, r_scr, lw_scr, k_scr, v_scr, a_scr, b_scr, g_scr,
                 *, chunk):
    tb = pl.program_id(2)
    t_rows = zr_ref.shape[1]
    lanes = zr_ref.shape[2]
    L = chunk
    SL = RW_PACK * L

    @pl.when(tb == 0)
    def _():
        s_scr[...] = jnp.zeros_like(s_scr)
        prev_scr[...] = jnp.zeros_like(prev_scr)
        prevl_scr[...] = jnp.zeros_like(prevl_scr)

    def pv(i):
        return pv_ref[i:i + 1, :]

    zr = zr_ref[0]
    zk = zk_ref[0]
    zv = zv_ref[0]
    zl = zl_ref[0]
    r = _shift_lerp(zr, prev_scr[0:1, :], pv(_PV_MU_R))
    k = _shift_lerp(zk, prev_scr[1:2, :], pv(_PV_MU_K))
    v = _shift_lerp(zv, prev_scr[2:3, :], pv(_PV_MU_V))
    zls = _shift_lerp(zl, prevl_scr[0:1, :], mul_ref[...])
    prev_scr[0:1, :] = zr[t_rows - 1:t_rows, :]
    prev_scr[1:2, :] = zk[t_rows - 1:t_rows, :]
    prev_scr[2:3, :] = zv[t_rows - 1:t_rows, :]
    prevl_scr[0:1, :] = zl[t_rows - 1:t_rows, :]

    x_wa = zls[:, :LANE]
    x_g = zls[:, LANE:3 * LANE]
    w_pre = pv(_PV_W0) + jnp.dot(jnp.tanh(x_wa), w2_ref[...], precision=HI,
                                 preferred_element_type=F32)
    t = -w_pre
    softplus = jnp.maximum(t, 0.0) + jnp.log1p(jnp.exp(-jnp.abs(t)))
    log_decay = -jnp.exp(-softplus - 0.5)
    a = jax.nn.sigmoid(pv(_PV_A0) + jnp.dot(x_wa, a2_ref[...], precision=HI,
                                            preferred_element_type=F32))
    g = jnp.dot(jax.nn.sigmoid(x_g), g2_ref[...], precision=HI, preferred_element_type=F32)

    li = lax.broadcasted_iota(jnp.int32, (lanes, lanes), 0) // RW_HEAD
    lj = lax.broadcasted_iota(jnp.int32, (lanes, lanes), 1) // RW_HEAD
    same_head = li == lj
    e_head = same_head.astype(F32)

    kk = k * pv(_PV_KK)
    kk_ss = jnp.dot(kk * kk, e_head, precision=HI, preferred_element_type=F32)
    kk = kk / jnp.maximum(jnp.sqrt(kk_ss), 1e-12)
    k = k * (1.0 + (a - 1.0) * pv(_PV_KA))

    r_scr[...] = r
    lw_scr[...] = log_decay
    k_scr[...] = k
    v_scr[...] = v
    a_scr[...] = -kk
    b_scr[...] = kk * a
    g_scr[...] = g

    ti = lax.broadcasted_iota(jnp.int32, (L, L), 0)
    tj = lax.broadcasted_iota(jnp.int32, (L, L), 1)
    tri_incl = (ti >= tj).astype(F32)
    si = lax.broadcasted_iota(jnp.int32, (SL, SL), 0)
    sj = lax.broadcasted_iota(jnp.int32, (SL, SL), 1)
    same_blk = (si // L) == (sj // L)
    m_strict = same_blk & (si > sj)
    m_incl = same_blk & (si >= sj)
    eye = (si == sj).astype(F32)
    lane_head = lax.broadcasted_iota(jnp.int32, (1, lanes), 1) // RW_HEAD
    n_sq = max(L.bit_length() - 2, 0)

    def stack(x):
        return jnp.concatenate([jnp.where(lane_head == h, x, 0.0) for h in range(RW_PACK)], axis=0)

    def unstack(x):
        out = x[0:L]
        for h in range(1, RW_PACK):
            out = out + x[h * L:(h + 1) * L]
        return out

    def mm(x, y):
        return jnp.dot(x, y, precision=HI, preferred_element_type=F32)

    def mm_nt(x, y):
        return lax.dot_general(x, y, NT_DIMS, precision=HI, preferred_element_type=F32)

    def chunk_body(c, carry):
        rows = pl.ds(pl.multiple_of(c * L, L), L)
        rc = r_scr[rows, :]
        lw = lw_scr[rows, :]
        kc = k_scr[rows, :]
        vc = v_scr[rows, :]
        ac = a_scr[rows, :]
        bc = b_scr[rows, :]
        gc = g_scr[rows, :]

        cum = mm(tri_incl, lw)
        cum_last = cum[L - 1:L, :]
        r_t = rc * jnp.exp(cum)
        a_t = ac * jnp.exp(cum - lw)
        e_neg = jnp.exp(-cum)
        k_h = kc * e_neg
        b_h = bc * e_neg
        e_rem = jnp.exp(cum_last - cum)
        k_b = kc * e_rem
        b_b = bc * e_rem

        a_st = stack(a_t)
        r_st = stack(r_t)
        k_rep = jnp.concatenate([k_h] * RW_PACK, axis=0)
        b_rep = jnp.concatenate([b_h] * RW_PACK, axis=0)
        a_ab = jnp.where(m_strict, mm_nt(a_st, b_rep), 0.0)
        a_ak = jnp.where(m_strict, mm_nt(a_st, k_rep), 0.0)
        a_rk = jnp.where(m_incl, mm_nt(r_st, k_rep), 0.0)
        a_rb = jnp.where(m_incl, mm_nt(r_st, b_rep), 0.0)

        xp = a_ab
        tinv = eye + a_ab
        for _ in range(n_sq):
            xp = mm(xp, xp)
            tinv = tinv + mm(tinv, xp)

        s0 = s_scr[...]
        v_st = stack(vc)
        x_st = stack(mm_nt(a_t, s0)) + mm(a_ak, v_st)
        u_st = mm(tinv, x_st)
        y = mm_nt(r_t, s0) + unstack(mm(a_rk, v_st) + mm(a_rb, u_st))
        u = unstack(u_st)

        vu = jnp.concatenate([vc, u], axis=0)
        kb = jnp.concatenate([k_b, b_b], axis=0)
        s_new = s0 * jnp.exp(cum_last) + lax.dot_general(vu, kb, TN_DIMS, precision=HI,
                                                        preferred_element_type=F32)
        s_scr[...] = jnp.where(same_head, s_new, 0.0)

        inv_n = 1.0 / RW_HEAD
        mean = mm(y, e_head) * inv_n
        dy = y - mean
        var = mm(dy * dy, e_head) * inv_n
        yn = dy * lax.rsqrt(var + GN_EPS) * pv(_PV_LG) + pv(_PV_LB)
        bonus = mm(rc * kc * pv(_PV_RK), e_head) * vc
        o_ref[0, rows, :] = ((yn + bonus) * gc).astype(o_ref.dtype)
        return carry

    lax.fori_loop(0, t_rows // L, chunk_body, 0)


def _rwkv(z, col_r, col_k, col_v, col_l, lora_w, pvec, mu_l, w2p, a2p, g2p, width, t_rows=512):
    bsz, s, _ = z.shape
    n_blk = width // LANE
    kern = functools.partial(_rwkv_kernel, chunk=RW_CHUNK)
    tile = pltpu.VMEM((t_rows, LANE), F32)
    return pl.pallas_call(
        kern,
        grid=(bsz, n_blk, s // t_rows),
        in_specs=[pl.BlockSpec((1, t_rows, LANE), lambda b, p, t: (b, t, col_r + p)),
                  pl.BlockSpec((1, t_rows, LANE), lambda b, p, t: (b, t, col_k + p)),
                  pl.BlockSpec((1, t_rows, LANE), lambda b, p, t: (b, t, col_v + p)),
                  pl.BlockSpec((1, t_rows, lora_w), lambda b, p, t: (b, t, col_l)),
                  pl.BlockSpec((_PV_ROWS, LANE), lambda b, p, t: (0, p)),
                  pl.BlockSpec((1, lora_w), lambda b, p, t: (0, 0)),
                  pl.BlockSpec((LANE, LANE), lambda b, p, t: (0, p)),
                  pl.BlockSpec((LANE, LANE), lambda b, p, t: (0, p)),
                  pl.BlockSpec((2 * LANE, LANE), lambda b, p, t: (0, p))],
        out_specs=pl.BlockSpec((1, t_rows, LANE), lambda b, p, t: (b, t, p)),
        out_shape=jax.ShapeDtypeStruct((bsz, s, width), BF16),
        scratch_shapes=[pltpu.VMEM((LANE, LANE), F32),
                        pltpu.VMEM((8, LANE), F32),
                        pltpu.VMEM((8, lora_w), F32),
                        tile, tile, tile, tile, tile, tile, tile],
        compiler_params=_params(("parallel", "parallel", "arbitrary")),
        name="rwkv7",
    )(z, z, z, z, pvec, mu_l, w2p, a2p, g2p)


def _moba_kernel(q_ref, k_ref, v_ref, o_ref, kmean_scr, vt_scr, gate_scr, *, n_heads):
    h = pl.program_id(1)
    qb = pl.program_id(2)
    blk = q_ref.shape[1]
    dh = q_ref.shape[2]
    nb = k_ref.shape[1] // blk
    scale = dh ** -0.5

    @pl.when(qb == 0)
    def _():
        for j in range(nb):
            kj = k_ref[0, j * blk:(j + 1) * blk, :].astype(F32)
            kmean_scr[j:j + 1, :] = jnp.mean(kj, axis=0, keepdims=True)
            vt_scr[j] = v_ref[0, j * blk:(j + 1) * blk, :].astype(F32).T.astype(BF16)

    q = q_ref[0]
    gate_scr[...] = lax.dot_general(kmean_scr[...], q.astype(F32), NT_DIMS, precision=HI,
                                    preferred_element_type=F32)
    gate = gate_scr[...]
    blk_id = lax.broadcasted_iota(jnp.int32, (nb, 1), 0)
    past = blk_id < qb

    slope = jnp.exp(jnp.full((1, blk), -8.0 / n_heads * 0.6931471805599453, F32)
                    * (h + 1).astype(F32))
    ik = lax.broadcasted_iota(jnp.int32, (blk, blk), 0)
    iq = lax.broadcasted_iota(jnp.int32, (blk, blk), 1)
    rel = (iq - ik).astype(F32)

    k_own = k_ref[0, pl.ds(pl.multiple_of(qb * blk, blk), blk), :]
    s_own = lax.dot_general(k_own, q, NT_DIMS, preferred_element_type=F32) * scale - slope * rel
    s_own = jnp.where(rel >= 0, s_own, -jnp.inf)
    m0 = jnp.max(s_own, axis=0, keepdims=True)
    p0 = jnp.exp(s_own - m0)
    l0 = jnp.sum(p0, axis=0, keepdims=True)
    acc0 = jnp.dot(vt_scr[qb], p0.astype(BF16), preferred_element_type=F32)

    def body(n, carry):
        m, l, acc = carry
        g_n = gate_scr[pl.ds(n, 1), :]
        beats = past & ((gate > g_n) | ((gate == g_n) & (blk_id < n)))
        rank = jnp.sum(beats.astype(F32), axis=0, keepdims=True)
        sel = rank < float(MOBA_TOPK)
        k_n = k_ref[0, pl.ds(pl.multiple_of(n * blk, blk), blk), :]
        off = ((qb - n) * blk).astype(F32)
        s = (lax.dot_general(k_n, q, NT_DIMS, preferred_element_type=F32) * scale
             - slope * (rel + off))
        s = jnp.where(sel, s, -jnp.inf)
        m_new = jnp.maximum(m, jnp.max(s, axis=0, keepdims=True))
        alpha = jnp.exp(m - m_new)
        p = jnp.exp(s - m_new)
        l = alpha * l + jnp.sum(p, axis=0, keepdims=True)
        acc = alpha * acc + jnp.dot(vt_scr[n], p.astype(BF16), preferred_element_type=F32)
        return m_new, l, acc

    m, l, acc = lax.fori_loop(0, qb, body, (m0, l0, acc0))
    o_ref[0] = (acc / l).T.astype(o_ref.dtype)


def _moba(qkv, n_heads):
    bsz, s, d3 = qkv.shape
    d = d3 // 3
    dh = d // n_heads
    blk = MOBA_BLOCK
    nb = s // blk
    kern = functools.partial(_moba_kernel, n_heads=n_heads)
    return pl.pallas_call(
        kern,
        grid=(bsz, n_heads, nb),
        in_specs=[pl.BlockSpec((1, blk, dh), lambda b, h, t: (b, t, h)),
                  pl.BlockSpec((1, s, dh), lambda b, h, t: (b, 0, n_heads + h)),
                  pl.BlockSpec((1, s, dh), lambda b, h, t: (b, 0, 2 * n_heads + h))],
        out_specs=pl.BlockSpec((1, blk, dh), lambda b, h, t: (b, t, h)),
        out_shape=jax.ShapeDtypeStruct((bsz, s, d), BF16),
        scratch_shapes=[pltpu.VMEM((nb, dh), F32),
                        pltpu.VMEM((nb, dh, blk), BF16),
                        pltpu.VMEM((nb, blk), F32)],
        compiler_params=_params(("parallel", "parallel", "arbitrary")),
        name="moba",
    )(qkv, qkv, qkv)


def _pad_cols(w, n):
    return jnp.pad(w, ((0, 0), (0, n - w.shape[1])))


def _pad_rows(w, n, before=0):
    return jnp.pad(w, ((before, n - before - w.shape[0]), (0, 0)))


def kernel(x, c, w_ada, b_ada, g_pre_mix, g_post_mix, g_pre_ffn, g_post_ffn, w_ffn_in, w_ffn_out,
           w_in_ab, w_out_ab, a_v_gain, a_v_bias, a_w_s, a_b_s, b_mu, b_w0, b_w2, b_a0, b_a2, b_g2,
           b_k_k, b_k_a, b_r_k, b_lnx_gain, b_lnx_bias, w_qkv, w_o):
    bsz, s, d = x.shape
    depth = w_ada.shape[0]
    a_width = a_v_gain.shape[1]
    b_width = b_w0.shape[1]
    n_lw = b_w2.shape[1]
    n_la = b_a2.shape[1]
    n_lg = b_g2.shape[1]
    n_heads = d // ATT_HEAD
    assert s % MOBA_BLOCK == 0 and s % 512 == 0
    assert n_lw + n_la <= LANE and n_lg <= 2 * LANE

    mod = _ada_mod(c, w_ada, b_ada)
    w_ffn_in_h = w_ffn_in.astype(BF16)
    w_ffn_out_h = w_ffn_out.astype(BF16)

    for layer in range(depth):
        mod3 = mod[layer].reshape(bsz, 1, 6 * d)
        i = layer // 2
        if layer % 2 == 0:
            lora_w = 4 * LANE
            nz = 2 * a_width + 3 * b_width + lora_w
            w_in = _pad_cols(w_in_ab[i], nz).astype(BF16)
            z = _norm_mm(x, g_pre_mix[layer], mod3, 1, 0, w_in, F32, name="in_proj_ab")
            y_a = _mixer_a(z, a_v_gain[i], a_v_bias[i], a_w_s[i], a_b_s[i], a_width)

            mu = b_mu[i]
            pvec = jnp.stack([mu[0:b_width], mu[b_width:2 * b_width], mu[2 * b_width:3 * b_width],
                              b_w0[i], b_a0[i], b_k_k[i], b_k_a[i], b_r_k[i].reshape(-1),
                              b_lnx_gain[i], b_lnx_bias[i]])
            pvec = _pad_rows(pvec, _PV_ROWS)
            mu_l = _pad_cols(mu[3 * b_width:].reshape(1, -1), lora_w)
            w2p = _pad_rows(b_w2[i], LANE)
            a2p = _pad_rows(b_a2[i], LANE, before=n_lw)
            g2p = _pad_rows(b_g2[i], 2 * LANE)
            cb = 2 * a_width // LANE
            nb_w = b_width // LANE
            y_b = _rwkv(z, cb, cb + nb_w, cb + 2 * nb_w, (2 * a_width + 3 * b_width) // lora_w,
                        lora_w, pvec, mu_l, w2p, a2p, g2p, b_width)
            x = _out_proj(y_a, y_b, 0, 0, w_out_ab[i].astype(BF16), x, mod3, g_post_mix[layer])
        else:
            qkv = _norm_mm(x, g_pre_mix[layer], mod3, 1, 0, w_qkv[i].astype(BF16), BF16,
                           name="qkv_proj")
            o = _moba(qkv, n_heads)
            x = _out_proj(o, o, 0, 1, w_o[i].astype(BF16), x, mod3, g_post_mix[layer])
        x = _ffn(x, g_pre_ffn[layer], g_post_ffn[layer], mod3, w_ffn_in_h, w_ffn_out_h, layer)
    return x
```

```python
import functools

import jax
import jax.numpy as jnp
from jax import lax
from jax.experimental import pallas as pl
from jax.experimental.pallas import tpu as pltpu

F32 = jnp.float32
BF16 = jnp.bfloat16
HI = lax.Precision.HIGHEST

NORM_EPS = 1e-6
LN_EPS = 1e-5
GN_EPS = 64e-5

LANE = 128
A_GROUPS = 8
A_CHUNK = 128
RW_HEAD = 64
RW_CHUNK = 64
RW_PACK = 2
MOBA_BLOCK = 256
MOBA_TOPK = 3
ATT_HEAD = 128

NT_DIMS = (((1,), (1,)), ((), ()))
TN_DIMS = (((0,), (0,)), ((), ()))

VMEM_LIMIT = 56 * 1024 * 1024


def _params(sem):
    return pltpu.CompilerParams(dimension_semantics=sem, vmem_limit_bytes=VMEM_LIMIT)


def _rms(x, gain):
    ms = jnp.mean(x * x, axis=-1, keepdims=True)
    return x * lax.rsqrt(ms + NORM_EPS) * gain


def _ada_kernel(c_ref, w_ref, b_ref, o_ref):
    c = c_ref[...]
    cond = (c * jax.nn.sigmoid(c)).astype(BF16)
    o_ref[0] = jnp.dot(cond, w_ref[0].astype(BF16), preferred_element_type=F32) + b_ref[0]


def _ada_mod(c, w_ada, b_ada):
    depth, d, n = w_ada.shape
    bsz = c.shape[0]
    bp = 8
    c_p = jnp.pad(c, ((0, bp - bsz), (0, 0)))
    tn = 1024
    out = pl.pallas_call(
        _ada_kernel,
        grid=(depth, n // tn),
        in_specs=[pl.BlockSpec((bp, d), lambda l, j: (0, 0)),
                  pl.BlockSpec((1, d, tn), lambda l, j: (l, 0, j)),
                  pl.BlockSpec((1, 1, tn), lambda l, j: (l, 0, j))],
        out_specs=pl.BlockSpec((1, bp, tn), lambda l, j: (l, 0, j)),
        out_shape=jax.ShapeDtypeStruct((depth, bp, n), F32),
        compiler_params=_params(("parallel", "parallel")),
        name="ada_mod",
    )(c_p, w_ada, b_ada.reshape(depth, 1, n))
    return out[:, :bsz]


def _norm_mm_kernel(x_ref, g_ref, sc_ref, sh_ref, w_ref, o_ref, h_scr):
    @pl.when(pl.program_id(2) == 0)
    def _():
        h = _rms(x_ref[0], g_ref[...])
        h_scr[...] = (h * (1.0 + sc_ref[0]) + sh_ref[0]).astype(BF16)

    o_ref[0] = jnp.dot(h_scr[...], w_ref[...], preferred_element_type=F32).astype(o_ref.dtype)


def _norm_mm(x, gain, mod3, sc_idx, sh_idx, w, out_dtype, tm=512, tn=512, name="norm_mm"):
    bsz, s, d = x.shape
    n = w.shape[1]
    return pl.pallas_call(
        _norm_mm_kernel,
        grid=(bsz, s // tm, n // tn),
        in_specs=[pl.BlockSpec((1, tm, d), lambda b, m, j: (b, m, 0)),
                  pl.BlockSpec((1, d), lambda b, m, j: (0, 0)),
                  pl.BlockSpec((1, 1, d), lambda b, m, j: (b, 0, sc_idx)),
                  pl.BlockSpec((1, 1, d), lambda b, m, j: (b, 0, sh_idx)),
                  pl.BlockSpec((d, tn), lambda b, m, j: (0, j))],
        out_specs=pl.BlockSpec((1, tm, tn), lambda b, m, j: (b, m, j)),
        out_shape=jax.ShapeDtypeStruct((bsz, s, n), out_dtype),
        scratch_shapes=[pltpu.VMEM((tm, d), BF16)],
        compiler_params=_params(("parallel", "parallel", "arbitrary")),
        name=name,
    )(x, gain.reshape(1, d), mod3, mod3, w)


def _ffn_kernel(x_ref, gpre_ref, sc_ref, sh_ref, gt_ref, gpost_ref, wg_ref, wu_ref, wo_ref,
                o_ref, h_scr, acc_scr):
    f = pl.program_id(2)

    @pl.when(f == 0)
    def _():
        h = _rms(x_ref[0], gpre_ref[...])
        h_scr[...] = (h * (1.0 + sc_ref[0]) + sh_ref[0]).astype(BF16)

    h = h_scr[...]
    g = jnp.dot(h, wg_ref[...], preferred_element_type=F32)
    u = jnp.dot(h, wu_ref[...], preferred_element_type=F32)
    a = (g * jax.nn.sigmoid(g) * u).astype(BF16)
    y = jnp.dot(a, wo_ref[...], preferred_element_type=F32)

    @pl.when(f == 0)
    def _():
        acc_scr[...] = y

    @pl.when(f > 0)
    def _():
        acc_scr[...] += y

    @pl.when(f == pl.num_programs(2) - 1)
    def _():
        o_ref[0] = x_ref[0] + gt_ref[0] * _rms(acc_scr[...], gpost_ref[...])


def _ffn(x, gpre, gpost, mod3, w_in, w_out, layer, tm=512, tf=512):
    bsz, s, d = x.shape
    fh = w_out.shape[1]
    nf = fh // tf
    return pl.pallas_call(
        _ffn_kernel,
        grid=(bsz, s // tm, nf),
        in_specs=[pl.BlockSpec((1, tm, d), lambda b, m, f: (b, m, 0)),
                  pl.BlockSpec((1, d), lambda b, m, f: (0, 0)),
                  pl.BlockSpec((1, 1, d), lambda b, m, f: (b, 0, 4)),
                  pl.BlockSpec((1, 1, d), lambda b, m, f: (b, 0, 3)),
                  pl.BlockSpec((1, 1, d), lambda b, m, f: (b, 0, 5)),
                  pl.BlockSpec((1, d), lambda b, m, f: (0, 0)),
                  pl.BlockSpec((None, d, tf), lambda b, m, f: (layer, 0, f)),
                  pl.BlockSpec((None, d, tf), lambda b, m, f: (layer, 0, nf + f)),
                  pl.BlockSpec((None, tf, d), lambda b, m, f: (layer, f, 0))],
        out_specs=pl.BlockSpec((1, tm, d), lambda b, m, f: (b, m, 0)),
        out_shape=jax.ShapeDtypeStruct((bsz, s, d), F32),
        scratch_shapes=[pltpu.VMEM((tm, d), BF16), pltpu.VMEM((tm, d), F32)],
        compiler_params=_params(("parallel", "parallel", "arbitrary")),
        name="ffn",
    )(x, gpre.reshape(1, d), mod3, mod3, mod3, gpost.reshape(1, d), w_in, w_in, w_out)


def _out_proj_kernel(a0_ref, a1_ref, w0_ref, w1_ref, x_ref, gt_ref, gpost_ref, o_ref):
    y = (jnp.dot(a0_ref[0], w0_ref[...], preferred_element_type=F32)
         + jnp.dot(a1_ref[0], w1_ref[...], preferred_element_type=F32))
    o_ref[0] = x_ref[0] + gt_ref[0] * _rms(y, gpost_ref[...])


def _out_proj(a0, a1, col0, col1, w, x, mod3, gpost, tm=256):
    bsz, s, d = x.shape
    kh = w.shape[0] // 2
    return pl.pallas_call(
        _out_proj_kernel,
        grid=(bsz, s // tm),
        in_specs=[pl.BlockSpec((1, tm, kh), lambda b, m: (b, m, col0)),
                  pl.BlockSpec((1, tm, kh), lambda b, m: (b, m, col1)),
                  pl.BlockSpec((kh, d), lambda b, m: (0, 0)),
                  pl.BlockSpec((kh, d), lambda b, m: (1, 0)),
                  pl.BlockSpec((1, tm, d), lambda b, m: (b, m, 0)),
                  pl.BlockSpec((1, 1, d), lambda b, m: (b, 0, 2)),
                  pl.BlockSpec((1, d), lambda b, m: (0, 0))],
        out_specs=pl.BlockSpec((1, tm, d), lambda b, m: (b, m, 0)),
        out_shape=jax.ShapeDtypeStruct((bsz, s, d), F32),
        compiler_params=_params(("parallel", "parallel")),
        name="out_proj",
    )(a0, a1, w, w, x, mod3, gpost.reshape(1, d))


def _mixer_a_kernel(z_ref, vg_ref, vb_ref, ws_ref, bst_ref, o_ref):
    z = jax.nn.gelu(z_ref[0].astype(F32))
    wdt = z.shape[1] // 2
    u = z[:, :wdt]
    v = z[:, wdt:]
    mu = jnp.mean(v, axis=-1, keepdims=True)
    dv = v - mu
    var = jnp.mean(dv * dv, axis=-1, keepdims=True)
    vn = (dv * lax.rsqrt(var + LN_EPS) * vg_ref[...] + vb_ref[...]).astype(BF16)
    ch = z.shape[0]
    causal = (lax.broadcasted_iota(jnp.int32, (ch, ch), 0)
              >= lax.broadcasted_iota(jnp.int32, (ch, ch), 1))
    gd = wdt // A_GROUPS
    for g in range(A_GROUPS):
        w = jnp.where(causal, ws_ref[g], 0.0).astype(BF16)
        sv = jnp.dot(w, vn[:, g * gd:(g + 1) * gd], preferred_element_type=F32)
        sv = sv + bst_ref[:, g:g + 1]
        o_ref[0, :, g * gd:(g + 1) * gd] = (u[:, g * gd:(g + 1) * gd] * sv).astype(o_ref.dtype)


def _mixer_a(z, v_gain, v_bias, w_s, b_s, width):
    bsz, s, _ = z.shape
    ch = A_CHUNK
    return pl.pallas_call(
        _mixer_a_kernel,
        grid=(bsz, s // ch),
        in_specs=[pl.BlockSpec((1, ch, 2 * width), lambda b, c: (b, c, 0)),
                  pl.BlockSpec((1, width), lambda b, c: (0, 0)),
                  pl.BlockSpec((1, width), lambda b, c: (0, 0)),
                  pl.BlockSpec((A_GROUPS, ch, ch), lambda b, c: (0, 0, 0)),
                  pl.BlockSpec((ch, A_GROUPS), lambda b, c: (0, 0))],
        out_specs=pl.BlockSpec((1, ch, width), lambda b, c: (b, c, 0)),
        out_shape=jax.ShapeDtypeStruct((bsz, s, width), BF16),
        compiler_params=_params(("parallel", "parallel")),
        name="mixer_a",
    )(z, v_gain.reshape(1, width), v_bias.reshape(1, width), w_s, b_s.T)


_PV_MU_R, _PV_MU_K, _PV_MU_V, _PV_W0, _PV_A0, _PV_KK, _PV_KA, _PV_RK, _PV_LG, _PV_LB = range(10)
_PV_ROWS = 16


def _shift_lerp(x, prev_row, mu):
    rolled = pltpu.roll(x, 1, axis=0)
    first = lax.broadcasted_iota(jnp.int32, x.shape, 0) == 0
    xp = jnp.where(first, prev_row, rolled)
    return x + mu * (xp - x)


def _split_bf16(x):
    hi = x.astype(BF16)
    lo = (x - hi.astype(F32)).astype(BF16)
    return hi, lo


def _mm(x, y):
    return jnp.dot(x.astype(BF16), y.astype(BF16), preferred_element_type=F32)


def _mm_nt(x, y):
    return lax.dot_general(x.astype(BF16), y.astype(BF16), NT_DIMS, preferred_element_type=F32)


def _mm_tn(x, y):
    return lax.dot_general(x.astype(BF16), y.astype(BF16), TN_DIMS, preferred_element_type=F32)


def _mm_x3(x, y):
    xh, xl = _split_bf16(x)
    yh, yl = _split_bf16(y)
    return (jnp.dot(xh, yh, preferred_element_type=F32) + jnp.dot(xl, yh, preferred_element_type=F32)
            + jnp.dot(xh, yl, preferred_element_type=F32))


def _mm_exact_rhs(x, e_bf16):
    xh, xl = _split_bf16(x)
    return (jnp.dot(xh, e_bf16, preferred_element_type=F32)
            + jnp.dot(xl, e_bf16, preferred_element_type=F32))


def _rwkv_kernel(zr_ref, zk_ref, zv_ref, zl_ref, pv_ref, mul_ref, w2_ref, a2_ref, g2_ref, o_ref,
                 s_scr, prev_scr, prevl_scr, q_scr, y_scr, gm_scr, cm_scr, pl_scr, *, chunk):
    tb = pl.program_id(2)
    t_rows = zr_ref.shape[1]
    lanes = zr_ref.shape[2]
    L = chunk
    SL = RW_PACK * L
    n_chunks = t_rows // L

    @pl.when(tb == 0)
    def _():
        s_scr[...] = jnp.zeros_like(s_scr)
        prev_scr[...] = jnp.zeros_like(prev_scr)
        prevl_scr[...] = jnp.zeros_like(prevl_scr)

    def pv(i):
        return pv_ref[i:i + 1, :]

    zr = zr_ref[0]
    zk = zk_ref[0]
    zv = zv_ref[0]
    zl = zl_ref[0]
    r = _shift_lerp(zr, prev_scr[0:1, :], pv(_PV_MU_R))
    k = _shift_lerp(zk, prev_scr[1:2, :], pv(_PV_MU_K))
    v = _shift_lerp(zv, prev_scr[2:3, :], pv(_PV_MU_V))
    zls = _shift_lerp(zl, prevl_scr[0:1, :], mul_ref[...])
    prev_scr[0:1, :] = zr[t_rows - 1:t_rows, :]
    prev_scr[1:2, :] = zk[t_rows - 1:t_rows, :]
    prev_scr[2:3, :] = zv[t_rows - 1:t_rows, :]
    prevl_scr[0:1, :] = zl[t_rows - 1:t_rows, :]

    x_wa = zls[:, :LANE]
    x_g = zls[:, LANE:3 * LANE]
    w_pre = pv(_PV_W0) + _mm_x3(jnp.tanh(x_wa), w2_ref[...])
    t = -w_pre
    softplus = jnp.maximum(t, 0.0) + jnp.log1p(jnp.exp(-jnp.abs(t)))
    log_decay = -jnp.exp(-softplus - 0.5)
    a = jax.nn.sigmoid(pv(_PV_A0) + _mm_x3(x_wa, a2_ref[...]))
    g = _mm(jax.nn.sigmoid(x_g), g2_ref[...])

    li = lax.broadcasted_iota(jnp.int32, (lanes, lanes), 0) // RW_HEAD
    lj = lax.broadcasted_iota(jnp.int32, (lanes, lanes), 1) // RW_HEAD
    same_head = li == lj
    e_head = jnp.where(same_head, 1.0, 0.0).astype(BF16)

    kk = k * pv(_PV_KK)
    kk = kk / jnp.maximum(jnp.sqrt(_mm_exact_rhs(kk * kk, e_head)), 1e-12)
    k = k * (1.0 + (a - 1.0) * pv(_PV_KA))
    av = -kk
    bv = kk * a

    ri = lax.broadcasted_iota(jnp.int32, (t_rows, t_rows), 0)
    ci = lax.broadcasted_iota(jnp.int32, (t_rows, t_rows), 1)
    tri_bd = jnp.where((ri >= ci) & ((ri // L) == (ci // L)), 1.0, 0.0).astype(BF16)
    lw_hi, lw_lo = _split_bf16(log_decay)
    cum = (jnp.dot(tri_bd, lw_hi, preferred_element_type=F32)
           + jnp.dot(tri_bd, lw_lo, preferred_element_type=F32))

    si = lax.broadcasted_iota(jnp.int32, (SL, SL), 0)
    sj = lax.broadcasted_iota(jnp.int32, (SL, SL), 1)
    same_blk = (si // L) == (sj // L)
    m_strict = same_blk & (si > sj)
    m_incl = same_blk & (si >= sj)
    eye = jnp.where(si == sj, 1.0, 0.0)
    lane_head = lax.broadcasted_iota(jnp.int32, (1, lanes), 1) // RW_HEAD
    n_sq = max(L.bit_length() - 2, 0)

    def stack(x):
        return jnp.concatenate([jnp.where(lane_head == h, x, 0.0) for h in range(RW_PACK)], axis=0)

    def unstack(x):
        out = x[0:L]
        for h in range(1, RW_PACK):
            out = out + x[h * L:(h + 1) * L]
        return out

    for c in range(n_chunks):
        lo, hi = c * L, (c + 1) * L
        rc, kc, vc, ac, bc = r[lo:hi], k[lo:hi], v[lo:hi], av[lo:hi], bv[lo:hi]
        lw = log_decay[lo:hi]
        cm = cum[lo:hi]
        cm_last = cm[L - 1:L, :]
        r_t = rc * jnp.exp(cm)
        a_t = ac * jnp.exp(cm - lw)
        e_neg = jnp.exp(-cm)
        k_h = (kc * e_neg).astype(BF16)
        b_h = (bc * e_neg).astype(BF16)
        e_rem = jnp.exp(cm_last - cm)
        k_b = kc * e_rem
        b_b = bc * e_rem

        a_st = stack(a_t).astype(BF16)
        ar_st = jnp.concatenate([a_st, stack(r_t).astype(BF16)], axis=0)
        k_rep = jnp.concatenate([k_h] * RW_PACK, axis=0)
        b_rep = jnp.concatenate([b_h] * RW_PACK, axis=0)
        p_k = _mm_nt(ar_st, k_rep)
        p_b = _mm_nt(ar_st, b_rep)
        a_ak = jnp.where(m_strict, p_k[:SL], 0.0)
        a_rk = jnp.where(m_incl, p_k[SL:], 0.0)
        a_ab = jnp.where(m_strict, p_b[:SL], 0.0)
        a_rb = jnp.where(m_incl, p_b[SL:], 0.0)

        xp = a_ab
        tinv = eye + a_ab
        for _ in range(n_sq):
            xp = _mm(xp, xp)
            tinv = tinv + _mm(tinv, xp)

        v_st = stack(vc).astype(BF16)
        x0 = _mm(a_ak, v_st)
        wu = _mm(tinv, jnp.concatenate([a_st, x0.astype(BF16)], axis=1))
        yq = _mm(a_rb, wu)
        w1 = unstack(wu[:, :lanes])
        u0 = unstack(wu[:, lanes:])
        q_scr[c] = (r_t + unstack(yq[:, :lanes])).astype(BF16)
        y_scr[lo:hi, :] = unstack(yq[:, lanes:] + _mm(a_rk, v_st))

        kb = jnp.concatenate([k_b, b_b], axis=0)
        cmat = _mm_tn(jnp.concatenate([vc, u0], axis=0), kb)
        gmat = _mm_tn(w1, b_b)
        cm_scr[c] = jnp.where(same_head, cmat, 0.0)
        gm_scr[c] = jnp.where(same_head, gmat, 0.0).astype(BF16)
        pl_scr[c] = jnp.broadcast_to(jnp.exp(cm_last), (8, lanes))

    s0 = s_scr[...]
    for c in range(n_chunks):
        lo, hi = c * L, (c + 1) * L
        s0b = s0.astype(BF16)
        y_scr[lo:hi, :] = y_scr[lo:hi, :] + _mm_nt(q_scr[c], s0b)
        s0 = s0 * pl_scr[c][0:1, :] + _mm(s0b, gm_scr[c]) + cm_scr[c]
    s_scr[...] = s0

    y = y_scr[...]
    inv_n = 1.0 / RW_HEAD
    mean = _mm_exact_rhs(y, e_head) * inv_n
    dy = y - mean
    var = _mm_exact_rhs(dy * dy, e_head) * inv_n
    yn = dy * lax.rsqrt(var + GN_EPS) * pv(_PV_LG) + pv(_PV_LB)
    bonus = _mm_exact_rhs(r * k * pv(_PV_RK), e_head) * v
    o_ref[0] = ((yn + bonus) * g).astype(o_ref.dtype)


def _rwkv(z, col_r, col_k, col_v, col_l, lora_w, pvec, mu_l, w2p, a2p, g2p, width, t_rows=512):
    bsz, s, _ = z.shape
    n_blk = width // LANE
    n_chunks = t_rows // RW_CHUNK
    kern = functools.partial(_rwkv_kernel, chunk=RW_CHUNK)
    return pl.pallas_call(
        kern,
        grid=(bsz, n_blk, s // t_rows),
        in_specs=[pl.BlockSpec((1, t_rows, LANE), lambda b, p, t: (b, t, col_r + p)),
                  pl.BlockSpec((1, t_rows, LANE), lambda b, p, t: (b, t, col_k + p)),
                  pl.BlockSpec((1, t_rows, LANE), lambda b, p, t: (b, t, col_v + p)),
                  pl.BlockSpec((1, t_rows, lora_w), lambda b, p, t: (b, t, col_l)),
                  pl.BlockSpec((_PV_ROWS, LANE), lambda b, p, t: (0, p)),
                  pl.BlockSpec((1, lora_w), lambda b, p, t: (0, 0)),
                  pl.BlockSpec((LANE, LANE), lambda b, p, t: (0, p)),
                  pl.BlockSpec((LANE, LANE), lambda b, p, t: (0, p)),
                  pl.BlockSpec((2 * LANE, LANE), lambda b, p, t: (0, p))],
        out_specs=pl.BlockSpec((1, t_rows, LANE), lambda b, p, t: (b, t, p)),
        out_shape=jax.ShapeDtypeStruct((bsz, s, width), BF16),
        scratch_shapes=[pltpu.VMEM((LANE, LANE), F32),
                        pltpu.VMEM((8, LANE), F32),
                        pltpu.VMEM((8, lora_w), F32),
                        pltpu.VMEM((n_chunks, RW_CHUNK, LANE), BF16),
                        pltpu.VMEM((t_rows, LANE), F32),
                        pltpu.VMEM((n_chunks, LANE, LANE), BF16),
                        pltpu.VMEM((n_chunks, LANE, LANE), F32),
                        pltpu.VMEM((n_chunks, 8, LANE), F32)],
        compiler_params=_params(("parallel", "parallel", "arbitrary")),
        name="rwkv7",
    )(z, z, z, z, pvec, mu_l, w2p, a2p, g2p)


def _moba_kernel(q_ref, k_ref, v_ref, o_ref, kmean_scr, vt_scr, gate_scr, *, n_heads):
    h = pl.program_id(1)
    qb = pl.program_id(2)
    blk = q_ref.shape[1]
    dh = q_ref.shape[2]
    nb = k_ref.shape[1] // blk
    scale = dh ** -0.5

    @pl.when(qb == 0)
    def _():
        for j in range(nb):
            kj = k_ref[0, j * blk:(j + 1) * blk, :].astype(F32)
            kmean_scr[j:j + 1, :] = jnp.mean(kj, axis=0, keepdims=True)
            vt_scr[j] = v_ref[0, j * blk:(j + 1) * blk, :].astype(F32).T.astype(BF16)

    q = q_ref[0]
    gate_scr[...] = lax.dot_general(kmean_scr[...], q.astype(F32), NT_DIMS, precision=HI,
                                    preferred_element_type=F32)
    gate = gate_scr[...]
    blk_id = lax.broadcasted_iota(jnp.int32, (nb, 1), 0)
    past = blk_id < qb

    slope = jnp.exp(jnp.full((1, blk), -8.0 / n_heads * 0.6931471805599453, F32)
                    * (h + 1).astype(F32))
    ik = lax.broadcasted_iota(jnp.int32, (blk, blk), 0)
    iq = lax.broadcasted_iota(jnp.int32, (blk, blk), 1)
    rel = (iq - ik).astype(F32)

    k_own = k_ref[0, pl.ds(pl.multiple_of(qb * blk, blk), blk), :]
    s_own = lax.dot_general(k_own, q, NT_DIMS, preferred_element_type=F32) * scale - slope * rel
    s_own = jnp.where(rel >= 0, s_own, -jnp.inf)
    m0 = jnp.max(s_own, axis=0, keepdims=True)
    p0 = jnp.exp(s_own - m0)
    l0 = jnp.sum(p0, axis=0, keepdims=True)
    acc0 = jnp.dot(vt_scr[qb], p0.astype(BF16), preferred_element_type=F32)

    def body(n, carry):
        m, l, acc = carry
        g_n = gate_scr[pl.ds(n, 1), :]
        beats = past & ((gate > g_n) | ((gate == g_n) & (blk_id < n)))
        rank = jnp.sum(beats.astype(F32), axis=0, keepdims=True)
        sel = rank < float(MOBA_TOPK)
        k_n = k_ref[0, pl.ds(pl.multiple_of(n * blk, blk), blk), :]
        off = ((qb - n) * blk).astype(F32)
        s = (lax.dot_general(k_n, q, NT_DIMS, preferred_element_type=F32) * scale
             - slope * (rel + off))
        s = jnp.where(sel, s, -jnp.inf)
        m_new = jnp.maximum(m, jnp.max(s, axis=0, keepdims=True))
        alpha = jnp.exp(m - m_new)
        p = jnp.exp(s - m_new)
        l = alpha * l + jnp.sum(p, axis=0, keepdims=True)
        acc = alpha * acc + jnp.dot(vt_scr[n], p.astype(BF16), preferred_element_type=F32)
        return m_new, l, acc

    m, l, acc = lax.fori_loop(0, qb, body, (m0, l0, acc0))
    o_ref[0] = (acc / l).T.astype(o_ref.dtype)


def _moba(qkv, n_heads):
    bsz, s, d3 = qkv.shape
    d = d3 // 3
    dh = d // n_heads
    blk = MOBA_BLOCK
    nb = s // blk
    kern = functools.partial(_moba_kernel, n_heads=n_heads)
    return pl.pallas_call(
        kern,
        grid=(bsz, n_heads, nb),
        in_specs=[pl.BlockSpec((1, blk, dh), lambda b, h, t: (b, t, h)),
                  pl.BlockSpec((1, s, dh), lambda b, h, t: (b, 0, n_heads + h)),
                  pl.BlockSpec((1, s, dh), lambda b, h, t: (b, 0, 2 * n_heads + h))],
        out_specs=pl.BlockSpec((1, blk, dh), lambda b, h, t: (b, t, h)),
        out_shape=jax.ShapeDtypeStruct((bsz, s, d), BF16),
        scratch_shapes=[pltpu.VMEM((nb, dh), F32),
                        pltpu.VMEM((nb, dh, blk), BF16),
                        pltpu.VMEM((nb, blk), F32)],
        compiler_params=_params(("parallel", "parallel", "arbitrary")),
        name="moba",
    )(qkv, qkv, qkv)


def _pad_cols(w, n):
    return jnp.pad(w, ((0, 0), (0, n - w.shape[1])))


def _pad_rows(w, n, before=0):
    return jnp.pad(w, ((before, n - before - w.shape[0]), (0, 0)))


def kernel(x, c, w_ada, b_ada, g_pre_mix, g_post_mix, g_pre_ffn, g_post_ffn, w_ffn_in, w_ffn_out,
           w_in_ab, w_out_ab, a_v_gain, a_v_bias, a_w_s, a_b_s, b_mu, b_w0, b_w2, b_a0, b_a2, b_g2,
           b_k_k, b_k_a, b_r_k, b_lnx_gain, b_lnx_bias, w_qkv, w_o):
    bsz, s, d = x.shape
    depth = w_ada.shape[0]
    a_width = a_v_gain.shape[1]
    b_width = b_w0.shape[1]
    n_lw = b_w2.shape[1]
    n_la = b_a2.shape[1]
    n_lg = b_g2.shape[1]
    n_heads = d // ATT_HEAD
    assert s % MOBA_BLOCK == 0 and s % 512 == 0
    assert n_lw + n_la <= LANE and n_lg <= 2 * LANE

    mod = _ada_mod(c, w_ada, b_ada)
    w_ffn_in_h = w_ffn_in.astype(BF16)
    w_ffn_out_h = w_ffn_out.astype(BF16)

    for layer in range(depth):
        mod3 = mod[layer].reshape(bsz, 1, 6 * d)
        i = layer // 2
        if layer % 2 == 0:
            lora_w = 4 * LANE
            nz = 2 * a_width + 3 * b_width + lora_w
            w_in = _pad_cols(w_in_ab[i], nz).astype(BF16)
            z = _norm_mm(x, g_pre_mix[layer], mod3, 1, 0, w_in, F32, name="in_proj_ab")
            y_a = _mixer_a(z, a_v_gain[i], a_v_bias[i], a_w_s[i], a_b_s[i], a_width)

            mu = b_mu[i]
            pvec = jnp.stack([mu[0:b_width], mu[b_width:2 * b_width], mu[2 * b_width:3 * b_width],
                              b_w0[i], b_a0[i], b_k_k[i], b_k_a[i], b_r_k[i].reshape(-1),
                              b_lnx_gain[i], b_lnx_bias[i]])
            pvec = _pad_rows(pvec, _PV_ROWS)
            mu_l = _pad_cols(mu[3 * b_width:].reshape(1, -1), lora_w)
            w2p = _pad_rows(b_w2[i], LANE)
            a2p = _pad_rows(b_a2[i], LANE, before=n_lw)
            g2p = _pad_rows(b_g2[i], 2 * LANE)
            cb = 2 * a_width // LANE
            nb_w = b_width // LANE
            y_b = _rwkv(z, cb, cb + nb_w, cb + 2 * nb_w, (2 * a_width + 3 * b_width) // lora_w,
                        lora_w, pvec, mu_l, w2p, a2p, g2p, b_width)
            x = _out_proj(y_a, y_b, 0, 0, w_out_ab[i].astype(BF16), x, mod3, g_post_mix[layer])
        else:
            qkv = _norm_mm(x, g_pre_mix[layer], mod3, 1, 0, w_qkv[i].astype(BF16), BF16,
                           name="qkv_proj")
            o = _moba(qkv, n_heads)
            x = _out_proj(o, o, 0, 1, w_o[i].astype(BF16), x, mod3, g_post_mix[layer])
        x = _ffn(x, g_pre_ffn[layer], g_post_ffn[layer], mod3, w_ffn_in_h, w_ffn_out_h, layer)
    return x
```

```python
import functools

import jax
import jax.numpy as jnp
from jax import lax
from jax.experimental import pallas as pl
from jax.experimental.pallas import tpu as pltpu

F32 = jnp.float32
BF16 = jnp.bfloat16
HI = lax.Precision.HIGHEST

NORM_EPS = 1e-6
LN_EPS = 1e-5
GN_EPS = 64e-5

LANE = 128
A_GROUPS = 8
A_CHUNK = 128
RW_HEAD = 64
RW_CHUNK = 64
RW_PACK = 2
MOBA_BLOCK = 256
MOBA_TOPK = 3
ATT_HEAD = 128

NT_DIMS = (((1,), (1,)), ((), ()))
TN_DIMS = (((0,), (0,)), ((), ()))

VMEM_LIMIT = 56 * 1024 * 1024


def _params(sem):
    return pltpu.CompilerParams(dimension_semantics=sem, vmem_limit_bytes=VMEM_LIMIT)


def _rms(x, gain):
    ms = jnp.mean(x * x, axis=-1, keepdims=True)
    return x * lax.rsqrt(ms + NORM_EPS) * gain


def _ada_kernel(c_ref, w_ref, b_ref, o_ref):
    c = c_ref[...]
    cond = (c * jax.nn.sigmoid(c)).astype(BF16)
    o_ref[0] = jnp.dot(cond, w_ref[0].astype(BF16), preferred_element_type=F32) + b_ref[0]


def _ada_mod(c, w_ada, b_ada):
    depth, d, n = w_ada.shape
    bsz = c.shape[0]
    bp = 8
    c_p = jnp.pad(c, ((0, bp - bsz), (0, 0)))
    tn = 1024
    out = pl.pallas_call(
        _ada_kernel,
        grid=(depth, n // tn),
        in_specs=[pl.BlockSpec((bp, d), lambda l, j: (0, 0)),
                  pl.BlockSpec((1, d, tn), lambda l, j: (l, 0, j)),
                  pl.BlockSpec((1, 1, tn), lambda l, j: (l, 0, j))],
        out_specs=pl.BlockSpec((1, bp, tn), lambda l, j: (l, 0, j)),
        out_shape=jax.ShapeDtypeStruct((depth, bp, n), F32),
        compiler_params=_params(("parallel", "parallel")),
        name="ada_mod",
    )(c_p, w_ada, b_ada.reshape(depth, 1, n))
    return out[:, :bsz]


def _norm_mm_kernel(x_ref, g_ref, sc_ref, sh_ref, w_ref, o_ref, h_scr):
    @pl.when(pl.program_id(2) == 0)
    def _():
        h = _rms(x_ref[0], g_ref[...])
        h_scr[...] = (h * (1.0 + sc_ref[0]) + sh_ref[0]).astype(BF16)

    o_ref[0] = jnp.dot(h_scr[...], w_ref[...], preferred_element_type=F32).astype(o_ref.dtype)


def _norm_mm(x, gain, mod3, sc_idx, sh_idx, w, out_dtype, tm=512, tn=512, name="norm_mm"):
    bsz, s, d = x.shape
    n = w.shape[1]
    return pl.pallas_call(
        _norm_mm_kernel,
        grid=(bsz, s // tm, n // tn),
        in_specs=[pl.BlockSpec((1, tm, d), lambda b, m, j: (b, m, 0)),
                  pl.BlockSpec((1, d), lambda b, m, j: (0, 0)),
                  pl.BlockSpec((1, 1, d), lambda b, m, j: (b, 0, sc_idx)),
                  pl.BlockSpec((1, 1, d), lambda b, m, j: (b, 0, sh_idx)),
                  pl.BlockSpec((d, tn), lambda b, m, j: (0, j))],
        out_specs=pl.BlockSpec((1, tm, tn), lambda b, m, j: (b, m, j)),
        out_shape=jax.ShapeDtypeStruct((bsz, s, n), out_dtype),
        scratch_shapes=[pltpu.VMEM((tm, d), BF16)],
        compiler_params=_params(("parallel", "parallel", "arbitrary")),
        name=name,
    )(x, gain.reshape(1, d), mod3, mod3, w)


def _ffn_kernel(x_ref, gpre_ref, sc_ref, sh_ref, gt_ref, gpost_ref, wg_ref, wu_ref, wo_ref,
                o_ref, h_scr, acc_scr):
    f = pl.program_id(2)

    @pl.when(f == 0)
    def _():
        h = _rms(x_ref[0], gpre_ref[...])
        h_scr[...] = (h * (1.0 + sc_ref[0]) + sh_ref[0]).astype(BF16)

    h = h_scr[...]
    g = jnp.dot(h, wg_ref[...], preferred_element_type=F32)
    u = jnp.dot(h, wu_ref[...], preferred_element_type=F32)
    a = (g * jax.nn.sigmoid(g) * u).astype(BF16)
    y = jnp.dot(a, wo_ref[...], preferred_element_type=F32)

    @pl.when(f == 0)
    def _():
        acc_scr[...] = y

    @pl.when(f > 0)
    def _():
        acc_scr[...] += y

    @pl.when(f == pl.num_programs(2) - 1)
    def _():
        o_ref[0] = x_ref[0] + gt_ref[0] * _rms(acc_scr[...], gpost_ref[...])


def _ffn(x, gpre, gpost, mod3, w_in, w_out, layer, tm=512, tf=512):
    bsz, s, d = x.shape
    fh = w_out.shape[1]
    nf = fh // tf
    return pl.pallas_call(
        _ffn_kernel,
        grid=(bsz, s // tm, nf),
        in_specs=[pl.BlockSpec((1, tm, d), lambda b, m, f: (b, m, 0)),
                  pl.BlockSpec((1, d), lambda b, m, f: (0, 0)),
                  pl.BlockSpec((1, 1, d), lambda b, m, f: (b, 0, 4)),
                  pl.BlockSpec((1, 1, d), lambda b, m, f: (b, 0, 3)),
                  pl.BlockSpec((1, 1, d), lambda b, m, f: (b, 0, 5)),
                  pl.BlockSpec((1, d), lambda b, m, f: (0, 0)),
                  pl.BlockSpec((None, d, tf), lambda b, m, f: (layer, 0, f)),
                  pl.BlockSpec((None, d, tf), lambda b, m, f: (layer, 0, nf + f)),
                  pl.BlockSpec((None, tf, d), lambda b, m, f: (layer, f, 0))],
        out_specs=pl.BlockSpec((1, tm, d), lambda b, m, f: (b, m, 0)),
        out_shape=jax.ShapeDtypeStruct((bsz, s, d), F32),
        scratch_shapes=[pltpu.VMEM((tm, d), BF16), pltpu.VMEM((tm, d), F32)],
        compiler_params=_params(("parallel", "parallel", "arbitrary")),
        name="ffn",
    )(x, gpre.reshape(1, d), mod3, mod3, mod3, gpost.reshape(1, d), w_in, w_in, w_out)


def _out_proj_kernel(a0_ref, a1_ref, w0_ref, w1_ref, x_ref, gt_ref, gpost_ref, o_ref):
    y = (jnp.dot(a0_ref[0], w0_ref[...], preferred_element_type=F32)
         + jnp.dot(a1_ref[0], w1_ref[...], preferred_element_type=F32))
    o_ref[0] = x_ref[0] + gt_ref[0] * _rms(y, gpost_ref[...])


def _out_proj(a0, a1, col0, col1, w, x, mod3, gpost, tm=256):
    bsz, s, d = x.shape
    kh = w.shape[0] // 2
    return pl.pallas_call(
        _out_proj_kernel,
        grid=(bsz, s // tm),
        in_specs=[pl.BlockSpec((1, tm, kh), lambda b, m: (b, m, col0)),
                  pl.BlockSpec((1, tm, kh), lambda b, m: (b, m, col1)),
                  pl.BlockSpec((kh, d), lambda b, m: (0, 0)),
                  pl.BlockSpec((kh, d), lambda b, m: (1, 0)),
                  pl.BlockSpec((1, tm, d), lambda b, m: (b, m, 0)),
                  pl.BlockSpec((1, 1, d), lambda b, m: (b, 0, 2)),
                  pl.BlockSpec((1, d), lambda b, m: (0, 0))],
        out_specs=pl.BlockSpec((1, tm, d), lambda b, m: (b, m, 0)),
        out_shape=jax.ShapeDtypeStruct((bsz, s, d), F32),
        compiler_params=_params(("parallel", "parallel")),
        name="out_proj",
    )(a0, a1, w, w, x, mod3, gpost.reshape(1, d))


def _mixer_a_kernel(z_ref, vg_ref, vb_ref, ws_ref, bst_ref, o_ref):
    z = jax.nn.gelu(z_ref[0].astype(F32))
    wdt = z.shape[1] // 2
    u = z[:, :wdt]
    v = z[:, wdt:]
    mu = jnp.mean(v, axis=-1, keepdims=True)
    dv = v - mu
    var = jnp.mean(dv * dv, axis=-1, keepdims=True)
    vn = (dv * lax.rsqrt(var + LN_EPS) * vg_ref[...] + vb_ref[...]).astype(BF16)
    ch = z.shape[0]
    causal = (lax.broadcasted_iota(jnp.int32, (ch, ch), 0)
              >= lax.broadcasted_iota(jnp.int32, (ch, ch), 1))
    gd = wdt // A_GROUPS
    for g in range(A_GROUPS):
        w = jnp.where(causal, ws_ref[g], 0.0).astype(BF16)
        sv = jnp.dot(w, vn[:, g * gd:(g + 1) * gd], preferred_element_type=F32)
        sv = sv + bst_ref[:, g:g + 1]
        o_ref[0, :, g * gd:(g + 1) * gd] = (u[:, g * gd:(g + 1) * gd] * sv).astype(o_ref.dtype)


def _mixer_a(z, v_gain, v_bias, w_s, b_s, width):
    bsz, s, _ = z.shape
    ch = A_CHUNK
    return pl.pallas_call(
        _mixer_a_kernel,
        grid=(bsz, s // ch),
        in_specs=[pl.BlockSpec((1, ch, 2 * width), lambda b, c: (b, c, 0)),
                  pl.BlockSpec((1, width), lambda b, c: (0, 0)),
                  pl.BlockSpec((1, width), lambda b, c: (0, 0)),
                  pl.BlockSpec((A_GROUPS, ch, ch), lambda b, c: (0, 0, 0)),
                  pl.BlockSpec((ch, A_GROUPS), lambda b, c: (0, 0))],
        out_specs=pl.BlockSpec((1, ch, width), lambda b, c: (b, c, 0)),
        out_shape=jax.ShapeDtypeStruct((bsz, s, width), BF16),
        compiler_params=_params(("parallel", "parallel")),
        name="mixer_a",
    )(z, v_gain.reshape(1, width), v_bias.reshape(1, width), w_s, b_s.T)


_PV_MU_R, _PV_MU_K, _PV_MU_V, _PV_W0, _PV_A0, _PV_KK, _PV_KA, _PV_RK, _PV_LG, _PV_LB = range(10)
_PV_ROWS = 16


def _shift_lerp(x, prev_row, mu):
    rolled = pltpu.roll(x, 1, axis=0)
    first = lax.broadcasted_iota(jnp.int32, x.shape, 0) == 0
    xp = jnp.where(first, prev_row, rolled)
    return x + mu * (xp - x)


def _split_bf16(x):
    hi = x.astype(BF16)
    lo = (x - hi.astype(F32)).astype(BF16)
    return hi, lo


def _mm(x, y):
    return jnp.dot(x.astype(BF16), y.astype(BF16), preferred_element_type=F32)


def _mm_nt(x, y):
    return lax.dot_general(x.astype(BF16), y.astype(BF16), NT_DIMS, preferred_element_type=F32)


def _mm_tn(x, y):
    return lax.dot_general(x.astype(BF16), y.astype(BF16), TN_DIMS, preferred_element_type=F32)


def _mm_x3(x, y):
    xh, xl = _split_bf16(x)
    yh, yl = _split_bf16(y)
    return (jnp.dot(xh, yh, preferred_element_type=F32) + jnp.dot(xl, yh, preferred_element_type=F32)
            + jnp.dot(xh, yl, preferred_element_type=F32))


def _mm_exact_rhs(x, e_bf16):
    xh, xl = _split_bf16(x)
    return (jnp.dot(xh, e_bf16, preferred_element_type=F32)
            + jnp.dot(xl, e_bf16, preferred_element_type=F32))


def _rwkv_kernel(zr_ref, zk_ref, zv_ref, zl_ref, pv_ref, mul_ref, w2_ref, a2_ref, g2_ref, o_ref,
                 s_scr, prev_scr, prevl_scr, q_scr, y_scr, gm_scr, cm_scr, pl_scr, *, chunk):
    tb = pl.program_id(2)
    t_rows = zr_ref.shape[1]
    lanes = zr_ref.shape[2]
    L = chunk
    SL = RW_PACK * L
    n_chunks = t_rows // L

    @pl.when(tb == 0)
    def _():
        s_scr[...] = jnp.zeros_like(s_scr)
        prev_scr[...] = jnp.zeros_like(prev_scr)
        prevl_scr[...] = jnp.zeros_like(prevl_scr)

    def pv(i):
        return pv_ref[i:i + 1, :]

    zr = zr_ref[0]
    zk = zk_ref[0]
    zv = zv_ref[0]
    zl = zl_ref[0]
    r = _shift_lerp(zr, prev_scr[0:1, :], pv(_PV_MU_R))
    k = _shift_lerp(zk, prev_scr[1:2, :], pv(_PV_MU_K))
    v = _shift_lerp(zv, prev_scr[2:3, :], pv(_PV_MU_V))
    zls = _shift_lerp(zl, prevl_scr[0:1, :], mul_ref[...])
    prev_scr[0:1, :] = zr[t_rows - 1:t_rows, :]
    prev_scr[1:2, :] = zk[t_rows - 1:t_rows, :]
    prev_scr[2:3, :] = zv[t_rows - 1:t_rows, :]
    prevl_scr[0:1, :] = zl[t_rows - 1:t_rows, :]

    x_wa = zls[:, :LANE]
    x_g = zls[:, LANE:3 * LANE]
    w_pre = pv(_PV_W0) + _mm_x3(jnp.tanh(x_wa), w2_ref[...])
    t = -w_pre
    softplus = jnp.maximum(t, 0.0) + jnp.log1p(jnp.exp(-jnp.abs(t)))
    log_decay = -jnp.exp(-softplus - 0.5)
    a = jax.nn.sigmoid(pv(_PV_A0) + _mm_x3(x_wa, a2_ref[...]))
    g = _mm(jax.nn.sigmoid(x_g), g2_ref[...])

    li = lax.broadcasted_iota(jnp.int32, (lanes, lanes), 0) // RW_HEAD
    lj = lax.broadcasted_iota(jnp.int32, (lanes, lanes), 1) // RW_HEAD
    same_head = li == lj
    e_head = jnp.where(same_head, 1.0, 0.0).astype(BF16)

    kk = k * pv(_PV_KK)
    kk = kk / jnp.maximum(jnp.sqrt(_mm_exact_rhs(kk * kk, e_head)), 1e-12)
    k = k * (1.0 + (a - 1.0) * pv(_PV_KA))
    av = -kk
    bv = kk * a

    ri = lax.broadcasted_iota(jnp.int32, (t_rows, t_rows), 0)
    ci = lax.broadcasted_iota(jnp.int32, (t_rows, t_rows), 1)
    tri_bd = jnp.where((ri >= ci) & ((ri // L) == (ci // L)), 1.0, 0.0).astype(BF16)
    lw_hi, lw_lo = _split_bf16(log_decay)
    cum = (jnp.dot(tri_bd, lw_hi, preferred_element_type=F32)
           + jnp.dot(tri_bd, lw_lo, preferred_element_type=F32))

    si = lax.broadcasted_iota(jnp.int32, (SL, SL), 0)
    sj = lax.broadcasted_iota(jnp.int32, (SL, SL), 1)
    same_blk = (si // L) == (sj // L)
    m_strict = same_blk & (si > sj)
    m_incl = same_blk & (si >= sj)
    eye = jnp.where(si == sj, 1.0, 0.0)
    lane_head = lax.broadcasted_iota(jnp.int32, (1, lanes), 1) // RW_HEAD
    n_sq = max(L.bit_length() - 2, 0)

    def stack(x):
        return jnp.concatenate([jnp.where(lane_head == h, x, 0.0) for h in range(RW_PACK)], axis=0)

    def unstack(x):
        out = x[0:L]
        for h in range(1, RW_PACK):
            out = out + x[h * L:(h + 1) * L]
        return out

    cs = range(n_chunks)
    rows = [slice(c * L, (c + 1) * L) for c in cs]
    cm = [cum[rw] for rw in rows]
    cm_last = [x[L - 1:L, :] for x in cm]
    r_t = [r[rw] * jnp.exp(cm[c]) for c, rw in enumerate(rows)]
    a_st = [stack(av[rw] * jnp.exp(cm[c] - log_decay[rw])).astype(BF16) for c, rw in enumerate(rows)]
    e_neg = [jnp.exp(-x) for x in cm]
    k_rep = [jnp.concatenate([(k[rw] * e_neg[c]).astype(BF16)] * RW_PACK, axis=0)
             for c, rw in enumerate(rows)]
    b_rep = [jnp.concatenate([(bv[rw] * e_neg[c]).astype(BF16)] * RW_PACK, axis=0)
             for c, rw in enumerate(rows)]
    ar_st = [jnp.concatenate([a_st[c], stack(r_t[c]).astype(BF16)], axis=0) for c in cs]
    p_k = [_mm_nt(ar_st[c], k_rep[c]) for c in cs]
    p_b = [_mm_nt(ar_st[c], b_rep[c]) for c in cs]
    a_ak = [jnp.where(m_strict, x[:SL], 0.0).astype(BF16) for x in p_k]
    a_rk = [jnp.where(m_incl, x[SL:], 0.0).astype(BF16) for x in p_k]
    a_ab = [jnp.where(m_strict, x[:SL], 0.0) for x in p_b]
    a_rb = [jnp.where(m_incl, x[SL:], 0.0).astype(BF16) for x in p_b]

    xp = [x.astype(BF16) for x in a_ab]
    tinv = [eye + x for x in a_ab]
    for _ in range(n_sq):
        xp = [_mm(x, x).astype(BF16) for x in xp]
        tinv = [tinv[c] + _mm(tinv[c], xp[c]) for c in cs]
    tinv = [x.astype(BF16) for x in tinv]

    v_st = [stack(v[rw]).astype(BF16) for rw in rows]
    x0 = [_mm(a_ak[c], v_st[c]).astype(BF16) for c in cs]
    wu = [_mm(tinv[c], jnp.concatenate([a_st[c], x0[c]], axis=1)) for c in cs]
    yq = [_mm(a_rb[c], wu[c]) for c in cs]
    y0 = [_mm(a_rk[c], v_st[c]) for c in cs]
    for c, rw in enumerate(rows):
        q_scr[c] = (r_t[c] + unstack(yq[c][:, :lanes])).astype(BF16)
        y_scr[rw, :] = unstack(yq[c][:, lanes:] + y0[c])
    for c, rw in enumerate(rows):
        e_rem = jnp.exp(cm_last[c] - cm[c])
        b_b = bv[rw] * e_rem
        kb = jnp.concatenate([k[rw] * e_rem, b_b], axis=0)
        vu = jnp.concatenate([v[rw], unstack(wu[c][:, lanes:])], axis=0)
        cm_scr[c] = jnp.where(same_head, _mm_tn(vu, kb), 0.0)
        gm_scr[c] = jnp.where(same_head, _mm_tn(unstack(wu[c][:, :lanes]), b_b), 0.0).astype(BF16)
        pl_scr[c] = jnp.broadcast_to(jnp.exp(cm_last[c]), (8, lanes))

    s0 = s_scr[...]
    for c in range(n_chunks):
        lo, hi = c * L, (c + 1) * L
        s0b = s0.astype(BF16)
        y_scr[lo:hi, :] = y_scr[lo:hi, :] + _mm_nt(q_scr[c], s0b)
        s0 = s0 * pl_scr[c][0:1, :] + _mm(s0b, gm_scr[c]) + cm_scr[c]
    s_scr[...] = s0

    y = y_scr[...]
    inv_n = 1.0 / RW_HEAD
    mean = _mm_exact_rhs(y, e_head) * inv_n
    dy = y - mean
    var = _mm_exact_rhs(dy * dy, e_head) * inv_n
    yn = dy * lax.rsqrt(var + GN_EPS) * pv(_PV_LG) + pv(_PV_LB)
    bonus = _mm_exact_rhs(r * k * pv(_PV_RK), e_head) * v
    o_ref[0] = ((yn + bonus) * g).astype(o_ref.dtype)


def _rwkv(z, col_r, col_k, col_v, col_l, lora_w, pvec, mu_l, w2p, a2p, g2p, width, t_rows=512):
    bsz, s, _ = z.shape
    n_blk = width // LANE
    n_chunks = t_rows // RW_CHUNK
    kern = functools.partial(_rwkv_kernel, chunk=RW_CHUNK)
    return pl.pallas_call(
        kern,
        grid=(bsz, n_blk, s // t_rows),
        in_specs=[pl.BlockSpec((1, t_rows, LANE), lambda b, p, t: (b, t, col_r + p)),
                  pl.BlockSpec((1, t_rows, LANE), lambda b, p, t: (b, t, col_k + p)),
                  pl.BlockSpec((1, t_rows, LANE), lambda b, p, t: (b, t, col_v + p)),
                  pl.BlockSpec((1, t_rows, lora_w), lambda b, p, t: (b, t, col_l)),
                  pl.BlockSpec((_PV_ROWS, LANE), lambda b, p, t: (0, p)),
                  pl.BlockSpec((1, lora_w), lambda b, p, t: (0, 0)),
                  pl.BlockSpec((LANE, LANE), lambda b, p, t: (0, p)),
                  pl.BlockSpec((LANE, LANE), lambda b, p, t: (0, p)),
                  pl.BlockSpec((2 * LANE, LANE), lambda b, p, t: (0, p))],
        out_specs=pl.BlockSpec((1, t_rows, LANE), lambda b, p, t: (b, t, p)),
        out_shape=jax.ShapeDtypeStruct((bsz, s, width), BF16),
        scratch_shapes=[pltpu.VMEM((LANE, LANE), F32),
                        pltpu.VMEM((8, LANE), F32),
                        pltpu.VMEM((8, lora_w), F32),
                        pltpu.VMEM((n_chunks, RW_CHUNK, LANE), BF16),
                        pltpu.VMEM((t_rows, LANE), F32),
                        pltpu.VMEM((n_chunks, LANE, LANE), BF16),
                        pltpu.VMEM((n_chunks, LANE, LANE), F32),
                        pltpu.VMEM((n_chunks, 8, LANE), F32)],
        compiler_params=_params(("parallel", "parallel", "arbitrary")),
        name="rwkv7",
    )(z, z, z, z, pvec, mu_l, w2p, a2p, g2p)


def _moba_kernel(q_ref, k_ref, v_ref, o_ref, kmean_scr, vt_scr, gate_scr, *, n_heads):
    h = pl.program_id(1)
    qb = pl.program_id(2)
    blk = q_ref.shape[1]
    dh = q_ref.shape[2]
    nb = k_ref.shape[1] // blk
    scale = dh ** -0.5

    @pl.when(qb == 0)
    def _():
        for j in range(nb):
            kj = k_ref[0, j * blk:(j + 1) * blk, :].astype(F32)
            kmean_scr[j:j + 1, :] = jnp.mean(kj, axis=0, keepdims=True)
            vt_scr[j] = v_ref[0, j * blk:(j + 1) * blk, :].astype(F32).T.astype(BF16)

    q = q_ref[0]
    gate_scr[...] = lax.dot_general(kmean_scr[...], q.astype(F32), NT_DIMS, precision=HI,
                                    preferred_element_type=F32)
    gate = gate_scr[...]
    blk_id = lax.broadcasted_iota(jnp.int32, (nb, 1), 0)
    past = blk_id < qb

    slope = jnp.exp(jnp.full((1, blk), -8.0 / n_heads * 0.6931471805599453, F32)
                    * (h + 1).astype(F32))
    ik = lax.broadcasted_iota(jnp.int32, (blk, blk), 0)
    iq = lax.broadcasted_iota(jnp.int32, (blk, blk), 1)
    rel = (iq - ik).astype(F32)

    k_own = k_ref[0, pl.ds(pl.multiple_of(qb * blk, blk), blk), :]
    s_own = lax.dot_general(k_own, q, NT_DIMS, preferred_element_type=F32) * scale - slope * rel
    s_own = jnp.where(rel >= 0, s_own, -jnp.inf)
    m0 = jnp.max(s_own, axis=0, keepdims=True)
    p0 = jnp.exp(s_own - m0)
    l0 = jnp.sum(p0, axis=0, keepdims=True)
    acc0 = jnp.dot(vt_scr[qb], p0.astype(BF16), preferred_element_type=F32)

    def body(n, carry):
        m, l, acc = carry
        g_n = gate_scr[pl.ds(n, 1), :]
        beats = past & ((gate > g_n) | ((gate == g_n) & (blk_id < n)))
        rank = jnp.sum(beats.astype(F32), axis=0, keepdims=True)
        sel = rank < float(MOBA_TOPK)
        k_n = k_ref[0, pl.ds(pl.multiple_of(n * blk, blk), blk), :]
        off = ((qb - n) * blk).astype(F32)
        s = (lax.dot_general(k_n, q, NT_DIMS, preferred_element_type=F32) * scale
             - slope * (rel + off))
        s = jnp.where(sel, s, -jnp.inf)
        m_new = jnp.maximum(m, jnp.max(s, axis=0, keepdims=True))
        alpha = jnp.exp(m - m_new)
        p = jnp.exp(s - m_new)
        l = alpha * l + jnp.sum(p, axis=0, keepdims=True)
        acc = alpha * acc + jnp.dot(vt_scr[n], p.astype(BF16), preferred_element_type=F32)
        return m_new, l, acc

    m, l, acc = lax.fori_loop(0, qb, body, (m0, l0, acc0))
    o_ref[0] = (acc / l).T.astype(o_ref.dtype)


def _moba(qkv, n_heads):
    bsz, s, d3 = qkv.shape
    d = d3 // 3
    dh = d // n_heads
    blk = MOBA_BLOCK
    nb = s // blk
    kern = functools.partial(_moba_kernel, n_heads=n_heads)
    return pl.pallas_call(
        kern,
        grid=(bsz, n_heads, nb),
        in_specs=[pl.BlockSpec((1, blk, dh), lambda b, h, t: (b, t, h)),
                  pl.BlockSpec((1, s, dh), lambda b, h, t: (b, 0, n_heads + h)),
                  pl.BlockSpec((1, s, dh), lambda b, h, t: (b, 0, 2 * n_heads + h))],
        out_specs=pl.BlockSpec((1, blk, dh), lambda b, h, t: (b, t, h)),
        out_shape=jax.ShapeDtypeStruct((bsz, s, d), BF16),
        scratch_shapes=[pltpu.VMEM((nb, dh), F32),
                        pltpu.VMEM((nb, dh, blk), BF16),
                        pltpu.VMEM((nb, blk), F32)],
        compiler_params=_params(("parallel", "parallel", "arbitrary")),
        name="moba",
    )(qkv, qkv, qkv)


def _pad_cols(w, n):
    return jnp.pad(w, ((0, 0), (0, n - w.shape[1])))


def _pad_rows(w, n, before=0):
    return jnp.pad(w, ((before, n - before - w.shape[0]), (0, 0)))


def kernel(x, c, w_ada, b_ada, g_pre_mix, g_post_mix, g_pre_ffn, g_post_ffn, w_ffn_in, w_ffn_out,
           w_in_ab, w_out_ab, a_v_gain, a_v_bias, a_w_s, a_b_s, b_mu, b_w0, b_w2, b_a0, b_a2, b_g2,
           b_k_k, b_k_a, b_r_k, b_lnx_gain, b_lnx_bias, w_qkv, w_o):
    bsz, s, d = x.shape
    depth = w_ada.shape[0]
    a_width = a_v_gain.shape[1]
    b_width = b_w0.shape[1]
    n_lw = b_w2.shape[1]
    n_la = b_a2.shape[1]
    n_lg = b_g2.shape[1]
    n_heads = d // ATT_HEAD
    assert s % MOBA_BLOCK == 0 and s % 512 == 0
    assert n_lw + n_la <= LANE and n_lg <= 2 * LANE

    mod = _ada_mod(c, w_ada, b_ada)
    w_ffn_in_h = w_ffn_in.astype(BF16)
    w_ffn_out_h = w_ffn_out.astype(BF16)

    for layer in range(depth):
        mod3 = mod[layer].reshape(bsz, 1, 6 * d)
        i = layer // 2
        if layer % 2 == 0:
            lora_w = 4 * LANE
            nz = 2 * a_width + 3 * b_width + lora_w
            w_in = _pad_cols(w_in_ab[i], nz).astype(BF16)
            z = _norm_mm(x, g_pre_mix[layer], mod3, 1, 0, w_in, F32, name="in_proj_ab")
            y_a = _mixer_a(z, a_v_gain[i], a_v_bias[i], a_w_s[i], a_b_s[i], a_width)

            mu = b_mu[i]
            pvec = jnp.stack([mu[0:b_width], mu[b_width:2 * b_width], mu[2 * b_width:3 * b_width],
                              b_w0[i], b_a0[i], b_k_k[i], b_k_a[i], b_r_k[i].reshape(-1),
                              b_lnx_gain[i], b_lnx_bias[i]])
            pvec = _pad_rows(pvec, _PV_ROWS)
            mu_l = _pad_cols(mu[3 * b_width:].reshape(1, -1), lora_w)
            w2p = _pad_rows(b_w2[i], LANE)
            a2p = _pad_rows(b_a2[i], LANE, before=n_lw)
            g2p = _pad_rows(b_g2[i], 2 * LANE)
            cb = 2 * a_width // LANE
            nb_w = b_width // LANE
            y_b = _rwkv(z, cb, cb + nb_w, cb + 2 * nb_w, (2 * a_width + 3 * b_width) // lora_w,
                        lora_w, pvec, mu_l, w2p, a2p, g2p, b_width)
            x = _out_proj(y_a, y_b, 0, 0, w_out_ab[i].astype(BF16), x, mod3, g_post_mix[layer])
        else:
            qkv = _norm_mm(x, g_pre_mix[layer], mod3, 1, 0, w_qkv[i].astype(BF16), BF16,
                           name="qkv_proj")
            o = _moba(qkv, n_heads)
            x = _out_proj(o, o, 0, 1, w_o[i].astype(BF16), x, mod3, g_post_mix[layer])
        x = _ffn(x, g_pre_ffn[layer], g_post_ffn[layer], mod3, w_ffn_in_h, w_ffn_out_h, layer)
    return x
```

```python
import functools

import jax
import jax.numpy as jnp
from jax import lax
from jax.experimental import pallas as pl
from jax.experimental.pallas import tpu as pltpu

F32 = jnp.float32
BF16 = jnp.bfloat16
HI = lax.Precision.HIGHEST

NORM_EPS = 1e-6
LN_EPS = 1e-5
GN_EPS = 64e-5

LANE = 128
A_GROUPS = 8
A_CHUNK = 128
RW_HEAD = 64
RW_CHUNK = 64
RW_PACK = 2
MOBA_BLOCK = 256
MOBA_TOPK = 3
ATT_HEAD = 128

NT_DIMS = (((1,), (1,)), ((), ()))
TN_DIMS = (((0,), (0,)), ((), ()))

VMEM_LIMIT = 56 * 1024 * 1024


def _params(sem):
    return pltpu.CompilerParams(dimension_semantics=sem, vmem_limit_bytes=VMEM_LIMIT)


def _rms(x, gain):
    ms = jnp.mean(x * x, axis=-1, keepdims=True)
    return x * lax.rsqrt(ms + NORM_EPS) * gain


def _ada_kernel(c_ref, w_ref, b_ref, o_ref):
    c = c_ref[...]
    cond = (c * jax.nn.sigmoid(c)).astype(BF16)
    o_ref[0] = jnp.dot(cond, w_ref[0].astype(BF16), preferred_element_type=F32) + b_ref[0]


def _ada_mod(c, w_ada, b_ada):
    depth, d, n = w_ada.shape
    bsz = c.shape[0]
    bp = 8
    c_p = jnp.pad(c, ((0, bp - bsz), (0, 0)))
    tn = 1024
    out = pl.pallas_call(
        _ada_kernel,
        grid=(depth, n // tn),
        in_specs=[pl.BlockSpec((bp, d), lambda l, j: (0, 0)),
                  pl.BlockSpec((1, d, tn), lambda l, j: (l, 0, j)),
                  pl.BlockSpec((1, 1, tn), lambda l, j: (l, 0, j))],
        out_specs=pl.BlockSpec((1, bp, tn), lambda l, j: (l, 0, j)),
        out_shape=jax.ShapeDtypeStruct((depth, bp, n), F32),
        compiler_params=_params(("parallel", "parallel")),
        name="ada_mod",
    )(c_p, w_ada, b_ada.reshape(depth, 1, n))
    return out[:, :bsz]


def _norm_mm_kernel(x_ref, g_ref, sc_ref, sh_ref, w_ref, o_ref, h_scr):
    @pl.when(pl.program_id(2) == 0)
    def _():
        h = _rms(x_ref[0], g_ref[...])
        h_scr[...] = (h * (1.0 + sc_ref[0]) + sh_ref[0]).astype(BF16)

    o_ref[0] = jnp.dot(h_scr[...], w_ref[...], preferred_element_type=F32).astype(o_ref.dtype)


def _norm_mm(x, gain, mod3, sc_idx, sh_idx, w, out_dtype, tm=512, tn=512, name="norm_mm"):
    bsz, s, d = x.shape
    n = w.shape[1]
    return pl.pallas_call(
        _norm_mm_kernel,
        grid=(bsz, s // tm, n // tn),
        in_specs=[pl.BlockSpec((1, tm, d), lambda b, m, j: (b, m, 0)),
                  pl.BlockSpec((1, d), lambda b, m, j: (0, 0)),
                  pl.BlockSpec((1, 1, d), lambda b, m, j: (b, 0, sc_idx)),
                  pl.BlockSpec((1, 1, d), lambda b, m, j: (b, 0, sh_idx)),
                  pl.BlockSpec((d, tn), lambda b, m, j: (0, j))],
        out_specs=pl.BlockSpec((1, tm, tn), lambda b, m, j: (b, m, j)),
        out_shape=jax.ShapeDtypeStruct((bsz, s, n), out_dtype),
        scratch_shapes=[pltpu.VMEM((tm, d), BF16)],
        compiler_params=_params(("parallel", "parallel", "arbitrary")),
        name=name,
    )(x, gain.reshape(1, d), mod3, mod3, w)


def _ffn_kernel(x_ref, gpre_ref, sc_ref, sh_ref, gt_ref, gpost_ref, wg_ref, wu_ref, wo_ref,
                o_ref, h_scr, acc_scr):
    f = pl.program_id(2)

    @pl.when(f == 0)
    def _():
        h = _rms(x_ref[0], gpre_ref[...])
        h_scr[...] = (h * (1.0 + sc_ref[0]) + sh_ref[0]).astype(BF16)

    h = h_scr[...]
    g = jnp.dot(h, wg_ref[...], preferred_element_type=F32)
    u = jnp.dot(h, wu_ref[...], preferred_element_type=F32)
    a = (g * jax.nn.sigmoid(g) * u).astype(BF16)
    y = jnp.dot(a, wo_ref[...], preferred_element_type=F32)

    @pl.when(f == 0)
    def _():
        acc_scr[...] = y

    @pl.when(f > 0)
    def _():
        acc_scr[...] += y

    @pl.when(f == pl.num_programs(2) - 1)
    def _():
        o_ref[0] = x_ref[0] + gt_ref[0] * _rms(acc_scr[...], gpost_ref[...])


def _ffn(x, gpre, gpost, mod3, w_in, w_out, layer, tm=512, tf=512):
    bsz, s, d = x.shape
    fh = w_out.shape[1]
    nf = fh // tf
    return pl.pallas_call(
        _ffn_kernel,
        grid=(bsz, s // tm, nf),
        in_specs=[pl.BlockSpec((1, tm, d), lambda b, m, f: (b, m, 0)),
                  pl.BlockSpec((1, d), lambda b, m, f: (0, 0)),
                  pl.BlockSpec((1, 1, d), lambda b, m, f: (b, 0, 4)),
                  pl.BlockSpec((1, 1, d), lambda b, m, f: (b, 0, 3)),
                  pl.BlockSpec((1, 1, d), lambda b, m, f: (b, 0, 5)),
                  pl.BlockSpec((1, d), lambda b, m, f: (0, 0)),
                  pl.BlockSpec((None, d, tf), lambda b, m, f: (layer, 0, f)),
                  pl.BlockSpec((None, d, tf), lambda b, m, f: (layer, 0, nf + f)),
                  pl.BlockSpec((None, tf, d), lambda b, m, f: (layer, f, 0))],
        out_specs=pl.BlockSpec((1, tm, d), lambda b, m, f: (b, m, 0)),
        out_shape=jax.ShapeDtypeStruct((bsz, s, d), F32),
        scratch_shapes=[pltpu.VMEM((tm, d), BF16), pltpu.VMEM((tm, d), F32)],
        compiler_params=_params(("parallel", "parallel", "arbitrary")),
        name="ffn",
    )(x, gpre.reshape(1, d), mod3, mod3, mod3, gpost.reshape(1, d), w_in, w_in, w_out)


def _out_proj_kernel(a0_ref, a1_ref, w0_ref, w1_ref, x_ref, gt_ref, gpost_ref, o_ref):
    y = (jnp.dot(a0_ref[0], w0_ref[...], preferred_element_type=F32)
         + jnp.dot(a1_ref[0], w1_ref[...], preferred_element_type=F32))
    o_ref[0] = x_ref[0] + gt_ref[0] * _rms(y, gpost_ref[...])


def _out_proj(a0, a1, col0, col1, w, x, mod3, gpost, tm=256):
    bsz, s, d = x.shape
    kh = w.shape[0] // 2
    return pl.pallas_call(
        _out_proj_kernel,
        grid=(bsz, s // tm),
        in_specs=[pl.BlockSpec((1, tm, kh), lambda b, m: (b, m, col0)),
                  pl.BlockSpec((1, tm, kh), lambda b, m: (b, m, col1)),
                  pl.BlockSpec((kh, d), lambda b, m: (0, 0)),
                  pl.BlockSpec((kh, d), lambda b, m: (1, 0)),
                  pl.BlockSpec((1, tm, d), lambda b, m: (b, m, 0)),
                  pl.BlockSpec((1, 1, d), lambda b, m: (b, 0, 2)),
                  pl.BlockSpec((1, d), lambda b, m: (0, 0))],
        out_specs=pl.BlockSpec((1, tm, d), lambda b, m: (b, m, 0)),
        out_shape=jax.ShapeDtypeStruct((bsz, s, d), F32),
        compiler_params=_params(("parallel", "parallel")),
        name="out_proj",
    )(a0, a1, w, w, x, mod3, gpost.reshape(1, d))


def _mixer_a_kernel(z_ref, vg_ref, vb_ref, ws_ref, bst_ref, o_ref):
    z = jax.nn.gelu(z_ref[0].astype(F32))
    wdt = z.shape[1] // 2
    u = z[:, :wdt]
    v = z[:, wdt:]
    mu = jnp.mean(v, axis=-1, keepdims=True)
    dv = v - mu
    var = jnp.mean(dv * dv, axis=-1, keepdims=True)
    vn = (dv * lax.rsqrt(var + LN_EPS) * vg_ref[...] + vb_ref[...]).astype(BF16)
    ch = z.shape[0]
    causal = (lax.broadcasted_iota(jnp.int32, (ch, ch), 0)
              >= lax.broadcasted_iota(jnp.int32, (ch, ch), 1))
    gd = wdt // A_GROUPS
    for g in range(A_GROUPS):
        w = jnp.where(causal, ws_ref[g], 0.0).astype(BF16)
        sv = jnp.dot(w, vn[:, g * gd:(g + 1) * gd], preferred_element_type=F32)
        sv = sv + bst_ref[:, g:g + 1]
        o_ref[0, :, g * gd:(g + 1) * gd] = (u[:, g * gd:(g + 1) * gd] * sv).astype(o_ref.dtype)


def _mixer_a(z, v_gain, v_bias, w_s, b_s, width):
    bsz, s, _ = z.shape
    ch = A_CHUNK
    return pl.pallas_call(
        _mixer_a_kernel,
        grid=(bsz, s // ch),
        in_specs=[pl.BlockSpec((1, ch, 2 * width), lambda b, c: (b, c, 0)),
                  pl.BlockSpec((1, width), lambda b, c: (0, 0)),
                  pl.BlockSpec((1, width), lambda b, c: (0, 0)),
                  pl.BlockSpec((A_GROUPS, ch, ch), lambda b, c: (0, 0, 0)),
                  pl.BlockSpec((ch, A_GROUPS), lambda b, c: (0, 0))],
        out_specs=pl.BlockSpec((1, ch, width), lambda b, c: (b, c, 0)),
        out_shape=jax.ShapeDtypeStruct((bsz, s, width), BF16),
        compiler_params=_params(("parallel", "parallel")),
        name="mixer_a",
    )(z, v_gain.reshape(1, width), v_bias.reshape(1, width), w_s, b_s.T)


_PV_MU_R, _PV_MU_K, _PV_MU_V, _PV_W0, _PV_A0, _PV_KK, _PV_KA, _PV_RK, _PV_LG, _PV_LB = range(10)
_PV_ROWS = 16


def _shift_lerp(x, prev_row, mu):
    rolled = pltpu.roll(x, 1, axis=0)
    first = lax.broadcasted_iota(jnp.int32, x.shape, 0) == 0
    xp = jnp.where(first, prev_row, rolled)
    return x + mu * (xp - x)


def _split_bf16(x):
    hi = x.astype(BF16)
    lo = (x - hi.astype(F32)).astype(BF16)
    return hi, lo


def _mm(x, y):
    return jnp.dot(x.astype(BF16), y.astype(BF16), preferred_element_type=F32)


def _mm_nt(x, y):
    return lax.dot_general(x.astype(BF16), y.astype(BF16), NT_DIMS, preferred_element_type=F32)


def _mm_tn(x, y):
    return lax.dot_general(x.astype(BF16), y.astype(BF16), TN_DIMS, preferred_element_type=F32)


def _mm_x3(x, y):
    xh, xl = _split_bf16(x)
    yh, yl = _split_bf16(y)
    return (jnp.dot(xh, yh, preferred_element_type=F32) + jnp.dot(xl, yh, preferred_element_type=F32)
            + jnp.dot(xh, yl, preferred_element_type=F32))


def _mm_exact_rhs(x, e_bf16):
    xh, xl = _split_bf16(x)
    return (jnp.dot(xh, e_bf16, preferred_element_type=F32)
            + jnp.dot(xl, e_bf16, preferred_element_type=F32))


def _rwkv_kernel(zr_ref, zk_ref, zv_ref, zl_ref, pv_ref, mul_ref, w2_ref, a2_ref, g2_ref, o_ref,
                 s_scr, prev_scr, prevl_scr, q_scr, y_scr, gm_scr, cm_scr, pl_scr, *, chunk):
    tb = pl.program_id(2)
    t_rows = zr_ref.shape[1]
    lanes = zr_ref.shape[2]
    L = chunk
    SL = RW_PACK * L
    n_chunks = t_rows // L

    @pl.when(tb == 0)
    def _():
        s_scr[...] = jnp.zeros_like(s_scr)
        prev_scr[...] = jnp.zeros_like(prev_scr)
        prevl_scr[...] = jnp.zeros_like(prevl_scr)

    def pv(i):
        return pv_ref[i:i + 1, :]

    zr = zr_ref[0]
    zk = zk_ref[0]
    zv = zv_ref[0]
    zl = zl_ref[0]
    r = _shift_lerp(zr, prev_scr[0:1, :], pv(_PV_MU_R))
    k = _shift_lerp(zk, prev_scr[1:2, :], pv(_PV_MU_K))
    v = _shift_lerp(zv, prev_scr[2:3, :], pv(_PV_MU_V))
    zls = _shift_lerp(zl, prevl_scr[0:1, :], mul_ref[...])
    prev_scr[0:1, :] = zr[t_rows - 1:t_rows, :]
    prev_scr[1:2, :] = zk[t_rows - 1:t_rows, :]
    prev_scr[2:3, :] = zv[t_rows - 1:t_rows, :]
    prevl_scr[0:1, :] = zl[t_rows - 1:t_rows, :]

    x_wa = zls[:, :LANE]
    x_g = zls[:, LANE:3 * LANE]
    w_pre = pv(_PV_W0) + _mm_x3(jnp.tanh(x_wa), w2_ref[...])
    t = -w_pre
    softplus = jnp.maximum(t, 0.0) + jnp.log1p(jnp.exp(-jnp.abs(t)))
    log_decay = -jnp.exp(-softplus - 0.5)
    a = jax.nn.sigmoid(pv(_PV_A0) + _mm_x3(x_wa, a2_ref[...]))
    g = _mm(jax.nn.sigmoid(x_g), g2_ref[...])

    li = lax.broadcasted_iota(jnp.int32, (lanes, lanes), 0) // RW_HEAD
    lj = lax.broadcasted_iota(jnp.int32, (lanes, lanes), 1) // RW_HEAD
    same_head = li == lj
    e_head = jnp.where(same_head, 1.0, 0.0).astype(BF16)

    kk = k * pv(_PV_KK)
    kk = kk / jnp.maximum(jnp.sqrt(_mm_exact_rhs(kk * kk, e_head)), 1e-12)
    k = k * (1.0 + (a - 1.0) * pv(_PV_KA))
    av = -kk
    bv = kk * a

    ri = lax.broadcasted_iota(jnp.int32, (t_rows, t_rows), 0)
    ci = lax.broadcasted_iota(jnp.int32, (t_rows, t_rows), 1)
    tri_bd = jnp.where((ri >= ci) & ((ri // L) == (ci // L)), 1.0, 0.0).astype(BF16)
    lw_hi, lw_lo = _split_bf16(log_decay)
    cum = (jnp.dot(tri_bd, lw_hi, preferred_element_type=F32)
           + jnp.dot(tri_bd, lw_lo, preferred_element_type=F32))

    si = lax.broadcasted_iota(jnp.int32, (SL, SL), 0)
    sj = lax.broadcasted_iota(jnp.int32, (SL, SL), 1)
    same_blk = (si // L) == (sj // L)
    m_strict = same_blk & (si > sj)
    m_incl = same_blk & (si >= sj)
    eye = jnp.where(si == sj, 1.0, 0.0)
    lane_head = lax.broadcasted_iota(jnp.int32, (1, lanes), 1) // RW_HEAD
    n_sq = max(L.bit_length() - 2, 0)

    def stack(x):
        return jnp.concatenate([jnp.where(lane_head == h, x, 0.0) for h in range(RW_PACK)], axis=0)

    def unstack(x):
        out = x[0:L]
        for h in range(1, RW_PACK):
            out = out + x[h * L:(h + 1) * L]
        return out

    cs = range(n_chunks)
    rows = [slice(c * L, (c + 1) * L) for c in cs]
    cm = [cum[rw] for rw in rows]
    cm_last = [x[L - 1:L, :] for x in cm]
    r_t = [r[rw] * jnp.exp(cm[c]) for c, rw in enumerate(rows)]
    a_st = [stack(av[rw] * jnp.exp(cm[c] - log_decay[rw])).astype(BF16) for c, rw in enumerate(rows)]
    e_neg = [jnp.exp(-x) for x in cm]
    k_rep = [jnp.concatenate([(k[rw] * e_neg[c]).astype(BF16)] * RW_PACK, axis=0)
             for c, rw in enumerate(rows)]
    b_rep = [jnp.concatenate([(bv[rw] * e_neg[c]).astype(BF16)] * RW_PACK, axis=0)
             for c, rw in enumerate(rows)]
    ar_st = [jnp.concatenate([a_st[c], stack(r_t[c]).astype(BF16)], axis=0) for c in cs]
    p_k = [_mm_nt(ar_st[c], k_rep[c]) for c in cs]
    p_b = [_mm_nt(ar_st[c], b_rep[c]) for c in cs]
    a_ak = [jnp.where(m_strict, x[:SL], 0.0).astype(BF16) for x in p_k]
    a_rk = [jnp.where(m_incl, x[SL:], 0.0).astype(BF16) for x in p_k]
    a_ab = [jnp.where(m_strict, x[:SL], 0.0) for x in p_b]
    a_rb = [jnp.where(m_incl, x[SL:], 0.0).astype(BF16) for x in p_b]

    xp = [x.astype(BF16) for x in a_ab]
    tinv = [eye + x for x in a_ab]
    for _ in range(n_sq):
        xp = [_mm(x, x).astype(BF16) for x in xp]
        tinv = [tinv[c] + _mm(tinv[c], xp[c]) for c in cs]
    tinv = [x.astype(BF16) for x in tinv]

    v_st = [stack(v[rw]).astype(BF16) for rw in rows]
    x0 = [_mm(a_ak[c], v_st[c]).astype(BF16) for c in cs]
    wu = [_mm(tinv[c], jnp.concatenate([a_st[c], x0[c]], axis=1)) for c in cs]
    yq = [_mm(a_rb[c], wu[c]) for c in cs]
    y0 = [_mm(a_rk[c], v_st[c]) for c in cs]
    for c, rw in enumerate(rows):
        q_scr[c] = (r_t[c] + unstack(yq[c][:, :lanes])).astype(BF16)
        y_scr[rw, :] = unstack(yq[c][:, lanes:] + y0[c])
    for c, rw in enumerate(rows):
        e_rem = jnp.exp(cm_last[c] - cm[c])
        b_b = bv[rw] * e_rem
        kb = jnp.concatenate([k[rw] * e_rem, b_b], axis=0)
        vu = jnp.concatenate([v[rw], unstack(wu[c][:, lanes:])], axis=0)
        cm_scr[c] = jnp.where(same_head, _mm_tn(vu, kb), 0.0)
        gm_scr[c] = jnp.where(same_head, _mm_tn(unstack(wu[c][:, :lanes]), b_b), 0.0).astype(BF16)
        pl_scr[c] = jnp.broadcast_to(jnp.exp(cm_last[c]), (8, lanes))

    s0 = s_scr[...]
    for c in range(n_chunks):
        lo, hi = c * L, (c + 1) * L
        s0b = s0.astype(BF16)
        y_scr[lo:hi, :] = y_scr[lo:hi, :] + _mm_nt(q_scr[c], s0b)
        s0 = s0 * pl_scr[c][0:1, :] + _mm(s0b, gm_scr[c]) + cm_scr[c]
    s_scr[...] = s0

    y = y_scr[...]
    inv_n = 1.0 / RW_HEAD
    mean = _mm_exact_rhs(y, e_head) * inv_n
    dy = y - mean
    var = _mm_exact_rhs(dy * dy, e_head) * inv_n
    yn = dy * lax.rsqrt(var + GN_EPS) * pv(_PV_LG) + pv(_PV_LB)
    bonus = _mm_exact_rhs(r * k * pv(_PV_RK), e_head) * v
    o_ref[0] = ((yn + bonus) * g).astype(o_ref.dtype)


def _rwkv(z, col_r, col_k, col_v, col_l, lora_w, pvec, mu_l, w2p, a2p, g2p, width, t_rows=512):
    bsz, s, _ = z.shape
    n_blk = width // LANE
    n_chunks = t_rows // RW_CHUNK
    kern = functools.partial(_rwkv_kernel, chunk=RW_CHUNK)
    return pl.pallas_call(
        kern,
        grid=(bsz, n_blk, s // t_rows),
        in_specs=[pl.BlockSpec((1, t_rows, LANE), lambda b, p, t: (b, t, col_r + p)),
                  pl.BlockSpec((1, t_rows, LANE), lambda b, p, t: (b, t, col_k + p)),
                  pl.BlockSpec((1, t_rows, LANE), lambda b, p, t: (b, t, col_v + p)),
                  pl.BlockSpec((1, t_rows, lora_w), lambda b, p, t: (b, t, col_l)),
                  pl.BlockSpec((_PV_ROWS, LANE), lambda b, p, t: (0, p)),
                  pl.BlockSpec((1, lora_w), lambda b, p, t: (0, 0)),
                  pl.BlockSpec((LANE, LANE), lambda b, p, t: (0, p)),
                  pl.BlockSpec((LANE, LANE), lambda b, p, t: (0, p)),
                  pl.BlockSpec((2 * LANE, LANE), lambda b, p, t: (0, p))],
        out_specs=pl.BlockSpec((1, t_rows, LANE), lambda b, p, t: (b, t, p)),
        out_shape=jax.ShapeDtypeStruct((bsz, s, width), BF16),
        scratch_shapes=[pltpu.VMEM((LANE, LANE), F32),
                        pltpu.VMEM((8, LANE), F32),
                        pltpu.VMEM((8, lora_w), F32),
                        pltpu.VMEM((n_chunks, RW_CHUNK, LANE), BF16),
                        pltpu.VMEM((t_rows, LANE), F32),
                        pltpu.VMEM((n_chunks, LANE, LANE), BF16),
                        pltpu.VMEM((n_chunks, LANE, LANE), F32),
                        pltpu.VMEM((n_chunks, 8, LANE), F32)],
        compiler_params=_params(("parallel", "parallel", "arbitrary")),
        name="rwkv7",
    )(z, z, z, z, pvec, mu_l, w2p, a2p, g2p)


def _moba_kernel(q_ref, k_ref, v_ref, o_ref, vt_scr, s_scr, p_scr, *, n_heads):
    h = pl.program_id(1)
    s_len = q_ref.shape[1]
    dh = q_ref.shape[2]
    blk = MOBA_BLOCK
    nb = s_len // blk
    scale = dh ** -0.5
    neg_inf = -jnp.inf

    kmean = []
    for j in range(nb):
        rows = slice(j * blk, (j + 1) * blk)
        kmean.append(jnp.mean(k_ref[0, rows, :].astype(F32), axis=0, keepdims=True))
        vt_scr[:, rows] = v_ref[0, rows, :].astype(F32).T.astype(BF16)
    kmean = jnp.concatenate(kmean, axis=0)

    slope = jnp.exp(jnp.full((1, blk), -8.0 / n_heads * 0.6931471805599453, F32)
                    * (h + 1).astype(F32))
    jk = lax.broadcasted_iota(jnp.int32, (blk, blk), 0)
    iq = lax.broadcasted_iota(jnp.int32, (blk, blk), 1)
    bias_tile = slope * jk.astype(F32)
    bias_own = jnp.where(iq >= jk, bias_tile, neg_inf)
    blk_id = lax.broadcasted_iota(jnp.int32, (nb, 1), 0)

    for qb in range(nb):
        q = q_ref[0, qb * blk:(qb + 1) * blk, :]
        q_s = (q.astype(F32) * scale).astype(BF16)
        gate = lax.dot_general(kmean, q.astype(F32), NT_DIMS, precision=HI,
                               preferred_element_type=F32)
        past = blk_id < qb
        offs = []
        cols = []
        for n in range(qb + 1):
            k_n = k_ref[0, n * blk:(n + 1) * blk, :]
            s = lax.dot_general(k_n, q_s, NT_DIMS, preferred_element_type=F32)
            if n == qb:
                t = s + bias_own
                off = jnp.zeros((1, blk), F32)
            else:
                t = s + bias_tile
                g_n = gate[n:n + 1, :]
                beats = past & ((gate > g_n) | ((gate == g_n) & (blk_id < n)))
                rank = jnp.sum(jnp.where(beats, 1.0, 0.0), axis=0, keepdims=True)
                off = jnp.where(rank < float(MOBA_TOPK), slope * float((n - qb) * blk), neg_inf)
            s_scr[n] = t
            offs.append(off)
            cols.append(jnp.max(t, axis=0, keepdims=True) + off)
        m = cols[0]
        for cmax in cols[1:]:
            m = jnp.maximum(m, cmax)
        l = jnp.zeros((1, blk), F32)
        for n in range(qb + 1):
            p = jnp.exp(s_scr[n] - (m - offs[n]))
            l = l + jnp.sum(p, axis=0, keepdims=True)
            p_scr[n * blk:(n + 1) * blk, :] = p.astype(BF16)
        kk = (qb + 1) * blk
        acc = jnp.dot(vt_scr[:, :kk], p_scr[:kk, :], preferred_element_type=F32)
        o_ref[0, qb * blk:(qb + 1) * blk, :] = (acc / l).T.astype(o_ref.dtype)


def _moba(qkv, n_heads):
    bsz, s, d3 = qkv.shape
    d = d3 // 3
    dh = d // n_heads
    blk = MOBA_BLOCK
    nb = s // blk
    kern = functools.partial(_moba_kernel, n_heads=n_heads)
    return pl.pallas_call(
        kern,
        grid=(bsz, n_heads),
        in_specs=[pl.BlockSpec((1, s, dh), lambda b, h: (b, 0, h)),
                  pl.BlockSpec((1, s, dh), lambda b, h: (b, 0, n_heads + h)),
                  pl.BlockSpec((1, s, dh), lambda b, h: (b, 0, 2 * n_heads + h))],
        out_specs=pl.BlockSpec((1, s, dh), lambda b, h: (b, 0, h)),
        out_shape=jax.ShapeDtypeStruct((bsz, s, d), BF16),
        scratch_shapes=[pltpu.VMEM((dh, s), BF16),
                        pltpu.VMEM((nb, blk, blk), F32),
                        pltpu.VMEM((s, blk), BF16)],
        compiler_params=_params(("parallel", "parallel")),
        name="moba",
    )(qkv, qkv, qkv)


def _pad_cols(w, n):
    return jnp.pad(w, ((0, 0), (0, n - w.shape[1])))


def _pad_rows(w, n, before=0):
    return jnp.pad(w, ((before, n - before - w.shape[0]), (0, 0)))


def kernel(x, c, w_ada, b_ada, g_pre_mix, g_post_mix, g_pre_ffn, g_post_ffn, w_ffn_in, w_ffn_out,
           w_in_ab, w_out_ab, a_v_gain, a_v_bias, a_w_s, a_b_s, b_mu, b_w0, b_w2, b_a0, b_a2, b_g2,
           b_k_k, b_k_a, b_r_k, b_lnx_gain, b_lnx_bias, w_qkv, w_o):
    bsz, s, d = x.shape
    depth = w_ada.shape[0]
    a_width = a_v_gain.shape[1]
    b_width = b_w0.shape[1]
    n_lw = b_w2.shape[1]
    n_la = b_a2.shape[1]
    n_lg = b_g2.shape[1]
    n_heads = d // ATT_HEAD
    assert s % MOBA_BLOCK == 0 and s % 512 == 0
    assert n_lw + n_la <= LANE and n_lg <= 2 * LANE

    mod = _ada_mod(c, w_ada, b_ada)
    w_ffn_in_h = w_ffn_in.astype(BF16)
    w_ffn_out_h = w_ffn_out.astype(BF16)

    for layer in range(depth):
        mod3 = mod[layer].reshape(bsz, 1, 6 * d)
        i = layer // 2
        if layer % 2 == 0:
            lora_w = 4 * LANE
            nz = 2 * a_width + 3 * b_width + lora_w
            w_in = _pad_cols(w_in_ab[i], nz).astype(BF16)
            z = _norm_mm(x, g_pre_mix[layer], mod3, 1, 0, w_in, F32, name="in_proj_ab")
            y_a = _mixer_a(z, a_v_gain[i], a_v_bias[i], a_w_s[i], a_b_s[i], a_width)

            mu = b_mu[i]
            pvec = jnp.stack([mu[0:b_width], mu[b_width:2 * b_width], mu[2 * b_width:3 * b_width],
                              b_w0[i], b_a0[i], b_k_k[i], b_k_a[i], b_r_k[i].reshape(-1),
                              b_lnx_gain[i], b_lnx_bias[i]])
            pvec = _pad_rows(pvec, _PV_ROWS)
            mu_l = _pad_cols(mu[3 * b_width:].reshape(1, -1), lora_w)
            w2p = _pad_rows(b_w2[i], LANE)
            a2p = _pad_rows(b_a2[i], LANE, before=n_lw)
            g2p = _pad_rows(b_g2[i], 2 * LANE)
            cb = 2 * a_width // LANE
            nb_w = b_width // LANE
            y_b = _rwkv(z, cb, cb + nb_w, cb + 2 * nb_w, (2 * a_width + 3 * b_width) // lora_w,
                        lora_w, pvec, mu_l, w2p, a2p, g2p, b_width)
            x = _out_proj(y_a, y_b, 0, 0, w_out_ab[i].astype(BF16), x, mod3, g_post_mix[layer])
        else:
            qkv = _norm_mm(x, g_pre_mix[layer], mod3, 1, 0, w_qkv[i].astype(BF16), BF16,
                           name="qkv_proj")
            o = _moba(qkv, n_heads)
            x = _out_proj(o, o, 0, 1, w_o[i].astype(BF16), x, mod3, g_post_mix[layer])
        x = _ffn(x, g_pre_ffn[layer], g_post_ffn[layer], mod3, w_ffn_in_h, w_ffn_out_h, layer)
    return x
```

```python
import functools

import jax
import jax.numpy as jnp
from jax import lax
from jax.experimental import pallas as pl
from jax.experimental.pallas import tpu as pltpu

F32 = jnp.float32
BF16 = jnp.bfloat16
HI = lax.Precision.HIGHEST

NORM_EPS = 1e-6
LN_EPS = 1e-5
GN_EPS = 64e-5

LANE = 128
A_GROUPS = 8
A_CHUNK = 128
RW_HEAD = 64
RW_CHUNK = 64
RW_PACK = 2
MOBA_BLOCK = 256
MOBA_TOPK = 3
ATT_HEAD = 128

NT_DIMS = (((1,), (1,)), ((), ()))
TN_DIMS = (((0,), (0,)), ((), ()))

VMEM_LIMIT = 56 * 1024 * 1024


def _params(sem):
    return pltpu.CompilerParams(dimension_semantics=sem, vmem_limit_bytes=VMEM_LIMIT)


def _rms(x, gain):
    ms = jnp.mean(x * x, axis=-1, keepdims=True)
    return x * lax.rsqrt(ms + NORM_EPS) * gain


def _ada_kernel(c_ref, w_ref, b_ref, o_ref):
    c = c_ref[...]
    cond = (c * jax.nn.sigmoid(c)).astype(BF16)
    o_ref[0] = jnp.dot(cond, w_ref[0].astype(BF16), preferred_element_type=F32) + b_ref[0]


def _ada_mod(c, w_ada, b_ada):
    depth, d, n = w_ada.shape
    bsz = c.shape[0]
    bp = 8
    c_p = jnp.pad(c, ((0, bp - bsz), (0, 0)))
    tn = 1024
    out = pl.pallas_call(
        _ada_kernel,
        grid=(depth, n // tn),
        in_specs=[pl.BlockSpec((bp, d), lambda l, j: (0, 0)),
                  pl.BlockSpec((1, d, tn), lambda l, j: (l, 0, j)),
                  pl.BlockSpec((1, 1, tn), lambda l, j: (l, 0, j))],
        out_specs=pl.BlockSpec((1, bp, tn), lambda l, j: (l, 0, j)),
        out_shape=jax.ShapeDtypeStruct((depth, bp, n), F32),
        compiler_params=_params(("parallel", "parallel")),
        name="ada_mod",
    )(c_p, w_ada, b_ada.reshape(depth, 1, n))
    return out[:, :bsz]


def _norm_mm_kernel(x_ref, g_ref, sc_ref, sh_ref, w_ref, o_ref, h_scr):
    @pl.when(pl.program_id(2) == 0)
    def _():
        h = _rms(x_ref[0], g_ref[...])
        h_scr[...] = (h * (1.0 + sc_ref[0]) + sh_ref[0]).astype(BF16)

    o_ref[0] = jnp.dot(h_scr[...], w_ref[...], preferred_element_type=F32).astype(o_ref.dtype)


def _norm_mm(x, gain, mod3, sc_idx, sh_idx, w, out_dtype, tm=1024, tn=512, name="norm_mm"):
    bsz, s, d = x.shape
    n = w.shape[1]
    return pl.pallas_call(
        _norm_mm_kernel,
        grid=(bsz, s // tm, n // tn),
        in_specs=[pl.BlockSpec((1, tm, d), lambda b, m, j: (b, m, 0)),
                  pl.BlockSpec((1, d), lambda b, m, j: (0, 0)),
                  pl.BlockSpec((1, 1, d), lambda b, m, j: (b, 0, sc_idx)),
                  pl.BlockSpec((1, 1, d), lambda b, m, j: (b, 0, sh_idx)),
                  pl.BlockSpec((d, tn), lambda b, m, j: (0, j))],
        out_specs=pl.BlockSpec((1, tm, tn), lambda b, m, j: (b, m, j)),
        out_shape=jax.ShapeDtypeStruct((bsz, s, n), out_dtype),
        scratch_shapes=[pltpu.VMEM((tm, d), BF16)],
        compiler_params=_params(("parallel", "parallel", "arbitrary")),
        name=name,
    )(x, gain.reshape(1, d), mod3, mod3, w)


def _ffn_kernel(x_ref, gpre_ref, sc_ref, sh_ref, gt_ref, gpost_ref, wg_ref, wu_ref, wo_ref,
                o_ref, h_scr):
    f = pl.program_id(2)

    @pl.when(f == 0)
    def _():
        h = _rms(x_ref[0], gpre_ref[...])
        h_scr[...] = (h * (1.0 + sc_ref[0]) + sh_ref[0]).astype(BF16)
        o_ref[0] = jnp.zeros(o_ref.shape[1:], F32)

    h = h_scr[...]
    g = jnp.dot(h, wg_ref[...], preferred_element_type=F32)
    u = jnp.dot(h, wu_ref[...], preferred_element_type=F32)
    a = (g * jax.nn.sigmoid(g) * u).astype(BF16)
    o_ref[0] += jnp.dot(a, wo_ref[...], preferred_element_type=F32)

    @pl.when(f == pl.num_programs(2) - 1)
    def _():
        o_ref[0] = x_ref[0] + gt_ref[0] * _rms(o_ref[0], gpost_ref[...])


def _ffn(x, gpre, gpost, mod3, w_in, w_out, layer, tm=1024, tf=512):
    bsz, s, d = x.shape
    fh = w_out.shape[1]
    nf = fh // tf
    return pl.pallas_call(
        _ffn_kernel,
        grid=(bsz, s // tm, nf),
        in_specs=[pl.BlockSpec((1, tm, d), lambda b, m, f: (b, m, 0), pipeline_mode=pl.Buffered(1)),
                  pl.BlockSpec((1, d), lambda b, m, f: (0, 0)),
                  pl.BlockSpec((1, 1, d), lambda b, m, f: (b, 0, 4)),
                  pl.BlockSpec((1, 1, d), lambda b, m, f: (b, 0, 3)),
                  pl.BlockSpec((1, 1, d), lambda b, m, f: (b, 0, 5)),
                  pl.BlockSpec((1, d), lambda b, m, f: (0, 0)),
                  pl.BlockSpec((None, d, tf), lambda b, m, f: (layer, 0, f)),
                  pl.BlockSpec((None, d, tf), lambda b, m, f: (layer, 0, nf + f)),
                  pl.BlockSpec((None, tf, d), lambda b, m, f: (layer, f, 0))],
        out_specs=pl.BlockSpec((1, tm, d), lambda b, m, f: (b, m, 0)),
        out_shape=jax.ShapeDtypeStruct((bsz, s, d), F32),
        scratch_shapes=[pltpu.VMEM((tm, d), BF16)],
        compiler_params=_params(("parallel", "parallel", "arbitrary")),
        name="ffn",
    )(x, gpre.reshape(1, d), mod3, mod3, mod3, gpost.reshape(1, d), w_in, w_in, w_out)


def _out_proj_kernel(a0_ref, a1_ref, w0_ref, w1_ref, x_ref, gt_ref, gpost_ref, o_ref):
    y = (jnp.dot(a0_ref[0], w0_ref[...], preferred_element_type=F32)
         + jnp.dot(a1_ref[0], w1_ref[...], preferred_element_type=F32))
    o_ref[0] = x_ref[0] + gt_ref[0] * _rms(y, gpost_ref[...])


def _out_proj(a0, a1, col0, col1, w, x, mod3, gpost, tm=256):
    bsz, s, d = x.shape
    kh = w.shape[0] // 2
    return pl.pallas_call(
        _out_proj_kernel,
        grid=(bsz, s // tm),
        in_specs=[pl.BlockSpec((1, tm, kh), lambda b, m: (b, m, col0)),
                  pl.BlockSpec((1, tm, kh), lambda b, m: (b, m, col1)),
                  pl.BlockSpec((kh, d), lambda b, m: (0, 0)),
                  pl.BlockSpec((kh, d), lambda b, m: (1, 0)),
                  pl.BlockSpec((1, tm, d), lambda b, m: (b, m, 0)),
                  pl.BlockSpec((1, 1, d), lambda b, m: (b, 0, 2)),
                  pl.BlockSpec((1, d), lambda b, m: (0, 0))],
        out_specs=pl.BlockSpec((1, tm, d), lambda b, m: (b, m, 0)),
        out_shape=jax.ShapeDtypeStruct((bsz, s, d), F32),
        compiler_params=_params(("parallel", "parallel")),
        name="out_proj",
    )(a0, a1, w, w, x, mod3, gpost.reshape(1, d))


def _mixer_a_kernel(z_ref, vg_ref, vb_ref, ws_ref, bst_ref, o_ref):
    z = jax.nn.gelu(z_ref[0].astype(F32))
    wdt = z.shape[1] // 2
    u = z[:, :wdt]
    v = z[:, wdt:]
    mu = jnp.mean(v, axis=-1, keepdims=True)
    dv = v - mu
    var = jnp.mean(dv * dv, axis=-1, keepdims=True)
    vn = (dv * lax.rsqrt(var + LN_EPS) * vg_ref[...] + vb_ref[...]).astype(BF16)
    ch = z.shape[0]
    causal = (lax.broadcasted_iota(jnp.int32, (ch, ch), 0)
              >= lax.broadcasted_iota(jnp.int32, (ch, ch), 1))
    gd = wdt // A_GROUPS
    for g in range(A_GROUPS):
        w = jnp.where(causal, ws_ref[g], 0.0).astype(BF16)
        sv = jnp.dot(w, vn[:, g * gd:(g + 1) * gd], preferred_element_type=F32)
        sv = sv + bst_ref[:, g:g + 1]
        o_ref[0, :, g * gd:(g + 1) * gd] = (u[:, g * gd:(g + 1) * gd] * sv).astype(o_ref.dtype)


def _mixer_a(z, v_gain, v_bias, w_s, b_s, width):
    bsz, s, _ = z.shape
    ch = A_CHUNK
    return pl.pallas_call(
        _mixer_a_kernel,
        grid=(bsz, s // ch),
        in_specs=[pl.BlockSpec((1, ch, 2 * width), lambda b, c: (b, c, 0)),
                  pl.BlockSpec((1, width), lambda b, c: (0, 0)),
                  pl.BlockSpec((1, width), lambda b, c: (0, 0)),
                  pl.BlockSpec((A_GROUPS, ch, ch), lambda b, c: (0, 0, 0)),
                  pl.BlockSpec((ch, A_GROUPS), lambda b, c: (0, 0))],
        out_specs=pl.BlockSpec((1, ch, width), lambda b, c: (b, c, 0)),
        out_shape=jax.ShapeDtypeStruct((bsz, s, width), BF16),
        compiler_params=_params(("parallel", "parallel")),
        name="mixer_a",
    )(z, v_gain.reshape(1, width), v_bias.reshape(1, width), w_s, b_s.T)


_PV_MU_R, _PV_MU_K, _PV_MU_V, _PV_W0, _PV_A0, _PV_KK, _PV_KA, _PV_RK, _PV_LG, _PV_LB = range(10)
_PV_ROWS = 16


def _shift_lerp(x, prev_row, mu):
    rolled = pltpu.roll(x, 1, axis=0)
    first = lax.broadcasted_iota(jnp.int32, x.shape, 0) == 0
    xp = jnp.where(first, prev_row, rolled)
    return x + mu * (xp - x)


def _split_bf16(x):
    hi = x.astype(BF16)
    lo = (x - hi.astype(F32)).astype(BF16)
    return hi, lo


def _mm(x, y):
    return jnp.dot(x.astype(BF16), y.astype(BF16), preferred_element_type=F32)


def _mm_nt(x, y):
    return lax.dot_general(x.astype(BF16), y.astype(BF16), NT_DIMS, preferred_element_type=F32)


def _mm_tn(x, y):
    return lax.dot_general(x.astype(BF16), y.astype(BF16), TN_DIMS, preferred_element_type=F32)


def _mm_x3(x, y):
    xh, xl = _split_bf16(x)
    yh, yl = _split_bf16(y)
    return (jnp.dot(xh, yh, preferred_element_type=F32) + jnp.dot(xl, yh, preferred_element_type=F32)
            + jnp.dot(xh, yl, preferred_element_type=F32))


def _mm_exact_rhs(x, e_bf16):
    xh, xl = _split_bf16(x)
    return (jnp.dot(xh, e_bf16, preferred_element_type=F32)
            + jnp.dot(xl, e_bf16, preferred_element_type=F32))


def _rwkv_kernel(zr_ref, zk_ref, zv_ref, zl_ref, pv_ref, mul_ref, w2_ref, a2_ref, g2_ref, o_ref,
                 s_scr, prev_scr, prevl_scr, q_scr, y_scr, gm_scr, cm_scr, pl_scr, *, chunk):
    tb = pl.program_id(2)
    t_rows = zr_ref.shape[1]
    lanes = zr_ref.shape[2]
    L = chunk
    SL = RW_PACK * L
    n_chunks = t_rows // L

    @pl.when(tb == 0)
    def _():
        s_scr[...] = jnp.zeros_like(s_scr)
        prev_scr[...] = jnp.zeros_like(prev_scr)
        prevl_scr[...] = jnp.zeros_like(prevl_scr)

    def pv(i):
        return pv_ref[i:i + 1, :]

    zr = zr_ref[0].astype(F32)
    zk = zk_ref[0].astype(F32)
    zv = zv_ref[0].astype(F32)
    zl = zl_ref[0].astype(F32)
    r = _shift_lerp(zr, prev_scr[0:1, :], pv(_PV_MU_R))
    k = _shift_lerp(zk, prev_scr[1:2, :], pv(_PV_MU_K))
    v = _shift_lerp(zv, prev_scr[2:3, :], pv(_PV_MU_V))
    zls = _shift_lerp(zl, prevl_scr[0:1, :], mul_ref[...])
    prev_scr[0:1, :] = zr[t_rows - 1:t_rows, :]
    prev_scr[1:2, :] = zk[t_rows - 1:t_rows, :]
    prev_scr[2:3, :] = zv[t_rows - 1:t_rows, :]
    prevl_scr[0:1, :] = zl[t_rows - 1:t_rows, :]

    x_wa = zls[:, :LANE]
    x_g = zls[:, LANE:3 * LANE]
    w_pre = pv(_PV_W0) + _mm_x3(jnp.tanh(x_wa), w2_ref[...])
    t = -w_pre
    softplus = jnp.maximum(t, 0.0) + jnp.log1p(jnp.exp(-jnp.abs(t)))
    log_decay = -jnp.exp(-softplus - 0.5)
    a = jax.nn.sigmoid(pv(_PV_A0) + _mm_x3(x_wa, a2_ref[...]))
    g = _mm(jax.nn.sigmoid(x_g), g2_ref[...])

    li = lax.broadcasted_iota(jnp.int32, (lanes, lanes), 0) // RW_HEAD
    lj = lax.broadcasted_iota(jnp.int32, (lanes, lanes), 1) // RW_HEAD
    same_head = li == lj
    e_head = jnp.where(same_head, 1.0, 0.0).astype(BF16)

    kk = k * pv(_PV_KK)
    kk = kk / jnp.maximum(jnp.sqrt(_mm_exact_rhs(kk * kk, e_head)), 1e-12)
    k = k * (1.0 + (a - 1.0) * pv(_PV_KA))
    av = -kk
    bv = kk * a

    ri = lax.broadcasted_iota(jnp.int32, (t_rows, t_rows), 0)
    ci = lax.broadcasted_iota(jnp.int32, (t_rows, t_rows), 1)
    tri_bd = jnp.where((ri >= ci) & ((ri // L) == (ci // L)), 1.0, 0.0).astype(BF16)
    lw_hi, lw_lo = _split_bf16(log_decay)
    cum = (jnp.dot(tri_bd, lw_hi, preferred_element_type=F32)
           + jnp.dot(tri_bd, lw_lo, preferred_element_type=F32))

    si = lax.broadcasted_iota(jnp.int32, (SL, SL), 0)
    sj = lax.broadcasted_iota(jnp.int32, (SL, SL), 1)
    same_blk = (si // L) == (sj // L)
    m_strict = same_blk & (si > sj)
    m_incl = same_blk & (si >= sj)
    eye = jnp.where(si == sj, 1.0, 0.0)
    lane_head = lax.broadcasted_iota(jnp.int32, (1, lanes), 1) // RW_HEAD
    n_sq = max(L.bit_length() - 2, 0)

    def stack(x):
        return jnp.concatenate([jnp.where(lane_head == h, x, 0.0) for h in range(RW_PACK)], axis=0)

    def unstack(x):
        out = x[0:L]
        for h in range(1, RW_PACK):
            out = out + x[h * L:(h + 1) * L]
        return out

    cs = range(n_chunks)
    rows = [slice(c * L, (c + 1) * L) for c in cs]
    cm = [cum[rw] for rw in rows]
    cm_last = [x[L - 1:L, :] for x in cm]
    r_t = [r[rw] * jnp.exp(cm[c]) for c, rw in enumerate(rows)]
    a_st = [stack(av[rw] * jnp.exp(cm[c] - log_decay[rw])).astype(BF16) for c, rw in enumerate(rows)]
    e_neg = [jnp.exp(-x) for x in cm]
    k_rep = [jnp.concatenate([(k[rw] * e_neg[c]).astype(BF16)] * RW_PACK, axis=0)
             for c, rw in enumerate(rows)]
    b_rep = [jnp.concatenate([(bv[rw] * e_neg[c]).astype(BF16)] * RW_PACK, axis=0)
             for c, rw in enumerate(rows)]
    ar_st = [jnp.concatenate([a_st[c], stack(r_t[c]).astype(BF16)], axis=0) for c in cs]
    p_k = [_mm_nt(ar_st[c], k_rep[c]) for c in cs]
    p_b = [_mm_nt(ar_st[c], b_rep[c]) for c in cs]
    a_ak = [jnp.where(m_strict, x[:SL], 0.0).astype(BF16) for x in p_k]
    a_rk = [jnp.where(m_incl, x[SL:], 0.0).astype(BF16) for x in p_k]
    a_ab = [jnp.where(m_strict, x[:SL], 0.0) for x in p_b]
    a_rb = [jnp.where(m_incl, x[SL:], 0.0).astype(BF16) for x in p_b]

    xp = [x.astype(BF16) for x in a_ab]
    tinv = [eye + x for x in a_ab]
    for _ in range(n_sq):
        xp = [_mm(x, x).astype(BF16) for x in xp]
        tinv = [tinv[c] + _mm(tinv[c], xp[c]) for c in cs]
    tinv = [x.astype(BF16) for x in tinv]

    v_st = [stack(v[rw]).astype(BF16) for rw in rows]
    x0 = [_mm(a_ak[c], v_st[c]).astype(BF16) for c in cs]
    wu = [_mm(tinv[c], jnp.concatenate([a_st[c], x0[c]], axis=1)) for c in cs]
    yq = [_mm(a_rb[c], wu[c]) for c in cs]
    y0 = [_mm(a_rk[c], v_st[c]) for c in cs]
    for c, rw in enumerate(rows):
        q_scr[c] = (r_t[c] + unstack(yq[c][:, :lanes])).astype(BF16)
        y_scr[rw, :] = unstack(yq[c][:, lanes:] + y0[c])
    for c, rw in enumerate(rows):
        e_rem = jnp.exp(cm_last[c] - cm[c])
        b_b = bv[rw] * e_rem
        kb = jnp.concatenate([k[rw] * e_rem, b_b], axis=0)
        vu = jnp.concatenate([v[rw], unstack(wu[c][:, lanes:])], axis=0)
        cm_scr[c] = jnp.where(same_head, _mm_tn(vu, kb), 0.0)
        gm_scr[c] = jnp.where(same_head, _mm_tn(unstack(wu[c][:, :lanes]), b_b), 0.0).astype(BF16)
        pl_scr[c] = jnp.broadcast_to(jnp.exp(cm_last[c]), (8, lanes))

    s0 = s_scr[...]
    for c in range(n_chunks):
        lo, hi = c * L, (c + 1) * L
        s0b = s0.astype(BF16)
        y_scr[lo:hi, :] = y_scr[lo:hi, :] + _mm_nt(q_scr[c], s0b)
        s0 = s0 * pl_scr[c][0:1, :] + _mm(s0b, gm_scr[c]) + cm_scr[c]
    s_scr[...] = s0

    y = y_scr[...]
    inv_n = 1.0 / RW_HEAD
    mean = _mm_exact_rhs(y, e_head) * inv_n
    dy = y - mean
    var = _mm_exact_rhs(dy * dy, e_head) * inv_n
    yn = dy * lax.rsqrt(var + GN_EPS) * pv(_PV_LG) + pv(_PV_LB)
    bonus = _mm_exact_rhs(r * k * pv(_PV_RK), e_head) * v
    o_ref[0] = ((yn + bonus) * g).astype(o_ref.dtype)


def _rwkv(z, col_r, col_k, col_v, col_l, lora_w, pvec, mu_l, w2p, a2p, g2p, width, t_rows=512):
    bsz, s, _ = z.shape
    n_blk = width // LANE
    n_chunks = t_rows // RW_CHUNK
    kern = functools.partial(_rwkv_kernel, chunk=RW_CHUNK)
    return pl.pallas_call(
        kern,
        grid=(bsz, n_blk, s // t_rows),
        in_specs=[pl.BlockSpec((1, t_rows, LANE), lambda b, p, t: (b, t, col_r + p)),
                  pl.BlockSpec((1, t_rows, LANE), lambda b, p, t: (b, t, col_k + p)),
                  pl.BlockSpec((1, t_rows, LANE), lambda b, p, t: (b, t, col_v + p)),
                  pl.BlockSpec((1, t_rows, lora_w), lambda b, p, t: (b, t, col_l)),
                  pl.BlockSpec((_PV_ROWS, LANE), lambda b, p, t: (0, p)),
                  pl.BlockSpec((1, lora_w), lambda b, p, t: (0, 0)),
                  pl.BlockSpec((LANE, LANE), lambda b, p, t: (0, p)),
                  pl.BlockSpec((LANE, LANE), lambda b, p, t: (0, p)),
                  pl.BlockSpec((2 * LANE, LANE), lambda b, p, t: (0, p))],
        out_specs=pl.BlockSpec((1, t_rows, LANE), lambda b, p, t: (b, t, p)),
        out_shape=jax.ShapeDtypeStruct((bsz, s, width), BF16),
        scratch_shapes=[pltpu.VMEM((LANE, LANE), F32),
                        pltpu.VMEM((8, LANE), F32),
                        pltpu.VMEM((8, lora_w), F32),
                        pltpu.VMEM((n_chunks, RW_CHUNK, LANE), BF16),
                        pltpu.VMEM((t_rows, LANE), F32),
                        pltpu.VMEM((n_chunks, LANE, LANE), BF16),
                        pltpu.VMEM((n_chunks, LANE, LANE), F32),
                        pltpu.VMEM((n_chunks, 8, LANE), F32)],
        compiler_params=_params(("parallel", "parallel", "arbitrary")),
        name="rwkv7",
    )(z, z, z, z, pvec, mu_l, w2p, a2p, g2p)


def _moba_kernel(q_ref, k_ref, v_ref, o_ref, vt_scr, s_scr, p_scr, *, n_heads):
    h = pl.program_id(1)
    s_len = q_ref.shape[1]
    dh = q_ref.shape[2]
    blk = MOBA_BLOCK
    nb = s_len // blk
    scale = dh ** -0.5
    neg_inf = -jnp.inf

    kmean = []
    for j in range(nb):
        rows = slice(j * blk, (j + 1) * blk)
        kmean.append(jnp.mean(k_ref[0, rows, :].astype(F32), axis=0, keepdims=True))
        vt_scr[:, rows] = v_ref[0, rows, :].astype(F32).T.astype(BF16)
    kmean = jnp.concatenate(kmean, axis=0)

    slope = jnp.exp(jnp.full((1, blk), -8.0 / n_heads * 0.6931471805599453, F32)
                    * (h + 1).astype(F32))
    jk = lax.broadcasted_iota(jnp.int32, (blk, blk), 0)
    iq = lax.broadcasted_iota(jnp.int32, (blk, blk), 1)
    bias_tile = slope * jk.astype(F32)
    bias_own = jnp.where(iq >= jk, bias_tile, neg_inf)
    blk_id = lax.broadcasted_iota(jnp.int32, (nb, 1), 0)

    for qb in range(nb):
        q = q_ref[0, qb * blk:(qb + 1) * blk, :]
        q_s = (q.astype(F32) * scale).astype(BF16)
        gate = lax.dot_general(kmean, q.astype(F32), NT_DIMS, precision=HI,
                               preferred_element_type=F32)
        past = blk_id < qb
        offs = []
        cols = []
        for n in range(qb + 1):
            k_n = k_ref[0, n * blk:(n + 1) * blk, :]
            s = lax.dot_general(k_n, q_s, NT_DIMS, preferred_element_type=F32)
            if n == qb:
                t = s + bias_own
                off = jnp.zeros((1, blk), F32)
            else:
                t = s + bias_tile
                g_n = gate[n:n + 1, :]
                beats = past & ((gate > g_n) | ((gate == g_n) & (blk_id < n)))
                rank = jnp.sum(jnp.where(beats, 1.0, 0.0), axis=0, keepdims=True)
                off = jnp.where(rank < float(MOBA_TOPK), slope * float((n - qb) * blk), neg_inf)
            s_scr[n] = t
            offs.append(off)
            cols.append(jnp.max(t, axis=0, keepdims=True) + off)
        m = cols[0]
        for cmax in cols[1:]:
            m = jnp.maximum(m, cmax)
        l = jnp.zeros((1, blk), F32)
        for n in range(qb + 1):
            p = jnp.exp(s_scr[n] - (m - offs[n]))
            l = l + jnp.sum(p, axis=0, keepdims=True)
            p_scr[n * blk:(n + 1) * blk, :] = p.astype(BF16)
        kk = (qb + 1) * blk
        acc = jnp.dot(vt_scr[:, :kk], p_scr[:kk, :], preferred_element_type=F32)
        o_ref[0, qb * blk:(qb + 1) * blk, :] = (acc / l).T.astype(o_ref.dtype)


def _moba(qkv, n_heads):
    bsz, s, d3 = qkv.shape
    d = d3 // 3
    dh = d // n_heads
    blk = MOBA_BLOCK
    nb = s // blk
    kern = functools.partial(_moba_kernel, n_heads=n_heads)
    return pl.pallas_call(
        kern,
        grid=(bsz, n_heads),
        in_specs=[pl.BlockSpec((1, s, dh), lambda b, h: (b, 0, h)),
                  pl.BlockSpec((1, s, dh), lambda b, h: (b, 0, n_heads + h)),
                  pl.BlockSpec((1, s, dh), lambda b, h: (b, 0, 2 * n_heads + h))],
        out_specs=pl.BlockSpec((1, s, dh), lambda b, h: (b, 0, h)),
        out_shape=jax.ShapeDtypeStruct((bsz, s, d), BF16),
        scratch_shapes=[pltpu.VMEM((dh, s), BF16),
                        pltpu.VMEM((nb, blk, blk), F32),
                        pltpu.VMEM((s, blk), BF16)],
        compiler_params=_params(("parallel", "parallel")),
        name="moba",
    )(qkv, qkv, qkv)


def _pad_cols(w, n):
    return jnp.pad(w, ((0, 0), (0, n - w.shape[1])))


def _pad_rows(w, n, before=0):
    return jnp.pad(w, ((before, n - before - w.shape[0]), (0, 0)))


def kernel(x, c, w_ada, b_ada, g_pre_mix, g_post_mix, g_pre_ffn, g_post_ffn, w_ffn_in, w_ffn_out,
           w_in_ab, w_out_ab, a_v_gain, a_v_bias, a_w_s, a_b_s, b_mu, b_w0, b_w2, b_a0, b_a2, b_g2,
           b_k_k, b_k_a, b_r_k, b_lnx_gain, b_lnx_bias, w_qkv, w_o):
    bsz, s, d = x.shape
    depth = w_ada.shape[0]
    a_width = a_v_gain.shape[1]
    b_width = b_w0.shape[1]
    n_lw = b_w2.shape[1]
    n_la = b_a2.shape[1]
    n_lg = b_g2.shape[1]
    n_heads = d // ATT_HEAD
    assert s % MOBA_BLOCK == 0 and s % 1024 == 0
    assert n_lw + n_la <= LANE and n_lg <= 2 * LANE

    mod = _ada_mod(c, w_ada, b_ada)
    w_ffn_in_h = w_ffn_in.astype(BF16)
    w_ffn_out_h = w_ffn_out.astype(BF16)

    for layer in range(depth):
        mod3 = mod[layer].reshape(bsz, 1, 6 * d)
        i = layer // 2
        if layer % 2 == 0:
            lora_w = 4 * LANE
            nz = 2 * a_width + 3 * b_width + lora_w
            w_in = _pad_cols(w_in_ab[i], nz).astype(BF16)
            z = _norm_mm(x, g_pre_mix[layer], mod3, 1, 0, w_in, BF16, name="in_proj_ab")
            y_a = _mixer_a(z, a_v_gain[i], a_v_bias[i], a_w_s[i], a_b_s[i], a_width)

            mu = b_mu[i]
            pvec = jnp.stack([mu[0:b_width], mu[b_width:2 * b_width], mu[2 * b_width:3 * b_width],
                              b_w0[i], b_a0[i], b_k_k[i], b_k_a[i], b_r_k[i].reshape(-1),
                              b_lnx_gain[i], b_lnx_bias[i]])
            pvec = _pad_rows(pvec, _PV_ROWS)
            mu_l = _pad_cols(mu[3 * b_width:].reshape(1, -1), lora_w)
            w2p = _pad_rows(b_w2[i], LANE)
            a2p = _pad_rows(b_a2[i], LANE, before=n_lw)
            g2p = _pad_rows(b_g2[i], 2 * LANE)
            cb = 2 * a_width // LANE
            nb_w = b_width // LANE
            y_b = _rwkv(z, cb, cb + nb_w, cb + 2 * nb_w, (2 * a_width + 3 * b_width) // lora_w,
                        lora_w, pvec, mu_l, w2p, a2p, g2p, b_width)
            x = _out_proj(y_a, y_b, 0, 0, w_out_ab[i].astype(BF16), x, mod3, g_post_mix[layer])
        else:
            qkv = _norm_mm(x, g_pre_mix[layer], mod3, 1, 0, w_qkv[i].astype(BF16), BF16,
                           name="qkv_proj")
            o = _moba(qkv, n_heads)
            x = _out_proj(o, o, 0, 1, w_o[i].astype(BF16), x, mod3, g_post_mix[layer])
        x = _ffn(x, g_pre_ffn[layer], g_post_ffn[layer], mod3, w_ffn_in_h, w_ffn_out_h, layer)
    return x
```

```python
import functools

import jax
import jax.numpy as jnp
from jax import lax
from jax.experimental import pallas as pl
from jax.experimental.pallas import tpu as pltpu

F32 = jnp.float32
BF16 = jnp.bfloat16
HI = lax.Precision.HIGHEST

NORM_EPS = 1e-6
LN_EPS = 1e-5
GN_EPS = 64e-5

LANE = 128
A_GROUPS = 8
A_CHUNK = 128
RW_HEAD = 64
RW_CHUNK = 64
RW_PACK = 2
MOBA_BLOCK = 256
MOBA_TOPK = 3
ATT_HEAD = 128

NT_DIMS = (((1,), (1,)), ((), ()))
TN_DIMS = (((0,), (0,)), ((), ()))

VMEM_LIMIT = 56 * 1024 * 1024


def _params(sem):
    return pltpu.CompilerParams(dimension_semantics=sem, vmem_limit_bytes=VMEM_LIMIT)


def _rms(x, gain):
    ms = jnp.mean(x * x, axis=-1, keepdims=True)
    return x * lax.rsqrt(ms + NORM_EPS) * gain


ROW_CHUNK = 16
ROW_UNROLL = 8


def _for_row_chunks(n_rows, fn):
    def body(i, carry):
        fn(pl.ds(pl.multiple_of(i * ROW_CHUNK, ROW_CHUNK), ROW_CHUNK))
        return carry
    lax.fori_loop(0, n_rows // ROW_CHUNK, body, 0, unroll=ROW_UNROLL)


def _modulated_norm(x_ref, g_ref, sc_ref, sh_ref, gm_scr, h_scr):
    gm_scr[...] = g_ref[...] * (1.0 + sc_ref[0])

    def rows(r):
        x = x_ref[0, r, :]
        ms = jnp.mean(x * x, axis=-1, keepdims=True)
        h_scr[r, :] = (x * lax.rsqrt(ms + NORM_EPS) * gm_scr[...] + sh_ref[0]).astype(BF16)

    _for_row_chunks(h_scr.shape[0], rows)


def _post_norm_residual(y_scr, x_ref, gt_ref, gpost_ref, gm_scr, o_ref):
    gm_scr[...] = gt_ref[0] * gpost_ref[...]

    def rows(r):
        y = y_scr[r, :]
        ms = jnp.mean(y * y, axis=-1, keepdims=True)
        o_ref[0, r, :] = x_ref[0, r, :] + y * lax.rsqrt(ms + NORM_EPS) * gm_scr[...]

    _for_row_chunks(o_ref.shape[1], rows)


def _ada_kernel(c_ref, w_ref, b_ref, o_ref):
    c = c_ref[...]
    cond = (c * jax.nn.sigmoid(c)).astype(BF16)
    o_ref[0] = jnp.dot(cond, w_ref[0].astype(BF16), preferred_element_type=F32) + b_ref[0]


def _ada_mod(c, w_ada, b_ada):
    depth, d, n = w_ada.shape
    bsz = c.shape[0]
    bp = 8
    c_p = jnp.pad(c, ((0, bp - bsz), (0, 0)))
    tn = 1024
    out = pl.pallas_call(
        _ada_kernel,
        grid=(depth, n // tn),
        in_specs=[pl.BlockSpec((bp, d), lambda l, j: (0, 0)),
                  pl.BlockSpec((1, d, tn), lambda l, j: (l, 0, j)),
                  pl.BlockSpec((1, 1, tn), lambda l, j: (l, 0, j))],
        out_specs=pl.BlockSpec((1, bp, tn), lambda l, j: (l, 0, j)),
        out_shape=jax.ShapeDtypeStruct((depth, bp, n), F32),
        compiler_params=_params(("parallel", "parallel")),
        name="ada_mod",
    )(c_p, w_ada, b_ada.reshape(depth, 1, n))
    return out[:, :bsz]


def _norm_mm_kernel(x_ref, g_ref, sc_ref, sh_ref, w_ref, o_ref, h_scr, gm_scr):
    @pl.when(pl.program_id(2) == 0)
    def _():
        _modulated_norm(x_ref, g_ref, sc_ref, sh_ref, gm_scr, h_scr)

    o_ref[0] = jnp.dot(h_scr[...], w_ref[...], preferred_element_type=F32).astype(o_ref.dtype)


def _norm_mm(x, gain, mod3, sc_idx, sh_idx, w, out_dtype, tm=1024, tn=512, name="norm_mm"):
    bsz, s, d = x.shape
    n = w.shape[1]
    return pl.pallas_call(
        _norm_mm_kernel,
        grid=(bsz, s // tm, n // tn),
        in_specs=[pl.BlockSpec((1, tm, d), lambda b, m, j: (b, m, 0)),
                  pl.BlockSpec((1, d), lambda b, m, j: (0, 0)),
                  pl.BlockSpec((1, 1, d), lambda b, m, j: (b, 0, sc_idx)),
                  pl.BlockSpec((1, 1, d), lambda b, m, j: (b, 0, sh_idx)),
                  pl.BlockSpec((d, tn), lambda b, m, j: (0, j))],
        out_specs=pl.BlockSpec((1, tm, tn), lambda b, m, j: (b, m, j)),
        out_shape=jax.ShapeDtypeStruct((bsz, s, n), out_dtype),
        scratch_shapes=[pltpu.VMEM((tm, d), BF16), pltpu.VMEM((1, d), F32)],
        compiler_params=_params(("parallel", "parallel", "arbitrary")),
        name=name,
    )(x, gain.reshape(1, d), mod3, mod3, w)


def _ffn_kernel(x_ref, gpre_ref, sc_ref, sh_ref, gt_ref, gpost_ref, wg_ref, wu_ref, wo_ref,
                o_ref, h_scr, gm_scr, acc_scr):
    f = pl.program_id(2)

    @pl.when(f == 0)
    def _():
        _modulated_norm(x_ref, gpre_ref, sc_ref, sh_ref, gm_scr, h_scr)
        acc_scr[...] = jnp.zeros_like(acc_scr)

    h = h_scr[...]
    g = jnp.dot(h, wg_ref[...], preferred_element_type=F32)
    u = jnp.dot(h, wu_ref[...], preferred_element_type=F32)
    a = (g * jax.nn.sigmoid(g) * u).astype(BF16)
    acc_scr[...] += jnp.dot(a, wo_ref[...], preferred_element_type=F32)

    @pl.when(f == pl.num_programs(2) - 1)
    def _():
        _post_norm_residual(acc_scr, x_ref, gt_ref, gpost_ref, gm_scr, o_ref)


def _ffn(x, gpre, gpost, mod3, w_in, w_out, layer, tm=1024, tf=512):
    bsz, s, d = x.shape
    fh = w_out.shape[1]
    nf = fh // tf
    return pl.pallas_call(
        _ffn_kernel,
        grid=(bsz, s // tm, nf),
        in_specs=[pl.BlockSpec((1, tm, d), lambda b, m, f: (b, m, 0), pipeline_mode=pl.Buffered(1)),
                  pl.BlockSpec((1, d), lambda b, m, f: (0, 0)),
                  pl.BlockSpec((1, 1, d), lambda b, m, f: (b, 0, 4)),
                  pl.BlockSpec((1, 1, d), lambda b, m, f: (b, 0, 3)),
                  pl.BlockSpec((1, 1, d), lambda b, m, f: (b, 0, 5)),
                  pl.BlockSpec((1, d), lambda b, m, f: (0, 0)),
                  pl.BlockSpec((None, d, tf), lambda b, m, f: (layer, 0, f)),
                  pl.BlockSpec((None, d, tf), lambda b, m, f: (layer, 0, nf + f)),
                  pl.BlockSpec((None, tf, d), lambda b, m, f: (layer, f, 0))],
        out_specs=pl.BlockSpec((1, tm, d), lambda b, m, f: (b, m, 0), pipeline_mode=pl.Buffered(1)),
        out_shape=jax.ShapeDtypeStruct((bsz, s, d), F32),
        scratch_shapes=[pltpu.VMEM((tm, d), BF16), pltpu.VMEM((1, d), F32), pltpu.VMEM((tm, d), F32)],
        compiler_params=_params(("parallel", "parallel", "arbitrary")),
        name="ffn",
    )(x, gpre.reshape(1, d), mod3, mod3, mod3, gpost.reshape(1, d), w_in, w_in, w_out)


def _out_proj_kernel(a0_ref, a1_ref, w0_ref, w1_ref, x_ref, gt_ref, gpost_ref, o_ref, gm_scr, y_scr):
    y_scr[...] = (jnp.dot(a0_ref[0], w0_ref[...], preferred_element_type=F32)
                  + jnp.dot(a1_ref[0], w1_ref[...], preferred_element_type=F32))
    _post_norm_residual(y_scr, x_ref, gt_ref, gpost_ref, gm_scr, o_ref)


def _out_proj(a0, a1, col0, col1, w, x, mod3, gpost, tm=512):
    bsz, s, d = x.shape
    kh = w.shape[0] // 2
    return pl.pallas_call(
        _out_proj_kernel,
        grid=(bsz, s // tm),
        in_specs=[pl.BlockSpec((1, tm, kh), lambda b, m: (b, m, col0)),
                  pl.BlockSpec((1, tm, kh), lambda b, m: (b, m, col1)),
                  pl.BlockSpec((kh, d), lambda b, m: (0, 0)),
                  pl.BlockSpec((kh, d), lambda b, m: (1, 0)),
                  pl.BlockSpec((1, tm, d), lambda b, m: (b, m, 0)),
                  pl.BlockSpec((1, 1, d), lambda b, m: (b, 0, 2)),
                  pl.BlockSpec((1, d), lambda b, m: (0, 0))],
        out_specs=pl.BlockSpec((1, tm, d), lambda b, m: (b, m, 0)),
        out_shape=jax.ShapeDtypeStruct((bsz, s, d), F32),
        scratch_shapes=[pltpu.VMEM((1, d), F32), pltpu.VMEM((tm, d), F32)],
        compiler_params=_params(("parallel", "parallel")),
        name="out_proj",
    )(a0, a1, w, w, x, mod3, gpost.reshape(1, d))


def _mixer_a_kernel(z_ref, vg_ref, vb_ref, ws_ref, bst_ref, o_ref):
    z = jax.nn.gelu(z_ref[0].astype(F32))
    wdt = z.shape[1] // 2
    u = z[:, :wdt]
    v = z[:, wdt:]
    mu = jnp.mean(v, axis=-1, keepdims=True)
    dv = v - mu
    var = jnp.mean(dv * dv, axis=-1, keepdims=True)
    vn = (dv * lax.rsqrt(var + LN_EPS) * vg_ref[...] + vb_ref[...]).astype(BF16)
    ch = z.shape[0]
    causal = (lax.broadcasted_iota(jnp.int32, (ch, ch), 0)
              >= lax.broadcasted_iota(jnp.int32, (ch, ch), 1))
    gd = wdt // A_GROUPS
    for g in range(A_GROUPS):
        w = jnp.where(causal, ws_ref[g], 0.0).astype(BF16)
        sv = jnp.dot(w, vn[:, g * gd:(g + 1) * gd], preferred_element_type=F32)
        sv = sv + bst_ref[:, g:g + 1]
        o_ref[0, :, g * gd:(g + 1) * gd] = (u[:, g * gd:(g + 1) * gd] * sv).astype(o_ref.dtype)


def _mixer_a(z, v_gain, v_bias, w_s, b_s, width):
    bsz, s, _ = z.shape
    ch = A_CHUNK
    return pl.pallas_call(
        _mixer_a_kernel,
        grid=(bsz, s // ch),
        in_specs=[pl.BlockSpec((1, ch, 2 * width), lambda b, c: (b, c, 0)),
                  pl.BlockSpec((1, width), lambda b, c: (0, 0)),
                  pl.BlockSpec((1, width), lambda b, c: (0, 0)),
                  pl.BlockSpec((A_GROUPS, ch, ch), lambda b, c: (0, 0, 0)),
                  pl.BlockSpec((ch, A_GROUPS), lambda b, c: (0, 0))],
        out_specs=pl.BlockSpec((1, ch, width), lambda b, c: (b, c, 0)),
        out_shape=jax.ShapeDtypeStruct((bsz, s, width), BF16),
        compiler_params=_params(("parallel", "parallel")),
        name="mixer_a",
    )(z, v_gain.reshape(1, width), v_bias.reshape(1, width), w_s, b_s.T)


_PV_MU_R, _PV_MU_K, _PV_MU_V, _PV_W0, _PV_A0, _PV_KK, _PV_KA, _PV_RK, _PV_LG, _PV_LB = range(10)
_PV_ROWS = 16


def _shift_lerp(x, prev_row, mu):
    rolled = pltpu.roll(x, 1, axis=0)
    first = lax.broadcasted_iota(jnp.int32, x.shape, 0) == 0
    xp = jnp.where(first, prev_row, rolled)
    return x + mu * (xp - x)


def _split_bf16(x):
    hi = x.astype(BF16)
    lo = (x - hi.astype(F32)).astype(BF16)
    return hi, lo


def _mm(x, y):
    return jnp.dot(x.astype(BF16), y.astype(BF16), preferred_element_type=F32)


def _mm_nt(x, y):
    return lax.dot_general(x.astype(BF16), y.astype(BF16), NT_DIMS, preferred_element_type=F32)


def _mm_tn(x, y):
    return lax.dot_general(x.astype(BF16), y.astype(BF16), TN_DIMS, preferred_element_type=F32)


def _mm_x3(x, y):
    xh, xl = _split_bf16(x)
    yh, yl = _split_bf16(y)
    return (jnp.dot(xh, yh, preferred_element_type=F32) + jnp.dot(xl, yh, preferred_element_type=F32)
            + jnp.dot(xh, yl, preferred_element_type=F32))


def _mm_exact_rhs(x, e_bf16):
    xh, xl = _split_bf16(x)
    return (jnp.dot(xh, e_bf16, preferred_element_type=F32)
            + jnp.dot(xl, e_bf16, preferred_element_type=F32))


def _rwkv_kernel(zr_ref, zk_ref, zv_ref, zl_ref, pv_ref, mul_ref, w2_ref, a2_ref, g2_ref, o_ref,
                 s_scr, prev_scr, prevl_scr, q_scr, y_scr, gm_scr, cm_scr, pl_scr, *, chunk):
    tb = pl.program_id(2)
    t_rows = zr_ref.shape[1]
    lanes = zr_ref.shape[2]
    L = chunk
    SL = RW_PACK * L
    n_chunks = t_rows // L

    @pl.when(tb == 0)
    def _():
        s_scr[...] = jnp.zeros_like(s_scr)
        prev_scr[...] = jnp.zeros_like(prev_scr)
        prevl_scr[...] = jnp.zeros_like(prevl_scr)

    def pv(i):
        return pv_ref[i:i + 1, :]

    zr = zr_ref[0].astype(F32)
    zk = zk_ref[0].astype(F32)
    zv = zv_ref[0].astype(F32)
    zl = zl_ref[0].astype(F32)
    r = _shift_lerp(zr, prev_scr[0:1, :], pv(_PV_MU_R))
    k = _shift_lerp(zk, prev_scr[1:2, :], pv(_PV_MU_K))
    v = _shift_lerp(zv, prev_scr[2:3, :], pv(_PV_MU_V))
    zls = _shift_lerp(zl, prevl_scr[0:1, :], mul_ref[...])
    prev_scr[0:1, :] = zr[t_rows - 1:t_rows, :]
    prev_scr[1:2, :] = zk[t_rows - 1:t_rows, :]
    prev_scr[2:3, :] = zv[t_rows - 1:t_rows, :]
    prevl_scr[0:1, :] = zl[t_rows - 1:t_rows, :]

    x_wa = zls[:, :LANE]
    x_g = zls[:, LANE:3 * LANE]
    w_pre = pv(_PV_W0) + _mm_x3(jnp.tanh(x_wa), w2_ref[...])
    t = -w_pre
    softplus = jnp.maximum(t, 0.0) + jnp.log1p(jnp.exp(-jnp.abs(t)))
    log_decay = -jnp.exp(-softplus - 0.5)
    a = jax.nn.sigmoid(pv(_PV_A0) + _mm_x3(x_wa, a2_ref[...]))
    g = _mm(jax.nn.sigmoid(x_g), g2_ref[...])

    li = lax.broadcasted_iota(jnp.int32, (lanes, lanes), 0) // RW_HEAD
    lj = lax.broadcasted_iota(jnp.int32, (lanes, lanes), 1) // RW_HEAD
    same_head = li == lj
    e_head = jnp.where(same_head, 1.0, 0.0).astype(BF16)

    kk = k * pv(_PV_KK)
    kk = kk / jnp.maximum(jnp.sqrt(_mm_exact_rhs(kk * kk, e_head)), 1e-12)
    k = k * (1.0 + (a - 1.0) * pv(_PV_KA))
    av = -kk
    bv = kk * a

    ri = lax.broadcasted_iota(jnp.int32, (t_rows, t_rows), 0)
    ci = lax.broadcasted_iota(jnp.int32, (t_rows, t_rows), 1)
    tri_bd = jnp.where((ri >= ci) & ((ri // L) == (ci // L)), 1.0, 0.0).astype(BF16)
    lw_hi, lw_lo = _split_bf16(log_decay)
    cum = (jnp.dot(tri_bd, lw_hi, preferred_element_type=F32)
           + jnp.dot(tri_bd, lw_lo, preferred_element_type=F32))

    si = lax.broadcasted_iota(jnp.int32, (SL, SL), 0)
    sj = lax.broadcasted_iota(jnp.int32, (SL, SL), 1)
    same_blk = (si // L) == (sj // L)
    m_strict = same_blk & (si > sj)
    m_incl = same_blk & (si >= sj)
    eye = jnp.where(si == sj, 1.0, 0.0)
    lane_head = lax.broadcasted_iota(jnp.int32, (1, lanes), 1) // RW_HEAD
    n_sq = max(L.bit_length() - 2, 0)

    def stack(x):
        return jnp.concatenate([jnp.where(lane_head == h, x, 0.0) for h in range(RW_PACK)], axis=0)

    def unstack(x):
        out = x[0:L]
        for h in range(1, RW_PACK):
            out = out + x[h * L:(h + 1) * L]
        return out

    cs = range(n_chunks)
    rows = [slice(c * L, (c + 1) * L) for c in cs]
    cm = [cum[rw] for rw in rows]
    cm_last = [x[L - 1:L, :] for x in cm]
    r_t = [r[rw] * jnp.exp(cm[c]) for c, rw in enumerate(rows)]
    a_st = [stack(av[rw] * jnp.exp(cm[c] - log_decay[rw])).astype(BF16) for c, rw in enumerate(rows)]
    e_neg = [jnp.exp(-x) for x in cm]
    k_rep = [jnp.concatenate([(k[rw] * e_neg[c]).astype(BF16)] * RW_PACK, axis=0)
             for c, rw in enumerate(rows)]
    b_rep = [jnp.concatenate([(bv[rw] * e_neg[c]).astype(BF16)] * RW_PACK, axis=0)
             for c, rw in enumerate(rows)]
    ar_st = [jnp.concatenate([a_st[c], stack(r_t[c]).astype(BF16)], axis=0) for c in cs]
    p_k = [_mm_nt(ar_st[c], k_rep[c]) for c in cs]
    p_b = [_mm_nt(ar_st[c], b_rep[c]) for c in cs]
    a_ak = [jnp.where(m_strict, x[:SL], 0.0).astype(BF16) for x in p_k]
    a_rk = [jnp.where(m_incl, x[SL:], 0.0).astype(BF16) for x in p_k]
    a_ab = [jnp.where(m_strict, x[:SL], 0.0) for x in p_b]
    a_rb = [jnp.where(m_incl, x[SL:], 0.0).astype(BF16) for x in p_b]

    xp = [x.astype(BF16) for x in a_ab]
    tinv = [eye + x for x in a_ab]
    for _ in range(n_sq):
        xp = [_mm(x, x).astype(BF16) for x in xp]
        tinv = [tinv[c] + _mm(tinv[c], xp[c]) for c in cs]
    tinv = [x.astype(BF16) for x in tinv]

    v_st = [stack(v[rw]).astype(BF16) for rw in rows]
    x0 = [_mm(a_ak[c], v_st[c]).astype(BF16) for c in cs]
    wu = [_mm(tinv[c], jnp.concatenate([a_st[c], x0[c]], axis=1)) for c in cs]
    yq = [_mm(a_rb[c], wu[c]) for c in cs]
    y0 = [_mm(a_rk[c], v_st[c]) for c in cs]
    for c, rw in enumerate(rows):
        q_scr[c] = (r_t[c] + unstack(yq[c][:, :lanes])).astype(BF16)
        y_scr[rw, :] = unstack(yq[c][:, lanes:] + y0[c])
    for c, rw in enumerate(rows):
        e_rem = jnp.exp(cm_last[c] - cm[c])
        b_b = bv[rw] * e_rem
        kb = jnp.concatenate([k[rw] * e_rem, b_b], axis=0)
        vu = jnp.concatenate([v[rw], unstack(wu[c][:, lanes:])], axis=0)
        cm_scr[c] = jnp.where(same_head, _mm_tn(vu, kb), 0.0)
        gm_scr[c] = jnp.where(same_head, _mm_tn(unstack(wu[c][:, :lanes]), b_b), 0.0).astype(BF16)
        pl_scr[c] = jnp.broadcast_to(jnp.exp(cm_last[c]), (8, lanes))

    s0 = s_scr[...]
    for c in range(n_chunks):
        lo, hi = c * L, (c + 1) * L
        s0b = s0.astype(BF16)
        y_scr[lo:hi, :] = y_scr[lo:hi, :] + _mm_nt(q_scr[c], s0b)
        s0 = s0 * pl_scr[c][0:1, :] + _mm(s0b, gm_scr[c]) + cm_scr[c]
    s_scr[...] = s0

    y = y_scr[...]
    inv_n = 1.0 / RW_HEAD
    mean = _mm_exact_rhs(y, e_head) * inv_n
    dy = y - mean
    var = _mm_exact_rhs(dy * dy, e_head) * inv_n
    yn = dy * lax.rsqrt(var + GN_EPS) * pv(_PV_LG) + pv(_PV_LB)
    bonus = _mm_exact_rhs(r * k * pv(_PV_RK), e_head) * v
    o_ref[0] = ((yn + bonus) * g).astype(o_ref.dtype)


def _rwkv(z, col_r, col_k, col_v, col_l, lora_w, pvec, mu_l, w2p, a2p, g2p, width, t_rows=512):
    bsz, s, _ = z.shape
    n_blk = width // LANE
    n_chunks = t_rows // RW_CHUNK
    kern = functools.partial(_rwkv_kernel, chunk=RW_CHUNK)
    return pl.pallas_call(
        kern,
        grid=(bsz, n_blk, s // t_rows),
        in_specs=[pl.BlockSpec((1, t_rows, LANE), lambda b, p, t: (b, t, col_r + p)),
                  pl.BlockSpec((1, t_rows, LANE), lambda b, p, t: (b, t, col_k + p)),
                  pl.BlockSpec((1, t_rows, LANE), lambda b, p, t: (b, t, col_v + p)),
                  pl.BlockSpec((1, t_rows, lora_w), lambda b, p, t: (b, t, col_l)),
                  pl.BlockSpec((_PV_ROWS, LANE), lambda b, p, t: (0, p)),
                  pl.BlockSpec((1, lora_w), lambda b, p, t: (0, 0)),
                  pl.BlockSpec((LANE, LANE), lambda b, p, t: (0, p)),
                  pl.BlockSpec((LANE, LANE), lambda b, p, t: (0, p)),
                  pl.BlockSpec((2 * LANE, LANE), lambda b, p, t: (0, p))],
        out_specs=pl.BlockSpec((1, t_rows, LANE), lambda b, p, t: (b, t, p)),
        out_shape=jax.ShapeDtypeStruct((bsz, s, width), BF16),
        scratch_shapes=[pltpu.VMEM((LANE, LANE), F32),
                        pltpu.VMEM((8, LANE), F32),
                        pltpu.VMEM((8, lora_w), F32),
                        pltpu.VMEM((n_chunks, RW_CHUNK, LANE), BF16),
                        pltpu.VMEM((t_rows, LANE), F32),
                        pltpu.VMEM((n_chunks, LANE, LANE), BF16),
                        pltpu.VMEM((n_chunks, LANE, LANE), F32),
                        pltpu.VMEM((n_chunks, 8, LANE), F32)],
        compiler_params=_params(("parallel", "parallel", "arbitrary")),
        name="rwkv7",
    )(z, z, z, z, pvec, mu_l, w2p, a2p, g2p)


def _moba_kernel(q_ref, k_ref, v_ref, o_ref, vt_scr, s_scr, p_scr, *, n_heads):
    h = pl.program_id(1)
    s_len = q_ref.shape[1]
    dh = q_ref.shape[2]
    blk = MOBA_BLOCK
    nb = s_len // blk
    scale = dh ** -0.5
    neg_inf = -jnp.inf

    kmean = []
    for j in range(nb):
        rows = slice(j * blk, (j + 1) * blk)
        kmean.append(jnp.mean(k_ref[0, rows, :].astype(F32), axis=0, keepdims=True))
        vt_scr[:, rows] = v_ref[0, rows, :].astype(F32).T.astype(BF16)
    kmean = jnp.concatenate(kmean, axis=0)

    slope = jnp.exp(jnp.full((1, blk), -8.0 / n_heads * 0.6931471805599453, F32)
                    * (h + 1).astype(F32))
    jk = lax.broadcasted_iota(jnp.int32, (blk, blk), 0)
    iq = lax.broadcasted_iota(jnp.int32, (blk, blk), 1)
    bias_tile = slope * jk.astype(F32)
    bias_own = jnp.where(iq >= jk, bias_tile, neg_inf)
    blk_id = lax.broadcasted_iota(jnp.int32, (nb, 1), 0)

    for qb in range(nb):
        q = q_ref[0, qb * blk:(qb + 1) * blk, :]
        q_s = (q.astype(F32) * scale).astype(BF16)
        gate = lax.dot_general(kmean, q.astype(F32), NT_DIMS, precision=HI,
                               preferred_element_type=F32)
        past = blk_id < qb
        offs = []
        cols = []
        for n in range(qb + 1):
            k_n = k_ref[0, n * blk:(n + 1) * blk, :]
            s = lax.dot_general(k_n, q_s, NT_DIMS, preferred_element_type=F32)
            if n == qb:
                t = s + bias_own
                off = jnp.zeros((1, blk), F32)
            else:
                t = s + bias_tile
                g_n = gate[n:n + 1, :]
                beats = past & ((gate > g_n) | ((gate == g_n) & (blk_id < n)))
                rank = jnp.sum(jnp.where(beats, 1.0, 0.0), axis=0, keepdims=True)
                off = jnp.where(rank < float(MOBA_TOPK), slope * float((n - qb) * blk), neg_inf)
            s_scr[n] = t
            offs.append(off)
            cols.append(jnp.max(t, axis=0, keepdims=True) + off)
        m = cols[0]
        for cmax in cols[1:]:
            m = jnp.maximum(m, cmax)
        l = jnp.zeros((1, blk), F32)
        for n in range(qb + 1):
            p = jnp.exp(s_scr[n] - (m - offs[n]))
            l = l + jnp.sum(p, axis=0, keepdims=True)
            p_scr[n * blk:(n + 1) * blk, :] = p.astype(BF16)
        kk = (qb + 1) * blk
        acc = jnp.dot(vt_scr[:, :kk], p_scr[:kk, :], preferred_element_type=F32)
        o_ref[0, qb * blk:(qb + 1) * blk, :] = (acc / l).T.astype(o_ref.dtype)


def _moba(qkv, n_heads):
    bsz, s, d3 = qkv.shape
    d = d3 // 3
    dh = d // n_heads
    blk = MOBA_BLOCK
    nb = s // blk
    kern = functools.partial(_moba_kernel, n_heads=n_heads)
    return pl.pallas_call(
        kern,
        grid=(bsz, n_heads),
        in_specs=[pl.BlockSpec((1, s, dh), lambda b, h: (b, 0, h)),
                  pl.BlockSpec((1, s, dh), lambda b, h: (b, 0, n_heads + h)),
                  pl.BlockSpec((1, s, dh), lambda b, h: (b, 0, 2 * n_heads + h))],
        out_specs=pl.BlockSpec((1, s, dh), lambda b, h: (b, 0, h)),
        out_shape=jax.ShapeDtypeStruct((bsz, s, d), BF16),
        scratch_shapes=[pltpu.VMEM((dh, s), BF16),
                        pltpu.VMEM((nb, blk, blk), F32),
                        pltpu.VMEM((s, blk), BF16)],
        compiler_params=_params(("parallel", "parallel")),
        name="moba",
    )(qkv, qkv, qkv)


def _pad_cols(w, n):
    return jnp.pad(w, ((0, 0), (0, n - w.shape[1])))


def _pad_rows(w, n, before=0):
    return jnp.pad(w, ((before, n - before - w.shape[0]), (0, 0)))


def kernel(x, c, w_ada, b_ada, g_pre_mix, g_post_mix, g_pre_ffn, g_post_ffn, w_ffn_in, w_ffn_out,
           w_in_ab, w_out_ab, a_v_gain, a_v_bias, a_w_s, a_b_s, b_mu, b_w0, b_w2, b_a0, b_a2, b_g2,
           b_k_k, b_k_a, b_r_k, b_lnx_gain, b_lnx_bias, w_qkv, w_o):
    bsz, s, d = x.shape
    depth = w_ada.shape[0]
    a_width = a_v_gain.shape[1]
    b_width = b_w0.shape[1]
    n_lw = b_w2.shape[1]
    n_la = b_a2.shape[1]
    n_lg = b_g2.shape[1]
    n_heads = d // ATT_HEAD
    assert s % MOBA_BLOCK == 0 and s % 1024 == 0
    assert n_lw + n_la <= LANE and n_lg <= 2 * LANE

    mod = _ada_mod(c, w_ada, b_ada)
    w_ffn_in_h = w_ffn_in.astype(BF16)
    w_ffn_out_h = w_ffn_out.astype(BF16)

    for layer in range(depth):
        mod3 = mod[layer].reshape(bsz, 1, 6 * d)
        i = layer // 2
        if layer % 2 == 0:
            lora_w = 4 * LANE
            nz = 2 * a_width + 3 * b_width + lora_w
            w_in = _pad_cols(w_in_ab[i], nz).astype(BF16)
            z = _norm_mm(x, g_pre_mix[layer], mod3, 1, 0, w_in, BF16, name="in_proj_ab")
            y_a = _mixer_a(z, a_v_gain[i], a_v_bias[i], a_w_s[i], a_b_s[i], a_width)

            mu = b_mu[i]
            pvec = jnp.stack([mu[0:b_width], mu[b_width:2 * b_width], mu[2 * b_width:3 * b_width],
                              b_w0[i], b_a0[i], b_k_k[i], b_k_a[i], b_r_k[i].reshape(-1),
                              b_lnx_gain[i], b_lnx_bias[i]])
            pvec = _pad_rows(pvec, _PV_ROWS)
            mu_l = _pad_cols(mu[3 * b_width:].reshape(1, -1), lora_w)
            w2p = _pad_rows(b_w2[i], LANE)
            a2p = _pad_rows(b_a2[i], LANE, before=n_lw)
            g2p = _pad_rows(b_g2[i], 2 * LANE)
            cb = 2 * a_width // LANE
            nb_w = b_width // LANE
            y_b = _rwkv(z, cb, cb + nb_w, cb + 2 * nb_w, (2 * a_width + 3 * b_width) // lora_w,
                        lora_w, pvec, mu_l, w2p, a2p, g2p, b_width)
            x = _out_proj(y_a, y_b, 0, 0, w_out_ab[i].astype(BF16), x, mod3, g_post_mix[layer])
        else:
            qkv = _norm_mm(x, g_pre_mix[layer], mod3, 1, 0, w_qkv[i].astype(BF16), BF16,
                           name="qkv_proj")
            o = _moba(qkv, n_heads)
            x = _out_proj(o, o, 0, 1, w_o[i].astype(BF16), x, mod3, g_post_mix[layer])
        x = _ffn(x, g_pre_ffn[layer], g_post_ffn[layer], mod3, w_ffn_in_h, w_ffn_out_h, layer)
    return x
```

```python
import functools

import jax
import jax.numpy as jnp
from jax import lax
from jax.experimental import pallas as pl
from jax.experimental.pallas import tpu as pltpu

F32 = jnp.float32
BF16 = jnp.bfloat16
HI = lax.Precision.HIGHEST

NORM_EPS = 1e-6
LN_EPS = 1e-5
GN_EPS = 64e-5

LANE = 128
A_GROUPS = 8
A_CHUNK = 128
RW_HEAD = 64
RW_CHUNK = 64
RW_PACK = 2
MOBA_BLOCK = 256
MOBA_TOPK = 3
ATT_HEAD = 128

NT_DIMS = (((1,), (1,)), ((), ()))
TN_DIMS = (((0,), (0,)), ((), ()))

VMEM_LIMIT = 56 * 1024 * 1024


def _params(sem):
    return pltpu.CompilerParams(dimension_semantics=sem, vmem_limit_bytes=VMEM_LIMIT)


def _rms(x, gain):
    ms = jnp.mean(x * x, axis=-1, keepdims=True)
    return x * lax.rsqrt(ms + NORM_EPS) * gain


ROW_CHUNK = 16
ROW_UNROLL = 8


def _for_row_chunks(n_rows, fn):
    def body(i, carry):
        fn(pl.ds(pl.multiple_of(i * ROW_CHUNK, ROW_CHUNK), ROW_CHUNK))
        return carry
    lax.fori_loop(0, n_rows // ROW_CHUNK, body, 0, unroll=ROW_UNROLL)


def _modulated_norm(x_ref, g_ref, sc_ref, sh_ref, gm_scr, h_scr):
    gm_scr[...] = g_ref[...] * (1.0 + sc_ref[0])

    def rows(r):
        x = x_ref[0, r, :]
        ms = jnp.mean(x * x, axis=-1, keepdims=True)
        h_scr[r, :] = (x * lax.rsqrt(ms + NORM_EPS) * gm_scr[...] + sh_ref[0]).astype(BF16)

    _for_row_chunks(h_scr.shape[0], rows)


def _post_norm_residual(y_scr, x_ref, gt_ref, gpost_ref, gm_scr, o_ref):
    gm_scr[...] = gt_ref[0] * gpost_ref[...]

    def rows(r):
        y = y_scr[r, :]
        ms = jnp.mean(y * y, axis=-1, keepdims=True)
        o_ref[0, r, :] = x_ref[0, r, :] + y * lax.rsqrt(ms + NORM_EPS) * gm_scr[...]

    _for_row_chunks(o_ref.shape[1], rows)


def _ada_kernel(c_ref, w_ref, b_ref, o_ref):
    c = c_ref[...]
    cond = (c * jax.nn.sigmoid(c)).astype(BF16)
    o_ref[0] = jnp.dot(cond, w_ref[0].astype(BF16), preferred_element_type=F32) + b_ref[0]


def _ada_mod(c, w_ada, b_ada):
    depth, d, n = w_ada.shape
    bsz = c.shape[0]
    bp = 8
    c_p = jnp.pad(c, ((0, bp - bsz), (0, 0)))
    tn = 1024
    out = pl.pallas_call(
        _ada_kernel,
        grid=(depth, n // tn),
        in_specs=[pl.BlockSpec((bp, d), lambda l, j: (0, 0)),
                  pl.BlockSpec((1, d, tn), lambda l, j: (l, 0, j)),
                  pl.BlockSpec((1, 1, tn), lambda l, j: (l, 0, j))],
        out_specs=pl.BlockSpec((1, bp, tn), lambda l, j: (l, 0, j)),
        out_shape=jax.ShapeDtypeStruct((depth, bp, n), F32),
        compiler_params=_params(("parallel", "parallel")),
        name="ada_mod",
    )(c_p, w_ada, b_ada.reshape(depth, 1, n))
    return out[:, :bsz]


def _norm_mm_kernel(x_ref, g_ref, sc_ref, sh_ref, w_ref, o_ref, h_scr, gm_scr):
    @pl.when(pl.program_id(2) == 0)
    def _():
        _modulated_norm(x_ref, g_ref, sc_ref, sh_ref, gm_scr, h_scr)

    o_ref[0] = jnp.dot(h_scr[...], w_ref[...], preferred_element_type=F32).astype(o_ref.dtype)


def _norm_mm(x, gain, mod3, sc_idx, sh_idx, w, out_dtype, tm=1024, tn=512, name="norm_mm"):
    bsz, s, d = x.shape
    n = w.shape[1]
    return pl.pallas_call(
        _norm_mm_kernel,
        grid=(bsz, s // tm, n // tn),
        in_specs=[pl.BlockSpec((1, tm, d), lambda b, m, j: (b, m, 0)),
                  pl.BlockSpec((1, d), lambda b, m, j: (0, 0)),
                  pl.BlockSpec((1, 1, d), lambda b, m, j: (b, 0, sc_idx)),
                  pl.BlockSpec((1, 1, d), lambda b, m, j: (b, 0, sh_idx)),
                  pl.BlockSpec((d, tn), lambda b, m, j: (0, j))],
        out_specs=pl.BlockSpec((1, tm, tn), lambda b, m, j: (b, m, j)),
        out_shape=jax.ShapeDtypeStruct((bsz, s, n), out_dtype),
        scratch_shapes=[pltpu.VMEM((tm, d), BF16), pltpu.VMEM((1, d), F32)],
        compiler_params=_params(("parallel", "parallel", "arbitrary")),
        name=name,
    )(x, gain.reshape(1, d), mod3, mod3, w)


def _ffn_kernel(x_ref, gpre_ref, sc_ref, sh_ref, gt_ref, gpost_ref, wg_ref, wu_ref, wo_ref,
                o_ref, h_scr, gm_scr, acc_scr):
    f = pl.program_id(2)

    @pl.when(f == 0)
    def _():
        _modulated_norm(x_ref, gpre_ref, sc_ref, sh_ref, gm_scr, h_scr)
        acc_scr[...] = jnp.zeros_like(acc_scr)

    h = h_scr[...]
    g = jnp.dot(h, wg_ref[...], preferred_element_type=F32)
    u = jnp.dot(h, wu_ref[...], preferred_element_type=F32)
    a = (g * jax.nn.sigmoid(g) * u).astype(BF16)
    acc_scr[...] += jnp.dot(a, wo_ref[...], preferred_element_type=F32)

    @pl.when(f == pl.num_programs(2) - 1)
    def _():
        _post_norm_residual(acc_scr, x_ref, gt_ref, gpost_ref, gm_scr, o_ref)


def _ffn(x, gpre, gpost, mod3, w_in, w_out, layer, tm=1024, tf=512):
    bsz, s, d = x.shape
    fh = w_out.shape[1]
    nf = fh // tf
    return pl.pallas_call(
        _ffn_kernel,
        grid=(bsz, s // tm, nf),
        in_specs=[pl.BlockSpec((1, tm, d), lambda b, m, f: (b, m, 0), pipeline_mode=pl.Buffered(1)),
                  pl.BlockSpec((1, d), lambda b, m, f: (0, 0)),
                  pl.BlockSpec((1, 1, d), lambda b, m, f: (b, 0, 4)),
                  pl.BlockSpec((1, 1, d), lambda b, m, f: (b, 0, 3)),
                  pl.BlockSpec((1, 1, d), lambda b, m, f: (b, 0, 5)),
                  pl.BlockSpec((1, d), lambda b, m, f: (0, 0)),
                  pl.BlockSpec((None, d, tf), lambda b, m, f: (layer, 0, f)),
                  pl.BlockSpec((None, d, tf), lambda b, m, f: (layer, 0, nf + f)),
                  pl.BlockSpec((None, tf, d), lambda b, m, f: (layer, f, 0))],
        out_specs=pl.BlockSpec((1, tm, d), lambda b, m, f: (b, m, 0), pipeline_mode=pl.Buffered(1)),
        out_shape=jax.ShapeDtypeStruct((bsz, s, d), F32),
        scratch_shapes=[pltpu.VMEM((tm, d), BF16), pltpu.VMEM((1, d), F32), pltpu.VMEM((tm, d), F32)],
        compiler_params=_params(("parallel", "parallel", "arbitrary")),
        name="ffn",
    )(x, gpre.reshape(1, d), mod3, mod3, mod3, gpost.reshape(1, d), w_in, w_in, w_out)


def _out_proj_kernel(a0_ref, a1_ref, w0_ref, w1_ref, x_ref, gt_ref, gpost_ref, o_ref, gm_scr, y_scr):
    y_scr[...] = (jnp.dot(a0_ref[0], w0_ref[...], preferred_element_type=F32)
                  + jnp.dot(a1_ref[0], w1_ref[...], preferred_element_type=F32))
    _post_norm_residual(y_scr, x_ref, gt_ref, gpost_ref, gm_scr, o_ref)


def _out_proj(a0, a1, col0, col1, w, x, mod3, gpost, tm=512):
    bsz, s, d = x.shape
    kh = w.shape[0] // 2
    return pl.pallas_call(
        _out_proj_kernel,
        grid=(bsz, s // tm),
        in_specs=[pl.BlockSpec((1, tm, kh), lambda b, m: (b, m, col0)),
                  pl.BlockSpec((1, tm, kh), lambda b, m: (b, m, col1)),
                  pl.BlockSpec((kh, d), lambda b, m: (0, 0)),
                  pl.BlockSpec((kh, d), lambda b, m: (1, 0)),
                  pl.BlockSpec((1, tm, d), lambda b, m: (b, m, 0)),
                  pl.BlockSpec((1, 1, d), lambda b, m: (b, 0, 2)),
                  pl.BlockSpec((1, d), lambda b, m: (0, 0))],
        out_specs=pl.BlockSpec((1, tm, d), lambda b, m: (b, m, 0)),
        out_shape=jax.ShapeDtypeStruct((bsz, s, d), F32),
        scratch_shapes=[pltpu.VMEM((1, d), F32), pltpu.VMEM((tm, d), F32)],
        compiler_params=_params(("parallel", "parallel")),
        name="out_proj",
    )(a0, a1, w, w, x, mod3, gpost.reshape(1, d))


def _mixer_a_kernel(z_ref, vg_ref, vb_ref, ws_ref, bst_ref, o_ref):
    z = jax.nn.gelu(z_ref[0].astype(F32))
    wdt = z.shape[1] // 2
    u = z[:, :wdt]
    v = z[:, wdt:]
    mu = jnp.mean(v, axis=-1, keepdims=True)
    dv = v - mu
    var = jnp.mean(dv * dv, axis=-1, keepdims=True)
    vn = (dv * lax.rsqrt(var + LN_EPS) * vg_ref[...] + vb_ref[...]).astype(BF16)
    ch = z.shape[0]
    causal = (lax.broadcasted_iota(jnp.int32, (ch, ch), 0)
              >= lax.broadcasted_iota(jnp.int32, (ch, ch), 1))
    gd = wdt // A_GROUPS
    for g in range(A_GROUPS):
        w = jnp.where(causal, ws_ref[g], 0.0).astype(BF16)
        sv = jnp.dot(w, vn[:, g * gd:(g + 1) * gd], preferred_element_type=F32)
        sv = sv + bst_ref[:, g:g + 1]
        o_ref[0, :, g * gd:(g + 1) * gd] = (u[:, g * gd:(g + 1) * gd] * sv).astype(o_ref.dtype)


def _mixer_a(z, v_gain, v_bias, w_s, b_s, width):
    bsz, s, _ = z.shape
    ch = A_CHUNK
    return pl.pallas_call(
        _mixer_a_kernel,
        grid=(bsz, s // ch),
        in_specs=[pl.BlockSpec((1, ch, 2 * width), lambda b, c: (b, c, 0)),
                  pl.BlockSpec((1, width), lambda b, c: (0, 0)),
                  pl.BlockSpec((1, width), lambda b, c: (0, 0)),
                  pl.BlockSpec((A_GROUPS, ch, ch), lambda b, c: (0, 0, 0)),
                  pl.BlockSpec((ch, A_GROUPS), lambda b, c: (0, 0))],
        out_specs=pl.BlockSpec((1, ch, width), lambda b, c: (b, c, 0)),
        out_shape=jax.ShapeDtypeStruct((bsz, s, width), BF16),
        compiler_params=_params(("parallel", "parallel")),
        name="mixer_a",
    )(z, v_gain.reshape(1, width), v_bias.reshape(1, width), w_s, b_s.T)


_PV_MU_R, _PV_MU_K, _PV_MU_V, _PV_W0, _PV_A0, _PV_KK, _PV_KA, _PV_RK, _PV_LG, _PV_LB = range(10)
_PV_ROWS = 16


def _shift_lerp(x, prev_row, mu):
    rolled = pltpu.roll(x, 1, axis=0)
    first = lax.broadcasted_iota(jnp.int32, x.shape, 0) == 0
    xp = jnp.where(first, prev_row, rolled)
    return x + mu * (xp - x)


def _split_bf16(x):
    hi = x.astype(BF16)
    lo = (x - hi.astype(F32)).astype(BF16)
    return hi, lo


def _mm(x, y):
    return jnp.dot(x.astype(BF16), y.astype(BF16), preferred_element_type=F32)


def _mm_nt(x, y):
    return lax.dot_general(x.astype(BF16), y.astype(BF16), NT_DIMS, preferred_element_type=F32)


def _mm_tn(x, y):
    return lax.dot_general(x.astype(BF16), y.astype(BF16), TN_DIMS, preferred_element_type=F32)


def _mm_x3(x, y):
    xh, xl = _split_bf16(x)
    yh, yl = _split_bf16(y)
    return (jnp.dot(xh, yh, preferred_element_type=F32) + jnp.dot(xl, yh, preferred_element_type=F32)
            + jnp.dot(xh, yl, preferred_element_type=F32))


def _mm_exact_rhs(x, e_bf16):
    xh, xl = _split_bf16(x)
    return (jnp.dot(xh, e_bf16, preferred_element_type=F32)
            + jnp.dot(xl, e_bf16, preferred_element_type=F32))


def _rwkv_kernel(zr_ref, zk_ref, zv_ref, zl_ref, pv_ref, mul_ref, w2_ref, a2_ref, g2_ref, o_ref,
                 s_scr, prev_scr, prevl_scr, q_scr, y_scr, gm_scr, cm_scr, pl_scr, *, chunk):
    tb = pl.program_id(2)
    t_rows = zr_ref.shape[1]
    lanes = zr_ref.shape[2]
    L = chunk
    SL = RW_PACK * L
    n_chunks = t_rows // L

    @pl.when(tb == 0)
    def _():
        s_scr[...] = jnp.zeros_like(s_scr)
        prev_scr[...] = jnp.zeros_like(prev_scr)
        prevl_scr[...] = jnp.zeros_like(prevl_scr)

    def pv(i):
        return pv_ref[i:i + 1, :]

    zr = zr_ref[0].astype(F32)
    zk = zk_ref[0].astype(F32)
    zv = zv_ref[0].astype(F32)
    zl = zl_ref[0].astype(F32)
    r = _shift_lerp(zr, prev_scr[0:1, :], pv(_PV_MU_R))
    k = _shift_lerp(zk, prev_scr[1:2, :], pv(_PV_MU_K))
    v = _shift_lerp(zv, prev_scr[2:3, :], pv(_PV_MU_V))
    zls = _shift_lerp(zl, prevl_scr[0:1, :], mul_ref[...])
    prev_scr[0:1, :] = zr[t_rows - 1:t_rows, :]
    prev_scr[1:2, :] = zk[t_rows - 1:t_rows, :]
    prev_scr[2:3, :] = zv[t_rows - 1:t_rows, :]
    prevl_scr[0:1, :] = zl[t_rows - 1:t_rows, :]

    x_wa = zls[:, :LANE]
    x_g = zls[:, LANE:3 * LANE]
    w_pre = pv(_PV_W0) + _mm_x3(jnp.tanh(x_wa), w2_ref[...])
    t = -w_pre
    softplus = jnp.maximum(t, 0.0) + jnp.log1p(jnp.exp(-jnp.abs(t)))
    log_decay = -jnp.exp(-softplus - 0.5)
    a = jax.nn.sigmoid(pv(_PV_A0) + _mm_x3(x_wa, a2_ref[...]))
    g = _mm(jax.nn.sigmoid(x_g), g2_ref[...])

    li = lax.broadcasted_iota(jnp.int32, (lanes, lanes), 0) // RW_HEAD
    lj = lax.broadcasted_iota(jnp.int32, (lanes, lanes), 1) // RW_HEAD
    same_head = li == lj
    e_head = jnp.where(same_head, 1.0, 0.0).astype(BF16)

    kk = k * pv(_PV_KK)
    kk = kk / jnp.maximum(jnp.sqrt(_mm_exact_rhs(kk * kk, e_head)), 1e-12)
    k = k * (1.0 + (a - 1.0) * pv(_PV_KA))
    av = -kk
    bv = kk * a

    ri = lax.broadcasted_iota(jnp.int32, (t_rows, t_rows), 0)
    ci = lax.broadcasted_iota(jnp.int32, (t_rows, t_rows), 1)
    tri_bd = jnp.where((ri >= ci) & ((ri // L) == (ci // L)), 1.0, 0.0).astype(BF16)
    lw_hi, lw_lo = _split_bf16(log_decay)
    cum = (jnp.dot(tri_bd, lw_hi, preferred_element_type=F32)
           + jnp.dot(tri_bd, lw_lo, preferred_element_type=F32))

    si = lax.broadcasted_iota(jnp.int32, (SL, SL), 0)
    sj = lax.broadcasted_iota(jnp.int32, (SL, SL), 1)
    same_blk = (si // L) == (sj // L)
    m_strict = same_blk & (si > sj)
    m_incl = same_blk & (si >= sj)
    eye = jnp.where(si == sj, 1.0, 0.0)
    lane_head = lax.broadcasted_iota(jnp.int32, (1, lanes), 1) // RW_HEAD
    n_sq = max(L.bit_length() - 2, 0)

    def stack(x):
        return jnp.concatenate([jnp.where(lane_head == h, x, 0.0) for h in range(RW_PACK)], axis=0)

    def unstack(x):
        out = x[0:L]
        for h in range(1, RW_PACK):
            out = out + x[h * L:(h + 1) * L]
        return out

    cs = range(n_chunks)
    rows = [slice(c * L, (c + 1) * L) for c in cs]
    cm = [cum[rw] for rw in rows]
    cm_last = [x[L - 1:L, :] for x in cm]
    r_t = [r[rw] * jnp.exp(cm[c]) for c, rw in enumerate(rows)]
    a_st = [stack(av[rw] * jnp.exp(cm[c] - log_decay[rw])).astype(BF16) for c, rw in enumerate(rows)]
    e_neg = [jnp.exp(-x) for x in cm]
    k_rep = [jnp.concatenate([(k[rw] * e_neg[c]).astype(BF16)] * RW_PACK, axis=0)
             for c, rw in enumerate(rows)]
    b_rep = [jnp.concatenate([(bv[rw] * e_neg[c]).astype(BF16)] * RW_PACK, axis=0)
             for c, rw in enumerate(rows)]
    ar_st = [jnp.concatenate([a_st[c], stack(r_t[c]).astype(BF16)], axis=0) for c in cs]
    p_k = [_mm_nt(ar_st[c], k_rep[c]) for c in cs]
    p_b = [_mm_nt(ar_st[c], b_rep[c]) for c in cs]
    a_ak = [jnp.where(m_strict, x[:SL], 0.0).astype(BF16) for x in p_k]
    a_rk = [jnp.where(m_incl, x[SL:], 0.0).astype(BF16) for x in p_k]
    a_ab = [jnp.where(m_strict, x[:SL], 0.0) for x in p_b]
    a_rb = [jnp.where(m_incl, x[SL:], 0.0).astype(BF16) for x in p_b]

    xp = [x.astype(BF16) for x in a_ab]
    tinv = [eye + x for x in a_ab]
    for _ in range(n_sq):
        xp = [_mm(x, x).astype(BF16) for x in xp]
        tinv = [tinv[c] + _mm(tinv[c], xp[c]) for c in cs]
    tinv = [x.astype(BF16) for x in tinv]

    v_st = [stack(v[rw]).astype(BF16) for rw in rows]
    x0 = [_mm(a_ak[c], v_st[c]).astype(BF16) for c in cs]
    wu = [_mm(tinv[c], jnp.concatenate([a_st[c], x0[c]], axis=1)) for c in cs]
    yq = [_mm(a_rb[c], wu[c]) for c in cs]
    y0 = [_mm(a_rk[c], v_st[c]) for c in cs]
    for c, rw in enumerate(rows):
        q_scr[c] = (r_t[c] + unstack(yq[c][:, :lanes])).astype(BF16)
        y_scr[rw, :] = unstack(yq[c][:, lanes:] + y0[c])
    for c, rw in enumerate(rows):
        e_rem = jnp.exp(cm_last[c] - cm[c])
        b_b = bv[rw] * e_rem
        kb = jnp.concatenate([k[rw] * e_rem, b_b], axis=0)
        vu = jnp.concatenate([v[rw], unstack(wu[c][:, lanes:])], axis=0)
        cm_scr[c] = jnp.where(same_head, _mm_tn(vu, kb), 0.0)
        gm_scr[c] = jnp.where(same_head, _mm_tn(unstack(wu[c][:, :lanes]), b_b), 0.0).astype(BF16)
        pl_scr[c] = jnp.broadcast_to(jnp.exp(cm_last[c]), (8, lanes))

    s0 = s_scr[...]
    for c in range(n_chunks):
        lo, hi = c * L, (c + 1) * L
        s0b = s0.astype(BF16)
        y_scr[lo:hi, :] = y_scr[lo:hi, :] + _mm_nt(q_scr[c], s0b)
        s0 = s0 * pl_scr[c][0:1, :] + _mm(s0b, gm_scr[c]) + cm_scr[c]
    s_scr[...] = s0

    y = y_scr[...]
    inv_n = 1.0 / RW_HEAD
    mean = _mm_exact_rhs(y, e_head) * inv_n
    dy = y - mean
    var = _mm_exact_rhs(dy * dy, e_head) * inv_n
    yn = dy * lax.rsqrt(var + GN_EPS) * pv(_PV_LG) + pv(_PV_LB)
    bonus = _mm_exact_rhs(r * k * pv(_PV_RK), e_head) * v
    o_ref[0] = ((yn + bonus) * g).astype(o_ref.dtype)


def _rwkv(z, col_r, col_k, col_v, col_l, lora_w, pvec, mu_l, w2p, a2p, g2p, width, t_rows=512):
    bsz, s, _ = z.shape
    n_blk = width // LANE
    n_chunks = t_rows // RW_CHUNK
    kern = functools.partial(_rwkv_kernel, chunk=RW_CHUNK)
    return pl.pallas_call(
        kern,
        grid=(bsz, n_blk, s // t_rows),
        in_specs=[pl.BlockSpec((1, t_rows, LANE), lambda b, p, t: (b, t, col_r + p)),
                  pl.BlockSpec((1, t_rows, LANE), lambda b, p, t: (b, t, col_k + p)),
                  pl.BlockSpec((1, t_rows, LANE), lambda b, p, t: (b, t, col_v + p)),
                  pl.BlockSpec((1, t_rows, lora_w), lambda b, p, t: (b, t, col_l)),
                  pl.BlockSpec((_PV_ROWS, LANE), lambda b, p, t: (0, p)),
                  pl.BlockSpec((1, lora_w), lambda b, p, t: (0, 0)),
                  pl.BlockSpec((LANE, LANE), lambda b, p, t: (0, p)),
                  pl.BlockSpec((LANE, LANE), lambda b, p, t: (0, p)),
                  pl.BlockSpec((2 * LANE, LANE), lambda b, p, t: (0, p))],
        out_specs=pl.BlockSpec((1, t_rows, LANE), lambda b, p, t: (b, t, p)),
        out_shape=jax.ShapeDtypeStruct((bsz, s, width), BF16),
        scratch_shapes=[pltpu.VMEM((LANE, LANE), F32),
                        pltpu.VMEM((8, LANE), F32),
                        pltpu.VMEM((8, lora_w), F32),
                        pltpu.VMEM((n_chunks, RW_CHUNK, LANE), BF16),
                        pltpu.VMEM((t_rows, LANE), F32),
                        pltpu.VMEM((n_chunks, LANE, LANE), BF16),
                        pltpu.VMEM((n_chunks, LANE, LANE), F32),
                        pltpu.VMEM((n_chunks, 8, LANE), F32)],
        compiler_params=_params(("parallel", "parallel", "arbitrary")),
        name="rwkv7",
    )(z, z, z, z, pvec, mu_l, w2p, a2p, g2p)


def _moba_kernel(q_ref, k_ref, v_ref, o_ref, vt_scr, s_scr, p_scr, *, n_heads):
    h = pl.program_id(1)
    s_len = q_ref.shape[1]
    dh = q_ref.shape[2]
    blk = MOBA_BLOCK
    nb = s_len // blk
    log2e = 1.4426950408889634
    scale = dh ** -0.5 * log2e
    neg_inf = -jnp.inf

    kmean = []
    for j in range(nb):
        rows = slice(j * blk, (j + 1) * blk)
        kmean.append(jnp.mean(k_ref[0, rows, :].astype(F32), axis=0, keepdims=True))
        vt_scr[:, rows] = v_ref[0, rows, :].astype(F32).T.astype(BF16)
    kmean = jnp.concatenate(kmean, axis=0)

    slope = log2e * jnp.exp(jnp.full((1, blk), -8.0 / n_heads * 0.6931471805599453, F32)
                            * (h + 1).astype(F32))
    jk = lax.broadcasted_iota(jnp.int32, (blk, blk), 0)
    iq = lax.broadcasted_iota(jnp.int32, (blk, blk), 1)
    bias_tile = slope * jk.astype(F32)
    bias_own = jnp.where(iq >= jk, bias_tile, neg_inf)
    blk_id = lax.broadcasted_iota(jnp.int32, (nb, 1), 0)

    def scores(qb):
        q = q_ref[0, qb * blk:(qb + 1) * blk, :]
        q_s = (q.astype(F32) * scale).astype(BF16)
        gate = lax.dot_general(kmean, q.astype(F32), NT_DIMS, precision=HI,
                               preferred_element_type=F32)
        past = blk_id < qb
        offs = []
        m = None
        for n in range(qb + 1):
            k_n = k_ref[0, n * blk:(n + 1) * blk, :]
            s = lax.dot_general(k_n, q_s, NT_DIMS, preferred_element_type=F32)
            if n == qb:
                t = s + bias_own
                off = jnp.zeros((1, blk), F32)
            else:
                t = s + bias_tile
                g_n = gate[n:n + 1, :]
                beats = past & ((gate > g_n) | ((gate == g_n) & (blk_id < n)))
                rank = jnp.sum(jnp.where(beats, 1.0, 0.0), axis=0, keepdims=True)
                off = jnp.where(rank < float(MOBA_TOPK), slope * float((n - qb) * blk), neg_inf)
            s_scr[qb % 2, n] = t
            offs.append(off)
            cmax = jnp.max(t, axis=0, keepdims=True) + off
            m = cmax if m is None else jnp.maximum(m, cmax)
        return m, offs

    def attend(qb, m, offs):
        l = jnp.zeros((1, blk), F32)
        for n in range(qb + 1):
            p = jnp.exp2(s_scr[qb % 2, n] - (m - offs[n]))
            l = l + jnp.sum(p, axis=0, keepdims=True)
            p_scr[qb % 2, n * blk:(n + 1) * blk, :] = p.astype(BF16)
        kk = (qb + 1) * blk
        acc = jnp.dot(vt_scr[:, :kk], p_scr[qb % 2, :kk, :], preferred_element_type=F32)
        o_ref[0, qb * blk:(qb + 1) * blk, :] = (acc / l).T.astype(o_ref.dtype)

    pending = scores(0)
    for qb in range(nb):
        nxt = scores(qb + 1) if qb + 1 < nb else None
        attend(qb, *pending)
        pending = nxt


def _moba(qkv, n_heads):
    bsz, s, d3 = qkv.shape
    d = d3 // 3
    dh = d // n_heads
    blk = MOBA_BLOCK
    nb = s // blk
    kern = functools.partial(_moba_kernel, n_heads=n_heads)
    return pl.pallas_call(
        kern,
        grid=(bsz, n_heads),
        in_specs=[pl.BlockSpec((1, s, dh), lambda b, h: (b, 0, h)),
                  pl.BlockSpec((1, s, dh), lambda b, h: (b, 0, n_heads + h)),
                  pl.BlockSpec((1, s, dh), lambda b, h: (b, 0, 2 * n_heads + h))],
        out_specs=pl.BlockSpec((1, s, dh), lambda b, h: (b, 0, h)),
        out_shape=jax.ShapeDtypeStruct((bsz, s, d), BF16),
        scratch_shapes=[pltpu.VMEM((dh, s), BF16),
                        pltpu.VMEM((2, nb, blk, blk), F32),
                        pltpu.VMEM((2, s, blk), BF16)],
        compiler_params=_params(("parallel", "parallel")),
        name="moba",
    )(qkv, qkv, qkv)


def _pad_cols(w, n):
    return jnp.pad(w, ((0, 0), (0, n - w.shape[1])))


def _pad_rows(w, n, before=0):
    return jnp.pad(w, ((before, n - before - w.shape[0]), (0, 0)))


def kernel(x, c, w_ada, b_ada, g_pre_mix, g_post_mix, g_pre_ffn, g_post_ffn, w_ffn_in, w_ffn_out,
           w_in_ab, w_out_ab, a_v_gain, a_v_bias, a_w_s, a_b_s, b_mu, b_w0, b_w2, b_a0, b_a2, b_g2,
           b_k_k, b_k_a, b_r_k, b_lnx_gain, b_lnx_bias, w_qkv, w_o):
    bsz, s, d = x.shape
    depth = w_ada.shape[0]
    a_width = a_v_gain.shape[1]
    b_width = b_w0.shape[1]
    n_lw = b_w2.shape[1]
    n_la = b_a2.shape[1]
    n_lg = b_g2.shape[1]
    n_heads = d // ATT_HEAD
    assert s % MOBA_BLOCK == 0 and s % 1024 == 0
    assert n_lw + n_la <= LANE and n_lg <= 2 * LANE

    mod = _ada_mod(c, w_ada, b_ada)
    w_ffn_in_h = w_ffn_in.astype(BF16)
    w_ffn_out_h = w_ffn_out.astype(BF16)

    for layer in range(depth):
        mod3 = mod[layer].reshape(bsz, 1, 6 * d)
        i = layer // 2
        if layer % 2 == 0:
            lora_w = 4 * LANE
            nz = 2 * a_width + 3 * b_width + lora_w
            w_in = _pad_cols(w_in_ab[i], nz).astype(BF16)
            z = _norm_mm(x, g_pre_mix[layer], mod3, 1, 0, w_in, BF16, name="in_proj_ab")
            y_a = _mixer_a(z, a_v_gain[i], a_v_bias[i], a_w_s[i], a_b_s[i], a_width)

            mu = b_mu[i]
            pvec = jnp.stack([mu[0:b_width], mu[b_width:2 * b_width], mu[2 * b_width:3 * b_width],
                              b_w0[i], b_a0[i], b_k_k[i], b_k_a[i], b_r_k[i].reshape(-1),
                              b_lnx_gain[i], b_lnx_bias[i]])
            pvec = _pad_rows(pvec, _PV_ROWS)
            mu_l = _pad_cols(mu[3 * b_width:].reshape(1, -1), lora_w)
            w2p = _pad_rows(b_w2[i], LANE)
            a2p = _pad_rows(b_a2[i], LANE, before=n_lw)
            g2p = _pad_rows(b_g2[i], 2 * LANE)
            cb = 2 * a_width // LANE
            nb_w = b_width // LANE
            y_b = _rwkv(z, cb, cb + nb_w, cb + 2 * nb_w, (2 * a_width + 3 * b_width) // lora_w,
                        lora_w, pvec, mu_l, w2p, a2p, g2p, b_width)
            x = _out_proj(y_a, y_b, 0, 0, w_out_ab[i].astype(BF16), x, mod3, g_post_mix[layer])
        else:
            qkv = _norm_mm(x, g_pre_mix[layer], mod3, 1, 0, w_qkv[i].astype(BF16), BF16,
                           name="qkv_proj")
            o = _moba(qkv, n_heads)
            x = _out_proj(o, o, 0, 1, w_o[i].astype(BF16), x, mod3, g_post_mix[layer])
        x = _ffn(x, g_pre_ffn[layer], g_post_ffn[layer], mod3, w_ffn_in_h, w_ffn_out_h, layer)
    return x
```

```python
import functools

import jax
import jax.numpy as jnp
from jax import lax
from jax.experimental import pallas as pl
from jax.experimental.pallas import tpu as pltpu

F32 = jnp.float32
BF16 = jnp.bfloat16

NORM_EPS = 1e-6
LN_EPS = 1e-5
GN_EPS = 64e-5

LANE = 128
A_GROUPS = 8
A_CHUNK = 128
RW_HEAD = 64
RW_CHUNK = 64
RW_PACK = 2
MOBA_BLOCK = 256
MOBA_TOPK = 3
ATT_HEAD = 128

NT_DIMS = (((1,), (1,)), ((), ()))
TN_DIMS = (((0,), (0,)), ((), ()))

VMEM_LIMIT = 56 * 1024 * 1024


def _params(sem):
    return pltpu.CompilerParams(dimension_semantics=sem, vmem_limit_bytes=VMEM_LIMIT)


def _slab_specs(side, lead, axis, unit, n_steps, flat_index):
    total = side.shape[1 + axis]
    n_slab = max(n for n in range(1, n_steps + 1) if total % (n * unit) == 0)
    shape = list(side.shape[1:])
    shape[axis] = total // n_slab

    def idx(*grid):
        slab = jnp.minimum(flat_index(*grid), n_slab - 1)
        return (slab, 0) if axis == 0 else (0, slab)

    return (pl.BlockSpec((None,) + tuple(shape), lambda *g: (lead,) + idx(*g)),
            pl.BlockSpec(tuple(shape), idx))


def _rms(x, gain):
    ms = jnp.mean(x * x, axis=-1, keepdims=True)
    return x * lax.rsqrt(ms + NORM_EPS) * gain


ROW_CHUNK = 16
ROW_UNROLL = 8


def _for_row_chunks(n_rows, fn):
    def body(i, carry):
        fn(pl.ds(pl.multiple_of(i * ROW_CHUNK, ROW_CHUNK), ROW_CHUNK))
        return carry
    lax.fori_loop(0, n_rows // ROW_CHUNK, body, 0, unroll=ROW_UNROLL)


def _modulated_norm(x_ref, g_ref, sc_ref, sh_ref, gm_scr, h_scr):
    gm_scr[...] = g_ref[...] * (1.0 + sc_ref[0])

    def rows(r):
        x = x_ref[0, r, :]
        ms = jnp.mean(x * x, axis=-1, keepdims=True)
        h_scr[r, :] = (x * lax.rsqrt(ms + NORM_EPS) * gm_scr[...] + sh_ref[0]).astype(BF16)

    _for_row_chunks(h_scr.shape[0], rows)


def _post_norm_residual(y_scr, x_ref, gt_ref, gpost_ref, gm_scr, o_ref):
    gm_scr[...] = gt_ref[0] * gpost_ref[...]

    def rows(r):
        y = y_scr[r, :]
        ms = jnp.mean(y * y, axis=-1, keepdims=True)
        o_ref[0, r, :] = x_ref[0, r, :] + y * lax.rsqrt(ms + NORM_EPS) * gm_scr[...]

    _for_row_chunks(o_ref.shape[1], rows)


def _ada_kernel(c_ref, w_ref, b_ref, o_ref):
    c = c_ref[...]
    cond = (c * jax.nn.sigmoid(c)).astype(BF16)
    o_ref[0] = jnp.dot(cond, w_ref[0].astype(BF16), preferred_element_type=F32) + b_ref[0]


def _ada_mod(c, w_ada, b_ada):
    depth, d, n = w_ada.shape
    bsz = c.shape[0]
    bp = 8
    c_p = jnp.pad(c, ((0, bp - bsz), (0, 0)))
    tn = 1024
    out = pl.pallas_call(
        _ada_kernel,
        grid=(depth, n // tn),
        in_specs=[pl.BlockSpec((bp, d), lambda l, j: (0, 0)),
                  pl.BlockSpec((1, d, tn), lambda l, j: (l, 0, j)),
                  pl.BlockSpec((1, 1, tn), lambda l, j: (l, 0, j))],
        out_specs=pl.BlockSpec((1, bp, tn), lambda l, j: (l, 0, j)),
        out_shape=jax.ShapeDtypeStruct((depth, bp, n), F32),
        compiler_params=_params(("parallel", "parallel")),
        name="ada_mod",
    )(c_p, w_ada, b_ada.reshape(depth, 1, n))
    return out[:, :bsz]


def _norm_mm_kernel(x_ref, g_ref, sc_ref, sh_ref, w_ref, side_ref, o_ref, side_o_ref, h_scr, gm_scr):
    @pl.when(pl.program_id(2) == 0)
    def _():
        _modulated_norm(x_ref, g_ref, sc_ref, sh_ref, gm_scr, h_scr)

    o_ref[0] = jnp.dot(h_scr[...], w_ref[...].astype(BF16),
                       preferred_element_type=F32).astype(o_ref.dtype)
    side_o_ref[...] = side_ref[...].astype(BF16)


def _norm_mm(x, gain, mod3, sc_idx, sh_idx, w, out_dtype, side, side_lead, tm=1024, tn=512,
             name="norm_mm"):
    bsz, s, d = x.shape
    n = w.shape[1]
    nm, nj = s // tm, n // tn
    side_in, side_out = _slab_specs(side, side_lead, 1, LANE, bsz * nm * nj,
                                    lambda b, m, j: (b * nm + m) * nj + j)
    return pl.pallas_call(
        _norm_mm_kernel,
        grid=(bsz, nm, nj),
        in_specs=[pl.BlockSpec((1, tm, d), lambda b, m, j: (b, m, 0)),
                  pl.BlockSpec((1, d), lambda b, m, j: (0, 0)),
                  pl.BlockSpec((1, 1, d), lambda b, m, j: (b, 0, sc_idx)),
                  pl.BlockSpec((1, 1, d), lambda b, m, j: (b, 0, sh_idx)),
                  pl.BlockSpec((d, tn), lambda b, m, j: (0, j)),
                  side_in],
        out_specs=(pl.BlockSpec((1, tm, tn), lambda b, m, j: (b, m, j)), side_out),
        out_shape=(jax.ShapeDtypeStruct((bsz, s, n), out_dtype),
                   jax.ShapeDtypeStruct(side.shape[1:], BF16)),
        scratch_shapes=[pltpu.VMEM((tm, d), BF16), pltpu.VMEM((1, d), F32)],
        compiler_params=_params(("arbitrary", "arbitrary", "arbitrary")),
        name=name,
    )(x, gain.reshape(1, d), mod3, mod3, w, side)


def _ffn_kernel(x_ref, gpre_ref, sc_ref, sh_ref, gt_ref, gpost_ref, wg_ref, wu_ref, wo_ref,
                o_ref, h_scr, gm_scr, acc_scr):
    f = pl.program_id(2)

    @pl.when(f == 0)
    def _():
        _modulated_norm(x_ref, gpre_ref, sc_ref, sh_ref, gm_scr, h_scr)
        acc_scr[...] = jnp.zeros_like(acc_scr)

    h = h_scr[...]
    g = jnp.dot(h, wg_ref[...], preferred_element_type=F32)
    u = jnp.dot(h, wu_ref[...], preferred_element_type=F32)
    a = (g * jax.nn.sigmoid(g) * u).astype(BF16)
    acc_scr[...] += jnp.dot(a, wo_ref[...], preferred_element_type=F32)

    @pl.when(f == pl.num_programs(2) - 1)
    def _():
        _post_norm_residual(acc_scr, x_ref, gt_ref, gpost_ref, gm_scr, o_ref)


def _ffn(x, gpre, gpost, mod3, w_in, w_out, tm=1024, tf=512):
    bsz, s, d = x.shape
    fh = w_out.shape[0]
    nf = fh // tf
    return pl.pallas_call(
        _ffn_kernel,
        grid=(bsz, s // tm, nf),
        in_specs=[pl.BlockSpec((1, tm, d), lambda b, m, f: (b, m, 0), pipeline_mode=pl.Buffered(1)),
                  pl.BlockSpec((1, d), lambda b, m, f: (0, 0)),
                  pl.BlockSpec((1, 1, d), lambda b, m, f: (b, 0, 4)),
                  pl.BlockSpec((1, 1, d), lambda b, m, f: (b, 0, 3)),
                  pl.BlockSpec((1, 1, d), lambda b, m, f: (b, 0, 5)),
                  pl.BlockSpec((1, d), lambda b, m, f: (0, 0)),
                  pl.BlockSpec((d, tf), lambda b, m, f: (0, f)),
                  pl.BlockSpec((d, tf), lambda b, m, f: (0, nf + f)),
                  pl.BlockSpec((tf, d), lambda b, m, f: (f, 0))],
        out_specs=pl.BlockSpec((1, tm, d), lambda b, m, f: (b, m, 0), pipeline_mode=pl.Buffered(1)),
        out_shape=jax.ShapeDtypeStruct((bsz, s, d), F32),
        scratch_shapes=[pltpu.VMEM((tm, d), BF16), pltpu.VMEM((1, d), F32), pltpu.VMEM((tm, d), F32)],
        compiler_params=_params(("parallel", "parallel", "arbitrary")),
        name="ffn",
    )(x, gpre.reshape(1, d), mod3, mod3, mod3, gpost.reshape(1, d), w_in, w_in, w_out)


def _out_proj_kernel(a0_ref, a1_ref, w0_ref, w1_ref, x_ref, gt_ref, gpost_ref, side_ref,
                     o_ref, side_o_ref, gm_scr, y_scr):
    side_o_ref[...] = side_ref[...].astype(BF16)
    y_scr[...] = (jnp.dot(a0_ref[0], w0_ref[...], preferred_element_type=F32)
                  + jnp.dot(a1_ref[0], w1_ref[...], preferred_element_type=F32))
    _post_norm_residual(y_scr, x_ref, gt_ref, gpost_ref, gm_scr, o_ref)


def _out_proj(a0, a1, col0, col1, w, x, mod3, gpost, side, side_lead, tm=512):
    bsz, s, d = x.shape
    kh = w.shape[0] // 2
    nm = s // tm
    side_in, side_out = _slab_specs(side, side_lead, 0, ROW_CHUNK, bsz * nm, lambda b, m: b * nm + m)
    return pl.pallas_call(
        _out_proj_kernel,
        grid=(bsz, nm),
        in_specs=[pl.BlockSpec((1, tm, kh), lambda b, m: (b, m, col0)),
                  pl.BlockSpec((1, tm, kh), lambda b, m: (b, m, col1)),
                  pl.BlockSpec((kh, d), lambda b, m: (0, 0)),
                  pl.BlockSpec((kh, d), lambda b, m: (1, 0)),
                  pl.BlockSpec((1, tm, d), lambda b, m: (b, m, 0)),
                  pl.BlockSpec((1, 1, d), lambda b, m: (b, 0, 2)),
                  pl.BlockSpec((1, d), lambda b, m: (0, 0)),
                  side_in],
        out_specs=(pl.BlockSpec((1, tm, d), lambda b, m: (b, m, 0)), side_out),
        out_shape=(jax.ShapeDtypeStruct((bsz, s, d), F32),
                   jax.ShapeDtypeStruct(side.shape[1:], BF16)),
        scratch_shapes=[pltpu.VMEM((1, d), F32), pltpu.VMEM((tm, d), F32)],
        compiler_params=_params(("arbitrary", "arbitrary")),
        name="out_proj",
    )(a0, a1, w, w, x, mod3, gpost.reshape(1, d), side)


def _mixer_a_kernel(z_ref, vg_ref, vb_ref, ws_ref, bst_ref, o_ref):
    z = jax.nn.gelu(z_ref[0].astype(F32))
    wdt = z.shape[1] // 2
    u = z[:, :wdt]
    v = z[:, wdt:]
    mu = jnp.mean(v, axis=-1, keepdims=True)
    dv = v - mu
    var = jnp.mean(dv * dv, axis=-1, keepdims=True)
    vn = (dv * lax.rsqrt(var + LN_EPS) * vg_ref[...] + vb_ref[...]).astype(BF16)
    ch = z.shape[0]
    causal = (lax.broadcasted_iota(jnp.int32, (ch, ch), 0)
              >= lax.broadcasted_iota(jnp.int32, (ch, ch), 1))
    gd = wdt // A_GROUPS
    for g in range(A_GROUPS):
        w = jnp.where(causal, ws_ref[g], 0.0).astype(BF16)
        sv = jnp.dot(w, vn[:, g * gd:(g + 1) * gd], preferred_element_type=F32)
        sv = sv + bst_ref[:, g:g + 1]
        o_ref[0, :, g * gd:(g + 1) * gd] = (u[:, g * gd:(g + 1) * gd] * sv).astype(o_ref.dtype)


def _mixer_a(z, v_gain, v_bias, w_s, b_s, width):
    bsz, s, _ = z.shape
    ch = A_CHUNK
    return pl.pallas_call(
        _mixer_a_kernel,
        grid=(bsz, s // ch),
        in_specs=[pl.BlockSpec((1, ch, 2 * width), lambda b, c: (b, c, 0)),
                  pl.BlockSpec((1, width), lambda b, c: (0, 0)),
                  pl.BlockSpec((1, width), lambda b, c: (0, 0)),
                  pl.BlockSpec((A_GROUPS, ch, ch), lambda b, c: (0, 0, 0)),
                  pl.BlockSpec((ch, A_GROUPS), lambda b, c: (0, 0))],
        out_specs=pl.BlockSpec((1, ch, width), lambda b, c: (b, c, 0)),
        out_shape=jax.ShapeDtypeStruct((bsz, s, width), BF16),
        compiler_params=_params(("parallel", "parallel")),
        name="mixer_a",
    )(z, v_gain.reshape(1, width), v_bias.reshape(1, width), w_s, b_s.T)


_PV_MU_R, _PV_MU_K, _PV_MU_V, _PV_W0, _PV_A0, _PV_KK, _PV_KA, _PV_RK, _PV_LG, _PV_LB = range(10)
_PV_ROWS = 16


def _shift_lerp(x, prev_row, mu):
    rolled = pltpu.roll(x, 1, axis=0)
    first = lax.broadcasted_iota(jnp.int32, x.shape, 0) == 0
    xp = jnp.where(first, prev_row, rolled)
    return x + mu * (xp - x)


def _split_bf16(x):
    hi = x.astype(BF16)
    lo = (x - hi.astype(F32)).astype(BF16)
    return hi, lo


def _mm(x, y):
    return jnp.dot(x.astype(BF16), y.astype(BF16), preferred_element_type=F32)


def _mm_nt(x, y):
    return lax.dot_general(x.astype(BF16), y.astype(BF16), NT_DIMS, preferred_element_type=F32)


def _mm_tn(x, y):
    return lax.dot_general(x.astype(BF16), y.astype(BF16), TN_DIMS, preferred_element_type=F32)


def _mm_x3(x, y):
    xh, xl = _split_bf16(x)
    yh, yl = _split_bf16(y)
    return (jnp.dot(xh, yh, preferred_element_type=F32) + jnp.dot(xl, yh, preferred_element_type=F32)
            + jnp.dot(xh, yl, preferred_element_type=F32))


def _mm_exact_rhs(x, e_bf16):
    xh, xl = _split_bf16(x)
    return (jnp.dot(xh, e_bf16, preferred_element_type=F32)
            + jnp.dot(xl, e_bf16, preferred_element_type=F32))


def _rwkv_kernel(zr_ref, zk_ref, zv_ref, zl_ref, pv_ref, mul_ref, w2_ref, a2_ref, g2_ref, side_ref,
                 o_ref, side_o_ref, s_scr, prev_scr, prevl_scr, q_scr, y_scr, gm_scr, cm_scr, pl_scr, *, chunk):
    tb = pl.program_id(2)
    t_rows = zr_ref.shape[1]
    lanes = zr_ref.shape[2]
    L = chunk
    side_o_ref[...] = side_ref[...].astype(BF16)
    SL = RW_PACK * L
    n_chunks = t_rows // L

    @pl.when(tb == 0)
    def _():
        s_scr[...] = jnp.zeros_like(s_scr)
        prev_scr[...] = jnp.zeros_like(prev_scr)
        prevl_scr[...] = jnp.zeros_like(prevl_scr)

    def pv(i):
        return pv_ref[i:i + 1, :]

    zr = zr_ref[0].astype(F32)
    zk = zk_ref[0].astype(F32)
    zv = zv_ref[0].astype(F32)
    zl = zl_ref[0].astype(F32)
    r = _shift_lerp(zr, prev_scr[0:1, :], pv(_PV_MU_R))
    k = _shift_lerp(zk, prev_scr[1:2, :], pv(_PV_MU_K))
    v = _shift_lerp(zv, prev_scr[2:3, :], pv(_PV_MU_V))
    zls = _shift_lerp(zl, prevl_scr[0:1, :], mul_ref[...])
    prev_scr[0:1, :] = zr[t_rows - 1:t_rows, :]
    prev_scr[1:2, :] = zk[t_rows - 1:t_rows, :]
    prev_scr[2:3, :] = zv[t_rows - 1:t_rows, :]
    prevl_scr[0:1, :] = zl[t_rows - 1:t_rows, :]

    x_wa = zls[:, :LANE]
    x_g = zls[:, LANE:3 * LANE]
    w_pre = pv(_PV_W0) + _mm_x3(jnp.tanh(x_wa), w2_ref[...])
    t = -w_pre
    softplus = jnp.maximum(t, 0.0) + jnp.log1p(jnp.exp(-jnp.abs(t)))
    log_decay = -jnp.exp(-softplus - 0.5)
    a = jax.nn.sigmoid(pv(_PV_A0) + _mm_x3(x_wa, a2_ref[...]))
    g = _mm(jax.nn.sigmoid(x_g), g2_ref[...])

    li = lax.broadcasted_iota(jnp.int32, (lanes, lanes), 0) // RW_HEAD
    lj = lax.broadcasted_iota(jnp.int32, (lanes, lanes), 1) // RW_HEAD
    same_head = li == lj
    e_head = jnp.where(same_head, 1.0, 0.0).astype(BF16)

    kk = k * pv(_PV_KK)
    kk = kk / jnp.maximum(jnp.sqrt(_mm_exact_rhs(kk * kk, e_head)), 1e-12)
    k = k * (1.0 + (a - 1.0) * pv(_PV_KA))
    av = -kk
    bv = kk * a

    ri = lax.broadcasted_iota(jnp.int32, (t_rows, t_rows), 0)
    ci = lax.broadcasted_iota(jnp.int32, (t_rows, t_rows), 1)
    tri_bd = jnp.where((ri >= ci) & ((ri // L) == (ci // L)), 1.0, 0.0).astype(BF16)
    lw_hi, lw_lo = _split_bf16(log_decay)
    cum = (jnp.dot(tri_bd, lw_hi, preferred_element_type=F32)
           + jnp.dot(tri_bd, lw_lo, preferred_element_type=F32))

    si = lax.broadcasted_iota(jnp.int32, (SL, SL), 0)
    sj = lax.broadcasted_iota(jnp.int32, (SL, SL), 1)
    same_blk = (si // L) == (sj // L)
    m_strict = same_blk & (si > sj)
    m_incl = same_blk & (si >= sj)
    eye = jnp.where(si == sj, 1.0, 0.0)
    lane_head = lax.broadcasted_iota(jnp.int32, (1, lanes), 1) // RW_HEAD
    n_sq = max(L.bit_length() - 2, 0)

    def stack(x):
        return jnp.concatenate([jnp.where(lane_head == h, x, 0.0) for h in range(RW_PACK)], axis=0)

    def unstack(x):
        out = x[0:L]
        for h in range(1, RW_PACK):
            out = out + x[h * L:(h + 1) * L]
        return out

    cs = range(n_chunks)
    rows = [slice(c * L, (c + 1) * L) for c in cs]
    cm = [cum[rw] for rw in rows]
    cm_last = [x[L - 1:L, :] for x in cm]
    r_t = [r[rw] * jnp.exp(cm[c]) for c, rw in enumerate(rows)]
    a_st = [stack(av[rw] * jnp.exp(cm[c] - log_decay[rw])).astype(BF16) for c, rw in enumerate(rows)]
    e_neg = [jnp.exp(-x) for x in cm]
    k_rep = [jnp.concatenate([(k[rw] * e_neg[c]).astype(BF16)] * RW_PACK, axis=0)
             for c, rw in enumerate(rows)]
    b_rep = [jnp.concatenate([(bv[rw] * e_neg[c]).astype(BF16)] * RW_PACK, axis=0)
             for c, rw in enumerate(rows)]
    ar_st = [jnp.concatenate([a_st[c], stack(r_t[c]).astype(BF16)], axis=0) for c in cs]
    p_k = [_mm_nt(ar_st[c], k_rep[c]) for c in cs]
    p_b = [_mm_nt(ar_st[c], b_rep[c]) for c in cs]
    a_ak = [jnp.where(m_strict, x[:SL], 0.0).astype(BF16) for x in p_k]
    a_rk = [jnp.where(m_incl, x[SL:], 0.0).astype(BF16) for x in p_k]
    a_ab = [jnp.where(m_strict, x[:SL], 0.0) for x in p_b]
    a_rb = [jnp.where(m_incl, x[SL:], 0.0).astype(BF16) for x in p_b]

    xp = [x.astype(BF16) for x in a_ab]
    tinv = [eye + x for x in a_ab]
    for _ in range(n_sq):
        xp = [_mm(x, x).astype(BF16) for x in xp]
        tinv = [tinv[c] + _mm(tinv[c], xp[c]) for c in cs]
    tinv = [x.astype(BF16) for x in tinv]

    v_st = [stack(v[rw]).astype(BF16) for rw in rows]
    x0 = [_mm(a_ak[c], v_st[c]).astype(BF16) for c in cs]
    wu = [_mm(tinv[c], jnp.concatenate([a_st[c], x0[c]], axis=1)) for c in cs]
    yq = [_mm(a_rb[c], wu[c]) for c in cs]
    y0 = [_mm(a_rk[c], v_st[c]) for c in cs]
    for c, rw in enumerate(rows):
        q_scr[c] = (r_t[c] + unstack(yq[c][:, :lanes])).astype(BF16)
        y_scr[rw, :] = unstack(yq[c][:, lanes:] + y0[c])
    for c, rw in enumerate(rows):
        e_rem = jnp.exp(cm_last[c] - cm[c])
        b_b = bv[rw] * e_rem
        kb = jnp.concatenate([k[rw] * e_rem, b_b], axis=0)
        vu = jnp.concatenate([v[rw], unstack(wu[c][:, lanes:])], axis=0)
        cm_scr[c] = jnp.where(same_head, _mm_tn(vu, kb), 0.0)
        gm_scr[c] = jnp.where(same_head, _mm_tn(unstack(wu[c][:, :lanes]), b_b), 0.0).astype(BF16)
        pl_scr[c] = jnp.broadcast_to(jnp.exp(cm_last[c]), (8, lanes))

    s0 = s_scr[...]
    for c in range(n_chunks):
        lo, hi = c * L, (c + 1) * L
        s0b = s0.astype(BF16)
        y_scr[lo:hi, :] = y_scr[lo:hi, :] + _mm_nt(q_scr[c], s0b)
        s0 = s0 * pl_scr[c][0:1, :] + _mm(s0b, gm_scr[c]) + cm_scr[c]
    s_scr[...] = s0

    y = y_scr[...]
    inv_n = 1.0 / RW_HEAD
    mean = _mm_exact_rhs(y, e_head) * inv_n
    dy = y - mean
    var = _mm_exact_rhs(dy * dy, e_head) * inv_n
    yn = dy * lax.rsqrt(var + GN_EPS) * pv(_PV_LG) + pv(_PV_LB)
    bonus = _mm_exact_rhs(r * k * pv(_PV_RK), e_head) * v
    o_ref[0] = ((yn + bonus) * g).astype(o_ref.dtype)


def _rwkv(z, col_r, col_k, col_v, col_l, lora_w, pvec, mu_l, w2p, a2p, g2p, width, side, side_lead,
          t_rows=512):
    bsz, s, _ = z.shape
    n_blk = width // LANE
    n_chunks = t_rows // RW_CHUNK
    nt = s // t_rows
    side_in, side_out = _slab_specs(side, side_lead, 0, ROW_CHUNK, bsz * n_blk * nt,
                                    lambda b, p, t: (b * n_blk + p) * nt + t)
    kern = functools.partial(_rwkv_kernel, chunk=RW_CHUNK)
    return pl.pallas_call(
        kern,
        grid=(bsz, n_blk, s // t_rows),
        in_specs=[pl.BlockSpec((1, t_rows, LANE), lambda b, p, t: (b, t, col_r + p)),
                  pl.BlockSpec((1, t_rows, LANE), lambda b, p, t: (b, t, col_k + p)),
                  pl.BlockSpec((1, t_rows, LANE), lambda b, p, t: (b, t, col_v + p)),
                  pl.BlockSpec((1, t_rows, lora_w), lambda b, p, t: (b, t, col_l)),
                  pl.BlockSpec((_PV_ROWS, LANE), lambda b, p, t: (0, p)),
                  pl.BlockSpec((1, lora_w), lambda b, p, t: (0, 0)),
                  pl.BlockSpec((LANE, LANE), lambda b, p, t: (0, p)),
                  pl.BlockSpec((LANE, LANE), lambda b, p, t: (0, p)),
                  pl.BlockSpec((2 * LANE, LANE), lambda b, p, t: (0, p)),
                  side_in],
        out_specs=(pl.BlockSpec((1, t_rows, LANE), lambda b, p, t: (b, t, p)), side_out),
        out_shape=(jax.ShapeDtypeStruct((bsz, s, width), BF16),
                   jax.ShapeDtypeStruct(side.shape[1:], BF16)),
        scratch_shapes=[pltpu.VMEM((LANE, LANE), F32),
                        pltpu.VMEM((8, LANE), F32),
                        pltpu.VMEM((8, lora_w), F32),
                        pltpu.VMEM((n_chunks, RW_CHUNK, LANE), BF16),
                        pltpu.VMEM((t_rows, LANE), F32),
                        pltpu.VMEM((n_chunks, LANE, LANE), BF16),
                        pltpu.VMEM((n_chunks, LANE, LANE), F32),
                        pltpu.VMEM((n_chunks, 8, LANE), F32)],
        compiler_params=_params(("arbitrary", "arbitrary", "arbitrary")),
        name="rwkv7",
    )(z, z, z, z, pvec, mu_l, w2p, a2p, g2p, side)


def _moba_kernel(q_ref, k_ref, v_ref, side_ref, o_ref, side_o_ref, vt_scr, s_scr, p_scr, *, n_heads):
    h = pl.program_id(1)
    side_o_ref[...] = side_ref[...].astype(BF16)
    s_len = q_ref.shape[1]
    dh = q_ref.shape[2]
    blk = MOBA_BLOCK
    nb = s_len // blk
    log2e = 1.4426950408889634
    scale = dh ** -0.5 * log2e
    neg_inf = -jnp.inf

    kmean = []
    for j in range(nb):
        rows = slice(j * blk, (j + 1) * blk)
        kmean.append(jnp.mean(k_ref[0, rows, :].astype(F32), axis=0, keepdims=True))
        vt_scr[:, rows] = v_ref[0, rows, :].astype(F32).T.astype(BF16)
    kmean = jnp.concatenate(kmean, axis=0)
    kmean_parts = []
    for _ in range(3):
        part = kmean.astype(BF16)
        kmean_parts.append(part)
        kmean = kmean - part.astype(F32)

    slope = log2e * jnp.exp(jnp.full((1, blk), -8.0 / n_heads * 0.6931471805599453, F32)
                            * (h + 1).astype(F32))
    jk = lax.broadcasted_iota(jnp.int32, (blk, blk), 0)
    iq = lax.broadcasted_iota(jnp.int32, (blk, blk), 1)
    bias_tile = slope * jk.astype(F32)
    bias_own = jnp.where(iq >= jk, bias_tile, neg_inf)
    blk_id = lax.broadcasted_iota(jnp.int32, (nb, 1), 0)

    def scores(qb):
        q = q_ref[0, qb * blk:(qb + 1) * blk, :]
        q_s = (q.astype(F32) * scale).astype(BF16)
        gate = sum(lax.dot_general(part, q, NT_DIMS, preferred_element_type=F32)
                   for part in kmean_parts)
        past = blk_id < qb
        offs = []
        m = None
        for n in range(qb + 1):
            k_n = k_ref[0, n * blk:(n + 1) * blk, :]
            s = lax.dot_general(k_n, q_s, NT_DIMS, preferred_element_type=F32)
            if n == qb:
                t = s + bias_own
                off = jnp.zeros((1, blk), F32)
            else:
                t = s + bias_tile
                g_n = gate[n:n + 1, :]
                beats = past & ((gate > g_n) | ((gate == g_n) & (blk_id < n)))
                rank = jnp.sum(jnp.where(beats, 1.0, 0.0), axis=0, keepdims=True)
                off = jnp.where(rank < float(MOBA_TOPK), slope * float((n - qb) * blk), neg_inf)
            s_scr[qb % 2, n] = t
            offs.append(off)
            cmax = jnp.max(t, axis=0, keepdims=True) + off
            m = cmax if m is None else jnp.maximum(m, cmax)
        return m, offs

    def attend(qb, m, offs):
        l = jnp.zeros((1, blk), F32)
        for n in range(qb + 1):
            p = jnp.exp2(s_scr[qb % 2, n] - (m - offs[n]))
            l = l + jnp.sum(p, axis=0, keepdims=True)
            p_scr[qb % 2, n * blk:(n + 1) * blk, :] = p.astype(BF16)
        kk = (qb + 1) * blk
        acc = jnp.dot(vt_scr[:, :kk], p_scr[qb % 2, :kk, :], preferred_element_type=F32)
        o_ref[0, qb * blk:(qb + 1) * blk, :] = (acc / l).T.astype(o_ref.dtype)

    pending = scores(0)
    for qb in range(nb):
        nxt = scores(qb + 1) if qb + 1 < nb else None
        attend(qb, *pending)
        pending = nxt


def _moba(qkv, n_heads, side, side_lead):
    bsz, s, d3 = qkv.shape
    d = d3 // 3
    dh = d // n_heads
    blk = MOBA_BLOCK
    nb = s // blk
    side_in, side_out = _slab_specs(side, side_lead, 0, ROW_CHUNK, bsz * n_heads,
                                    lambda b, h: b * n_heads + h)
    kern = functools.partial(_moba_kernel, n_heads=n_heads)
    return pl.pallas_call(
        kern,
        grid=(bsz, n_heads),
        in_specs=[pl.BlockSpec((1, s, dh), lambda b, h: (b, 0, h)),
                  pl.BlockSpec((1, s, dh), lambda b, h: (b, 0, n_heads + h)),
                  pl.BlockSpec((1, s, dh), lambda b, h: (b, 0, 2 * n_heads + h)),
                  side_in],
        out_specs=(pl.BlockSpec((1, s, dh), lambda b, h: (b, 0, h)), side_out),
        out_shape=(jax.ShapeDtypeStruct((bsz, s, d), BF16),
                   jax.ShapeDtypeStruct(side.shape[1:], BF16)),
        scratch_shapes=[pltpu.VMEM((dh, s), BF16),
                        pltpu.VMEM((2, nb, blk, blk), F32),
                        pltpu.VMEM((2, s, blk), BF16)],
        compiler_params=_params(("arbitrary", "arbitrary")),
        name="moba",
    )(qkv, qkv, qkv, side)


def _pad_cols(w, n):
    return jnp.pad(w, ((0, 0), (0, n - w.shape[1])))


def _pad_rows(w, n, before=0):
    return jnp.pad(w, ((before, n - before - w.shape[0]), (0, 0)))


def kernel(x, c, w_ada, b_ada, g_pre_mix, g_post_mix, g_pre_ffn, g_post_ffn, w_ffn_in, w_ffn_out,
           w_in_ab, w_out_ab, a_v_gain, a_v_bias, a_w_s, a_b_s, b_mu, b_w0, b_w2, b_a0, b_a2, b_g2,
           b_k_k, b_k_a, b_r_k, b_lnx_gain, b_lnx_bias, w_qkv, w_o):
    bsz, s, d = x.shape
    depth = w_ada.shape[0]
    a_width = a_v_gain.shape[1]
    b_width = b_w0.shape[1]
    n_lw = b_w2.shape[1]
    n_la = b_a2.shape[1]
    n_lg = b_g2.shape[1]
    n_heads = d // ATT_HEAD
    assert s % MOBA_BLOCK == 0 and s % 1024 == 0
    assert n_lw + n_la <= LANE and n_lg <= 2 * LANE

    mod = _ada_mod(c, w_ada, b_ada)

    for layer in range(depth):
        mod3 = mod[layer].reshape(bsz, 1, 6 * d)
        i = layer // 2
        if layer % 2 == 0:
            lora_w = 4 * LANE
            nz = 2 * a_width + 3 * b_width + lora_w
            w_in = _pad_cols(w_in_ab[i], nz)
            z, w_ffn_in_h = _norm_mm(x, g_pre_mix[layer], mod3, 1, 0, w_in, BF16, w_ffn_in, layer,
                                     name="in_proj_ab")
            y_a = _mixer_a(z, a_v_gain[i], a_v_bias[i], a_w_s[i], a_b_s[i], a_width)

            mu = b_mu[i]
            pvec = jnp.stack([mu[0:b_width], mu[b_width:2 * b_width], mu[2 * b_width:3 * b_width],
                              b_w0[i], b_a0[i], b_k_k[i], b_k_a[i], b_r_k[i].reshape(-1),
                              b_lnx_gain[i], b_lnx_bias[i]])
            pvec = _pad_rows(pvec, _PV_ROWS)
            mu_l = _pad_cols(mu[3 * b_width:].reshape(1, -1), lora_w)
            w2p = _pad_rows(b_w2[i], LANE)
            a2p = _pad_rows(b_a2[i], LANE, before=n_lw)
            g2p = _pad_rows(b_g2[i], 2 * LANE)
            cb = 2 * a_width // LANE
            nb_w = b_width // LANE
            y_b, w_out_h = _rwkv(z, cb, cb + nb_w, cb + 2 * nb_w,
                                 (2 * a_width + 3 * b_width) // lora_w, lora_w, pvec, mu_l, w2p, a2p,
                                 g2p, b_width, w_out_ab, i)
            x, w_ffn_out_h = _out_proj(y_a, y_b, 0, 0, w_out_h, x, mod3, g_post_mix[layer],
                                       w_ffn_out, layer)
        else:
            qkv, w_ffn_in_h = _norm_mm(x, g_pre_mix[layer], mod3, 1, 0, w_qkv[i], BF16, w_ffn_in,
                                       layer, name="qkv_proj")
            o, w_o_h = _moba(qkv, n_heads, w_o, i)
            x, w_ffn_out_h = _out_proj(o, o, 0, 1, w_o_h, x, mod3, g_post_mix[layer],
                                       w_ffn_out, layer)
        x = _ffn(x, g_pre_ffn[layer], g_post_ffn[layer], mod3, w_ffn_in_h, w_ffn_out_h)
    return x
```

```python
import functools

import jax
import jax.numpy as jnp
from jax import lax
from jax.experimental import pallas as pl
from jax.experimental.pallas import tpu as pltpu

F32 = jnp.float32
BF16 = jnp.bfloat16

NORM_EPS = 1e-6
LN_EPS = 1e-5
GN_EPS = 64e-5

LANE = 128
A_GROUPS = 8
A_CHUNK = 128
RW_HEAD = 64
RW_CHUNK = 64
RW_PACK = 2
MOBA_BLOCK = 256
MOBA_TOPK = 3
ATT_HEAD = 128

NT_DIMS = (((1,), (1,)), ((), ()))
TN_DIMS = (((0,), (0,)), ((), ()))

VMEM_LIMIT = 56 * 1024 * 1024


def _params(sem):
    return pltpu.CompilerParams(dimension_semantics=sem, vmem_limit_bytes=VMEM_LIMIT)


def _slab_specs(side, lead, axis, unit, n_steps, flat_index):
    total = side.shape[1 + axis]
    n_slab = max(n for n in range(1, n_steps + 1) if total % (n * unit) == 0)
    shape = list(side.shape[1:])
    shape[axis] = total // n_slab

    def idx(*grid):
        slab = jnp.minimum(flat_index(*grid), n_slab - 1)
        return (slab, 0) if axis == 0 else (0, slab)

    return (pl.BlockSpec((None,) + tuple(shape), lambda *g: (lead,) + idx(*g)),
            pl.BlockSpec(tuple(shape), idx))


def _rms(x, gain):
    ms = jnp.mean(x * x, axis=-1, keepdims=True)
    return x * lax.rsqrt(ms + NORM_EPS) * gain


ROW_CHUNK = 16
ROW_UNROLL = 8


def _for_row_chunks(n_rows, fn):
    def body(i, carry):
        fn(pl.ds(pl.multiple_of(i * ROW_CHUNK, ROW_CHUNK), ROW_CHUNK))
        return carry
    lax.fori_loop(0, n_rows // ROW_CHUNK, body, 0, unroll=ROW_UNROLL)


def _modulated_norm(x_ref, g_ref, sc_ref, sh_ref, gm_scr, h_scr):
    gm_scr[...] = g_ref[...] * (1.0 + sc_ref[0])

    def rows(r):
        x = x_ref[0, r, :]
        ms = jnp.mean(x * x, axis=-1, keepdims=True)
        h_scr[r, :] = (x * lax.rsqrt(ms + NORM_EPS) * gm_scr[...] + sh_ref[0]).astype(BF16)

    _for_row_chunks(h_scr.shape[0], rows)


def _post_norm_residual(y_scr, x_ref, gt_ref, gpost_ref, gm_scr, o_ref):
    gm_scr[...] = gt_ref[0] * gpost_ref[...]

    def rows(r):
        y = y_scr[r, :]
        ms = jnp.mean(y * y, axis=-1, keepdims=True)
        o_ref[0, r, :] = x_ref[0, r, :] + y * lax.rsqrt(ms + NORM_EPS) * gm_scr[...]

    _for_row_chunks(o_ref.shape[1], rows)


def _ada_kernel(c_ref, w_ref, b_ref, o_ref):
    c = c_ref[...]
    cond = (c * jax.nn.sigmoid(c)).astype(BF16)
    o_ref[0] = jnp.dot(cond, w_ref[0].astype(BF16), preferred_element_type=F32) + b_ref[0]


def _ada_mod(c, w_ada, b_ada):
    depth, d, n = w_ada.shape
    bsz = c.shape[0]
    bp = 8
    c_p = jnp.pad(c, ((0, bp - bsz), (0, 0)))
    tn = 1024
    out = pl.pallas_call(
        _ada_kernel,
        grid=(depth, n // tn),
        in_specs=[pl.BlockSpec((bp, d), lambda l, j: (0, 0)),
                  pl.BlockSpec((1, d, tn), lambda l, j: (l, 0, j)),
                  pl.BlockSpec((1, 1, tn), lambda l, j: (l, 0, j))],
        out_specs=pl.BlockSpec((1, bp, tn), lambda l, j: (l, 0, j)),
        out_shape=jax.ShapeDtypeStruct((depth, bp, n), F32),
        compiler_params=_params(("parallel", "parallel")),
        name="ada_mod",
    )(c_p, w_ada, b_ada.reshape(depth, 1, n))
    return out[:, :bsz]


def _norm_mm_kernel(*refs, n_main, transposed):
    if n_main is None:
        x_ref, g_ref, sc_ref, sh_ref, w_ref, side_ref, o_ref, side_o_ref, h_scr, gm_scr = refs
        wt_ref = None
    else:
        x_ref, g_ref, sc_ref, sh_ref, w_ref, wt_ref, side_ref, o_ref, side_o_ref, h_scr, gm_scr = refs
    j = pl.program_id(2)

    @pl.when(j == 0)
    def _():
        _modulated_norm(x_ref, g_ref, sc_ref, sh_ref, gm_scr, h_scr)

    def project(wr):
        w = wr[...].astype(BF16)
        if transposed:
            y = lax.dot_general(h_scr[...], w, NT_DIMS, preferred_element_type=F32)
        else:
            y = jnp.dot(h_scr[...], w, preferred_element_type=F32)
        o_ref[0] = y.astype(o_ref.dtype)

    if n_main is None:
        project(w_ref)
    else:
        pl.when(j < n_main)(lambda: project(w_ref))
        pl.when(j >= n_main)(lambda: project(wt_ref))
    side_o_ref[...] = side_ref[...].astype(BF16)


def _norm_mm(x, gain, mod3, sc_idx, sh_idx, w, out_dtype, side, side_lead, w_tail=None,
             transposed=False, tm=1024, tn=512, name="norm_mm"):
    bsz, s, d = x.shape
    n_w = w.shape[0] if transposed else w.shape[1]
    n_main = None if w_tail is None else n_w // tn
    n = n_w if w_tail is None else (n_main + 1) * tn
    nm, nj = s // tm, n // tn
    side_in, side_out = _slab_specs(side, side_lead, 1, LANE, bsz * nm * nj,
                                    lambda b, m, j: (b * nm + m) * nj + j)
    last = nj - 1 if n_main is None else n_main - 1
    if transposed:
        w_specs = [pl.BlockSpec((tn, d), lambda b, m, j: (jnp.minimum(j, last), 0))]
        tail_spec = pl.BlockSpec((tn, d), lambda b, m, j: (0, 0), pipeline_mode=pl.Buffered(1))
    else:
        w_specs = [pl.BlockSpec((d, tn), lambda b, m, j: (0, jnp.minimum(j, last)))]
        tail_spec = pl.BlockSpec((d, tn), lambda b, m, j: (0, 0), pipeline_mode=pl.Buffered(1))
    w_args = [w]
    if w_tail is not None:
        w_specs.append(tail_spec)
        w_args.append(w_tail)
    return pl.pallas_call(
        functools.partial(_norm_mm_kernel, n_main=n_main, transposed=transposed),
        grid=(bsz, nm, nj),
        in_specs=[pl.BlockSpec((1, tm, d), lambda b, m, j: (b, m, 0)),
                  pl.BlockSpec((1, d), lambda b, m, j: (0, 0)),
                  pl.BlockSpec((1, 1, d), lambda b, m, j: (b, 0, sc_idx)),
                  pl.BlockSpec((1, 1, d), lambda b, m, j: (b, 0, sh_idx)),
                  *w_specs,
                  side_in],
        out_specs=(pl.BlockSpec((1, tm, tn), lambda b, m, j: (b, m, j)), side_out),
        out_shape=(jax.ShapeDtypeStruct((bsz, s, n), out_dtype),
                   jax.ShapeDtypeStruct(side.shape[1:], BF16)),
        scratch_shapes=[pltpu.VMEM((tm, d), BF16), pltpu.VMEM((1, d), F32)],
        compiler_params=_params(("arbitrary", "arbitrary", "arbitrary")),
        name=name,
    )(x, gain.reshape(1, d), mod3, mod3, *w_args, side)


def _ffn_kernel(*refs, has_side):
    if has_side:
        (x_ref, gpre_ref, sc_ref, sh_ref, gt_ref, gpost_ref, wg_ref, wu_ref, wo_ref, side_ref,
         o_ref, side_o_ref, h_scr, gm_scr, acc_scr) = refs
        side_o_ref[...] = side_ref[...].astype(BF16)
    else:
        (x_ref, gpre_ref, sc_ref, sh_ref, gt_ref, gpost_ref, wg_ref, wu_ref, wo_ref,
         o_ref, h_scr, gm_scr, acc_scr) = refs
    f = pl.program_id(2)

    @pl.when(f == 0)
    def _():
        _modulated_norm(x_ref, gpre_ref, sc_ref, sh_ref, gm_scr, h_scr)
        acc_scr[...] = jnp.zeros_like(acc_scr)

    h = h_scr[...]
    g = jnp.dot(h, wg_ref[...], preferred_element_type=F32)
    u = jnp.dot(h, wu_ref[...], preferred_element_type=F32)
    a = (g * jax.nn.sigmoid(g) * u).astype(BF16)
    acc_scr[...] += jnp.dot(a, wo_ref[...], preferred_element_type=F32)

    @pl.when(f == pl.num_programs(2) - 1)
    def _():
        _post_norm_residual(acc_scr, x_ref, gt_ref, gpost_ref, gm_scr, o_ref)


def _ffn(x, gpre, gpost, mod3, w_in, w_out, side=None, side_lead=0, tm=1024, tf=512):
    bsz, s, d = x.shape
    fh = w_out.shape[0]
    nf = fh // tf
    nm = s // tm
    out_spec = pl.BlockSpec((1, tm, d), lambda b, m, f: (b, m, 0), pipeline_mode=pl.Buffered(1))
    out_shape = jax.ShapeDtypeStruct((bsz, s, d), F32)
    side_specs, side_args = [], []
    if side is not None:
        side_in, side_out = _slab_specs(side, side_lead, 1, LANE, bsz * nm * nf,
                                        lambda b, m, f: (b * nm + m) * nf + f)
        side_specs, side_args = [side_in], [side]
        out_spec = (out_spec, side_out)
        out_shape = (out_shape, jax.ShapeDtypeStruct(side.shape[1:], BF16))
    return pl.pallas_call(
        functools.partial(_ffn_kernel, has_side=side is not None),
        grid=(bsz, nm, nf),
        in_specs=[pl.BlockSpec((1, tm, d), lambda b, m, f: (b, m, 0), pipeline_mode=pl.Buffered(1)),
                  pl.BlockSpec((1, d), lambda b, m, f: (0, 0)),
                  pl.BlockSpec((1, 1, d), lambda b, m, f: (b, 0, 4)),
                  pl.BlockSpec((1, 1, d), lambda b, m, f: (b, 0, 3)),
                  pl.BlockSpec((1, 1, d), lambda b, m, f: (b, 0, 5)),
                  pl.BlockSpec((1, d), lambda b, m, f: (0, 0)),
                  pl.BlockSpec((d, tf), lambda b, m, f: (0, f)),
                  pl.BlockSpec((d, tf), lambda b, m, f: (0, nf + f)),
                  pl.BlockSpec((tf, d), lambda b, m, f: (f, 0)),
                  *side_specs],
        out_specs=out_spec,
        out_shape=out_shape,
        scratch_shapes=[pltpu.VMEM((tm, d), BF16), pltpu.VMEM((1, d), F32), pltpu.VMEM((tm, d), F32)],
        compiler_params=_params(("arbitrary", "arbitrary", "arbitrary")),
        name="ffn",
    )(x, gpre.reshape(1, d), mod3, mod3, mod3, gpost.reshape(1, d), w_in, w_in, w_out, *side_args)


def _out_proj_kernel(a0_ref, a1_ref, w0_ref, w1_ref, x_ref, gt_ref, gpost_ref, side_ref,
                     o_ref, side_o_ref, gm_scr, y_scr):
    side_o_ref[...] = side_ref[...].astype(BF16)
    y_scr[...] = (jnp.dot(a0_ref[0], w0_ref[...], preferred_element_type=F32)
                  + jnp.dot(a1_ref[0], w1_ref[...], preferred_element_type=F32))
    _post_norm_residual(y_scr, x_ref, gt_ref, gpost_ref, gm_scr, o_ref)


def _out_proj(a0, a1, col0, col1, w, x, mod3, gpost, side, side_lead, tm=512):
    bsz, s, d = x.shape
    kh = w.shape[0] // 2
    nm = s // tm
    side_in, side_out = _slab_specs(side, side_lead, 0, ROW_CHUNK, bsz * nm, lambda b, m: b * nm + m)
    return pl.pallas_call(
        _out_proj_kernel,
        grid=(bsz, nm),
        in_specs=[pl.BlockSpec((1, tm, kh), lambda b, m: (b, m, col0)),
                  pl.BlockSpec((1, tm, kh), lambda b, m: (b, m, col1)),
                  pl.BlockSpec((kh, d), lambda b, m: (0, 0)),
                  pl.BlockSpec((kh, d), lambda b, m: (1, 0)),
                  pl.BlockSpec((1, tm, d), lambda b, m: (b, m, 0)),
                  pl.BlockSpec((1, 1, d), lambda b, m: (b, 0, 2)),
                  pl.BlockSpec((1, d), lambda b, m: (0, 0)),
                  side_in],
        out_specs=(pl.BlockSpec((1, tm, d), lambda b, m: (b, m, 0)), side_out),
        out_shape=(jax.ShapeDtypeStruct((bsz, s, d), F32),
                   jax.ShapeDtypeStruct(side.shape[1:], BF16)),
        scratch_shapes=[pltpu.VMEM((1, d), F32), pltpu.VMEM((tm, d), F32)],
        compiler_params=_params(("arbitrary", "arbitrary")),
        name="out_proj",
    )(a0, a1, w, w, x, mod3, gpost.reshape(1, d), side)


def _mixer_a_kernel(z_ref, vg_ref, vb_ref, ws_ref, bst_ref, o_ref):
    z = jax.nn.gelu(z_ref[0].astype(F32))
    wdt = z.shape[1] // 2
    u = z[:, :wdt]
    v = z[:, wdt:]
    mu = jnp.mean(v, axis=-1, keepdims=True)
    dv = v - mu
    var = jnp.mean(dv * dv, axis=-1, keepdims=True)
    vn = (dv * lax.rsqrt(var + LN_EPS) * vg_ref[...] + vb_ref[...]).astype(BF16)
    ch = z.shape[0]
    causal = (lax.broadcasted_iota(jnp.int32, (ch, ch), 0)
              >= lax.broadcasted_iota(jnp.int32, (ch, ch), 1))
    gd = wdt // A_GROUPS
    for g in range(A_GROUPS):
        w = jnp.where(causal, ws_ref[g], 0.0).astype(BF16)
        sv = jnp.dot(w, vn[:, g * gd:(g + 1) * gd], preferred_element_type=F32)
        sv = sv + bst_ref[:, g:g + 1]
        o_ref[0, :, g * gd:(g + 1) * gd] = (u[:, g * gd:(g + 1) * gd] * sv).astype(o_ref.dtype)


def _mixer_a(z, v_gain, v_bias, w_s, b_s, width):
    bsz, s, _ = z.shape
    ch = A_CHUNK
    return pl.pallas_call(
        _mixer_a_kernel,
        grid=(bsz, s // ch),
        in_specs=[pl.BlockSpec((1, ch, 2 * width), lambda b, c: (b, c, 0)),
                  pl.BlockSpec((1, width), lambda b, c: (0, 0)),
                  pl.BlockSpec((1, width), lambda b, c: (0, 0)),
                  pl.BlockSpec((A_GROUPS, ch, ch), lambda b, c: (0, 0, 0)),
                  pl.BlockSpec((ch, A_GROUPS), lambda b, c: (0, 0))],
        out_specs=pl.BlockSpec((1, ch, width), lambda b, c: (b, c, 0)),
        out_shape=jax.ShapeDtypeStruct((bsz, s, width), BF16),
        compiler_params=_params(("parallel", "parallel")),
        name="mixer_a",
    )(z, v_gain.reshape(1, width), v_bias.reshape(1, width), w_s, b_s.T)


_PV_MU_R, _PV_MU_K, _PV_MU_V, _PV_W0, _PV_A0, _PV_KK, _PV_KA, _PV_RK, _PV_LG, _PV_LB = range(10)
_PV_ROWS = 16


def _shift_lerp(x, prev_row, mu):
    rolled = pltpu.roll(x, 1, axis=0)
    first = lax.broadcasted_iota(jnp.int32, x.shape, 0) == 0
    xp = jnp.where(first, prev_row, rolled)
    return x + mu * (xp - x)


def _split_bf16(x):
    hi = x.astype(BF16)
    lo = (x - hi.astype(F32)).astype(BF16)
    return hi, lo


def _mm(x, y):
    return jnp.dot(x.astype(BF16), y.astype(BF16), preferred_element_type=F32)


def _mm_nt(x, y):
    return lax.dot_general(x.astype(BF16), y.astype(BF16), NT_DIMS, preferred_element_type=F32)


def _mm_tn(x, y):
    return lax.dot_general(x.astype(BF16), y.astype(BF16), TN_DIMS, preferred_element_type=F32)


def _mm_x3(x, y):
    xh, xl = _split_bf16(x)
    yh, yl = _split_bf16(y)
    return (jnp.dot(xh, yh, preferred_element_type=F32) + jnp.dot(xl, yh, preferred_element_type=F32)
            + jnp.dot(xh, yl, preferred_element_type=F32))


def _mm_exact_rhs(x, e_bf16):
    xh, xl = _split_bf16(x)
    return (jnp.dot(xh, e_bf16, preferred_element_type=F32)
            + jnp.dot(xl, e_bf16, preferred_element_type=F32))


def _rwkv_kernel(zr_ref, zk_ref, zv_ref, zl_ref, pv_ref, mul_ref, w2_ref, a2_ref, g2_ref, side_ref,
                 o_ref, side_o_ref, s_scr, prev_scr, prevl_scr, q_scr, y_scr, gm_scr, cm_scr, pl_scr, *, chunk):
    tb = pl.program_id(2)
    t_rows = zr_ref.shape[1]
    lanes = zr_ref.shape[2]
    L = chunk
    side_o_ref[...] = side_ref[...].astype(BF16)
    SL = RW_PACK * L
    n_chunks = t_rows // L

    @pl.when(tb == 0)
    def _():
        s_scr[...] = jnp.zeros_like(s_scr)
        prev_scr[...] = jnp.zeros_like(prev_scr)
        prevl_scr[...] = jnp.zeros_like(prevl_scr)

    def pv(i):
        return pv_ref[i:i + 1, :]

    zr = zr_ref[0].astype(F32)
    zk = zk_ref[0].astype(F32)
    zv = zv_ref[0].astype(F32)
    zl = zl_ref[0].astype(F32)
    r = _shift_lerp(zr, prev_scr[0:1, :], pv(_PV_MU_R))
    k = _shift_lerp(zk, prev_scr[1:2, :], pv(_PV_MU_K))
    v = _shift_lerp(zv, prev_scr[2:3, :], pv(_PV_MU_V))
    zls = _shift_lerp(zl, prevl_scr[0:1, :], mul_ref[...])
    prev_scr[0:1, :] = zr[t_rows - 1:t_rows, :]
    prev_scr[1:2, :] = zk[t_rows - 1:t_rows, :]
    prev_scr[2:3, :] = zv[t_rows - 1:t_rows, :]
    prevl_scr[0:1, :] = zl[t_rows - 1:t_rows, :]

    x_wa = zls[:, :LANE]
    x_g = zls[:, LANE:3 * LANE]
    w_pre = pv(_PV_W0) + _mm_x3(jnp.tanh(x_wa), w2_ref[...])
    t = -w_pre
    softplus = jnp.maximum(t, 0.0) + jnp.log1p(jnp.exp(-jnp.abs(t)))
    log_decay = -jnp.exp(-softplus - 0.5)
    a = jax.nn.sigmoid(pv(_PV_A0) + _mm_x3(x_wa, a2_ref[...]))
    g = _mm(jax.nn.sigmoid(x_g), g2_ref[...])

    li = lax.broadcasted_iota(jnp.int32, (lanes, lanes), 0) // RW_HEAD
    lj = lax.broadcasted_iota(jnp.int32, (lanes, lanes), 1) // RW_HEAD
    same_head = li == lj
    e_head = jnp.where(same_head, 1.0, 0.0).astype(BF16)

    kk = k * pv(_PV_KK)
    kk = kk / jnp.maximum(jnp.sqrt(_mm_exact_rhs(kk * kk, e_head)), 1e-12)
    k = k * (1.0 + (a - 1.0) * pv(_PV_KA))
    av = -kk
    bv = kk * a

    ri = lax.broadcasted_iota(jnp.int32, (t_rows, t_rows), 0)
    ci = lax.broadcasted_iota(jnp.int32, (t_rows, t_rows), 1)
    tri_bd = jnp.where((ri >= ci) & ((ri // L) == (ci // L)), 1.0, 0.0).astype(BF16)
    lw_hi, lw_lo = _split_bf16(log_decay)
    cum = (jnp.dot(tri_bd, lw_hi, preferred_element_type=F32)
           + jnp.dot(tri_bd, lw_lo, preferred_element_type=F32))

    si = lax.broadcasted_iota(jnp.int32, (SL, SL), 0)
    sj = lax.broadcasted_iota(jnp.int32, (SL, SL), 1)
    same_blk = (si // L) == (sj // L)
    m_strict = same_blk & (si > sj)
    m_incl = same_blk & (si >= sj)
    eye = jnp.where(si == sj, 1.0, 0.0)
    lane_head = lax.broadcasted_iota(jnp.int32, (1, lanes), 1) // RW_HEAD
    n_sq = max(L.bit_length() - 2, 0)

    def stack(x):
        return jnp.concatenate([jnp.where(lane_head == h, x, 0.0) for h in range(RW_PACK)], axis=0)

    def unstack(x):
        out = x[0:L]
        for h in range(1, RW_PACK):
            out = out + x[h * L:(h + 1) * L]
        return out

    cs = range(n_chunks)
    rows = [slice(c * L, (c + 1) * L) for c in cs]
    cm = [cum[rw] for rw in rows]
    cm_last = [x[L - 1:L, :] for x in cm]
    r_t = [r[rw] * jnp.exp(cm[c]) for c, rw in enumerate(rows)]
    a_st = [stack(av[rw] * jnp.exp(cm[c] - log_decay[rw])).astype(BF16) for c, rw in enumerate(rows)]
    e_neg = [jnp.exp(-x) for x in cm]
    k_rep = [jnp.concatenate([(k[rw] * e_neg[c]).astype(BF16)] * RW_PACK, axis=0)
             for c, rw in enumerate(rows)]
    b_rep = [jnp.concatenate([(bv[rw] * e_neg[c]).astype(BF16)] * RW_PACK, axis=0)
             for c, rw in enumerate(rows)]
    ar_st = [jnp.concatenate([a_st[c], stack(r_t[c]).astype(BF16)], axis=0) for c in cs]
    p_k = [_mm_nt(ar_st[c], k_rep[c]) for c in cs]
    p_b = [_mm_nt(ar_st[c], b_rep[c]) for c in cs]
    a_ak = [jnp.where(m_strict, x[:SL], 0.0).astype(BF16) for x in p_k]
    a_rk = [jnp.where(m_incl, x[SL:], 0.0).astype(BF16) for x in p_k]
    a_ab = [jnp.where(m_strict, x[:SL], 0.0) for x in p_b]
    a_rb = [jnp.where(m_incl, x[SL:], 0.0).astype(BF16) for x in p_b]

    xp = [x.astype(BF16) for x in a_ab]
    tinv = [eye + x for x in a_ab]
    for _ in range(n_sq):
        xp = [_mm(x, x).astype(BF16) for x in xp]
        tinv = [tinv[c] + _mm(tinv[c], xp[c]) for c in cs]
    tinv = [x.astype(BF16) for x in tinv]

    v_st = [stack(v[rw]).astype(BF16) for rw in rows]
    x0 = [_mm(a_ak[c], v_st[c]).astype(BF16) for c in cs]
    wu = [_mm(tinv[c], jnp.concatenate([a_st[c], x0[c]], axis=1)) for c in cs]
    yq = [_mm(a_rb[c], wu[c]) for c in cs]
    y0 = [_mm(a_rk[c], v_st[c]) for c in cs]
    for c, rw in enumerate(rows):
        q_scr[c] = (r_t[c] + unstack(yq[c][:, :lanes])).astype(BF16)
        y_scr[rw, :] = unstack(yq[c][:, lanes:] + y0[c])
    for c, rw in enumerate(rows):
        e_rem = jnp.exp(cm_last[c] - cm[c])
        b_b = bv[rw] * e_rem
        kb = jnp.concatenate([k[rw] * e_rem, b_b], axis=0)
        vu = jnp.concatenate([v[rw], unstack(wu[c][:, lanes:])], axis=0)
        cm_scr[c] = jnp.where(same_head, _mm_tn(vu, kb), 0.0)
        gm_scr[c] = jnp.where(same_head, _mm_tn(unstack(wu[c][:, :lanes]), b_b), 0.0).astype(BF16)
        pl_scr[c] = jnp.broadcast_to(jnp.exp(cm_last[c]), (8, lanes))

    s0 = s_scr[...]
    for c in range(n_chunks):
        lo, hi = c * L, (c + 1) * L
        s0b = s0.astype(BF16)
        y_scr[lo:hi, :] = y_scr[lo:hi, :] + _mm_nt(q_scr[c], s0b)
        s0 = s0 * pl_scr[c][0:1, :] + _mm(s0b, gm_scr[c]) + cm_scr[c]
    s_scr[...] = s0

    y = y_scr[...]
    inv_n = 1.0 / RW_HEAD
    mean = _mm_exact_rhs(y, e_head) * inv_n
    dy = y - mean
    var = _mm_exact_rhs(dy * dy, e_head) * inv_n
    yn = dy * lax.rsqrt(var + GN_EPS) * pv(_PV_LG) + pv(_PV_LB)
    bonus = _mm_exact_rhs(r * k * pv(_PV_RK), e_head) * v
    o_ref[0] = ((yn + bonus) * g).astype(o_ref.dtype)


def _rwkv(z, col_r, col_k, col_v, col_l, lora_w, pvec, mu_l, w2p, a2p, g2p, width, side, side_lead,
          t_rows=512):
    bsz, s, _ = z.shape
    n_blk = width // LANE
    n_chunks = t_rows // RW_CHUNK
    nt = s // t_rows
    side_in, side_out = _slab_specs(side, side_lead, 0, ROW_CHUNK, bsz * n_blk * nt,
                                    lambda b, p, t: (b * n_blk + p) * nt + t)
    kern = functools.partial(_rwkv_kernel, chunk=RW_CHUNK)
    return pl.pallas_call(
        kern,
        grid=(bsz, n_blk, s // t_rows),
        in_specs=[pl.BlockSpec((1, t_rows, LANE), lambda b, p, t: (b, t, col_r + p)),
                  pl.BlockSpec((1, t_rows, LANE), lambda b, p, t: (b, t, col_k + p)),
                  pl.BlockSpec((1, t_rows, LANE), lambda b, p, t: (b, t, col_v + p)),
                  pl.BlockSpec((1, t_rows, lora_w), lambda b, p, t: (b, t, col_l)),
                  pl.BlockSpec((_PV_ROWS, LANE), lambda b, p, t: (0, p)),
                  pl.BlockSpec((1, lora_w), lambda b, p, t: (0, 0)),
                  pl.BlockSpec((LANE, LANE), lambda b, p, t: (0, p)),
                  pl.BlockSpec((LANE, LANE), lambda b, p, t: (0, p)),
                  pl.BlockSpec((2 * LANE, LANE), lambda b, p, t: (0, p)),
                  side_in],
        out_specs=(pl.BlockSpec((1, t_rows, LANE), lambda b, p, t: (b, t, p)), side_out),
        out_shape=(jax.ShapeDtypeStruct((bsz, s, width), BF16),
                   jax.ShapeDtypeStruct(side.shape[1:], BF16)),
        scratch_shapes=[pltpu.VMEM((LANE, LANE), F32),
                        pltpu.VMEM((8, LANE), F32),
                        pltpu.VMEM((8, lora_w), F32),
                        pltpu.VMEM((n_chunks, RW_CHUNK, LANE), BF16),
                        pltpu.VMEM((t_rows, LANE), F32),
                        pltpu.VMEM((n_chunks, LANE, LANE), BF16),
                        pltpu.VMEM((n_chunks, LANE, LANE), F32),
                        pltpu.VMEM((n_chunks, 8, LANE), F32)],
        compiler_params=_params(("arbitrary", "arbitrary", "arbitrary")),
        name="rwkv7",
    )(z, z, z, z, pvec, mu_l, w2p, a2p, g2p, side)


def _moba_kernel(q_ref, k_ref, v_ref, side_ref, o_ref, side_o_ref, vt_scr, s_scr, p_scr, *, n_heads):
    h = pl.program_id(1)
    side_o_ref[...] = side_ref[...].astype(BF16)
    s_len = q_ref.shape[1]
    dh = q_ref.shape[2]
    blk = MOBA_BLOCK
    nb = s_len // blk
    log2e = 1.4426950408889634
    scale = dh ** -0.5 * log2e
    neg_inf = -jnp.inf

    kmean = []
    for j in range(nb):
        rows = slice(j * blk, (j + 1) * blk)
        kmean.append(jnp.mean(k_ref[0, rows, :].astype(F32), axis=0, keepdims=True))
        vt_scr[:, rows] = v_ref[0, rows, :].astype(F32).T.astype(BF16)
    kmean = jnp.concatenate(kmean, axis=0)
    kmean_parts = []
    for _ in range(3):
        part = kmean.astype(BF16)
        kmean_parts.append(part)
        kmean = kmean - part.astype(F32)

    slope = log2e * jnp.exp(jnp.full((1, blk), -8.0 / n_heads * 0.6931471805599453, F32)
                            * (h + 1).astype(F32))
    jk = lax.broadcasted_iota(jnp.int32, (blk, blk), 0)
    iq = lax.broadcasted_iota(jnp.int32, (blk, blk), 1)
    bias_tile = slope * jk.astype(F32)
    bias_own = jnp.where(iq >= jk, bias_tile, neg_inf)
    blk_id = lax.broadcasted_iota(jnp.int32, (nb, 1), 0)

    def scores(qb):
        q = q_ref[0, qb * blk:(qb + 1) * blk, :]
        q_s = (q.astype(F32) * scale).astype(BF16)
        gate = sum(lax.dot_general(part, q, NT_DIMS, preferred_element_type=F32)
                   for part in kmean_parts)
        past = blk_id < qb
        offs = []
        m = None
        for n in range(qb + 1):
            k_n = k_ref[0, n * blk:(n + 1) * blk, :]
            s = lax.dot_general(k_n, q_s, NT_DIMS, preferred_element_type=F32)
            if n == qb:
                t = s + bias_own
                off = jnp.zeros((1, blk), F32)
            else:
                t = s + bias_tile
                g_n = gate[n:n + 1, :]
                beats = past & ((gate > g_n) | ((gate == g_n) & (blk_id < n)))
                rank = jnp.sum(jnp.where(beats, 1.0, 0.0), axis=0, keepdims=True)
                off = jnp.where(rank < float(MOBA_TOPK), slope * float((n - qb) * blk), neg_inf)
            s_scr[qb % 2, n] = t
            offs.append(off)
            cmax = jnp.max(t, axis=0, keepdims=True) + off
            m = cmax if m is None else jnp.maximum(m, cmax)
        return m, offs

    def attend(qb, m, offs):
        l = jnp.zeros((1, blk), F32)
        for n in range(qb + 1):
            p = jnp.exp2(s_scr[qb % 2, n] - (m - offs[n]))
            l = l + jnp.sum(p, axis=0, keepdims=True)
            p_scr[qb % 2, n * blk:(n + 1) * blk, :] = p.astype(BF16)
        kk = (qb + 1) * blk
        acc = jnp.dot(vt_scr[:, :kk], p_scr[qb % 2, :kk, :], preferred_element_type=F32)
        o_ref[0, qb * blk:(qb + 1) * blk, :] = (acc / l).T.astype(o_ref.dtype)

    pending = scores(0)
    for qb in range(nb):
        nxt = scores(qb + 1) if qb + 1 < nb else None
        attend(qb, *pending)
        pending = nxt


def _moba(qkv, n_heads, side, side_lead):
    bsz, s, d3 = qkv.shape
    d = d3 // 3
    dh = d // n_heads
    blk = MOBA_BLOCK
    nb = s // blk
    side_in, side_out = _slab_specs(side, side_lead, 0, ROW_CHUNK, bsz * n_heads,
                                    lambda b, h: b * n_heads + h)
    kern = functools.partial(_moba_kernel, n_heads=n_heads)
    return pl.pallas_call(
        kern,
        grid=(bsz, n_heads),
        in_specs=[pl.BlockSpec((1, s, dh), lambda b, h: (b, 0, h)),
                  pl.BlockSpec((1, s, dh), lambda b, h: (b, 0, n_heads + h)),
                  pl.BlockSpec((1, s, dh), lambda b, h: (b, 0, 2 * n_heads + h)),
                  side_in],
        out_specs=(pl.BlockSpec((1, s, dh), lambda b, h: (b, 0, h)), side_out),
        out_shape=(jax.ShapeDtypeStruct((bsz, s, d), BF16),
                   jax.ShapeDtypeStruct(side.shape[1:], BF16)),
        scratch_shapes=[pltpu.VMEM((dh, s), BF16),
                        pltpu.VMEM((2, nb, blk, blk), F32),
                        pltpu.VMEM((2, s, blk), BF16)],
        compiler_params=_params(("arbitrary", "arbitrary")),
        name="moba",
    )(qkv, qkv, qkv, side)


def _pad_cols(w, n):
    return jnp.pad(w, ((0, 0), (0, n - w.shape[1])))


def _pad_rows(w, n, before=0):
    return jnp.pad(w, ((before, n - before - w.shape[0]), (0, 0)))


def kernel(x, c, w_ada, b_ada, g_pre_mix, g_post_mix, g_pre_ffn, g_post_ffn, w_ffn_in, w_ffn_out,
           w_in_ab, w_out_ab, a_v_gain, a_v_bias, a_w_s, a_b_s, b_mu, b_w0, b_w2, b_a0, b_a2, b_g2,
           b_k_k, b_k_a, b_r_k, b_lnx_gain, b_lnx_bias, w_qkv, w_o):
    bsz, s, d = x.shape
    depth = w_ada.shape[0]
    a_width = a_v_gain.shape[1]
    b_width = b_w0.shape[1]
    n_lw = b_w2.shape[1]
    n_la = b_a2.shape[1]
    n_lg = b_g2.shape[1]
    n_heads = d // ATT_HEAD
    assert s % MOBA_BLOCK == 0 and s % 1024 == 0
    assert n_lw + n_la <= LANE and n_lg <= 2 * LANE

    mod = _ada_mod(c, w_ada, b_ada)

    w_qkv_h = None
    for layer in range(depth):
        mod3 = mod[layer].reshape(bsz, 1, 6 * d)
        i = layer // 2
        if layer % 2 == 0:
            lora_w = 4 * LANE
            n_main = 2 * a_width + 3 * b_width
            w_t = jnp.swapaxes(w_in_ab[i], 0, 1)
            w_tail = _pad_rows(w_t[n_main:], lora_w)
            z, w_out_h = _norm_mm(x, g_pre_mix[layer], mod3, 1, 0, w_t, BF16, w_out_ab, i,
                                  w_tail=w_tail, transposed=True, tn=lora_w, name="in_proj_ab")
            y_a = _mixer_a(z, a_v_gain[i], a_v_bias[i], a_w_s[i], a_b_s[i], a_width)

            mu = b_mu[i]
            pvec = jnp.stack([mu[0:b_width], mu[b_width:2 * b_width], mu[2 * b_width:3 * b_width],
                              b_w0[i], b_a0[i], b_k_k[i], b_k_a[i], b_r_k[i].reshape(-1),
                              b_lnx_gain[i], b_lnx_bias[i]])
            pvec = _pad_rows(pvec, _PV_ROWS)
            mu_l = _pad_cols(mu[3 * b_width:].reshape(1, -1), lora_w)
            w2p = _pad_rows(b_w2[i], LANE)
            a2p = _pad_rows(b_a2[i], LANE, before=n_lw)
            g2p = _pad_rows(b_g2[i], 2 * LANE)
            cb = 2 * a_width // LANE
            nb_w = b_width // LANE
            y_b, w_ffn_in_h = _rwkv(z, cb, cb + nb_w, cb + 2 * nb_w, n_main // lora_w, lora_w, pvec,
                                    mu_l, w2p, a2p, g2p, b_width, w_ffn_in, layer)
            x, w_ffn_out_h = _out_proj(y_a, y_b, 0, 0, w_out_h, x, mod3, g_post_mix[layer],
                                       w_ffn_out, layer)
        else:
            w_q = w_qkv[i] if w_qkv_h is None else w_qkv_h
            qkv, w_o_h = _norm_mm(x, g_pre_mix[layer], mod3, 1, 0, w_q, BF16, w_o, i,
                                  name="qkv_proj")
            o, w_ffn_in_h = _moba(qkv, n_heads, w_ffn_in, layer)
            x, w_ffn_out_h = _out_proj(o, o, 0, 1, w_o_h, x, mod3, g_post_mix[layer],
                                       w_ffn_out, layer)
        if layer + 1 < depth and (layer + 1) % 2 == 1:
            x, w_qkv_h = _ffn(x, g_pre_ffn[layer], g_post_ffn[layer], mod3, w_ffn_in_h, w_ffn_out_h,
                              w_qkv, (layer + 1) // 2)
        else:
            x = _ffn(x, g_pre_ffn[layer], g_post_ffn[layer], mod3, w_ffn_in_h, w_ffn_out_h)
            w_qkv_h = None
    return x
```

```python
import functools

import jax
import jax.numpy as jnp
from jax import lax
from jax.experimental import pallas as pl
from jax.experimental.pallas import tpu as pltpu

F32 = jnp.float32
BF16 = jnp.bfloat16

NORM_EPS = 1e-6
LN_EPS = 1e-5
GN_EPS = 64e-5

LANE = 128
A_GROUPS = 8
A_CHUNK = 128
RW_HEAD = 64
RW_CHUNK = 64
RW_PACK = 2
MOBA_BLOCK = 256
MOBA_TOPK = 3
ATT_HEAD = 128

NT_DIMS = (((1,), (1,)), ((), ()))
TN_DIMS = (((0,), (0,)), ((), ()))

VMEM_LIMIT = 56 * 1024 * 1024


def _params(sem):
    return pltpu.CompilerParams(dimension_semantics=sem, vmem_limit_bytes=VMEM_LIMIT)


def _slab_specs(side, lead, axis, unit, n_steps, flat_index):
    total = side.shape[1 + axis]
    n_slab = max(n for n in range(1, n_steps + 1) if total % (n * unit) == 0)
    shape = list(side.shape[1:])
    shape[axis] = total // n_slab

    def idx(*grid):
        slab = jnp.minimum(flat_index(*grid), n_slab - 1)
        return (slab, 0) if axis == 0 else (0, slab)

    return (pl.BlockSpec((None,) + tuple(shape), lambda *g: (lead,) + idx(*g)),
            pl.BlockSpec(tuple(shape), idx))


def _rms(x, gain):
    ms = jnp.mean(x * x, axis=-1, keepdims=True)
    return x * lax.rsqrt(ms + NORM_EPS) * gain


ROW_CHUNK = 16
ROW_UNROLL = 8


def _for_row_chunks(n_rows, fn):
    def body(i, carry):
        fn(pl.ds(pl.multiple_of(i * ROW_CHUNK, ROW_CHUNK), ROW_CHUNK))
        return carry
    lax.fori_loop(0, n_rows // ROW_CHUNK, body, 0, unroll=ROW_UNROLL)


def _modulated_norm(x_ref, g_ref, sc_ref, sh_ref, gm_scr, h_scr):
    gm_scr[...] = g_ref[...] * (1.0 + sc_ref[0])

    def rows(r):
        x = x_ref[0, r, :]
        ms = jnp.mean(x * x, axis=-1, keepdims=True)
        h_scr[r, :] = (x * lax.rsqrt(ms + NORM_EPS) * gm_scr[...] + sh_ref[0]).astype(BF16)

    _for_row_chunks(h_scr.shape[0], rows)


def _post_norm_residual(y_scr, x_ref, gt_ref, gpost_ref, gm_scr, o_ref):
    gm_scr[...] = gt_ref[0] * gpost_ref[...]

    def rows(r):
        y = y_scr[r, :]
        ms = jnp.mean(y * y, axis=-1, keepdims=True)
        o_ref[0, r, :] = x_ref[0, r, :] + y * lax.rsqrt(ms + NORM_EPS) * gm_scr[...]

    _for_row_chunks(o_ref.shape[1], rows)


def _ada_kernel(c_ref, w_ref, b_ref, o_ref):
    c = c_ref[...]
    cond = (c * jax.nn.sigmoid(c)).astype(BF16)
    o_ref[0] = jnp.dot(cond, w_ref[0].astype(BF16), preferred_element_type=F32) + b_ref[0]


def _ada_mod(c, w_ada, b_ada):
    depth, d, n = w_ada.shape
    bsz = c.shape[0]
    bp = 8
    c_p = jnp.pad(c, ((0, bp - bsz), (0, 0)))
    tn = 1024
    out = pl.pallas_call(
        _ada_kernel,
        grid=(depth, n // tn),
        in_specs=[pl.BlockSpec((bp, d), lambda l, j: (0, 0)),
                  pl.BlockSpec((1, d, tn), lambda l, j: (l, 0, j)),
                  pl.BlockSpec((1, 1, tn), lambda l, j: (l, 0, j))],
        out_specs=pl.BlockSpec((1, bp, tn), lambda l, j: (l, 0, j)),
        out_shape=jax.ShapeDtypeStruct((depth, bp, n), F32),
        compiler_params=_params(("parallel", "parallel")),
        name="ada_mod",
    )(c_p, w_ada, b_ada.reshape(depth, 1, n))
    return out[:, :bsz]


def _norm_mm_kernel(*refs, n_main, transposed):
    if n_main is None:
        x_ref, g_ref, sc_ref, sh_ref, w_ref, side_ref, o_ref, side_o_ref, h_scr, gm_scr = refs
        wt_ref = None
    else:
        x_ref, g_ref, sc_ref, sh_ref, w_ref, wt_ref, side_ref, o_ref, side_o_ref, h_scr, gm_scr = refs
    j = pl.program_id(2)

    @pl.when(j == 0)
    def _():
        _modulated_norm(x_ref, g_ref, sc_ref, sh_ref, gm_scr, h_scr)

    def project(wr):
        w = wr[...].astype(BF16)
        if transposed:
            y = lax.dot_general(h_scr[...], w, NT_DIMS, preferred_element_type=F32)
        else:
            y = jnp.dot(h_scr[...], w, preferred_element_type=F32)
        o_ref[0] = y.astype(o_ref.dtype)

    if n_main is None:
        project(w_ref)
    else:
        pl.when(j < n_main)(lambda: project(w_ref))
        pl.when(j >= n_main)(lambda: project(wt_ref))
    side_o_ref[...] = side_ref[...].astype(BF16)


def _norm_mm(x, gain, mod3, sc_idx, sh_idx, w, out_dtype, side, side_lead, w_tail=None,
             transposed=False, tm=1024, tn=512, name="norm_mm"):
    bsz, s, d = x.shape
    n_w = w.shape[0] if transposed else w.shape[1]
    n_main = None if w_tail is None else n_w // tn
    n = n_w if w_tail is None else (n_main + 1) * tn
    nm, nj = s // tm, n // tn
    side_in, side_out = _slab_specs(side, side_lead, 1, LANE, bsz * nm * nj,
                                    lambda b, m, j: (b * nm + m) * nj + j)
    last = nj - 1 if n_main is None else n_main - 1
    if transposed:
        w_specs = [pl.BlockSpec((tn, d), lambda b, m, j: (jnp.minimum(j, last), 0))]
        tail_spec = pl.BlockSpec((tn, d), lambda b, m, j: (0, 0), pipeline_mode=pl.Buffered(1))
    else:
        w_specs = [pl.BlockSpec((d, tn), lambda b, m, j: (0, jnp.minimum(j, last)))]
        tail_spec = pl.BlockSpec((d, tn), lambda b, m, j: (0, 0), pipeline_mode=pl.Buffered(1))
    w_args = [w]
    if w_tail is not None:
        w_specs.append(tail_spec)
        w_args.append(w_tail)
    return pl.pallas_call(
        functools.partial(_norm_mm_kernel, n_main=n_main, transposed=transposed),
        grid=(bsz, nm, nj),
        in_specs=[pl.BlockSpec((1, tm, d), lambda b, m, j: (b, m, 0)),
                  pl.BlockSpec((1, d), lambda b, m, j: (0, 0)),
                  pl.BlockSpec((1, 1, d), lambda b, m, j: (b, 0, sc_idx)),
                  pl.BlockSpec((1, 1, d), lambda b, m, j: (b, 0, sh_idx)),
                  *w_specs,
                  side_in],
        out_specs=(pl.BlockSpec((1, tm, tn), lambda b, m, j: (b, m, j)), side_out),
        out_shape=(jax.ShapeDtypeStruct((bsz, s, n), out_dtype),
                   jax.ShapeDtypeStruct(side.shape[1:], BF16)),
        scratch_shapes=[pltpu.VMEM((tm, d), BF16), pltpu.VMEM((1, d), F32)],
        compiler_params=_params(("arbitrary", "arbitrary", "arbitrary")),
        name=name,
    )(x, gain.reshape(1, d), mod3, mod3, *w_args, side)


def _ffn_kernel(*refs, has_side):
    if has_side:
        (x_ref, gpre_ref, sc_ref, sh_ref, gt_ref, gpost_ref, wg_ref, wu_ref, wo_ref, side_ref,
         o_ref, side_o_ref, h_scr, gm_scr, acc_scr) = refs
        side_o_ref[...] = side_ref[...].astype(BF16)
    else:
        (x_ref, gpre_ref, sc_ref, sh_ref, gt_ref, gpost_ref, wg_ref, wu_ref, wo_ref,
         o_ref, h_scr, gm_scr, acc_scr) = refs
    f = pl.program_id(2)

    @pl.when(f == 0)
    def _():
        _modulated_norm(x_ref, gpre_ref, sc_ref, sh_ref, gm_scr, h_scr)
        acc_scr[...] = jnp.zeros_like(acc_scr)

    h = h_scr[...]
    g = jnp.dot(h, wg_ref[...], preferred_element_type=F32)
    u = jnp.dot(h, wu_ref[...], preferred_element_type=F32)
    a = (g * jax.nn.sigmoid(g) * u).astype(BF16)
    acc_scr[...] += jnp.dot(a, wo_ref[...], preferred_element_type=F32)

    @pl.when(f == pl.num_programs(2) - 1)
    def _():
        _post_norm_residual(acc_scr, x_ref, gt_ref, gpost_ref, gm_scr, o_ref)


def _ffn(x, gpre, gpost, mod3, w_in, w_out, side=None, side_lead=0, tm=1024, tf=512):
    bsz, s, d = x.shape
    fh = w_out.shape[0]
    nf = fh // tf
    nm = s // tm
    out_spec = pl.BlockSpec((1, tm, d), lambda b, m, f: (b, m, 0), pipeline_mode=pl.Buffered(1))
    out_shape = jax.ShapeDtypeStruct((bsz, s, d), F32)
    side_specs, side_args = [], []
    if side is not None:
        side_in, side_out = _slab_specs(side, side_lead, 1, LANE, bsz * nm * nf,
                                        lambda b, m, f: (b * nm + m) * nf + f)
        side_specs, side_args = [side_in], [side]
        out_spec = (out_spec, side_out)
        out_shape = (out_shape, jax.ShapeDtypeStruct(side.shape[1:], BF16))
    return pl.pallas_call(
        functools.partial(_ffn_kernel, has_side=side is not None),
        grid=(bsz, nm, nf),
        in_specs=[pl.BlockSpec((1, tm, d), lambda b, m, f: (b, m, 0), pipeline_mode=pl.Buffered(1)),
                  pl.BlockSpec((1, d), lambda b, m, f: (0, 0)),
                  pl.BlockSpec((1, 1, d), lambda b, m, f: (b, 0, 4)),
                  pl.BlockSpec((1, 1, d), lambda b, m, f: (b, 0, 3)),
                  pl.BlockSpec((1, 1, d), lambda b, m, f: (b, 0, 5)),
                  pl.BlockSpec((1, d), lambda b, m, f: (0, 0)),
                  pl.BlockSpec((d, tf), lambda b, m, f: (0, f)),
                  pl.BlockSpec((d, tf), lambda b, m, f: (0, nf + f)),
                  pl.BlockSpec((tf, d), lambda b, m, f: (f, 0)),
                  *side_specs],
        out_specs=out_spec,
        out_shape=out_shape,
        scratch_shapes=[pltpu.VMEM((tm, d), BF16), pltpu.VMEM((1, d), F32), pltpu.VMEM((tm, d), F32)],
        compiler_params=_params(("arbitrary", "arbitrary", "arbitrary")),
        name="ffn",
    )(x, gpre.reshape(1, d), mod3, mod3, mod3, gpost.reshape(1, d), w_in, w_in, w_out, *side_args)


def _out_proj_kernel(a0_ref, a1_ref, w0_ref, w1_ref, x_ref, gt_ref, gpost_ref, side_ref,
                     o_ref, side_o_ref, gm_scr, y_scr):
    side_o_ref[...] = side_ref[...].astype(BF16)
    y_scr[...] = (jnp.dot(a0_ref[0], w0_ref[...], preferred_element_type=F32)
                  + jnp.dot(a1_ref[0], w1_ref[...], preferred_element_type=F32))
    _post_norm_residual(y_scr, x_ref, gt_ref, gpost_ref, gm_scr, o_ref)


def _out_proj(a0, a1, col0, col1, w, x, mod3, gpost, side, side_lead, tm=512):
    bsz, s, d = x.shape
    kh = w.shape[0] // 2
    nm = s // tm
    side_in, side_out = _slab_specs(side, side_lead, 0, ROW_CHUNK, bsz * nm, lambda b, m: b * nm + m)
    return pl.pallas_call(
        _out_proj_kernel,
        grid=(bsz, nm),
        in_specs=[pl.BlockSpec((1, tm, kh), lambda b, m: (b, m, col0)),
                  pl.BlockSpec((1, tm, kh), lambda b, m: (b, m, col1)),
                  pl.BlockSpec((kh, d), lambda b, m: (0, 0)),
                  pl.BlockSpec((kh, d), lambda b, m: (1, 0)),
                  pl.BlockSpec((1, tm, d), lambda b, m: (b, m, 0)),
                  pl.BlockSpec((1, 1, d), lambda b, m: (b, 0, 2)),
                  pl.BlockSpec((1, d), lambda b, m: (0, 0)),
                  side_in],
        out_specs=(pl.BlockSpec((1, tm, d), lambda b, m: (b, m, 0)), side_out),
        out_shape=(jax.ShapeDtypeStruct((bsz, s, d), F32),
                   jax.ShapeDtypeStruct(side.shape[1:], BF16)),
        scratch_shapes=[pltpu.VMEM((1, d), F32), pltpu.VMEM((tm, d), F32)],
        compiler_params=_params(("arbitrary", "arbitrary")),
        name="out_proj",
    )(a0, a1, w, w, x, mod3, gpost.reshape(1, d), side)


def _mixer_a_kernel(z_ref, vg_ref, vb_ref, ws_ref, bst_ref, o_ref):
    z = jax.nn.gelu(z_ref[0].astype(F32))
    wdt = z.shape[1] // 2
    u = z[:, :wdt]
    v = z[:, wdt:]
    mu = jnp.mean(v, axis=-1, keepdims=True)
    dv = v - mu
    var = jnp.mean(dv * dv, axis=-1, keepdims=True)
    vn = (dv * lax.rsqrt(var + LN_EPS) * vg_ref[...] + vb_ref[...]).astype(BF16)
    ch = z.shape[0]
    causal = (lax.broadcasted_iota(jnp.int32, (ch, ch), 0)
              >= lax.broadcasted_iota(jnp.int32, (ch, ch), 1))
    gd = wdt // A_GROUPS
    for g in range(A_GROUPS):
        w = jnp.where(causal, ws_ref[g], 0.0).astype(BF16)
        sv = jnp.dot(w, vn[:, g * gd:(g + 1) * gd], preferred_element_type=F32)
        sv = sv + bst_ref[:, g:g + 1]
        o_ref[0, :, g * gd:(g + 1) * gd] = (u[:, g * gd:(g + 1) * gd] * sv).astype(o_ref.dtype)


def _mixer_a(z, v_gain, v_bias, w_s, b_s, width):
    bsz, s, _ = z.shape
    ch = A_CHUNK
    return pl.pallas_call(
        _mixer_a_kernel,
        grid=(bsz, s // ch),
        in_specs=[pl.BlockSpec((1, ch, 2 * width), lambda b, c: (b, c, 0)),
                  pl.BlockSpec((1, width), lambda b, c: (0, 0)),
                  pl.BlockSpec((1, width), lambda b, c: (0, 0)),
                  pl.BlockSpec((A_GROUPS, ch, ch), lambda b, c: (0, 0, 0)),
                  pl.BlockSpec((ch, A_GROUPS), lambda b, c: (0, 0))],
        out_specs=pl.BlockSpec((1, ch, width), lambda b, c: (b, c, 0)),
        out_shape=jax.ShapeDtypeStruct((bsz, s, width), BF16),
        compiler_params=_params(("parallel", "parallel")),
        name="mixer_a",
    )(z, v_gain.reshape(1, width), v_bias.reshape(1, width), w_s, b_s.T)


_PV_MU_R, _PV_MU_K, _PV_MU_V, _PV_W0, _PV_A0, _PV_KK, _PV_KA, _PV_RK, _PV_LG, _PV_LB = range(10)
_PV_ROWS = 16


def _shift_lerp(x, prev_row, mu):
    rolled = pltpu.roll(x, 1, axis=0)
    first = lax.broadcasted_iota(jnp.int32, x.shape, 0) == 0
    xp = jnp.where(first, prev_row, rolled)
    return x + mu * (xp - x)


def _split_bf16(x):
    hi = x.astype(BF16)
    lo = (x - hi.astype(F32)).astype(BF16)
    return hi, lo


def _mm(x, y):
    return jnp.dot(x.astype(BF16), y.astype(BF16), preferred_element_type=F32)


def _mm_nt(x, y):
    return lax.dot_general(x.astype(BF16), y.astype(BF16), NT_DIMS, preferred_element_type=F32)


def _mm_tn(x, y):
    return lax.dot_general(x.astype(BF16), y.astype(BF16), TN_DIMS, preferred_element_type=F32)


def _mm_x3(x, y):
    xh, xl = _split_bf16(x)
    yh, yl = _split_bf16(y)
    return (jnp.dot(xh, yh, preferred_element_type=F32) + jnp.dot(xl, yh, preferred_element_type=F32)
            + jnp.dot(xh, yl, preferred_element_type=F32))


def _mm_exact_rhs(x, e_bf16):
    xh, xl = _split_bf16(x)
    return (jnp.dot(xh, e_bf16, preferred_element_type=F32)
            + jnp.dot(xl, e_bf16, preferred_element_type=F32))


def _rwkv_kernel(zr_ref, zk_ref, zv_ref, zl_ref, pva_ref, pvb_ref, mul_ref, w2_ref, a2_ref, g2_ref,
                 side_ref, o_ref, side_o_ref,
                 s_scr, prev_scr, prevl_scr, th_hi_scr, th_lo_scr, xw_scr, sg_scr,
                 q_scr, y_scr, gm_scr, cm_scr, pl_scr, bonus_scr, g_scr, *, chunk, n_t, n_p, n_tiles):
    i = pl.program_id(0)
    t_rows = zr_ref.shape[1]
    lanes = zr_ref.shape[2]
    lora = xw_scr.shape[1] + sg_scr.shape[1]
    L = chunk
    SL = RW_PACK * L
    n_chunks = t_rows // L
    side_o_ref[...] = side_ref[...].astype(BF16)

    ia = jnp.minimum(i, n_tiles - 1)
    ib = jnp.maximum(i - 1, 0)
    t_a, p_a = (ia // n_p) % n_t, ia % n_p
    t_b, p_b = (ib // n_p) % n_t, ib % n_p
    slot_a = i % 2
    slot_b = 1 - slot_a

    @pl.when(i == 0)
    def _():
        for ref in (s_scr, prev_scr, prevl_scr, q_scr, y_scr, gm_scr, cm_scr, pl_scr, bonus_scr, g_scr):
            ref[...] = jnp.zeros_like(ref)

    @pl.when(p_a == 0)
    def _():
        zl = zl_ref[0][:, :lora].astype(F32)
        prev = jnp.where(t_a == 0, 0.0, prevl_scr[0:1, :lora])
        zls = _shift_lerp(zl, prev, mul_ref[:, :lora])
        prevl_scr[0:1, :lora] = zl[t_rows - 1:t_rows, :]
        x_wa = zls[:, :LANE]
        th_hi, th_lo = _split_bf16(jnp.tanh(x_wa))
        th_hi_scr[...] = th_hi
        th_lo_scr[...] = th_lo
        xw_scr[...] = x_wa.astype(BF16)
        sg_scr[...] = jax.nn.sigmoid(zls[:, LANE:]).astype(BF16)

    def pva(r):
        return pva_ref[r:r + 1, :]

    zr = zr_ref[0].astype(F32)
    zk = zk_ref[0].astype(F32)
    zv = zv_ref[0].astype(F32)
    prev = jnp.where(t_a == 0, 0.0, prev_scr[p_a])
    r = _shift_lerp(zr, prev[0:1, :], pva(_PV_MU_R))
    k = _shift_lerp(zk, prev[1:2, :], pva(_PV_MU_K))
    v = _shift_lerp(zv, prev[2:3, :], pva(_PV_MU_V))
    prev_scr[p_a] = jnp.concatenate([zr[t_rows - 1:t_rows, :], zk[t_rows - 1:t_rows, :],
                                     zv[t_rows - 1:t_rows, :], jnp.zeros((5, lanes), F32)], axis=0)

    w2_hi, w2_lo = _split_bf16(w2_ref[...])
    th_hi = th_hi_scr[...]
    w_pre = (pva(_PV_W0) + jnp.dot(th_hi, w2_hi, preferred_element_type=F32)
             + jnp.dot(th_lo_scr[...], w2_hi, preferred_element_type=F32)
             + jnp.dot(th_hi, w2_lo, preferred_element_type=F32))
    t = -w_pre
    softplus = jnp.maximum(t, 0.0) + jnp.log1p(jnp.exp(-jnp.abs(t)))
    log_decay = -jnp.exp(-softplus - 0.5)
    a = jax.nn.sigmoid(pva(_PV_A0) + _mm(xw_scr[...], a2_ref[...]))
    g = _mm(sg_scr[...], g2_ref[...])

    li = lax.broadcasted_iota(jnp.int32, (lanes, lanes), 0) // RW_HEAD
    lj = lax.broadcasted_iota(jnp.int32, (lanes, lanes), 1) // RW_HEAD
    same_head = li == lj
    e_head = jnp.where(same_head, 1.0, 0.0).astype(BF16)

    kk = k * pva(_PV_KK)
    kk = kk / jnp.maximum(jnp.sqrt(_mm(kk * kk, e_head)), 1e-12)
    k = k * (1.0 + (a - 1.0) * pva(_PV_KA))
    av = -kk
    bv = kk * a
    bonus = _mm(r * k * pva(_PV_RK), e_head) * v

    ti = lax.broadcasted_iota(jnp.int32, (L, L), 0)
    tj = lax.broadcasted_iota(jnp.int32, (L, L), 1)
    tri = jnp.where(ti >= tj, 1.0, 0.0).astype(BF16)
    lw_hi, lw_lo = _split_bf16(log_decay)

    si = lax.broadcasted_iota(jnp.int32, (SL, SL), 0)
    sj = lax.broadcasted_iota(jnp.int32, (SL, SL), 1)
    same_blk = (si // L) == (sj // L)
    m_strict = same_blk & (si > sj)
    m_incl = same_blk & (si >= sj)
    eye = jnp.where(si == sj, 1.0, 0.0)
    lane_head = lax.broadcasted_iota(jnp.int32, (1, lanes), 1) // RW_HEAD
    n_sq = max(L.bit_length() - 2, 0)

    def stack(x):
        return jnp.concatenate([jnp.where(lane_head == h, x, 0.0) for h in range(RW_PACK)], axis=0)

    def unstack(x):
        out = x[0:L]
        for h in range(1, RW_PACK):
            out = out + x[h * L:(h + 1) * L]
        return out

    chain = {"s": jnp.where(t_b == 0, 0.0, s_scr[p_b]), "c": 0, "y": []}

    def chain_step():
        c = chain["c"]
        if c >= n_chunks:
            return
        s0 = chain["s"]
        s0b = s0.astype(BF16)
        chain["y"].append(y_scr[slot_b, c * L:(c + 1) * L, :] + _mm_nt(q_scr[slot_b, c], s0b))
        chain["s"] = s0 * pl_scr[slot_b, c][0:1, :] + _mm(s0b, gm_scr[slot_b, c]) + cm_scr[slot_b, c]
        chain["c"] = c + 1

    cs = range(n_chunks)
    rows = [slice(c * L, (c + 1) * L) for c in cs]
    cm = [jnp.dot(tri, lw_hi[rw], preferred_element_type=F32)
          + jnp.dot(tri, lw_lo[rw], preferred_element_type=F32) for rw in rows]
    cm_last = [x[L - 1:L, :] for x in cm]
    r_t = [r[rw] * jnp.exp(cm[c]) for c, rw in enumerate(rows)]
    a_st = [stack(av[rw] * jnp.exp(cm[c] - log_decay[rw])).astype(BF16) for c, rw in enumerate(rows)]
    e_neg = [jnp.exp(-x) for x in cm]
    k_rep = [jnp.concatenate([(k[rw] * e_neg[c]).astype(BF16)] * RW_PACK, axis=0)
             for c, rw in enumerate(rows)]
    b_rep = [jnp.concatenate([(bv[rw] * e_neg[c]).astype(BF16)] * RW_PACK, axis=0)
             for c, rw in enumerate(rows)]
    ar_st = [jnp.concatenate([a_st[c], stack(r_t[c]).astype(BF16)], axis=0) for c in cs]
    chain_step()
    prod_k = [_mm_nt(ar_st[c], k_rep[c]) for c in cs]
    prod_b = [_mm_nt(ar_st[c], b_rep[c]) for c in cs]
    a_ak = [jnp.where(m_strict, x[:SL], 0.0).astype(BF16) for x in prod_k]
    a_rk = [jnp.where(m_incl, x[SL:], 0.0).astype(BF16) for x in prod_k]
    a_ab = [jnp.where(m_strict, x[:SL], 0.0) for x in prod_b]
    a_rb = [jnp.where(m_incl, x[SL:], 0.0).astype(BF16) for x in prod_b]
    chain_step()

    xp = [x.astype(BF16) for x in a_ab]
    tinv = [eye + x for x in a_ab]
    for _ in range(n_sq):
        xp = [_mm(x, x).astype(BF16) for x in xp]
        tinv = [tinv[c] + _mm(tinv[c], xp[c]) for c in cs]
        chain_step()
    tinv = [x.astype(BF16) for x in tinv]

    v_st = [stack(v[rw]).astype(BF16) for rw in rows]
    x0 = [_mm(a_ak[c], v_st[c]).astype(BF16) for c in cs]
    chain_step()
    wu = [_mm(tinv[c], jnp.concatenate([a_st[c], x0[c]], axis=1)) for c in cs]
    chain_step()
    yq = [_mm(a_rb[c], wu[c]) for c in cs]
    y0 = [_mm(a_rk[c], v_st[c]) for c in cs]
    while chain["c"] < n_chunks:
        chain_step()
    cmats, gmats = [], []
    for c, rw in enumerate(rows):
        e_rem = jnp.exp(cm_last[c] - cm[c])
        b_b = bv[rw] * e_rem
        kb = jnp.concatenate([k[rw] * e_rem, b_b], axis=0)
        vu = jnp.concatenate([v[rw], unstack(wu[c][:, lanes:])], axis=0)
        cmats.append(jnp.where(same_head, _mm_tn(vu, kb), 0.0))
        gmats.append(jnp.where(same_head, _mm_tn(unstack(wu[c][:, :lanes]), b_b), 0.0).astype(BF16))

    s_scr[p_b] = chain["s"]
    y = jnp.concatenate(chain["y"], axis=0)
    inv_n = 1.0 / RW_HEAD
    mean = _mm_exact_rhs(y, e_head) * inv_n
    dy = y - mean
    var = _mm(dy * dy, e_head) * inv_n
    yn = dy * lax.rsqrt(var + GN_EPS) * pvb_ref[_PV_LG:_PV_LG + 1, :] + pvb_ref[_PV_LB:_PV_LB + 1, :]
    o_ref[0] = ((yn + bonus_scr[slot_b]) * g_scr[slot_b]).astype(o_ref.dtype)

    for c, rw in enumerate(rows):
        q_scr[slot_a, c] = (r_t[c] + unstack(yq[c][:, :lanes])).astype(BF16)
        y_scr[slot_a, rw, :] = unstack(yq[c][:, lanes:] + y0[c])
        cm_scr[slot_a, c] = cmats[c]
        gm_scr[slot_a, c] = gmats[c]
        pl_scr[slot_a, c] = jnp.broadcast_to(jnp.exp(cm_last[c]), (8, lanes))
    bonus_scr[slot_a] = bonus
    g_scr[slot_a] = g


def _rwkv(z, col_r, col_k, col_v, col_l, lora_w, pvec, mu_l, w2p, a2p, g2p, width, side, side_lead,
          t_rows=512):
    bsz, s, _ = z.shape
    n_p = width // LANE
    n_chunks = t_rows // RW_CHUNK
    n_t = s // t_rows
    n_tiles = bsz * n_t * n_p
    lora = 3 * LANE

    def tile_a(i):
        ia = jnp.minimum(i, n_tiles - 1)
        return ia // (n_t * n_p), (ia // n_p) % n_t, ia % n_p

    def tile_b(i):
        ib = jnp.maximum(i - 1, 0)
        return ib // (n_t * n_p), (ib // n_p) % n_t, ib % n_p

    def z_spec(col):
        def idx(i):
            b, t, p = tile_a(i)
            return b, t, col + p
        return pl.BlockSpec((1, t_rows, LANE), idx)

    def zl_idx(i):
        b, t, _ = tile_a(i)
        return b, t, col_l

    def out_idx(i):
        b, t, p = tile_b(i)
        return b, t, p

    side_in, side_out = _slab_specs(side, side_lead, 0, ROW_CHUNK, n_tiles + 1, lambda i: i)
    kern = functools.partial(_rwkv_kernel, chunk=RW_CHUNK, n_t=n_t, n_p=n_p, n_tiles=n_tiles)
    tile_f32 = pltpu.VMEM((2, t_rows, LANE), F32)
    return pl.pallas_call(
        kern,
        grid=(n_tiles + 1,),
        in_specs=[z_spec(col_r), z_spec(col_k), z_spec(col_v),
                  pl.BlockSpec((1, t_rows, lora_w), zl_idx),
                  pl.BlockSpec((_PV_ROWS, LANE), lambda i: (0, tile_a(i)[2])),
                  pl.BlockSpec((_PV_ROWS, LANE), lambda i: (0, tile_b(i)[2])),
                  pl.BlockSpec((1, lora_w), lambda i: (0, 0)),
                  pl.BlockSpec((LANE, LANE), lambda i: (0, tile_a(i)[2])),
                  pl.BlockSpec((LANE, LANE), lambda i: (0, tile_a(i)[2])),
                  pl.BlockSpec((2 * LANE, LANE), lambda i: (0, tile_a(i)[2])),
                  side_in],
        out_specs=(pl.BlockSpec((1, t_rows, LANE), out_idx), side_out),
        out_shape=(jax.ShapeDtypeStruct((bsz, s, width), BF16),
                   jax.ShapeDtypeStruct(side.shape[1:], BF16)),
        scratch_shapes=[pltpu.VMEM((n_p, LANE, LANE), F32),
                        pltpu.VMEM((n_p, 8, LANE), F32),
                        pltpu.VMEM((8, lora_w), F32),
                        pltpu.VMEM((t_rows, LANE), BF16),
                        pltpu.VMEM((t_rows, LANE), BF16),
                        pltpu.VMEM((t_rows, LANE), BF16),
                        pltpu.VMEM((t_rows, lora - LANE), BF16),
                        pltpu.VMEM((2, n_chunks, RW_CHUNK, LANE), BF16),
                        tile_f32,
                        pltpu.VMEM((2, n_chunks, LANE, LANE), BF16),
                        pltpu.VMEM((2, n_chunks, LANE, LANE), F32),
                        pltpu.VMEM((2, n_chunks, 8, LANE), F32),
                        tile_f32,
                        tile_f32],
        compiler_params=_params(("arbitrary",)),
        name="rwkv7",
    )(z, z, z, z, pvec, pvec, mu_l, w2p, a2p, g2p, side)


def _moba_kernel(q_ref, k_ref, v_ref, side_ref, o_ref, side_o_ref, vt_scr, s_scr, p_scr, *, n_heads):
    h = pl.program_id(1)
    side_o_ref[...] = side_ref[...].astype(BF16)
    s_len = q_ref.shape[1]
    dh = q_ref.shape[2]
    blk = MOBA_BLOCK
    nb = s_len // blk
    log2e = 1.4426950408889634
    scale = dh ** -0.5 * log2e
    neg_inf = -jnp.inf

    kmean = []
    for j in range(nb):
        rows = slice(j * blk, (j + 1) * blk)
        kmean.append(jnp.mean(k_ref[0, rows, :].astype(F32), axis=0, keepdims=True))
        vt_scr[:, rows] = v_ref[0, rows, :].astype(F32).T.astype(BF16)
    kmean = jnp.concatenate(kmean, axis=0)
    kmean_parts = []
    for _ in range(3):
        part = kmean.astype(BF16)
        kmean_parts.append(part)
        kmean = kmean - part.astype(F32)

    slope = log2e * jnp.exp(jnp.full((1, blk), -8.0 / n_heads * 0.6931471805599453, F32)
                            * (h + 1).astype(F32))
    jk = lax.broadcasted_iota(jnp.int32, (blk, blk), 0)
    iq = lax.broadcasted_iota(jnp.int32, (blk, blk), 1)
    bias_tile = slope * jk.astype(F32)
    bias_own = jnp.where(iq >= jk, bias_tile, neg_inf)
    blk_id = lax.broadcasted_iota(jnp.int32, (nb, 1), 0)

    def scores(qb):
        q = q_ref[0, qb * blk:(qb + 1) * blk, :]
        q_s = (q.astype(F32) * scale).astype(BF16)
        gate = sum(lax.dot_general(part, q, NT_DIMS, preferred_element_type=F32)
                   for part in kmean_parts)
        past = blk_id < qb
        offs = []
        m = None
        for n in range(qb + 1):
            k_n = k_ref[0, n * blk:(n + 1) * blk, :]
            s = lax.dot_general(k_n, q_s, NT_DIMS, preferred_element_type=F32)
            if n == qb:
                t = s + bias_own
                off = jnp.zeros((1, blk), F32)
            else:
                t = s + bias_tile
                g_n = gate[n:n + 1, :]
                beats = past & ((gate > g_n) | ((gate == g_n) & (blk_id < n)))
                rank = jnp.sum(jnp.where(beats, 1.0, 0.0), axis=0, keepdims=True)
                off = jnp.where(rank < float(MOBA_TOPK), slope * float((n - qb) * blk), neg_inf)
            s_scr[qb % 2, n] = t
            offs.append(off)
            cmax = jnp.max(t, axis=0, keepdims=True) + off
            m = cmax if m is None else jnp.maximum(m, cmax)
        return m, offs

    def attend(qb, m, offs):
        l = jnp.zeros((1, blk), F32)
        for n in range(qb + 1):
            p = jnp.exp2(s_scr[qb % 2, n] - (m - offs[n]))
            l = l + jnp.sum(p, axis=0, keepdims=True)
            p_scr[qb % 2, n * blk:(n + 1) * blk, :] = p.astype(BF16)
        kk = (qb + 1) * blk
        acc = jnp.dot(vt_scr[:, :kk], p_scr[qb % 2, :kk, :], preferred_element_type=F32)
        o_ref[0, qb * blk:(qb + 1) * blk, :] = (acc / l).T.astype(o_ref.dtype)

    pending = scores(0)
    for qb in range(nb):
        nxt = scores(qb + 1) if qb + 1 < nb else None
        attend(qb, *pending)
        pending = nxt


def _moba(qkv, n_heads, side, side_lead):
    bsz, s, d3 = qkv.shape
    d = d3 // 3
    dh = d // n_heads
    blk = MOBA_BLOCK
    nb = s // blk
    side_in, side_out = _slab_specs(side, side_lead, 0, ROW_CHUNK, bsz * n_heads,
                                    lambda b, h: b * n_heads + h)
    kern = functools.partial(_moba_kernel, n_heads=n_heads)
    return pl.pallas_call(
        kern,
        grid=(bsz, n_heads),
        in_specs=[pl.BlockSpec((1, s, dh), lambda b, h: (b, 0, h)),
                  pl.BlockSpec((1, s, dh), lambda b, h: (b, 0, n_heads + h)),
                  pl.BlockSpec((1, s, dh), lambda b, h: (b, 0, 2 * n_heads + h)),
                  side_in],
        out_specs=(pl.BlockSpec((1, s, dh), lambda b, h: (b, 0, h)), side_out),
        out_shape=(jax.ShapeDtypeStruct((bsz, s, d), BF16),
                   jax.ShapeDtypeStruct(side.shape[1:], BF16)),
        scratch_shapes=[pltpu.VMEM((dh, s), BF16),
                        pltpu.VMEM((2, nb, blk, blk), F32),
                        pltpu.VMEM((2, s, blk), BF16)],
        compiler_params=_params(("arbitrary", "arbitrary")),
        name="moba",
    )(qkv, qkv, qkv, side)


def _pad_cols(w, n):
    return jnp.pad(w, ((0, 0), (0, n - w.shape[1])))


def _pad_rows(w, n, before=0):
    return jnp.pad(w, ((before, n - before - w.shape[0]), (0, 0)))


def kernel(x, c, w_ada, b_ada, g_pre_mix, g_post_mix, g_pre_ffn, g_post_ffn, w_ffn_in, w_ffn_out,
           w_in_ab, w_out_ab, a_v_gain, a_v_bias, a_w_s, a_b_s, b_mu, b_w0, b_w2, b_a0, b_a2, b_g2,
           b_k_k, b_k_a, b_r_k, b_lnx_gain, b_lnx_bias, w_qkv, w_o):
    bsz, s, d = x.shape
    depth = w_ada.shape[0]
    a_width = a_v_gain.shape[1]
    b_width = b_w0.shape[1]
    n_lw = b_w2.shape[1]
    n_la = b_a2.shape[1]
    n_lg = b_g2.shape[1]
    n_heads = d // ATT_HEAD
    assert s % MOBA_BLOCK == 0 and s % 1024 == 0
    assert n_lw + n_la <= LANE and n_lg <= 2 * LANE

    mod = _ada_mod(c, w_ada, b_ada)

    w_qkv_h = None
    for layer in range(depth):
        mod3 = mod[layer].reshape(bsz, 1, 6 * d)
        i = layer // 2
        if layer % 2 == 0:
            lora_w = 4 * LANE
            n_main = 2 * a_width + 3 * b_width
            w_t = jnp.swapaxes(w_in_ab[i], 0, 1)
            w_tail = _pad_rows(w_t[n_main:], lora_w)
            z, w_out_h = _norm_mm(x, g_pre_mix[layer], mod3, 1, 0, w_t, BF16, w_out_ab, i,
                                  w_tail=w_tail, transposed=True, tn=lora_w, name="in_proj_ab")
            y_a = _mixer_a(z, a_v_gain[i], a_v_bias[i], a_w_s[i], a_b_s[i], a_width)

            mu = b_mu[i]
            pvec = jnp.stack([mu[0:b_width], mu[b_width:2 * b_width], mu[2 * b_width:3 * b_width],
                              b_w0[i], b_a0[i], b_k_k[i], b_k_a[i], b_r_k[i].reshape(-1),
                              b_lnx_gain[i], b_lnx_bias[i]])
            pvec = _pad_rows(pvec, _PV_ROWS)
            mu_l = _pad_cols(mu[3 * b_width:].reshape(1, -1), lora_w)
            w2p = _pad_rows(b_w2[i], LANE)
            a2p = _pad_rows(b_a2[i], LANE, before=n_lw)
            g2p = _pad_rows(b_g2[i], 2 * LANE)
            cb = 2 * a_width // LANE
            nb_w = b_width // LANE
            y_b, w_ffn_in_h = _rwkv(z, cb, cb + nb_w, cb + 2 * nb_w, n_main // lora_w, lora_w, pvec,
                                    mu_l, w2p, a2p, g2p, b_width, w_ffn_in, layer)
            x, w_ffn_out_h = _out_proj(y_a, y_b, 0, 0, w_out_h, x, mod3, g_post_mix[layer],
                                       w_ffn_out, layer)
        else:
            w_q = w_qkv[i] if w_qkv_h is None else w_qkv_h
            qkv, w_o_h = _norm_mm(x, g_pre_mix[layer], mod3, 1, 0, w_q, BF16, w_o, i,
                                  name="qkv_proj")
            o, w_ffn_in_h = _moba(qkv, n_heads, w_ffn_in, layer)
            x, w_ffn_out_h = _out_proj(o, o, 0, 1, w_o_h, x, mod3, g_post_mix[layer],
                                       w_ffn_out, layer)
        if layer + 1 < depth and (layer + 1) % 2 == 1:
            x, w_qkv_h = _ffn(x, g_pre_ffn[layer], g_post_ffn[layer], mod3, w_ffn_in_h, w_ffn_out_h,
                              w_qkv, (layer + 1) // 2)
        else:
            x = _ffn(x, g_pre_ffn[layer], g_post_ffn[layer], mod3, w_ffn_in_h, w_ffn_out_h)
            w_qkv_h = None
    return x
```

```python
import functools

import jax
import jax.numpy as jnp
from jax import lax
from jax.experimental import pallas as pl
from jax.experimental.pallas import tpu as pltpu

F32 = jnp.float32
BF16 = jnp.bfloat16

NORM_EPS = 1e-6
LN_EPS = 1e-5
GN_EPS = 64e-5

LANE = 128
A_GROUPS = 8
A_CHUNK = 128
RW_HEAD = 64
RW_CHUNK = 64
RW_PACK = 2
MOBA_BLOCK = 256
MOBA_TOPK = 3
ATT_HEAD = 128

NT_DIMS = (((1,), (1,)), ((), ()))
TN_DIMS = (((0,), (0,)), ((), ()))

VMEM_LIMIT = 56 * 1024 * 1024


def _params(sem):
    return pltpu.CompilerParams(dimension_semantics=sem, vmem_limit_bytes=VMEM_LIMIT)


def _slab_specs(side, lead, axis, unit, n_steps, flat_index):
    total = side.shape[1 + axis]
    n_slab = max(n for n in range(1, n_steps + 1) if total % (n * unit) == 0)
    shape = list(side.shape[1:])
    shape[axis] = total // n_slab

    def idx(*grid):
        slab = jnp.minimum(flat_index(*grid), n_slab - 1)
        return (slab, 0) if axis == 0 else (0, slab)

    return (pl.BlockSpec((None,) + tuple(shape), lambda *g: (lead,) + idx(*g)),
            pl.BlockSpec(tuple(shape), idx))


def _rms(x, gain):
    ms = jnp.mean(x * x, axis=-1, keepdims=True)
    return x * lax.rsqrt(ms + NORM_EPS) * gain


ROW_CHUNK = 16
ROW_UNROLL = 8


def _for_row_chunks(n_rows, fn):
    def body(i, carry):
        fn(pl.ds(pl.multiple_of(i * ROW_CHUNK, ROW_CHUNK), ROW_CHUNK))
        return carry
    lax.fori_loop(0, n_rows // ROW_CHUNK, body, 0, unroll=ROW_UNROLL)


def _modulated_norm(x_ref, g_ref, sc_ref, sh_ref, gm_scr, h_scr):
    gm_scr[...] = g_ref[...] * (1.0 + sc_ref[0])

    def rows(r):
        x = x_ref[0, r, :]
        ms = jnp.mean(x * x, axis=-1, keepdims=True)
        h_scr[r, :] = (x * lax.rsqrt(ms + NORM_EPS) * gm_scr[...] + sh_ref[0]).astype(BF16)

    _for_row_chunks(h_scr.shape[0], rows)


def _post_norm_residual(y_scr, x_ref, gt_ref, gpost_ref, gm_scr, o_ref):
    gm_scr[...] = gt_ref[0] * gpost_ref[...]

    def rows(r):
        y = y_scr[r, :]
        ms = jnp.mean(y * y, axis=-1, keepdims=True)
        o_ref[0, r, :] = x_ref[0, r, :] + y * lax.rsqrt(ms + NORM_EPS) * gm_scr[...]

    _for_row_chunks(o_ref.shape[1], rows)


def _ada_kernel(c_ref, w_ref, b_ref, o_ref):
    c = c_ref[...]
    cond = (c * jax.nn.sigmoid(c)).astype(BF16)
    o_ref[0] = jnp.dot(cond, w_ref[0].astype(BF16), preferred_element_type=F32) + b_ref[0]


def _ada_mod(c, w_ada, b_ada):
    depth, d, n = w_ada.shape
    bsz = c.shape[0]
    bp = 8
    c_p = jnp.pad(c, ((0, bp - bsz), (0, 0)))
    tn = 1024
    out = pl.pallas_call(
        _ada_kernel,
        grid=(depth, n // tn),
        in_specs=[pl.BlockSpec((bp, d), lambda l, j: (0, 0)),
                  pl.BlockSpec((1, d, tn), lambda l, j: (l, 0, j)),
                  pl.BlockSpec((1, 1, tn), lambda l, j: (l, 0, j))],
        out_specs=pl.BlockSpec((1, bp, tn), lambda l, j: (l, 0, j)),
        out_shape=jax.ShapeDtypeStruct((depth, bp, n), F32),
        compiler_params=_params(("parallel", "parallel")),
        name="ada_mod",
    )(c_p, w_ada, b_ada.reshape(depth, 1, n))
    return out[:, :bsz]


def _norm_mm_kernel(*refs, n_main, transposed):
    if n_main is None:
        x_ref, g_ref, sc_ref, sh_ref, w_ref, side_ref, o_ref, side_o_ref, h_scr, gm_scr = refs
        wt_ref = None
    else:
        x_ref, g_ref, sc_ref, sh_ref, w_ref, wt_ref, side_ref, o_ref, side_o_ref, h_scr, gm_scr = refs
    j = pl.program_id(2)

    @pl.when(j == 0)
    def _():
        _modulated_norm(x_ref, g_ref, sc_ref, sh_ref, gm_scr, h_scr)

    def project(wr):
        w = wr[...].astype(BF16)
        if transposed:
            y = lax.dot_general(h_scr[...], w, NT_DIMS, preferred_element_type=F32)
        else:
            y = jnp.dot(h_scr[...], w, preferred_element_type=F32)
        o_ref[0] = y.astype(o_ref.dtype)

    if n_main is None:
        project(w_ref)
    else:
        pl.when(j < n_main)(lambda: project(w_ref))
        pl.when(j >= n_main)(lambda: project(wt_ref))
    side_o_ref[...] = side_ref[...].astype(BF16)


def _norm_mm(x, gain, mod3, sc_idx, sh_idx, w, out_dtype, side, side_lead, w_tail=None,
             transposed=False, tm=1024, tn=512, name="norm_mm"):
    bsz, s, d = x.shape
    n_w = w.shape[0] if transposed else w.shape[1]
    n_main = None if w_tail is None else n_w // tn
    n = n_w if w_tail is None else (n_main + 1) * tn
    nm, nj = s // tm, n // tn
    side_in, side_out = _slab_specs(side, side_lead, 1, LANE, bsz * nm * nj,
                                    lambda b, m, j: (b * nm + m) * nj + j)
    last = nj - 1 if n_main is None else n_main - 1
    if transposed:
        w_specs = [pl.BlockSpec((tn, d), lambda b, m, j: (jnp.minimum(j, last), 0))]
        tail_spec = pl.BlockSpec((tn, d), lambda b, m, j: (0, 0), pipeline_mode=pl.Buffered(1))
    else:
        w_specs = [pl.BlockSpec((d, tn), lambda b, m, j: (0, jnp.minimum(j, last)))]
        tail_spec = pl.BlockSpec((d, tn), lambda b, m, j: (0, 0), pipeline_mode=pl.Buffered(1))
    w_args = [w]
    if w_tail is not None:
        w_specs.append(tail_spec)
        w_args.append(w_tail)
    return pl.pallas_call(
        functools.partial(_norm_mm_kernel, n_main=n_main, transposed=transposed),
        grid=(bsz, nm, nj),
        in_specs=[pl.BlockSpec((1, tm, d), lambda b, m, j: (b, m, 0)),
                  pl.BlockSpec((1, d), lambda b, m, j: (0, 0)),
                  pl.BlockSpec((1, 1, d), lambda b, m, j: (b, 0, sc_idx)),
                  pl.BlockSpec((1, 1, d), lambda b, m, j: (b, 0, sh_idx)),
                  *w_specs,
                  side_in],
        out_specs=(pl.BlockSpec((1, tm, tn), lambda b, m, j: (b, m, j)), side_out),
        out_shape=(jax.ShapeDtypeStruct((bsz, s, n), out_dtype),
                   jax.ShapeDtypeStruct(side.shape[1:], BF16)),
        scratch_shapes=[pltpu.VMEM((tm, d), BF16), pltpu.VMEM((1, d), F32)],
        compiler_params=_params(("arbitrary", "arbitrary", "arbitrary")),
        name=name,
    )(x, gain.reshape(1, d), mod3, mod3, *w_args, side)


def _ffn_kernel(*refs, has_side):
    if has_side:
        (x_ref, gpre_ref, sc_ref, sh_ref, gt_ref, gpost_ref, wg_ref, wu_ref, wo_ref, side_ref,
         o_ref, side_o_ref, h_scr, gm_scr, acc_scr) = refs
        side_o_ref[...] = side_ref[...].astype(BF16)
    else:
        (x_ref, gpre_ref, sc_ref, sh_ref, gt_ref, gpost_ref, wg_ref, wu_ref, wo_ref,
         o_ref, h_scr, gm_scr, acc_scr) = refs
    f = pl.program_id(2)

    @pl.when(f == 0)
    def _():
        _modulated_norm(x_ref, gpre_ref, sc_ref, sh_ref, gm_scr, h_scr)
        acc_scr[...] = jnp.zeros_like(acc_scr)

    h = h_scr[...]
    g = jnp.dot(h, wg_ref[...], preferred_element_type=F32)
    u = jnp.dot(h, wu_ref[...], preferred_element_type=F32)
    a = (g * jax.nn.sigmoid(g) * u).astype(BF16)
    acc_scr[...] += jnp.dot(a, wo_ref[...], preferred_element_type=F32)

    @pl.when(f == pl.num_programs(2) - 1)
    def _():
        _post_norm_residual(acc_scr, x_ref, gt_ref, gpost_ref, gm_scr, o_ref)


def _ffn(x, gpre, gpost, mod3, w_in, w_out, side=None, side_lead=0, tm=1024, tf=512):
    bsz, s, d = x.shape
    fh = w_out.shape[0]
    nf = fh // tf
    nm = s // tm
    out_spec = pl.BlockSpec((1, tm, d), lambda b, m, f: (b, m, 0), pipeline_mode=pl.Buffered(1))
    out_shape = jax.ShapeDtypeStruct((bsz, s, d), F32)
    side_specs, side_args = [], []
    if side is not None:
        side_in, side_out = _slab_specs(side, side_lead, 1, LANE, bsz * nm * nf,
                                        lambda b, m, f: (b * nm + m) * nf + f)
        side_specs, side_args = [side_in], [side]
        out_spec = (out_spec, side_out)
        out_shape = (out_shape, jax.ShapeDtypeStruct(side.shape[1:], BF16))
    return pl.pallas_call(
        functools.partial(_ffn_kernel, has_side=side is not None),
        grid=(bsz, nm, nf),
        in_specs=[pl.BlockSpec((1, tm, d), lambda b, m, f: (b, m, 0), pipeline_mode=pl.Buffered(1)),
                  pl.BlockSpec((1, d), lambda b, m, f: (0, 0)),
                  pl.BlockSpec((1, 1, d), lambda b, m, f: (b, 0, 4)),
                  pl.BlockSpec((1, 1, d), lambda b, m, f: (b, 0, 3)),
                  pl.BlockSpec((1, 1, d), lambda b, m, f: (b, 0, 5)),
                  pl.BlockSpec((1, d), lambda b, m, f: (0, 0)),
                  pl.BlockSpec((d, tf), lambda b, m, f: (0, f)),
                  pl.BlockSpec((d, tf), lambda b, m, f: (0, nf + f)),
                  pl.BlockSpec((tf, d), lambda b, m, f: (f, 0)),
                  *side_specs],
        out_specs=out_spec,
        out_shape=out_shape,
        scratch_shapes=[pltpu.VMEM((tm, d), BF16), pltpu.VMEM((1, d), F32), pltpu.VMEM((tm, d), F32)],
        compiler_params=_params(("arbitrary", "arbitrary", "arbitrary")),
        name="ffn",
    )(x, gpre.reshape(1, d), mod3, mod3, mod3, gpost.reshape(1, d), w_in, w_in, w_out, *side_args)


def _out_proj_kernel(a0_ref, a1_ref, w0_ref, w1_ref, x_ref, gt_ref, gpost_ref, side_ref,
                     o_ref, side_o_ref, gm_scr, y_scr):
    side_o_ref[...] = side_ref[...].astype(BF16)
    y_scr[...] = (jnp.dot(a0_ref[0], w0_ref[...], preferred_element_type=F32)
                  + jnp.dot(a1_ref[0], w1_ref[...], preferred_element_type=F32))
    _post_norm_residual(y_scr, x_ref, gt_ref, gpost_ref, gm_scr, o_ref)


def _out_proj(a0, a1, col0, col1, w, x, mod3, gpost, side, side_lead, tm=512):
    bsz, s, d = x.shape
    kh = w.shape[0] // 2
    nm = s // tm
    side_in, side_out = _slab_specs(side, side_lead, 0, ROW_CHUNK, bsz * nm, lambda b, m: b * nm + m)
    return pl.pallas_call(
        _out_proj_kernel,
        grid=(bsz, nm),
        in_specs=[pl.BlockSpec((1, tm, kh), lambda b, m: (b, m, col0)),
                  pl.BlockSpec((1, tm, kh), lambda b, m: (b, m, col1)),
                  pl.BlockSpec((kh, d), lambda b, m: (0, 0)),
                  pl.BlockSpec((kh, d), lambda b, m: (1, 0)),
                  pl.BlockSpec((1, tm, d), lambda b, m: (b, m, 0)),
                  pl.BlockSpec((1, 1, d), lambda b, m: (b, 0, 2)),
                  pl.BlockSpec((1, d), lambda b, m: (0, 0)),
                  side_in],
        out_specs=(pl.BlockSpec((1, tm, d), lambda b, m: (b, m, 0)), side_out),
        out_shape=(jax.ShapeDtypeStruct((bsz, s, d), F32),
                   jax.ShapeDtypeStruct(side.shape[1:], BF16)),
        scratch_shapes=[pltpu.VMEM((1, d), F32), pltpu.VMEM((tm, d), F32)],
        compiler_params=_params(("arbitrary", "arbitrary")),
        name="out_proj",
    )(a0, a1, w, w, x, mod3, gpost.reshape(1, d), side)


def _mixer_a_kernel(z_ref, vg_ref, vb_ref, ws_ref, bst_ref, o_ref):
    z = jax.nn.gelu(z_ref[0].astype(F32))
    wdt = z.shape[1] // 2
    u = z[:, :wdt]
    v = z[:, wdt:]
    mu = jnp.mean(v, axis=-1, keepdims=True)
    dv = v - mu
    var = jnp.mean(dv * dv, axis=-1, keepdims=True)
    vn = (dv * lax.rsqrt(var + LN_EPS) * vg_ref[...] + vb_ref[...]).astype(BF16)
    ch = z.shape[0]
    causal = (lax.broadcasted_iota(jnp.int32, (ch, ch), 0)
              >= lax.broadcasted_iota(jnp.int32, (ch, ch), 1))
    gd = wdt // A_GROUPS
    for g in range(A_GROUPS):
        w = jnp.where(causal, ws_ref[g], 0.0).astype(BF16)
        sv = jnp.dot(w, vn[:, g * gd:(g + 1) * gd], preferred_element_type=F32)
        sv = sv + bst_ref[:, g:g + 1]
        o_ref[0, :, g * gd:(g + 1) * gd] = (u[:, g * gd:(g + 1) * gd] * sv).astype(o_ref.dtype)


def _mixer_a(z, v_gain, v_bias, w_s, b_s, width):
    bsz, s, _ = z.shape
    ch = A_CHUNK
    return pl.pallas_call(
        _mixer_a_kernel,
        grid=(bsz, s // ch),
        in_specs=[pl.BlockSpec((1, ch, 2 * width), lambda b, c: (b, c, 0)),
                  pl.BlockSpec((1, width), lambda b, c: (0, 0)),
                  pl.BlockSpec((1, width), lambda b, c: (0, 0)),
                  pl.BlockSpec((A_GROUPS, ch, ch), lambda b, c: (0, 0, 0)),
                  pl.BlockSpec((ch, A_GROUPS), lambda b, c: (0, 0))],
        out_specs=pl.BlockSpec((1, ch, width), lambda b, c: (b, c, 0)),
        out_shape=jax.ShapeDtypeStruct((bsz, s, width), BF16),
        compiler_params=_params(("parallel", "parallel")),
        name="mixer_a",
    )(z, v_gain.reshape(1, width), v_bias.reshape(1, width), w_s, b_s.T)


_PV_MU_R, _PV_MU_K, _PV_MU_V, _PV_W0, _PV_A0, _PV_KK, _PV_KA, _PV_RK, _PV_LG, _PV_LB = range(10)
_PV_ROWS = 16


def _shift_lerp(x, prev_row, mu):
    rolled = pltpu.roll(x, 1, axis=0)
    first = lax.broadcasted_iota(jnp.int32, x.shape, 0) == 0
    xp = jnp.where(first, prev_row, rolled)
    return x + mu * (xp - x)


def _split_bf16(x):
    hi = x.astype(BF16)
    lo = (x - hi.astype(F32)).astype(BF16)
    return hi, lo


def _mm(x, y):
    return jnp.dot(x.astype(BF16), y.astype(BF16), preferred_element_type=F32)


def _mm_nt(x, y):
    return lax.dot_general(x.astype(BF16), y.astype(BF16), NT_DIMS, preferred_element_type=F32)


def _mm_tn(x, y):
    return lax.dot_general(x.astype(BF16), y.astype(BF16), TN_DIMS, preferred_element_type=F32)


def _mm_x3(x, y):
    xh, xl = _split_bf16(x)
    yh, yl = _split_bf16(y)
    return (jnp.dot(xh, yh, preferred_element_type=F32) + jnp.dot(xl, yh, preferred_element_type=F32)
            + jnp.dot(xh, yl, preferred_element_type=F32))


def _mm_exact_rhs(x, e_bf16):
    xh, xl = _split_bf16(x)
    return (jnp.dot(xh, e_bf16, preferred_element_type=F32)
            + jnp.dot(xl, e_bf16, preferred_element_type=F32))


def _rwkv_kernel(zr_ref, zk_ref, zv_ref, zl_ref, pva_ref, pvb_ref, mul_ref, w2_ref, a2_ref, g2_ref,
                 side_ref, o_ref, side_o_ref,
                 s_scr, prev_scr, prevl_scr, th_hi_scr, th_lo_scr, xw_scr, sg_scr,
                 q_scr, y_scr, gm_scr, cm_scr, pl_scr, bonus_scr, g_scr, *, chunk, n_t, n_p, n_tiles):
    i = pl.program_id(0)
    t_rows = zr_ref.shape[1]
    lanes = zr_ref.shape[2]
    lora = xw_scr.shape[1] + sg_scr.shape[1]
    L = chunk
    SL = RW_PACK * L
    n_chunks = t_rows // L
    side_o_ref[...] = side_ref[...].astype(BF16)

    ia = jnp.minimum(i, n_tiles - 1)
    ib = jnp.maximum(i - 1, 0)
    t_a, p_a = (ia // n_p) % n_t, ia % n_p
    t_b, p_b = (ib // n_p) % n_t, ib % n_p
    slot_a = i % 2
    slot_b = 1 - slot_a

    @pl.when(i == 0)
    def _():
        for ref in (s_scr, prev_scr, prevl_scr, q_scr, y_scr, gm_scr, cm_scr, pl_scr, bonus_scr, g_scr):
            ref[...] = jnp.zeros_like(ref)

    @pl.when(p_a == 0)
    def _():
        zl = zl_ref[0][:, :lora].astype(F32)
        prev = jnp.where(t_a == 0, 0.0, prevl_scr[0:1, :lora])
        zls = _shift_lerp(zl, prev, mul_ref[:, :lora])
        prevl_scr[0:1, :lora] = zl[t_rows - 1:t_rows, :]
        x_wa = zls[:, :LANE]
        th_hi, th_lo = _split_bf16(jnp.tanh(x_wa))
        th_hi_scr[...] = th_hi
        th_lo_scr[...] = th_lo
        xw_scr[...] = x_wa.astype(BF16)
        sg_scr[...] = jax.nn.sigmoid(zls[:, LANE:]).astype(BF16)

    def pva(r):
        return pva_ref[r:r + 1, :]

    zr = zr_ref[0].astype(F32)
    zk = zk_ref[0].astype(F32)
    zv = zv_ref[0].astype(F32)
    prev = jnp.where(t_a == 0, 0.0, prev_scr[p_a])
    r = _shift_lerp(zr, prev[0:1, :], pva(_PV_MU_R))
    k = _shift_lerp(zk, prev[1:2, :], pva(_PV_MU_K))
    v = _shift_lerp(zv, prev[2:3, :], pva(_PV_MU_V))
    prev_scr[p_a] = jnp.concatenate([zr[t_rows - 1:t_rows, :], zk[t_rows - 1:t_rows, :],
                                     zv[t_rows - 1:t_rows, :], jnp.zeros((5, lanes), F32)], axis=0)

    w2_hi, w2_lo = _split_bf16(w2_ref[...])
    th_hi = th_hi_scr[...]
    w_pre = (pva(_PV_W0) + jnp.dot(th_hi, w2_hi, preferred_element_type=F32)
             + jnp.dot(th_lo_scr[...], w2_hi, preferred_element_type=F32)
             + jnp.dot(th_hi, w2_lo, preferred_element_type=F32))
    t = -w_pre
    softplus = jnp.maximum(t, 0.0) + jnp.log1p(jnp.exp(-jnp.abs(t)))
    log_decay = -jnp.exp(-softplus - 0.5)
    a = jax.nn.sigmoid(pva(_PV_A0) + _mm(xw_scr[...], a2_ref[...]))
    g = _mm(sg_scr[...], g2_ref[...])

    li = lax.broadcasted_iota(jnp.int32, (lanes, lanes), 0) // RW_HEAD
    lj = lax.broadcasted_iota(jnp.int32, (lanes, lanes), 1) // RW_HEAD
    same_head = li == lj
    e_head = jnp.where(same_head, 1.0, 0.0).astype(BF16)

    kk = k * pva(_PV_KK)
    kk = kk / jnp.maximum(jnp.sqrt(_mm(kk * kk, e_head)), 1e-12)
    k = k * (1.0 + (a - 1.0) * pva(_PV_KA))
    av = -kk
    bv = kk * a
    bonus = _mm(r * k * pva(_PV_RK), e_head) * v

    ti = lax.broadcasted_iota(jnp.int32, (L, L), 0)
    tj = lax.broadcasted_iota(jnp.int32, (L, L), 1)
    tri = jnp.where(ti >= tj, 1.0, 0.0).astype(BF16)
    lw_hi, lw_lo = _split_bf16(log_decay)

    si = lax.broadcasted_iota(jnp.int32, (SL, SL), 0)
    sj = lax.broadcasted_iota(jnp.int32, (SL, SL), 1)
    same_blk = (si // L) == (sj // L)
    m_strict = same_blk & (si > sj)
    m_incl = same_blk & (si >= sj)
    eye = jnp.where(si == sj, 1.0, 0.0)
    lane_head = lax.broadcasted_iota(jnp.int32, (1, lanes), 1) // RW_HEAD
    n_sq = max(L.bit_length() - 2, 0)

    def stack(x):
        return jnp.concatenate([jnp.where(lane_head == h, x, 0.0) for h in range(RW_PACK)], axis=0)

    def unstack(x):
        out = x[0:L]
        for h in range(1, RW_PACK):
            out = out + x[h * L:(h + 1) * L]
        return out

    chain = {"s": jnp.where(t_b == 0, 0.0, s_scr[p_b]), "c": 0, "y": []}

    def chain_step():
        c = chain["c"]
        if c >= n_chunks:
            return
        s0 = chain["s"]
        s0b = s0.astype(BF16)
        chain["y"].append(y_scr[slot_b, c * L:(c + 1) * L, :] + _mm_nt(q_scr[slot_b, c], s0b))
        chain["s"] = s0 * pl_scr[slot_b, c][0:1, :] + _mm(s0b, gm_scr[slot_b, c]) + cm_scr[slot_b, c]
        chain["c"] = c + 1

    cs = range(n_chunks)
    rows = [slice(c * L, (c + 1) * L) for c in cs]
    cm = [jnp.dot(tri, lw_hi[rw], preferred_element_type=F32)
          + jnp.dot(tri, lw_lo[rw], preferred_element_type=F32) for rw in rows]
    cm_last = [x[L - 1:L, :] for x in cm]
    r_t = [r[rw] * jnp.exp(cm[c]) for c, rw in enumerate(rows)]
    a_st = [stack(av[rw] * jnp.exp(cm[c] - log_decay[rw])).astype(BF16) for c, rw in enumerate(rows)]
    e_neg = [jnp.exp(-x) for x in cm]
    k_rep = [jnp.concatenate([(k[rw] * e_neg[c]).astype(BF16)] * RW_PACK, axis=0)
             for c, rw in enumerate(rows)]
    b_rep = [jnp.concatenate([(bv[rw] * e_neg[c]).astype(BF16)] * RW_PACK, axis=0)
             for c, rw in enumerate(rows)]
    ar_st = [jnp.concatenate([a_st[c], stack(r_t[c]).astype(BF16)], axis=0) for c in cs]
    chain_step()
    prod_k = [_mm_nt(ar_st[c], k_rep[c]) for c in cs]
    prod_b = [_mm_nt(ar_st[c], b_rep[c]) for c in cs]
    a_ak = [jnp.where(m_strict, x[:SL], 0.0).astype(BF16) for x in prod_k]
    a_rk = [jnp.where(m_incl, x[SL:], 0.0).astype(BF16) for x in prod_k]
    a_ab = [jnp.where(m_strict, x[:SL], 0.0) for x in prod_b]
    a_rb = [jnp.where(m_incl, x[SL:], 0.0).astype(BF16) for x in prod_b]
    chain_step()

    xp = [x.astype(BF16) for x in a_ab]
    tinv = [eye + x for x in a_ab]
    for _ in range(n_sq):
        xp = [_mm(x, x).astype(BF16) for x in xp]
        tinv = [tinv[c] + _mm(tinv[c], xp[c]) for c in cs]
        chain_step()
    tinv = [x.astype(BF16) for x in tinv]

    v_st = [stack(v[rw]).astype(BF16) for rw in rows]
    x0 = [_mm(a_ak[c], v_st[c]).astype(BF16) for c in cs]
    chain_step()
    wu = [_mm(tinv[c], jnp.concatenate([a_st[c], x0[c]], axis=1)) for c in cs]
    chain_step()
    yq = [_mm(a_rb[c], wu[c]) for c in cs]
    y0 = [_mm(a_rk[c], v_st[c]) for c in cs]
    while chain["c"] < n_chunks:
        chain_step()
    cmats, gmats = [], []
    for c, rw in enumerate(rows):
        e_rem = jnp.exp(cm_last[c] - cm[c])
        b_b = bv[rw] * e_rem
        kb = jnp.concatenate([k[rw] * e_rem, b_b], axis=0)
        vu = jnp.concatenate([v[rw], unstack(wu[c][:, lanes:])], axis=0)
        cmats.append(jnp.where(same_head, _mm_tn(vu, kb), 0.0))
        gmats.append(jnp.where(same_head, _mm_tn(unstack(wu[c][:, :lanes]), b_b), 0.0).astype(BF16))

    s_scr[p_b] = chain["s"]
    y = jnp.concatenate(chain["y"], axis=0)
    inv_n = 1.0 / RW_HEAD
    mean = _mm_exact_rhs(y, e_head) * inv_n
    dy = y - mean
    var = _mm(dy * dy, e_head) * inv_n
    yn = dy * lax.rsqrt(var + GN_EPS) * pvb_ref[_PV_LG:_PV_LG + 1, :] + pvb_ref[_PV_LB:_PV_LB + 1, :]
    o_ref[0] = ((yn + bonus_scr[slot_b]) * g_scr[slot_b]).astype(o_ref.dtype)

    for c, rw in enumerate(rows):
        q_scr[slot_a, c] = (r_t[c] + unstack(yq[c][:, :lanes])).astype(BF16)
        y_scr[slot_a, rw, :] = unstack(yq[c][:, lanes:] + y0[c])
        cm_scr[slot_a, c] = cmats[c]
        gm_scr[slot_a, c] = gmats[c]
        pl_scr[slot_a, c] = jnp.broadcast_to(jnp.exp(cm_last[c]), (8, lanes))
    bonus_scr[slot_a] = bonus
    g_scr[slot_a] = g


def _rwkv(z, col_r, col_k, col_v, col_l, lora_w, pvec, mu_l, w2p, a2p, g2p, width, side, side_lead,
          t_rows=512):
    bsz, s, _ = z.shape
    n_p = width // LANE
    n_chunks = t_rows // RW_CHUNK
    n_t = s // t_rows
    n_tiles = bsz * n_t * n_p
    lora = 3 * LANE

    def tile_a(i):
        ia = jnp.minimum(i, n_tiles - 1)
        return ia // (n_t * n_p), (ia // n_p) % n_t, ia % n_p

    def tile_b(i):
        ib = jnp.maximum(i - 1, 0)
        return ib // (n_t * n_p), (ib // n_p) % n_t, ib % n_p

    def z_spec(col):
        def idx(i):
            b, t, p = tile_a(i)
            return b, t, col + p
        return pl.BlockSpec((1, t_rows, LANE), idx)

    def zl_idx(i):
        b, t, _ = tile_a(i)
        return b, t, col_l

    def out_idx(i):
        b, t, p = tile_b(i)
        return b, t, p

    side_in, side_out = _slab_specs(side, side_lead, 0, ROW_CHUNK, n_tiles + 1, lambda i: i)
    kern = functools.partial(_rwkv_kernel, chunk=RW_CHUNK, n_t=n_t, n_p=n_p, n_tiles=n_tiles)
    tile_f32 = pltpu.VMEM((2, t_rows, LANE), F32)
    return pl.pallas_call(
        kern,
        grid=(n_tiles + 1,),
        in_specs=[z_spec(col_r), z_spec(col_k), z_spec(col_v),
                  pl.BlockSpec((1, t_rows, lora_w), zl_idx),
                  pl.BlockSpec((_PV_ROWS, LANE), lambda i: (0, tile_a(i)[2])),
                  pl.BlockSpec((_PV_ROWS, LANE), lambda i: (0, tile_b(i)[2])),
                  pl.BlockSpec((1, lora_w), lambda i: (0, 0)),
                  pl.BlockSpec((LANE, LANE), lambda i: (0, tile_a(i)[2])),
                  pl.BlockSpec((LANE, LANE), lambda i: (0, tile_a(i)[2])),
                  pl.BlockSpec((2 * LANE, LANE), lambda i: (0, tile_a(i)[2])),
                  side_in],
        out_specs=(pl.BlockSpec((1, t_rows, LANE), out_idx), side_out),
        out_shape=(jax.ShapeDtypeStruct((bsz, s, width), BF16),
                   jax.ShapeDtypeStruct(side.shape[1:], BF16)),
        scratch_shapes=[pltpu.VMEM((n_p, LANE, LANE), F32),
                        pltpu.VMEM((n_p, 8, LANE), F32),
                        pltpu.VMEM((8, lora_w), F32),
                        pltpu.VMEM((t_rows, LANE), BF16),
                        pltpu.VMEM((t_rows, LANE), BF16),
                        pltpu.VMEM((t_rows, LANE), BF16),
                        pltpu.VMEM((t_rows, lora - LANE), BF16),
                        pltpu.VMEM((2, n_chunks, RW_CHUNK, LANE), BF16),
                        tile_f32,
                        pltpu.VMEM((2, n_chunks, LANE, LANE), BF16),
                        pltpu.VMEM((2, n_chunks, LANE, LANE), F32),
                        pltpu.VMEM((2, n_chunks, 8, LANE), F32),
                        tile_f32,
                        tile_f32],
        compiler_params=_params(("arbitrary",)),
        name="rwkv7",
    )(z, z, z, z, pvec, pvec, mu_l, w2p, a2p, g2p, side)


def _moba_kernel(q_ref, k_ref, v_ref, side_ref, o_ref, side_o_ref, ka_scr, vt_scr, s_scr, p_scr, *,
                 n_heads):
    h = pl.program_id(1)
    side_o_ref[...] = side_ref[...].astype(BF16)
    s_len = q_ref.shape[1]
    dh = q_ref.shape[2]
    blk = MOBA_BLOCK
    nb = s_len // blk
    log2e = 1.4426950408889634
    scale = dh ** -0.5 * log2e
    neg_inf = -jnp.inf

    def slope_row(width):
        return log2e * jnp.exp(jnp.full((1, width), -8.0 / n_heads * 0.6931471805599453, F32)
                               * (h + 1).astype(F32))

    lane = lax.broadcasted_iota(jnp.int32, (blk, dh), 1)
    bias = slope_row(dh) * lax.broadcasted_iota(jnp.int32, (blk, dh), 0).astype(F32)
    extra = jnp.zeros((blk, dh), F32)
    for col in range(3):
        part = bias.astype(BF16).astype(F32)
        extra = jnp.where(lane == col, part, extra)
        bias = bias - part
    extra = extra.astype(BF16)
    ones_cols = jnp.where(lane < 3, 1.0, 0.0).astype(BF16)

    kmean = []
    for j in range(nb):
        rows = slice(j * blk, (j + 1) * blk)
        k_j = k_ref[0, rows, :]
        kmean.append(jnp.mean(k_j.astype(F32), axis=0, keepdims=True))
        ka_scr[rows, :dh] = k_j
        ka_scr[rows, dh:] = extra
        vt_scr[:dh, rows] = v_ref[0, rows, :].astype(F32).T.astype(BF16)
    sub = lax.broadcasted_iota(jnp.int32, (vt_scr.shape[0] - dh, s_len), 0)
    vt_scr[dh:, :] = jnp.where(sub == 0, 1.0, 0.0).astype(BF16)
    kmean = jnp.concatenate(kmean, axis=0)
    kmean_parts = []
    for _ in range(3):
        part = kmean.astype(BF16)
        kmean_parts.append(part)
        kmean = kmean - part.astype(F32)

    slope = slope_row(blk)
    causal = (lax.broadcasted_iota(jnp.int32, (blk, blk), 1)
              >= lax.broadcasted_iota(jnp.int32, (blk, blk), 0))
    blk_id = lax.broadcasted_iota(jnp.int32, (nb, 1), 0)

    def scores(qb):
        q = q_ref[0, qb * blk:(qb + 1) * blk, :]
        q_aug = jnp.concatenate([(q.astype(F32) * scale).astype(BF16), ones_cols], axis=1)
        gate = sum(lax.dot_general(part, q, NT_DIMS, preferred_element_type=F32)
                   for part in kmean_parts)
        past = blk_id < qb
        offs = []
        m = None
        for n in range(qb + 1):
            t = lax.dot_general(ka_scr[n * blk:(n + 1) * blk, :], q_aug, NT_DIMS,
                                preferred_element_type=F32)
            if n == qb:
                t = jnp.where(causal, t, neg_inf)
                off = jnp.zeros((1, blk), F32)
            else:
                g_n = gate[n:n + 1, :]
                beats = past & ((gate > g_n) | ((gate == g_n) & (blk_id < n)))
                rank = jnp.sum(jnp.where(beats, 1.0, 0.0), axis=0, keepdims=True)
                off = jnp.where(rank < float(MOBA_TOPK), slope * float((n - qb) * blk), neg_inf)
            s_scr[qb % 2, n] = t
            offs.append(off)
            cmax = jnp.max(t, axis=0, keepdims=True) + off
            m = cmax if m is None else jnp.maximum(m, cmax)
        return m, offs

    def attend(qb, m, offs):
        for n in range(qb + 1):
            p = jnp.exp2(s_scr[qb % 2, n] - (m - offs[n]))
            p_scr[qb % 2, n * blk:(n + 1) * blk, :] = p.astype(BF16)
        kk = (qb + 1) * blk
        acc = jnp.dot(vt_scr[:, :kk], p_scr[qb % 2, :kk, :], preferred_element_type=F32)
        o_ref[0, qb * blk:(qb + 1) * blk, :] = (acc[:dh] / acc[dh:dh + 1]).T.astype(o_ref.dtype)

    pending = scores(0)
    for qb in range(nb):
        nxt = scores(qb + 1) if qb + 1 < nb else None
        attend(qb, *pending)
        pending = nxt


def _moba(qkv, n_heads, side, side_lead):
    bsz, s, d3 = qkv.shape
    d = d3 // 3
    dh = d // n_heads
    blk = MOBA_BLOCK
    nb = s // blk
    side_in, side_out = _slab_specs(side, side_lead, 0, ROW_CHUNK, bsz * n_heads,
                                    lambda b, h: b * n_heads + h)
    kern = functools.partial(_moba_kernel, n_heads=n_heads)
    return pl.pallas_call(
        kern,
        grid=(bsz, n_heads),
        in_specs=[pl.BlockSpec((1, s, dh), lambda b, h: (b, 0, h)),
                  pl.BlockSpec((1, s, dh), lambda b, h: (b, 0, n_heads + h)),
                  pl.BlockSpec((1, s, dh), lambda b, h: (b, 0, 2 * n_heads + h)),
                  side_in],
        out_specs=(pl.BlockSpec((1, s, dh), lambda b, h: (b, 0, h)), side_out),
        out_shape=(jax.ShapeDtypeStruct((bsz, s, d), BF16),
                   jax.ShapeDtypeStruct(side.shape[1:], BF16)),
        scratch_shapes=[pltpu.VMEM((s, 2 * dh), BF16),
                        pltpu.VMEM((dh + 16, s), BF16),
                        pltpu.VMEM((2, nb, blk, blk), F32),
                        pltpu.VMEM((2, s, blk), BF16)],
        compiler_params=_params(("arbitrary", "arbitrary")),
        name="moba",
    )(qkv, qkv, qkv, side)


def _pad_cols(w, n):
    return jnp.pad(w, ((0, 0), (0, n - w.shape[1])))


def _pad_rows(w, n, before=0):
    return jnp.pad(w, ((before, n - before - w.shape[0]), (0, 0)))


def kernel(x, c, w_ada, b_ada, g_pre_mix, g_post_mix, g_pre_ffn, g_post_ffn, w_ffn_in, w_ffn_out,
           w_in_ab, w_out_ab, a_v_gain, a_v_bias, a_w_s, a_b_s, b_mu, b_w0, b_w2, b_a0, b_a2, b_g2,
           b_k_k, b_k_a, b_r_k, b_lnx_gain, b_lnx_bias, w_qkv, w_o):
    bsz, s, d = x.shape
    depth = w_ada.shape[0]
    a_width = a_v_gain.shape[1]
    b_width = b_w0.shape[1]
    n_lw = b_w2.shape[1]
    n_la = b_a2.shape[1]
    n_lg = b_g2.shape[1]
    n_heads = d // ATT_HEAD
    assert s % MOBA_BLOCK == 0 and s % 1024 == 0
    assert n_lw + n_la <= LANE and n_lg <= 2 * LANE

    mod = _ada_mod(c, w_ada, b_ada)

    w_qkv_h = None
    for layer in range(depth):
        mod3 = mod[layer].reshape(bsz, 1, 6 * d)
        i = layer // 2
        if layer % 2 == 0:
            lora_w = 4 * LANE
            n_main = 2 * a_width + 3 * b_width
            w_t = jnp.swapaxes(w_in_ab[i], 0, 1)
            w_tail = _pad_rows(w_t[n_main:], lora_w)
            z, w_out_h = _norm_mm(x, g_pre_mix[layer], mod3, 1, 0, w_t, BF16, w_out_ab, i,
                                  w_tail=w_tail, transposed=True, tn=lora_w, name="in_proj_ab")
            y_a = _mixer_a(z, a_v_gain[i], a_v_bias[i], a_w_s[i], a_b_s[i], a_width)

            mu = b_mu[i]
            pvec = jnp.stack([mu[0:b_width], mu[b_width:2 * b_width], mu[2 * b_width:3 * b_width],
                              b_w0[i], b_a0[i], b_k_k[i], b_k_a[i], b_r_k[i].reshape(-1),
                              b_lnx_gain[i], b_lnx_bias[i]])
            pvec = _pad_rows(pvec, _PV_ROWS)
            mu_l = _pad_cols(mu[3 * b_width:].reshape(1, -1), lora_w)
            w2p = _pad_rows(b_w2[i], LANE)
            a2p = _pad_rows(b_a2[i], LANE, before=n_lw)
            g2p = _pad_rows(b_g2[i], 2 * LANE)
            cb = 2 * a_width // LANE
            nb_w = b_width // LANE
            y_b, w_ffn_in_h = _rwkv(z, cb, cb + nb_w, cb + 2 * nb_w, n_main // lora_w, lora_w, pvec,
                                    mu_l, w2p, a2p, g2p, b_width, w_ffn_in, layer)
            x, w_ffn_out_h = _out_proj(y_a, y_b, 0, 0, w_out_h, x, mod3, g_post_mix[layer],
                                       w_ffn_out, layer)
        else:
            w_q = w_qkv[i] if w_qkv_h is None else w_qkv_h
            qkv, w_o_h = _norm_mm(x, g_pre_mix[layer], mod3, 1, 0, w_q, BF16, w_o, i,
                                  name="qkv_proj")
            o, w_ffn_in_h = _moba(qkv, n_heads, w_ffn_in, layer)
            x, w_ffn_out_h = _out_proj(o, o, 0, 1, w_o_h, x, mod3, g_post_mix[layer],
                                       w_ffn_out, layer)
        if layer + 1 < depth and (layer + 1) % 2 == 1:
            x, w_qkv_h = _ffn(x, g_pre_ffn[layer], g_post_ffn[layer], mod3, w_ffn_in_h, w_ffn_out_h,
                              w_qkv, (layer + 1) // 2)
        else:
            x = _ffn(x, g_pre_ffn[layer], g_post_ffn[layer], mod3, w_ffn_in_h, w_ffn_out_h)
            w_qkv_h = None
    return x
```

```python
import functools

import jax
import jax.numpy as jnp
from jax import lax
from jax.experimental import pallas as pl
from jax.experimental.pallas import tpu as pltpu

F32 = jnp.float32
BF16 = jnp.bfloat16

NORM_EPS = 1e-6
LN_EPS = 1e-5
GN_EPS = 64e-5

LANE = 128
A_GROUPS = 8
A_CHUNK = 128
RW_HEAD = 64
RW_CHUNK = 64
RW_PACK = 2
MOBA_BLOCK = 256
MOBA_TOPK = 3
ATT_HEAD = 128

NT_DIMS = (((1,), (1,)), ((), ()))
TN_DIMS = (((0,), (0,)), ((), ()))

VMEM_LIMIT = 56 * 1024 * 1024


def _params(sem):
    return pltpu.CompilerParams(dimension_semantics=sem, vmem_limit_bytes=VMEM_LIMIT)


def _slab_specs(side, lead, axis, unit, n_steps, flat_index):
    total = side.shape[1 + axis]
    n_slab = max(n for n in range(1, n_steps + 1) if total % (n * unit) == 0)
    shape = list(side.shape[1:])
    shape[axis] = total // n_slab

    def idx(*grid):
        slab = jnp.minimum(flat_index(*grid), n_slab - 1)
        return (slab, 0) if axis == 0 else (0, slab)

    return (pl.BlockSpec((None,) + tuple(shape), lambda *g: (lead,) + idx(*g)),
            pl.BlockSpec(tuple(shape), idx))


def _rms(x, gain):
    ms = jnp.mean(x * x, axis=-1, keepdims=True)
    return x * lax.rsqrt(ms + NORM_EPS) * gain


ROW_CHUNK = 16
ROW_UNROLL = 8


def _for_row_chunks(n_rows, fn):
    def body(i, carry):
        fn(pl.ds(pl.multiple_of(i * ROW_CHUNK, ROW_CHUNK), ROW_CHUNK))
        return carry
    lax.fori_loop(0, n_rows // ROW_CHUNK, body, 0, unroll=ROW_UNROLL)


def _modulated_norm(x_ref, g_ref, sc_ref, sh_ref, gm_scr, h_scr):
    gm_scr[...] = g_ref[...] * (1.0 + sc_ref[0])

    def rows(r):
        x = x_ref[0, r, :]
        ms = jnp.mean(x * x, axis=-1, keepdims=True)
        h_scr[r, :] = (x * lax.rsqrt(ms + NORM_EPS) * gm_scr[...] + sh_ref[0]).astype(BF16)

    _for_row_chunks(h_scr.shape[0], rows)


def _post_norm_residual(y_scr, x_ref, gt_ref, gpost_ref, gm_scr, o_ref):
    gm_scr[...] = gt_ref[0] * gpost_ref[...]

    def rows(r):
        y = y_scr[r, :]
        ms = jnp.mean(y * y, axis=-1, keepdims=True)
        o_ref[0, r, :] = x_ref[0, r, :] + y * lax.rsqrt(ms + NORM_EPS) * gm_scr[...]

    _for_row_chunks(o_ref.shape[1], rows)


def _ada_kernel(c_ref, w_ref, b_ref, o_ref):
    c = c_ref[...]
    cond = (c * jax.nn.sigmoid(c)).astype(BF16)
    o_ref[0] = jnp.dot(cond, w_ref[0].astype(BF16), preferred_element_type=F32) + b_ref[0]


def _ada_mod(c, w_ada, b_ada):
    depth, d, n = w_ada.shape
    bsz = c.shape[0]
    bp = 8
    c_p = jnp.pad(c, ((0, bp - bsz), (0, 0)))
    tn = 1024
    out = pl.pallas_call(
        _ada_kernel,
        grid=(depth, n // tn),
        in_specs=[pl.BlockSpec((bp, d), lambda l, j: (0, 0)),
                  pl.BlockSpec((1, d, tn), lambda l, j: (l, 0, j)),
                  pl.BlockSpec((1, 1, tn), lambda l, j: (l, 0, j))],
        out_specs=pl.BlockSpec((1, bp, tn), lambda l, j: (l, 0, j)),
        out_shape=jax.ShapeDtypeStruct((depth, bp, n), F32),
        compiler_params=_params(("parallel", "parallel")),
        name="ada_mod",
    )(c_p, w_ada, b_ada.reshape(depth, 1, n))
    return out[:, :bsz]


def _norm_mm_kernel(*refs, n_main, transposed):
    if n_main is None:
        x_ref, g_ref, sc_ref, sh_ref, w_ref, side_ref, o_ref, side_o_ref, h_scr, gm_scr = refs
        wt_ref = None
    else:
        x_ref, g_ref, sc_ref, sh_ref, w_ref, wt_ref, side_ref, o_ref, side_o_ref, h_scr, gm_scr = refs
    j = pl.program_id(2)

    @pl.when(j == 0)
    def _():
        _modulated_norm(x_ref, g_ref, sc_ref, sh_ref, gm_scr, h_scr)

    def project(wr):
        w = wr[...].astype(BF16)
        if transposed:
            y = lax.dot_general(h_scr[...], w, NT_DIMS, preferred_element_type=F32)
        else:
            y = jnp.dot(h_scr[...], w, preferred_element_type=F32)
        o_ref[0] = y.astype(o_ref.dtype)

    if n_main is None:
        project(w_ref)
    else:
        pl.when(j < n_main)(lambda: project(w_ref))
        pl.when(j >= n_main)(lambda: project(wt_ref))
    side_o_ref[...] = side_ref[...].astype(BF16)


def _norm_mm(x, gain, mod3, sc_idx, sh_idx, w, out_dtype, side, side_lead, w_tail=None,
             transposed=False, tm=1024, tn=512, name="norm_mm"):
    bsz, s, d = x.shape
    n_w = w.shape[0] if transposed else w.shape[1]
    n_main = None if w_tail is None else n_w // tn
    n = n_w if w_tail is None else (n_main + 1) * tn
    nm, nj = s // tm, n // tn
    side_in, side_out = _slab_specs(side, side_lead, 1, LANE, bsz * nm * nj,
                                    lambda b, m, j: (b * nm + m) * nj + j)
    last = nj - 1 if n_main is None else n_main - 1
    if transposed:
        w_specs = [pl.BlockSpec((tn, d), lambda b, m, j: (jnp.minimum(j, last), 0))]
        tail_spec = pl.BlockSpec((tn, d), lambda b, m, j: (0, 0), pipeline_mode=pl.Buffered(1))
    else:
        w_specs = [pl.BlockSpec((d, tn), lambda b, m, j: (0, jnp.minimum(j, last)))]
        tail_spec = pl.BlockSpec((d, tn), lambda b, m, j: (0, 0), pipeline_mode=pl.Buffered(1))
    w_args = [w]
    if w_tail is not None:
        w_specs.append(tail_spec)
        w_args.append(w_tail)
    return pl.pallas_call(
        functools.partial(_norm_mm_kernel, n_main=n_main, transposed=transposed),
        grid=(bsz, nm, nj),
        in_specs=[pl.BlockSpec((1, tm, d), lambda b, m, j: (b, m, 0)),
                  pl.BlockSpec((1, d), lambda b, m, j: (0, 0)),
                  pl.BlockSpec((1, 1, d), lambda b, m, j: (b, 0, sc_idx)),
                  pl.BlockSpec((1, 1, d), lambda b, m, j: (b, 0, sh_idx)),
                  *w_specs,
                  side_in],
        out_specs=(pl.BlockSpec((1, tm, tn), lambda b, m, j: (b, m, j)), side_out),
        out_shape=(jax.ShapeDtypeStruct((bsz, s, n), out_dtype),
                   jax.ShapeDtypeStruct(side.shape[1:], BF16)),
        scratch_shapes=[pltpu.VMEM((tm, d), BF16), pltpu.VMEM((1, d), F32)],
        compiler_params=_params(("arbitrary", "arbitrary", "arbitrary")),
        name=name,
    )(x, gain.reshape(1, d), mod3, mod3, *w_args, side)


def _ffn_kernel(*refs, has_side):
    if has_side:
        (x_ref, gpre_ref, sc_ref, sh_ref, gt_ref, gpost_ref, wg_ref, wu_ref, wo_ref, side_ref,
         o_ref, side_o_ref, h_scr, gm_scr, acc_scr) = refs
        side_o_ref[...] = side_ref[...].astype(BF16)
    else:
        (x_ref, gpre_ref, sc_ref, sh_ref, gt_ref, gpost_ref, wg_ref, wu_ref, wo_ref,
         o_ref, h_scr, gm_scr, acc_scr) = refs
    f = pl.program_id(2)

    @pl.when(f == 0)
    def _():
        _modulated_norm(x_ref, gpre_ref, sc_ref, sh_ref, gm_scr, h_scr)
        acc_scr[...] = jnp.zeros_like(acc_scr)

    h = h_scr[...]
    g = jnp.dot(h, wg_ref[...], preferred_element_type=F32)
    u = jnp.dot(h, wu_ref[...], preferred_element_type=F32)
    a = (g * jax.nn.sigmoid(g) * u).astype(BF16)
    acc_scr[...] += jnp.dot(a, wo_ref[...], preferred_element_type=F32)

    @pl.when(f == pl.num_programs(2) - 1)
    def _():
        _post_norm_residual(acc_scr, x_ref, gt_ref, gpost_ref, gm_scr, o_ref)


def _ffn(x, gpre, gpost, mod3, w_in, w_out, side=None, side_lead=0, tm=1024, tf=512):
    bsz, s, d = x.shape
    fh = w_out.shape[0]
    nf = fh // tf
    nm = s // tm
    out_spec = pl.BlockSpec((1, tm, d), lambda b, m, f: (b, m, 0), pipeline_mode=pl.Buffered(1))
    out_shape = jax.ShapeDtypeStruct((bsz, s, d), F32)
    side_specs, side_args = [], []
    if side is not None:
        side_in, side_out = _slab_specs(side, side_lead, 1, LANE, bsz * nm * nf,
                                        lambda b, m, f: (b * nm + m) * nf + f)
        side_specs, side_args = [side_in], [side]
        out_spec = (out_spec, side_out)
        out_shape = (out_shape, jax.ShapeDtypeStruct(side.shape[1:], BF16))
    return pl.pallas_call(
        functools.partial(_ffn_kernel, has_side=side is not None),
        grid=(bsz, nm, nf),
        in_specs=[pl.BlockSpec((1, tm, d), lambda b, m, f: (b, m, 0), pipeline_mode=pl.Buffered(1)),
                  pl.BlockSpec((1, d), lambda b, m, f: (0, 0)),
                  pl.BlockSpec((1, 1, d), lambda b, m, f: (b, 0, 4)),
                  pl.BlockSpec((1, 1, d), lambda b, m, f: (b, 0, 3)),
                  pl.BlockSpec((1, 1, d), lambda b, m, f: (b, 0, 5)),
                  pl.BlockSpec((1, d), lambda b, m, f: (0, 0)),
                  pl.BlockSpec((d, tf), lambda b, m, f: (0, f)),
                  pl.BlockSpec((d, tf), lambda b, m, f: (0, nf + f)),
                  pl.BlockSpec((tf, d), lambda b, m, f: (f, 0)),
                  *side_specs],
        out_specs=out_spec,
        out_shape=out_shape,
        scratch_shapes=[pltpu.VMEM((tm, d), BF16), pltpu.VMEM((1, d), F32), pltpu.VMEM((tm, d), F32)],
        compiler_params=_params(("arbitrary", "arbitrary", "arbitrary")),
        name="ffn",
    )(x, gpre.reshape(1, d), mod3, mod3, mod3, gpost.reshape(1, d), w_in, w_in, w_out, *side_args)


def _out_proj_kernel(a0_ref, a1_ref, w0_ref, w1_ref, x_ref, gt_ref, gpost_ref, side_ref,
                     o_ref, side_o_ref, gm_scr, y_scr):
    side_o_ref[...] = side_ref[...].astype(BF16)
    y_scr[...] = (jnp.dot(a0_ref[0], w0_ref[...], preferred_element_type=F32)
                  + jnp.dot(a1_ref[0], w1_ref[...], preferred_element_type=F32))
    _post_norm_residual(y_scr, x_ref, gt_ref, gpost_ref, gm_scr, o_ref)


def _out_proj(a0, a1, col0, col1, w, x, mod3, gpost, side, side_lead, tm=512):
    bsz, s, d = x.shape
    kh = w.shape[0] // 2
    nm = s // tm
    side_in, side_out = _slab_specs(side, side_lead, 0, ROW_CHUNK, bsz * nm, lambda b, m: b * nm + m)
    return pl.pallas_call(
        _out_proj_kernel,
        grid=(bsz, nm),
        in_specs=[pl.BlockSpec((1, tm, kh), lambda b, m: (b, m, col0)),
                  pl.BlockSpec((1, tm, kh), lambda b, m: (b, m, col1)),
                  pl.BlockSpec((kh, d), lambda b, m: (0, 0)),
                  pl.BlockSpec((kh, d), lambda b, m: (1, 0)),
                  pl.BlockSpec((1, tm, d), lambda b, m: (b, m, 0)),
                  pl.BlockSpec((1, 1, d), lambda b, m: (b, 0, 2)),
                  pl.BlockSpec((1, d), lambda b, m: (0, 0)),
                  side_in],
        out_specs=(pl.BlockSpec((1, tm, d), lambda b, m: (b, m, 0)), side_out),
        out_shape=(jax.ShapeDtypeStruct((bsz, s, d), F32),
                   jax.ShapeDtypeStruct(side.shape[1:], BF16)),
        scratch_shapes=[pltpu.VMEM((1, d), F32), pltpu.VMEM((tm, d), F32)],
        compiler_params=_params(("arbitrary", "arbitrary")),
        name="out_proj",
    )(a0, a1, w, w, x, mod3, gpost.reshape(1, d), side)


def _mixer_a_kernel(z_ref, vg_ref, vb_ref, ws_ref, bst_ref, o_ref):
    z = jax.nn.gelu(z_ref[0].astype(F32))
    wdt = z.shape[1] // 2
    u = z[:, :wdt]
    v = z[:, wdt:]
    mu = jnp.mean(v, axis=-1, keepdims=True)
    dv = v - mu
    var = jnp.mean(dv * dv, axis=-1, keepdims=True)
    vn = (dv * lax.rsqrt(var + LN_EPS) * vg_ref[...] + vb_ref[...]).astype(BF16)
    ch = z.shape[0]
    causal = (lax.broadcasted_iota(jnp.int32, (ch, ch), 0)
              >= lax.broadcasted_iota(jnp.int32, (ch, ch), 1))
    gd = wdt // A_GROUPS
    for g in range(A_GROUPS):
        w = jnp.where(causal, ws_ref[g], 0.0).astype(BF16)
        sv = jnp.dot(w, vn[:, g * gd:(g + 1) * gd], preferred_element_type=F32)
        sv = sv + bst_ref[:, g:g + 1]
        o_ref[0, :, g * gd:(g + 1) * gd] = (u[:, g * gd:(g + 1) * gd] * sv).astype(o_ref.dtype)


def _mixer_a(z, v_gain, v_bias, w_s, b_s, width):
    bsz, s, _ = z.shape
    ch = A_CHUNK
    return pl.pallas_call(
        _mixer_a_kernel,
        grid=(bsz, s // ch),
        in_specs=[pl.BlockSpec((1, ch, 2 * width), lambda b, c: (b, c, 0)),
                  pl.BlockSpec((1, width), lambda b, c: (0, 0)),
                  pl.BlockSpec((1, width), lambda b, c: (0, 0)),
                  pl.BlockSpec((A_GROUPS, ch, ch), lambda b, c: (0, 0, 0)),
                  pl.BlockSpec((ch, A_GROUPS), lambda b, c: (0, 0))],
        out_specs=pl.BlockSpec((1, ch, width), lambda b, c: (b, c, 0)),
        out_shape=jax.ShapeDtypeStruct((bsz, s, width), BF16),
        compiler_params=_params(("parallel", "parallel")),
        name="mixer_a",
    )(z, v_gain.reshape(1, width), v_bias.reshape(1, width), w_s, b_s.T)


_PV_MU_R, _PV_MU_K, _PV_MU_V, _PV_W0, _PV_A0, _PV_KK, _PV_KA, _PV_RK, _PV_LG, _PV_LB = range(10)
_PV_ROWS = 16


def _shift_lerp(x, prev_row, mu):
    rolled = pltpu.roll(x, 1, axis=0)
    first = lax.broadcasted_iota(jnp.int32, x.shape, 0) == 0
    xp = jnp.where(first, prev_row, rolled)
    return x + mu * (xp - x)


def _split_bf16(x):
    hi = x.astype(BF16)
    lo = (x - hi.astype(F32)).astype(BF16)
    return hi, lo


def _mm(x, y):
    return jnp.dot(x.astype(BF16), y.astype(BF16), preferred_element_type=F32)


def _mm_nt(x, y):
    return lax.dot_general(x.astype(BF16), y.astype(BF16), NT_DIMS, preferred_element_type=F32)


def _mm_tn(x, y):
    return lax.dot_general(x.astype(BF16), y.astype(BF16), TN_DIMS, preferred_element_type=F32)


def _mm_x3(x, y):
    xh, xl = _split_bf16(x)
    yh, yl = _split_bf16(y)
    return (jnp.dot(xh, yh, preferred_element_type=F32) + jnp.dot(xl, yh, preferred_element_type=F32)
            + jnp.dot(xh, yl, preferred_element_type=F32))


def _mm_exact_rhs(x, e_bf16):
    xh, xl = _split_bf16(x)
    return (jnp.dot(xh, e_bf16, preferred_element_type=F32)
            + jnp.dot(xl, e_bf16, preferred_element_type=F32))


def _rwkv_kernel(zr_ref, zk_ref, zv_ref, zl_ref, pva_ref, pvb_ref, mul_ref, w2_ref, a2_ref, g2_ref,
                 side_ref, o_ref, side_o_ref,
                 s_scr, prev_scr, prevl_scr, th_hi_scr, th_lo_scr, xw_scr, sg_scr,
                 q_scr, y_scr, gm_scr, cm_scr, pl_scr, bonus_scr, g_scr, *, chunk, n_t, n_p, n_tiles):
    i = pl.program_id(0)
    t_rows = zr_ref.shape[1]
    lanes = zr_ref.shape[2]
    lora = xw_scr.shape[1] + sg_scr.shape[1]
    L = chunk
    SL = RW_PACK * L
    n_chunks = t_rows // L
    side_o_ref[...] = side_ref[...].astype(BF16)

    ia = jnp.minimum(i, n_tiles - 1)
    ib = jnp.maximum(i - 1, 0)
    t_a, p_a = (ia // n_p) % n_t, ia % n_p
    t_b, p_b = (ib // n_p) % n_t, ib % n_p
    slot_a = i % 2
    slot_b = 1 - slot_a

    @pl.when(i == 0)
    def _():
        for ref in (s_scr, prev_scr, prevl_scr, q_scr, y_scr, gm_scr, cm_scr, pl_scr, bonus_scr, g_scr):
            ref[...] = jnp.zeros_like(ref)

    @pl.when(p_a == 0)
    def _():
        zl = zl_ref[0][:, :lora].astype(F32)
        prev = jnp.where(t_a == 0, 0.0, prevl_scr[0:1, :lora])
        zls = _shift_lerp(zl, prev, mul_ref[:, :lora])
        prevl_scr[0:1, :lora] = zl[t_rows - 1:t_rows, :]
        x_wa = zls[:, :LANE]
        th_hi, th_lo = _split_bf16(jnp.tanh(x_wa))
        th_hi_scr[...] = th_hi
        th_lo_scr[...] = th_lo
        xw_scr[...] = x_wa.astype(BF16)
        sg_scr[...] = jax.nn.sigmoid(zls[:, LANE:]).astype(BF16)

    def pva(r):
        return pva_ref[r:r + 1, :]

    zr = zr_ref[0].astype(F32)
    zk = zk_ref[0].astype(F32)
    zv = zv_ref[0].astype(F32)
    prev = jnp.where(t_a == 0, 0.0, prev_scr[p_a])
    r = _shift_lerp(zr, prev[0:1, :], pva(_PV_MU_R))
    k = _shift_lerp(zk, prev[1:2, :], pva(_PV_MU_K))
    v = _shift_lerp(zv, prev[2:3, :], pva(_PV_MU_V))
    prev_scr[p_a] = jnp.concatenate([zr[t_rows - 1:t_rows, :], zk[t_rows - 1:t_rows, :],
                                     zv[t_rows - 1:t_rows, :], jnp.zeros((5, lanes), F32)], axis=0)

    w2_hi, w2_lo = _split_bf16(w2_ref[...])
    th_hi = th_hi_scr[...]
    w_pre = (pva(_PV_W0) + jnp.dot(th_hi, w2_hi, preferred_element_type=F32)
             + jnp.dot(th_lo_scr[...], w2_hi, preferred_element_type=F32)
             + jnp.dot(th_hi, w2_lo, preferred_element_type=F32))
    t = -w_pre
    softplus = jnp.maximum(t, 0.0) + jnp.log1p(jnp.exp(-jnp.abs(t)))
    log_decay = -jnp.exp(-softplus - 0.5)
    a = jax.nn.sigmoid(pva(_PV_A0) + _mm(xw_scr[...], a2_ref[...]))
    g = _mm(sg_scr[...], g2_ref[...])

    li = lax.broadcasted_iota(jnp.int32, (lanes, lanes), 0) // RW_HEAD
    lj = lax.broadcasted_iota(jnp.int32, (lanes, lanes), 1) // RW_HEAD
    same_head = li == lj
    e_head = jnp.where(same_head, 1.0, 0.0).astype(BF16)

    kk = k * pva(_PV_KK)
    kk = kk / jnp.maximum(jnp.sqrt(_mm(kk * kk, e_head)), 1e-12)
    k = k * (1.0 + (a - 1.0) * pva(_PV_KA))
    av = -kk
    bv = kk * a
    bonus = _mm(r * k * pva(_PV_RK), e_head) * v

    ti = lax.broadcasted_iota(jnp.int32, (L, L), 0)
    tj = lax.broadcasted_iota(jnp.int32, (L, L), 1)
    tri = jnp.where(ti >= tj, 1.0, 0.0).astype(BF16)
    lw_hi, lw_lo = _split_bf16(log_decay)

    si = lax.broadcasted_iota(jnp.int32, (SL, SL), 0)
    sj = lax.broadcasted_iota(jnp.int32, (SL, SL), 1)
    same_blk = (si // L) == (sj // L)
    m_strict = same_blk & (si > sj)
    m_incl = same_blk & (si >= sj)
    eye = jnp.where(si == sj, 1.0, 0.0)
    lane_head = lax.broadcasted_iota(jnp.int32, (1, lanes), 1) // RW_HEAD
    n_sq = max(L.bit_length() - 2, 0)

    def stack(x):
        return jnp.concatenate([jnp.where(lane_head == h, x, 0.0) for h in range(RW_PACK)], axis=0)

    def unstack(x):
        out = x[0:L]
        for h in range(1, RW_PACK):
            out = out + x[h * L:(h + 1) * L]
        return out

    chain = {"s": jnp.where(t_b == 0, 0.0, s_scr[p_b]), "c": 0, "y": []}

    def chain_step():
        c = chain["c"]
        if c >= n_chunks:
            return
        s0 = chain["s"]
        s0b = s0.astype(BF16)
        chain["y"].append(y_scr[slot_b, c * L:(c + 1) * L, :] + _mm_nt(q_scr[slot_b, c], s0b))
        chain["s"] = s0 * pl_scr[slot_b, c][0:1, :] + _mm(s0b, gm_scr[slot_b, c]) + cm_scr[slot_b, c]
        chain["c"] = c + 1

    cs = range(n_chunks)
    rows = [slice(c * L, (c + 1) * L) for c in cs]
    cm = [jnp.dot(tri, lw_hi[rw], preferred_element_type=F32)
          + jnp.dot(tri, lw_lo[rw], preferred_element_type=F32) for rw in rows]
    cm_last = [x[L - 1:L, :] for x in cm]
    r_t = [r[rw] * jnp.exp(cm[c]) for c, rw in enumerate(rows)]
    a_st = [stack(av[rw] * jnp.exp(cm[c] - log_decay[rw])).astype(BF16) for c, rw in enumerate(rows)]
    e_neg = [jnp.exp(-x) for x in cm]
    k_rep = [jnp.concatenate([(k[rw] * e_neg[c]).astype(BF16)] * RW_PACK, axis=0)
             for c, rw in enumerate(rows)]
    b_rep = [jnp.concatenate([(bv[rw] * e_neg[c]).astype(BF16)] * RW_PACK, axis=0)
             for c, rw in enumerate(rows)]
    ar_st = [jnp.concatenate([a_st[c], stack(r_t[c]).astype(BF16)], axis=0) for c in cs]
    chain_step()
    prod_k = [_mm_nt(ar_st[c], k_rep[c]) for c in cs]
    prod_b = [_mm_nt(ar_st[c], b_rep[c]) for c in cs]
    a_ak = [jnp.where(m_strict, x[:SL], 0.0).astype(BF16) for x in prod_k]
    a_rk = [jnp.where(m_incl, x[SL:], 0.0).astype(BF16) for x in prod_k]
    a_ab = [jnp.where(m_strict, x[:SL], 0.0) for x in prod_b]
    a_rb = [jnp.where(m_incl, x[SL:], 0.0).astype(BF16) for x in prod_b]
    chain_step()

    xp = [x.astype(BF16) for x in a_ab]
    tinv = [eye + x for x in a_ab]
    for _ in range(n_sq):
        xp = [_mm(x, x).astype(BF16) for x in xp]
        tinv = [tinv[c] + _mm(tinv[c], xp[c]) for c in cs]
        chain_step()
    tinv = [x.astype(BF16) for x in tinv]

    v_st = [stack(v[rw]).astype(BF16) for rw in rows]
    x0 = [_mm(a_ak[c], v_st[c]).astype(BF16) for c in cs]
    chain_step()
    wu = [_mm(tinv[c], jnp.concatenate([a_st[c], x0[c]], axis=1)) for c in cs]
    chain_step()
    yq = [_mm(a_rb[c], wu[c]) for c in cs]
    y0 = [_mm(a_rk[c], v_st[c]) for c in cs]
    while chain["c"] < n_chunks:
        chain_step()
    cmats, gmats = [], []
    for c, rw in enumerate(rows):
        e_rem = jnp.exp(cm_last[c] - cm[c])
        b_b = bv[rw] * e_rem
        kb = jnp.concatenate([k[rw] * e_rem, b_b], axis=0)
        vu = jnp.concatenate([v[rw], unstack(wu[c][:, lanes:])], axis=0)
        cmats.append(jnp.where(same_head, _mm_tn(vu, kb), 0.0))
        gmats.append(jnp.where(same_head, _mm_tn(unstack(wu[c][:, :lanes]), b_b), 0.0).astype(BF16))

    s_scr[p_b] = chain["s"]
    y = jnp.concatenate(chain["y"], axis=0)
    inv_n = 1.0 / RW_HEAD
    mean = _mm_exact_rhs(y, e_head) * inv_n
    dy = y - mean
    var = _mm(dy * dy, e_head) * inv_n
    yn = dy * lax.rsqrt(var + GN_EPS) * pvb_ref[_PV_LG:_PV_LG + 1, :] + pvb_ref[_PV_LB:_PV_LB + 1, :]
    o_ref[0] = ((yn + bonus_scr[slot_b]) * g_scr[slot_b]).astype(o_ref.dtype)

    for c, rw in enumerate(rows):
        q_scr[slot_a, c] = (r_t[c] + unstack(yq[c][:, :lanes])).astype(BF16)
        y_scr[slot_a, rw, :] = unstack(yq[c][:, lanes:] + y0[c])
        cm_scr[slot_a, c] = cmats[c]
        gm_scr[slot_a, c] = gmats[c]
        pl_scr[slot_a, c] = jnp.broadcast_to(jnp.exp(cm_last[c]), (8, lanes))
    bonus_scr[slot_a] = bonus
    g_scr[slot_a] = g


def _rwkv(z, col_r, col_k, col_v, col_l, lora_w, pvec, mu_l, w2p, a2p, g2p, width, side, side_lead,
          t_rows=512):
    bsz, s, _ = z.shape
    n_p = width // LANE
    n_chunks = t_rows // RW_CHUNK
    n_t = s // t_rows
    n_tiles = bsz * n_t * n_p
    lora = 3 * LANE

    def tile_a(i):
        ia = jnp.minimum(i, n_tiles - 1)
        return ia // (n_t * n_p), (ia // n_p) % n_t, ia % n_p

    def tile_b(i):
        ib = jnp.maximum(i - 1, 0)
        return ib // (n_t * n_p), (ib // n_p) % n_t, ib % n_p

    def z_spec(col):
        def idx(i):
            b, t, p = tile_a(i)
            return b, t, col + p
        return pl.BlockSpec((1, t_rows, LANE), idx)

    def zl_idx(i):
        b, t, _ = tile_a(i)
        return b, t, col_l

    def out_idx(i):
        b, t, p = tile_b(i)
        return b, t, p

    side_in, side_out = _slab_specs(side, side_lead, 0, ROW_CHUNK, n_tiles + 1, lambda i: i)
    kern = functools.partial(_rwkv_kernel, chunk=RW_CHUNK, n_t=n_t, n_p=n_p, n_tiles=n_tiles)
    tile_f32 = pltpu.VMEM((2, t_rows, LANE), F32)
    return pl.pallas_call(
        kern,
        grid=(n_tiles + 1,),
        in_specs=[z_spec(col_r), z_spec(col_k), z_spec(col_v),
                  pl.BlockSpec((1, t_rows, lora_w), zl_idx),
                  pl.BlockSpec((_PV_ROWS, LANE), lambda i: (0, tile_a(i)[2])),
                  pl.BlockSpec((_PV_ROWS, LANE), lambda i: (0, tile_b(i)[2])),
                  pl.BlockSpec((1, lora_w), lambda i: (0, 0)),
                  pl.BlockSpec((LANE, LANE), lambda i: (0, tile_a(i)[2])),
                  pl.BlockSpec((LANE, LANE), lambda i: (0, tile_a(i)[2])),
                  pl.BlockSpec((2 * LANE, LANE), lambda i: (0, tile_a(i)[2])),
                  side_in],
        out_specs=(pl.BlockSpec((1, t_rows, LANE), out_idx), side_out),
        out_shape=(jax.ShapeDtypeStruct((bsz, s, width), BF16),
                   jax.ShapeDtypeStruct(side.shape[1:], BF16)),
        scratch_shapes=[pltpu.VMEM((n_p, LANE, LANE), F32),
                        pltpu.VMEM((n_p, 8, LANE), F32),
                        pltpu.VMEM((8, lora_w), F32),
                        pltpu.VMEM((t_rows, LANE), BF16),
                        pltpu.VMEM((t_rows, LANE), BF16),
                        pltpu.VMEM((t_rows, LANE), BF16),
                        pltpu.VMEM((t_rows, lora - LANE), BF16),
                        pltpu.VMEM((2, n_chunks, RW_CHUNK, LANE), BF16),
                        tile_f32,
                        pltpu.VMEM((2, n_chunks, LANE, LANE), BF16),
                        pltpu.VMEM((2, n_chunks, LANE, LANE), F32),
                        pltpu.VMEM((2, n_chunks, 8, LANE), F32),
                        tile_f32,
                        tile_f32],
        compiler_params=_params(("arbitrary",)),
        name="rwkv7",
    )(z, z, z, z, pvec, pvec, mu_l, w2p, a2p, g2p, side)


def _moba_kernel(q_ref, k_ref, v_ref, side_ref, o_ref, side_o_ref, ka_scr, vt_scr, s_scr, p_scr, *,
                 n_heads):
    h = pl.program_id(1)
    side_o_ref[...] = side_ref[...].astype(BF16)
    s_len = q_ref.shape[1]
    dh = q_ref.shape[2]
    blk = MOBA_BLOCK
    nb = s_len // blk
    log2e = 1.4426950408889634
    scale = dh ** -0.5 * log2e
    neg_inf = -jnp.inf

    def slope_row(width):
        return log2e * jnp.exp(jnp.full((1, width), -8.0 / n_heads * 0.6931471805599453, F32)
                               * (h + 1).astype(F32))

    lane = lax.broadcasted_iota(jnp.int32, (blk, dh), 1)
    bias = slope_row(dh) * lax.broadcasted_iota(jnp.int32, (blk, dh), 0).astype(F32)
    extra = jnp.zeros((blk, dh), F32)
    for col in range(3):
        part = bias.astype(BF16).astype(F32)
        extra = jnp.where(lane == col, part, extra)
        bias = bias - part
    extra = extra.astype(BF16)
    ones_cols = jnp.where(lane < 3, 1.0, 0.0).astype(BF16)

    kmean = []
    for j in range(nb):
        rows = slice(j * blk, (j + 1) * blk)
        k_j = k_ref[0, rows, :]
        kmean.append(jnp.mean(k_j.astype(F32), axis=0, keepdims=True))
        ka_scr[rows, :dh] = k_j
        ka_scr[rows, dh:] = extra
        vt_scr[:dh, rows] = v_ref[0, rows, :].astype(F32).T.astype(BF16)
    sub = lax.broadcasted_iota(jnp.int32, (vt_scr.shape[0] - dh, s_len), 0)
    vt_scr[dh:, :] = jnp.where(sub == 0, 1.0, 0.0).astype(BF16)
    kmean = jnp.concatenate(kmean, axis=0)
    kmean_parts = []
    for _ in range(3):
        part = kmean.astype(BF16)
        kmean_parts.append(part)
        kmean = kmean - part.astype(F32)

    slope = slope_row(blk)
    causal = (lax.broadcasted_iota(jnp.int32, (blk, blk), 1)
              >= lax.broadcasted_iota(jnp.int32, (blk, blk), 0))
    blk_id = lax.broadcasted_iota(jnp.int32, (nb, 1), 0)

    def scores(qb):
        q = q_ref[0, qb * blk:(qb + 1) * blk, :]
        q_aug = jnp.concatenate([(q.astype(F32) * scale).astype(BF16), ones_cols], axis=1)
        gate = sum(lax.dot_general(part, q, NT_DIMS, preferred_element_type=F32)
                   for part in kmean_parts)
        past = blk_id < qb
        offs = []
        m = None
        t_all = lax.dot_general(ka_scr[:(qb + 1) * blk, :], q_aug, NT_DIMS,
                                preferred_element_type=F32)
        for n in range(qb + 1):
            t = t_all[n * blk:(n + 1) * blk, :]
            if n == qb:
                t = jnp.where(causal, t, neg_inf)
                off = jnp.zeros((1, blk), F32)
            else:
                g_n = gate[n:n + 1, :]
                beats = past & ((gate > g_n) | ((gate == g_n) & (blk_id < n)))
                rank = jnp.sum(jnp.where(beats, 1.0, 0.0), axis=0, keepdims=True)
                off = jnp.where(rank < float(MOBA_TOPK), slope * float((n - qb) * blk), neg_inf)
            s_scr[qb % 2, n] = t
            offs.append(off)
            cmax = jnp.max(t, axis=0, keepdims=True) + off
            m = cmax if m is None else jnp.maximum(m, cmax)
        return m, offs

    def attend(qb, m, offs):
        for n in range(qb + 1):
            p = jnp.exp2(s_scr[qb % 2, n] - (m - offs[n]))
            p_scr[qb % 2, n * blk:(n + 1) * blk, :] = p.astype(BF16)
        kk = (qb + 1) * blk
        acc = jnp.dot(vt_scr[:, :kk], p_scr[qb % 2, :kk, :], preferred_element_type=F32)
        o_ref[0, qb * blk:(qb + 1) * blk, :] = (acc[:dh] / acc[dh:dh + 1]).T.astype(o_ref.dtype)

    pending = scores(0)
    for qb in range(nb):
        nxt = scores(qb + 1) if qb + 1 < nb else None
        attend(qb, *pending)
        pending = nxt


def _moba(qkv, n_heads, side, side_lead):
    bsz, s, d3 = qkv.shape
    d = d3 // 3
    dh = d // n_heads
    blk = MOBA_BLOCK
    nb = s // blk
    side_in, side_out = _slab_specs(side, side_lead, 0, ROW_CHUNK, bsz * n_heads,
                                    lambda b, h: b * n_heads + h)
    kern = functools.partial(_moba_kernel, n_heads=n_heads)
    return pl.pallas_call(
        kern,
        grid=(bsz, n_heads),
        in_specs=[pl.BlockSpec((1, s, dh), lambda b, h: (b, 0, h)),
                  pl.BlockSpec((1, s, dh), lambda b, h: (b, 0, n_heads + h)),
                  pl.BlockSpec((1, s, dh), lambda b, h: (b, 0, 2 * n_heads + h)),
                  side_in],
        out_specs=(pl.BlockSpec((1, s, dh), lambda b, h: (b, 0, h)), side_out),
        out_shape=(jax.ShapeDtypeStruct((bsz, s, d), BF16),
                   jax.ShapeDtypeStruct(side.shape[1:], BF16)),
        scratch_shapes=[pltpu.VMEM((s, 2 * dh), BF16),
                        pltpu.VMEM((dh + 16, s), BF16),
                        pltpu.VMEM((2, nb, blk, blk), F32),
                        pltpu.VMEM((2, s, blk), BF16)],
        compiler_params=_params(("arbitrary", "arbitrary")),
        name="moba",
    )(qkv, qkv, qkv, side)


def _pad_cols(w, n):
    return jnp.pad(w, ((0, 0), (0, n - w.shape[1])))


def _pad_rows(w, n, before=0):
    return jnp.pad(w, ((before, n - before - w.shape[0]), (0, 0)))


def kernel(x, c, w_ada, b_ada, g_pre_mix, g_post_mix, g_pre_ffn, g_post_ffn, w_ffn_in, w_ffn_out,
           w_in_ab, w_out_ab, a_v_gain, a_v_bias, a_w_s, a_b_s, b_mu, b_w0, b_w2, b_a0, b_a2, b_g2,
           b_k_k, b_k_a, b_r_k, b_lnx_gain, b_lnx_bias, w_qkv, w_o):
    bsz, s, d = x.shape
    depth = w_ada.shape[0]
    a_width = a_v_gain.shape[1]
    b_width = b_w0.shape[1]
    n_lw = b_w2.shape[1]
    n_la = b_a2.shape[1]
    n_lg = b_g2.shape[1]
    n_heads = d // ATT_HEAD
    assert s % MOBA_BLOCK == 0 and s % 1024 == 0
    assert n_lw + n_la <= LANE and n_lg <= 2 * LANE

    mod = _ada_mod(c, w_ada, b_ada)

    w_qkv_h = None
    for layer in range(depth):
        mod3 = mod[layer].reshape(bsz, 1, 6 * d)
        i = layer // 2
        if layer % 2 == 0:
            lora_w = 4 * LANE
            n_main = 2 * a_width + 3 * b_width
            w_t = jnp.swapaxes(w_in_ab[i], 0, 1)
            w_tail = _pad_rows(w_t[n_main:], lora_w)
            z, w_out_h = _norm_mm(x, g_pre_mix[layer], mod3, 1, 0, w_t, BF16, w_out_ab, i,
                                  w_tail=w_tail, transposed=True, tn=lora_w, name="in_proj_ab")
            y_a = _mixer_a(z, a_v_gain[i], a_v_bias[i], a_w_s[i], a_b_s[i], a_width)

            mu = b_mu[i]
            pvec = jnp.stack([mu[0:b_width], mu[b_width:2 * b_width], mu[2 * b_width:3 * b_width],
                              b_w0[i], b_a0[i], b_k_k[i], b_k_a[i], b_r_k[i].reshape(-1),
                              b_lnx_gain[i], b_lnx_bias[i]])
            pvec = _pad_rows(pvec, _PV_ROWS)
            mu_l = _pad_cols(mu[3 * b_width:].reshape(1, -1), lora_w)
            w2p = _pad_rows(b_w2[i], LANE)
            a2p = _pad_rows(b_a2[i], LANE, before=n_lw)
            g2p = _pad_rows(b_g2[i], 2 * LANE)
            cb = 2 * a_width // LANE
            nb_w = b_width // LANE
            y_b, w_ffn_in_h = _rwkv(z, cb, cb + nb_w, cb + 2 * nb_w, n_main // lora_w, lora_w, pvec,
                                    mu_l, w2p, a2p, g2p, b_width, w_ffn_in, layer)
            x, w_ffn_out_h = _out_proj(y_a, y_b, 0, 0, w_out_h, x, mod3, g_post_mix[layer],
                                       w_ffn_out, layer)
        else:
            w_q = w_qkv[i] if w_qkv_h is None else w_qkv_h
            qkv, w_o_h = _norm_mm(x, g_pre_mix[layer], mod3, 1, 0, w_q, BF16, w_o, i,
                                  tn=1024, name="qkv_proj")
            o, w_ffn_in_h = _moba(qkv, n_heads, w_ffn_in, layer)
            x, w_ffn_out_h = _out_proj(o, o, 0, 1, w_o_h, x, mod3, g_post_mix[layer],
                                       w_ffn_out, layer)
        if layer + 1 < depth and (layer + 1) % 2 == 1:
            x, w_qkv_h = _ffn(x, g_pre_ffn[layer], g_post_ffn[layer], mod3, w_ffn_in_h, w_ffn_out_h,
                              w_qkv, (layer + 1) // 2)
        else:
            x = _ffn(x, g_pre_ffn[layer], g_post_ffn[layer], mod3, w_ffn_in_h, w_ffn_out_h)
            w_qkv_h = None
    return x
```

```python
import functools

import jax
import jax.numpy as jnp
from jax import lax
from jax.experimental import pallas as pl
from jax.experimental.pallas import tpu as pltpu

F32 = jnp.float32
BF16 = jnp.bfloat16

NORM_EPS = 1e-6
LN_EPS = 1e-5
GN_EPS = 64e-5

LANE = 128
A_GROUPS = 8
A_CHUNK = 128
RW_HEAD = 64
RW_CHUNK = 64
RW_PACK = 2
MOBA_BLOCK = 256
MOBA_TOPK = 3
ATT_HEAD = 128

NT_DIMS = (((1,), (1,)), ((), ()))
TN_DIMS = (((0,), (0,)), ((), ()))

VMEM_LIMIT = 56 * 1024 * 1024


def _params(sem):
    return pltpu.CompilerParams(dimension_semantics=sem, vmem_limit_bytes=VMEM_LIMIT)


def _slab_specs(side, lead, axis, unit, n_steps, flat_index):
    total = side.shape[1 + axis]
    n_slab = max(n for n in range(1, n_steps + 1) if total % (n * unit) == 0)
    shape = list(side.shape[1:])
    shape[axis] = total // n_slab

    def idx(*grid):
        slab = jnp.minimum(flat_index(*grid), n_slab - 1)
        return (slab, 0) if axis == 0 else (0, slab)

    return (pl.BlockSpec((None,) + tuple(shape), lambda *g: (lead,) + idx(*g)),
            pl.BlockSpec(tuple(shape), idx))


def _rms(x, gain):
    ms = jnp.mean(x * x, axis=-1, keepdims=True)
    return x * lax.rsqrt(ms + NORM_EPS) * gain


ROW_CHUNK = 16
ROW_UNROLL = 8


def _for_row_chunks(n_rows, fn):
    def body(i, carry):
        fn(pl.ds(pl.multiple_of(i * ROW_CHUNK, ROW_CHUNK), ROW_CHUNK))
        return carry
    lax.fori_loop(0, n_rows // ROW_CHUNK, body, 0, unroll=ROW_UNROLL)


def _modulated_norm(x_ref, g_ref, sc_ref, sh_ref, gm_scr, h_scr):
    gm_scr[...] = g_ref[...] * (1.0 + sc_ref[0])

    def rows(r):
        x = x_ref[0, r, :]
        ms = jnp.mean(x * x, axis=-1, keepdims=True)
        h_scr[r, :] = (x * lax.rsqrt(ms + NORM_EPS) * gm_scr[...] + sh_ref[0]).astype(BF16)

    _for_row_chunks(h_scr.shape[0], rows)


def _post_norm_residual(y_scr, x_ref, gt_ref, gpost_ref, gm_scr, o_ref):
    gm_scr[...] = gt_ref[0] * gpost_ref[...]

    def rows(r):
        y = y_scr[r, :]
        ms = jnp.mean(y * y, axis=-1, keepdims=True)
        o_ref[0, r, :] = x_ref[0, r, :] + y * lax.rsqrt(ms + NORM_EPS) * gm_scr[...]

    _for_row_chunks(o_ref.shape[1], rows)


def _ada_kernel(c_ref, w_ref, b_ref, side_ref, o_ref, side_o_ref):
    c = c_ref[...]
    cond = (c * jax.nn.sigmoid(c)).astype(BF16)
    o_ref[0] = jnp.dot(cond, w_ref[0].astype(BF16), preferred_element_type=F32) + b_ref[0]
    side_o_ref[...] = side_ref[...].astype(BF16)


def _ada_mod(c, w_ada, b_ada, side, side_lead):
    depth, d, n = w_ada.shape
    bsz = c.shape[0]
    bp = 8
    c_p = jnp.pad(c, ((0, bp - bsz), (0, 0)))
    tn = 1024
    nj = n // tn
    side_in, side_out = _slab_specs(side, side_lead, 0, ROW_CHUNK, depth * nj, lambda l, j: l * nj + j)
    out, side_h = pl.pallas_call(
        _ada_kernel,
        grid=(depth, nj),
        in_specs=[pl.BlockSpec((bp, d), lambda l, j: (0, 0)),
                  pl.BlockSpec((1, d, tn), lambda l, j: (l, 0, j)),
                  pl.BlockSpec((1, 1, tn), lambda l, j: (l, 0, j)),
                  side_in],
        out_specs=(pl.BlockSpec((1, bp, tn), lambda l, j: (l, 0, j)), side_out),
        out_shape=(jax.ShapeDtypeStruct((depth, bp, n), F32),
                   jax.ShapeDtypeStruct(side.shape[1:], BF16)),
        compiler_params=_params(("arbitrary", "arbitrary")),
        name="ada_mod",
    )(c_p, w_ada, b_ada.reshape(depth, 1, n), side)
    return out[:, :bsz], side_h


def _norm_mm_kernel(*refs, has_tail, transposed):
    if has_tail:
        (x_ref, g_ref, sc_ref, sh_ref, w_ref, wt_ref, side_ref, o_ref, ot_ref, side_o_ref,
         h_scr, gm_scr) = refs
    else:
        x_ref, g_ref, sc_ref, sh_ref, w_ref, side_ref, o_ref, side_o_ref, h_scr, gm_scr = refs
    j = pl.program_id(2)

    @pl.when(j == 0)
    def _():
        _modulated_norm(x_ref, g_ref, sc_ref, sh_ref, gm_scr, h_scr)

    def project(wr, out_ref):
        w = wr[...].astype(BF16)
        if transposed:
            y = lax.dot_general(h_scr[...], w, NT_DIMS, preferred_element_type=F32)
        else:
            y = jnp.dot(h_scr[...], w, preferred_element_type=F32)
        out_ref[0] = y.astype(out_ref.dtype)

    project(w_ref, o_ref)
    if has_tail:
        pl.when(j == pl.num_programs(2) - 1)(lambda: project(wt_ref, ot_ref))
    side_o_ref[...] = side_ref[...].astype(BF16)


def _norm_mm(x, gain, mod3, sc_idx, sh_idx, w, out_dtype, side, side_lead, w_tail=None,
             transposed=False, tm=1024, tn=1024, name="norm_mm"):
    bsz, s, d = x.shape
    n_w = w.shape[0] if transposed else w.shape[1]
    nm, nj = s // tm, n_w // tn
    n = nj * tn
    side_in, side_out = _slab_specs(side, side_lead, 1, LANE, bsz * nm * nj,
                                    lambda b, m, j: (b * nm + m) * nj + j)
    if transposed:
        w_specs = [pl.BlockSpec((tn, d), lambda b, m, j: (j, 0))]
    else:
        w_specs = [pl.BlockSpec((d, tn), lambda b, m, j: (0, j))]
    w_args = [w]
    out_specs = [pl.BlockSpec((1, tm, tn), lambda b, m, j: (b, m, j))]
    out_shape = [jax.ShapeDtypeStruct((bsz, s, n), out_dtype)]
    if w_tail is not None:
        n_tail = w_tail.shape[0] if transposed else w_tail.shape[1]
        w_specs.append(pl.BlockSpec(w_tail.shape, lambda b, m, j: (0, 0), pipeline_mode=pl.Buffered(1)))
        w_args.append(w_tail)
        out_specs.append(pl.BlockSpec((1, tm, n_tail), lambda b, m, j: (b, m, 0)))
        out_shape.append(jax.ShapeDtypeStruct((bsz, s, n_tail), out_dtype))
    return pl.pallas_call(
        functools.partial(_norm_mm_kernel, has_tail=w_tail is not None, transposed=transposed),
        grid=(bsz, nm, nj),
        in_specs=[pl.BlockSpec((1, tm, d), lambda b, m, j: (b, m, 0)),
                  pl.BlockSpec((1, d), lambda b, m, j: (0, 0)),
                  pl.BlockSpec((1, 1, d), lambda b, m, j: (b, 0, sc_idx)),
                  pl.BlockSpec((1, 1, d), lambda b, m, j: (b, 0, sh_idx)),
                  *w_specs,
                  side_in],
        out_specs=(*out_specs, side_out),
        out_shape=(*out_shape, jax.ShapeDtypeStruct(side.shape[1:], BF16)),
        scratch_shapes=[pltpu.VMEM((tm, d), BF16), pltpu.VMEM((1, d), F32)],
        compiler_params=_params(("arbitrary", "arbitrary", "arbitrary")),
        name=name,
    )(x, gain.reshape(1, d), mod3, mod3, *w_args, side)


def _ffn_kernel(*refs, has_side):
    if has_side:
        (x_ref, gpre_ref, sc_ref, sh_ref, gt_ref, gpost_ref, wg_ref, wu_ref, wo_ref, side_ref,
         o_ref, side_o_ref, h_scr, gm_scr, acc_scr) = refs
        side_o_ref[...] = side_ref[...].astype(BF16)
    else:
        (x_ref, gpre_ref, sc_ref, sh_ref, gt_ref, gpost_ref, wg_ref, wu_ref, wo_ref,
         o_ref, h_scr, gm_scr, acc_scr) = refs
    f = pl.program_id(2)

    @pl.when(f == 0)
    def _():
        _modulated_norm(x_ref, gpre_ref, sc_ref, sh_ref, gm_scr, h_scr)
        acc_scr[...] = jnp.zeros_like(acc_scr)

    h = h_scr[...]
    g = jnp.dot(h, wg_ref[...], preferred_element_type=F32)
    u = jnp.dot(h, wu_ref[...], preferred_element_type=F32)
    a = (g * jax.nn.sigmoid(g) * u).astype(BF16)
    acc_scr[...] += jnp.dot(a, wo_ref[...], preferred_element_type=F32)

    @pl.when(f == pl.num_programs(2) - 1)
    def _():
        _post_norm_residual(acc_scr, x_ref, gt_ref, gpost_ref, gm_scr, o_ref)


def _ffn(x, gpre, gpost, mod3, w_in, w_out, side=None, side_lead=0, tm=1024, tf=512):
    bsz, s, d = x.shape
    fh = w_out.shape[0]
    nf = fh // tf
    nm = s // tm
    out_spec = pl.BlockSpec((1, tm, d), lambda b, m, f: (b, m, 0), pipeline_mode=pl.Buffered(1))
    out_shape = jax.ShapeDtypeStruct((bsz, s, d), F32)
    side_specs, side_args = [], []
    if side is not None:
        side_in, side_out = _slab_specs(side, side_lead, 1, LANE, bsz * nm * nf,
                                        lambda b, m, f: (b * nm + m) * nf + f)
        side_specs, side_args = [side_in], [side]
        out_spec = (out_spec, side_out)
        out_shape = (out_shape, jax.ShapeDtypeStruct(side.shape[1:], BF16))
    return pl.pallas_call(
        functools.partial(_ffn_kernel, has_side=side is not None),
        grid=(bsz, nm, nf),
        in_specs=[pl.BlockSpec((1, tm, d), lambda b, m, f: (b, m, 0), pipeline_mode=pl.Buffered(1)),
                  pl.BlockSpec((1, d), lambda b, m, f: (0, 0)),
                  pl.BlockSpec((1, 1, d), lambda b, m, f: (b, 0, 4)),
                  pl.BlockSpec((1, 1, d), lambda b, m, f: (b, 0, 3)),
                  pl.BlockSpec((1, 1, d), lambda b, m, f: (b, 0, 5)),
                  pl.BlockSpec((1, d), lambda b, m, f: (0, 0)),
                  pl.BlockSpec((d, tf), lambda b, m, f: (0, f)),
                  pl.BlockSpec((d, tf), lambda b, m, f: (0, nf + f)),
                  pl.BlockSpec((tf, d), lambda b, m, f: (f, 0)),
                  *side_specs],
        out_specs=out_spec,
        out_shape=out_shape,
        scratch_shapes=[pltpu.VMEM((tm, d), BF16), pltpu.VMEM((1, d), F32), pltpu.VMEM((tm, d), F32)],
        compiler_params=_params(("arbitrary", "arbitrary", "arbitrary")),
        name="ffn",
    )(x, gpre.reshape(1, d), mod3, mod3, mod3, gpost.reshape(1, d), w_in, w_in, w_out, *side_args)


def _out_proj_kernel(a0_ref, a1_ref, w0_ref, w1_ref, x_ref, gt_ref, gpost_ref, side_ref,
                     o_ref, side_o_ref, gm_scr, y_scr):
    side_o_ref[...] = side_ref[...].astype(BF16)
    y_scr[...] = (jnp.dot(a0_ref[0], w0_ref[...], preferred_element_type=F32)
                  + jnp.dot(a1_ref[0], w1_ref[...], preferred_element_type=F32))
    _post_norm_residual(y_scr, x_ref, gt_ref, gpost_ref, gm_scr, o_ref)


def _out_proj(a0, a1, col0, col1, w, x, mod3, gpost, side, side_lead, tm=512):
    bsz, s, d = x.shape
    kh = w.shape[0] // 2
    nm = s // tm
    side_in, side_out = _slab_specs(side, side_lead, 0, ROW_CHUNK, bsz * nm, lambda b, m: b * nm + m)
    return pl.pallas_call(
        _out_proj_kernel,
        grid=(bsz, nm),
        in_specs=[pl.BlockSpec((1, tm, kh), lambda b, m: (b, m, col0)),
                  pl.BlockSpec((1, tm, kh), lambda b, m: (b, m, col1)),
                  pl.BlockSpec((kh, d), lambda b, m: (0, 0)),
                  pl.BlockSpec((kh, d), lambda b, m: (1, 0)),
                  pl.BlockSpec((1, tm, d), lambda b, m: (b, m, 0)),
                  pl.BlockSpec((1, 1, d), lambda b, m: (b, 0, 2)),
                  pl.BlockSpec((1, d), lambda b, m: (0, 0)),
                  side_in],
        out_specs=(pl.BlockSpec((1, tm, d), lambda b, m: (b, m, 0)), side_out),
        out_shape=(jax.ShapeDtypeStruct((bsz, s, d), F32),
                   jax.ShapeDtypeStruct(side.shape[1:], BF16)),
        scratch_shapes=[pltpu.VMEM((1, d), F32), pltpu.VMEM((tm, d), F32)],
        compiler_params=_params(("arbitrary", "arbitrary")),
        name="out_proj",
    )(a0, a1, w, w, x, mod3, gpost.reshape(1, d), side)


def _mixer_a_kernel(z_ref, vg_ref, vb_ref, ws_ref, bst_ref, o_ref):
    z = jax.nn.gelu(z_ref[0].astype(F32))
    wdt = z.shape[1] // 2
    u = z[:, :wdt]
    v = z[:, wdt:]
    mu = jnp.mean(v, axis=-1, keepdims=True)
    dv = v - mu
    var = jnp.mean(dv * dv, axis=-1, keepdims=True)
    vn = (dv * lax.rsqrt(var + LN_EPS) * vg_ref[...] + vb_ref[...]).astype(BF16)
    ch = z.shape[0]
    causal = (lax.broadcasted_iota(jnp.int32, (ch, ch), 0)
              >= lax.broadcasted_iota(jnp.int32, (ch, ch), 1))
    gd = wdt // A_GROUPS
    for g in range(A_GROUPS):
        w = jnp.where(causal, ws_ref[g], 0.0).astype(BF16)
        sv = jnp.dot(w, vn[:, g * gd:(g + 1) * gd], preferred_element_type=F32)
        sv = sv + bst_ref[:, g:g + 1]
        o_ref[0, :, g * gd:(g + 1) * gd] = (u[:, g * gd:(g + 1) * gd] * sv).astype(o_ref.dtype)


def _mixer_a(z, v_gain, v_bias, w_s, b_s, width):
    bsz, s, _ = z.shape
    ch = A_CHUNK
    return pl.pallas_call(
        _mixer_a_kernel,
        grid=(bsz, s // ch),
        in_specs=[pl.BlockSpec((1, ch, 2 * width), lambda b, c: (b, c, 0)),
                  pl.BlockSpec((1, width), lambda b, c: (0, 0)),
                  pl.BlockSpec((1, width), lambda b, c: (0, 0)),
                  pl.BlockSpec((A_GROUPS, ch, ch), lambda b, c: (0, 0, 0)),
                  pl.BlockSpec((ch, A_GROUPS), lambda b, c: (0, 0))],
        out_specs=pl.BlockSpec((1, ch, width), lambda b, c: (b, c, 0)),
        out_shape=jax.ShapeDtypeStruct((bsz, s, width), BF16),
        compiler_params=_params(("parallel", "parallel")),
        name="mixer_a",
    )(z, v_gain.reshape(1, width), v_bias.reshape(1, width), w_s, b_s.T)


_PV_MU_R, _PV_MU_K, _PV_MU_V, _PV_W0, _PV_A0, _PV_KK, _PV_KA, _PV_RK, _PV_LG, _PV_LB = range(10)
_PV_ROWS = 16


def _shift_lerp(x, prev_row, mu):
    rolled = pltpu.roll(x, 1, axis=0)
    first = lax.broadcasted_iota(jnp.int32, x.shape, 0) == 0
    xp = jnp.where(first, prev_row, rolled)
    return x + mu * (xp - x)


def _split_bf16(x):
    hi = x.astype(BF16)
    lo = (x - hi.astype(F32)).astype(BF16)
    return hi, lo


def _mm(x, y):
    return jnp.dot(x.astype(BF16), y.astype(BF16), preferred_element_type=F32)


def _mm_nt(x, y):
    return lax.dot_general(x.astype(BF16), y.astype(BF16), NT_DIMS, preferred_element_type=F32)


def _mm_tn(x, y):
    return lax.dot_general(x.astype(BF16), y.astype(BF16), TN_DIMS, preferred_element_type=F32)


def _mm_x3(x, y):
    xh, xl = _split_bf16(x)
    yh, yl = _split_bf16(y)
    return (jnp.dot(xh, yh, preferred_element_type=F32) + jnp.dot(xl, yh, preferred_element_type=F32)
            + jnp.dot(xh, yl, preferred_element_type=F32))


def _mm_exact_rhs(x, e_bf16):
    xh, xl = _split_bf16(x)
    return (jnp.dot(xh, e_bf16, preferred_element_type=F32)
            + jnp.dot(xl, e_bf16, preferred_element_type=F32))


def _rwkv_kernel(zr_ref, zk_ref, zv_ref, zl_ref, pva_ref, pvb_ref, mul_ref, w2_ref, a2_ref, g2_ref,
                 side_ref, o_ref, side_o_ref,
                 s_scr, prev_scr, prevl_scr, th_hi_scr, th_lo_scr, xw_scr, sg_scr,
                 q_scr, y_scr, gm_scr, cm_scr, pl_scr, bonus_scr, g_scr, *, chunk, n_t, n_p, n_tiles):
    i = pl.program_id(0)
    t_rows = zr_ref.shape[1]
    lanes = zr_ref.shape[2]
    lora = xw_scr.shape[1] + sg_scr.shape[1]
    L = chunk
    SL = RW_PACK * L
    n_chunks = t_rows // L
    side_o_ref[...] = side_ref[...].astype(BF16)

    ia = jnp.minimum(i, n_tiles - 1)
    ib = jnp.maximum(i - 1, 0)
    t_a, p_a = (ia // n_p) % n_t, ia % n_p
    t_b, p_b = (ib // n_p) % n_t, ib % n_p
    slot_a = i % 2
    slot_b = 1 - slot_a

    @pl.when(i == 0)
    def _():
        for ref in (s_scr, prev_scr, prevl_scr, q_scr, y_scr, gm_scr, cm_scr, pl_scr, bonus_scr, g_scr):
            ref[...] = jnp.zeros_like(ref)

    @pl.when(p_a == 0)
    def _():
        zl = zl_ref[0][:, :lora].astype(F32)
        prev = jnp.where(t_a == 0, 0.0, prevl_scr[0:1, :lora])
        zls = _shift_lerp(zl, prev, mul_ref[:, :lora])
        prevl_scr[0:1, :lora] = zl[t_rows - 1:t_rows, :]
        x_wa = zls[:, :LANE]
        th_hi, th_lo = _split_bf16(jnp.tanh(x_wa))
        th_hi_scr[...] = th_hi
        th_lo_scr[...] = th_lo
        xw_scr[...] = x_wa.astype(BF16)
        sg_scr[...] = jax.nn.sigmoid(zls[:, LANE:]).astype(BF16)

    def pva(r):
        return pva_ref[r:r + 1, :]

    zr = zr_ref[0].astype(F32)
    zk = zk_ref[0].astype(F32)
    zv = zv_ref[0].astype(F32)
    prev = jnp.where(t_a == 0, 0.0, prev_scr[p_a])
    r = _shift_lerp(zr, prev[0:1, :], pva(_PV_MU_R))
    k = _shift_lerp(zk, prev[1:2, :], pva(_PV_MU_K))
    v = _shift_lerp(zv, prev[2:3, :], pva(_PV_MU_V))
    prev_scr[p_a] = jnp.concatenate([zr[t_rows - 1:t_rows, :], zk[t_rows - 1:t_rows, :],
                                     zv[t_rows - 1:t_rows, :], jnp.zeros((5, lanes), F32)], axis=0)

    w2_hi, w2_lo = _split_bf16(w2_ref[...])
    th_hi = th_hi_scr[...]
    w_pre = (pva(_PV_W0) + jnp.dot(th_hi, w2_hi, preferred_element_type=F32)
             + jnp.dot(th_lo_scr[...], w2_hi, preferred_element_type=F32)
             + jnp.dot(th_hi, w2_lo, preferred_element_type=F32))
    t = -w_pre
    softplus = jnp.maximum(t, 0.0) + jnp.log1p(jnp.exp(-jnp.abs(t)))
    log_decay = -jnp.exp(-softplus - 0.5)
    a = jax.nn.sigmoid(pva(_PV_A0) + _mm(xw_scr[...], a2_ref[...]))
    g = _mm(sg_scr[...], g2_ref[...])

    li = lax.broadcasted_iota(jnp.int32, (lanes, lanes), 0) // RW_HEAD
    lj = lax.broadcasted_iota(jnp.int32, (lanes, lanes), 1) // RW_HEAD
    same_head = li == lj
    e_head = jnp.where(same_head, 1.0, 0.0).astype(BF16)

    kk = k * pva(_PV_KK)
    kk = kk / jnp.maximum(jnp.sqrt(_mm(kk * kk, e_head)), 1e-12)
    k = k * (1.0 + (a - 1.0) * pva(_PV_KA))
    av = -kk
    bv = kk * a
    bonus = _mm(r * k * pva(_PV_RK), e_head) * v

    ti = lax.broadcasted_iota(jnp.int32, (L, L), 0)
    tj = lax.broadcasted_iota(jnp.int32, (L, L), 1)
    tri = jnp.where(ti >= tj, 1.0, 0.0).astype(BF16)
    lw_hi, lw_lo = _split_bf16(log_decay)

    si = lax.broadcasted_iota(jnp.int32, (SL, SL), 0)
    sj = lax.broadcasted_iota(jnp.int32, (SL, SL), 1)
    same_blk = (si // L) == (sj // L)
    m_strict = same_blk & (si > sj)
    m_incl = same_blk & (si >= sj)
    eye = jnp.where(si == sj, 1.0, 0.0)
    lane_head = lax.broadcasted_iota(jnp.int32, (1, lanes), 1) // RW_HEAD
    n_sq = max(L.bit_length() - 2, 0)

    def stack(x):
        return jnp.concatenate([jnp.where(lane_head == h, x, 0.0) for h in range(RW_PACK)], axis=0)

    def unstack(x):
        out = x[0:L]
        for h in range(1, RW_PACK):
            out = out + x[h * L:(h + 1) * L]
        return out

    chain = {"s": jnp.where(t_b == 0, 0.0, s_scr[p_b]), "c": 0, "y": []}

    def chain_step():
        c = chain["c"]
        if c >= n_chunks:
            return
        s0 = chain["s"]
        s0b = s0.astype(BF16)
        chain["y"].append(y_scr[slot_b, c * L:(c + 1) * L, :] + _mm_nt(q_scr[slot_b, c], s0b))
        chain["s"] = s0 * pl_scr[slot_b, c][0:1, :] + _mm(s0b, gm_scr[slot_b, c]) + cm_scr[slot_b, c]
        chain["c"] = c + 1

    cs = range(n_chunks)
    rows = [slice(c * L, (c + 1) * L) for c in cs]
    cm = [jnp.dot(tri, lw_hi[rw], preferred_element_type=F32)
          + jnp.dot(tri, lw_lo[rw], preferred_element_type=F32) for rw in rows]
    cm_last = [x[L - 1:L, :] for x in cm]
    r_t = [r[rw] * jnp.exp(cm[c]) for c, rw in enumerate(rows)]
    a_st = [stack(av[rw] * jnp.exp(cm[c] - log_decay[rw])).astype(BF16) for c, rw in enumerate(rows)]
    e_neg = [jnp.exp(-x) for x in cm]
    k_rep = [jnp.concatenate([(k[rw] * e_neg[c]).astype(BF16)] * RW_PACK, axis=0)
             for c, rw in enumerate(rows)]
    b_rep = [jnp.concatenate([(bv[rw] * e_neg[c]).astype(BF16)] * RW_PACK, axis=0)
             for c, rw in enumerate(rows)]
    ar_st = [jnp.concatenate([a_st[c], stack(r_t[c]).astype(BF16)], axis=0) for c in cs]
    chain_step()
    prod_k = [_mm_nt(ar_st[c], k_rep[c]) for c in cs]
    prod_b = [_mm_nt(ar_st[c], b_rep[c]) for c in cs]
    a_ak = [jnp.where(m_strict, x[:SL], 0.0).astype(BF16) for x in prod_k]
    a_rk = [jnp.where(m_incl, x[SL:], 0.0).astype(BF16) for x in prod_k]
    a_ab = [jnp.where(m_strict, x[:SL], 0.0) for x in prod_b]
    a_rb = [jnp.where(m_incl, x[SL:], 0.0).astype(BF16) for x in prod_b]
    chain_step()

    xp = [x.astype(BF16) for x in a_ab]
    tinv = [eye + x for x in a_ab]
    for _ in range(n_sq):
        xp = [_mm(x, x).astype(BF16) for x in xp]
        tinv = [tinv[c] + _mm(tinv[c], xp[c]) for c in cs]
        chain_step()
    tinv = [x.astype(BF16) for x in tinv]

    v_st = [stack(v[rw]).astype(BF16) for rw in rows]
    x0 = [_mm(a_ak[c], v_st[c]).astype(BF16) for c in cs]
    chain_step()
    wu = [_mm(tinv[c], jnp.concatenate([a_st[c], x0[c]], axis=1)) for c in cs]
    chain_step()
    yq = [_mm(a_rb[c], wu[c]) for c in cs]
    y0 = [_mm(a_rk[c], v_st[c]) for c in cs]
    while chain["c"] < n_chunks:
        chain_step()
    cmats, gmats = [], []
    for c, rw in enumerate(rows):
        e_rem = jnp.exp(cm_last[c] - cm[c])
        b_b = bv[rw] * e_rem
        kb = jnp.concatenate([k[rw] * e_rem, b_b], axis=0)
        vu = jnp.concatenate([v[rw], unstack(wu[c][:, lanes:])], axis=0)
        cmats.append(jnp.where(same_head, _mm_tn(vu, kb), 0.0))
        gmats.append(jnp.where(same_head, _mm_tn(unstack(wu[c][:, :lanes]), b_b), 0.0).astype(BF16))

    s_scr[p_b] = chain["s"]
    y = jnp.concatenate(chain["y"], axis=0)
    inv_n = 1.0 / RW_HEAD
    mean = _mm_exact_rhs(y, e_head) * inv_n
    dy = y - mean
    var = _mm(dy * dy, e_head) * inv_n
    yn = dy * lax.rsqrt(var + GN_EPS) * pvb_ref[_PV_LG:_PV_LG + 1, :] + pvb_ref[_PV_LB:_PV_LB + 1, :]
    o_ref[0] = ((yn + bonus_scr[slot_b]) * g_scr[slot_b]).astype(o_ref.dtype)

    for c, rw in enumerate(rows):
        q_scr[slot_a, c] = (r_t[c] + unstack(yq[c][:, :lanes])).astype(BF16)
        y_scr[slot_a, rw, :] = unstack(yq[c][:, lanes:] + y0[c])
        cm_scr[slot_a, c] = cmats[c]
        gm_scr[slot_a, c] = gmats[c]
        pl_scr[slot_a, c] = jnp.broadcast_to(jnp.exp(cm_last[c]), (8, lanes))
    bonus_scr[slot_a] = bonus
    g_scr[slot_a] = g


def _rwkv(z, z_lora, col_r, col_k, col_v, pvec, mu_l, w2p, a2p, g2p, width, side, side_lead,
          t_rows=512):
    bsz, s, _ = z.shape
    lora_w = z_lora.shape[2]
    n_p = width // LANE
    n_chunks = t_rows // RW_CHUNK
    n_t = s // t_rows
    n_tiles = bsz * n_t * n_p
    lora = 3 * LANE

    def tile_a(i):
        ia = jnp.minimum(i, n_tiles - 1)
        return ia // (n_t * n_p), (ia // n_p) % n_t, ia % n_p

    def tile_b(i):
        ib = jnp.maximum(i - 1, 0)
        return ib // (n_t * n_p), (ib // n_p) % n_t, ib % n_p

    def z_spec(col):
        def idx(i):
            b, t, p = tile_a(i)
            return b, t, col + p
        return pl.BlockSpec((1, t_rows, LANE), idx)

    def zl_idx(i):
        b, t, _ = tile_a(i)
        return b, t, 0

    def out_idx(i):
        b, t, p = tile_b(i)
        return b, t, p

    side_in, side_out = _slab_specs(side, side_lead, 0, ROW_CHUNK, n_tiles + 1, lambda i: i)
    kern = functools.partial(_rwkv_kernel, chunk=RW_CHUNK, n_t=n_t, n_p=n_p, n_tiles=n_tiles)
    tile_f32 = pltpu.VMEM((2, t_rows, LANE), F32)
    return pl.pallas_call(
        kern,
        grid=(n_tiles + 1,),
        in_specs=[z_spec(col_r), z_spec(col_k), z_spec(col_v),
                  pl.BlockSpec((1, t_rows, lora_w), zl_idx),
                  pl.BlockSpec((_PV_ROWS, LANE), lambda i: (0, tile_a(i)[2])),
                  pl.BlockSpec((_PV_ROWS, LANE), lambda i: (0, tile_b(i)[2])),
                  pl.BlockSpec((1, lora_w), lambda i: (0, 0)),
                  pl.BlockSpec((LANE, LANE), lambda i: (0, tile_a(i)[2])),
                  pl.BlockSpec((LANE, LANE), lambda i: (0, tile_a(i)[2])),
                  pl.BlockSpec((2 * LANE, LANE), lambda i: (0, tile_a(i)[2])),
                  side_in],
        out_specs=(pl.BlockSpec((1, t_rows, LANE), out_idx), side_out),
        out_shape=(jax.ShapeDtypeStruct((bsz, s, width), BF16),
                   jax.ShapeDtypeStruct(side.shape[1:], BF16)),
        scratch_shapes=[pltpu.VMEM((n_p, LANE, LANE), F32),
                        pltpu.VMEM((n_p, 8, LANE), F32),
                        pltpu.VMEM((8, lora_w), F32),
                        pltpu.VMEM((t_rows, LANE), BF16),
                        pltpu.VMEM((t_rows, LANE), BF16),
                        pltpu.VMEM((t_rows, LANE), BF16),
                        pltpu.VMEM((t_rows, lora - LANE), BF16),
                        pltpu.VMEM((2, n_chunks, RW_CHUNK, LANE), BF16),
                        tile_f32,
                        pltpu.VMEM((2, n_chunks, LANE, LANE), BF16),
                        pltpu.VMEM((2, n_chunks, LANE, LANE), F32),
                        pltpu.VMEM((2, n_chunks, 8, LANE), F32),
                        tile_f32,
                        tile_f32],
        compiler_params=_params(("arbitrary",)),
        name="rwkv7",
    )(z, z, z, z_lora, pvec, pvec, mu_l, w2p, a2p, g2p, side)


def _moba_kernel(q_ref, k_ref, v_ref, side_ref, o_ref, side_o_ref, ka_scr, vt_scr, s_scr, p_scr, *,
                 n_heads):
    h = pl.program_id(1)
    side_o_ref[...] = side_ref[...].astype(BF16)
    s_len = q_ref.shape[1]
    dh = q_ref.shape[2]
    blk = MOBA_BLOCK
    nb = s_len // blk
    log2e = 1.4426950408889634
    scale = dh ** -0.5 * log2e
    neg_inf = -jnp.inf

    def slope_row(width):
        return log2e * jnp.exp(jnp.full((1, width), -8.0 / n_heads * 0.6931471805599453, F32)
                               * (h + 1).astype(F32))

    lane = lax.broadcasted_iota(jnp.int32, (blk, dh), 1)
    bias = slope_row(dh) * lax.broadcasted_iota(jnp.int32, (blk, dh), 0).astype(F32)
    extra = jnp.zeros((blk, dh), F32)
    for col in range(3):
        part = bias.astype(BF16).astype(F32)
        extra = jnp.where(lane == col, part, extra)
        bias = bias - part
    extra = extra.astype(BF16)
    ones_cols = jnp.where(lane < 3, 1.0, 0.0).astype(BF16)

    kmean = []
    for j in range(nb):
        rows = slice(j * blk, (j + 1) * blk)
        k_j = k_ref[0, rows, :]
        kmean.append(jnp.mean(k_j.astype(F32), axis=0, keepdims=True))
        ka_scr[rows, :dh] = k_j
        ka_scr[rows, dh:] = extra
        vt_scr[:dh, rows] = v_ref[0, rows, :].astype(F32).T.astype(BF16)
    sub = lax.broadcasted_iota(jnp.int32, (vt_scr.shape[0] - dh, s_len), 0)
    vt_scr[dh:, :] = jnp.where(sub == 0, 1.0, 0.0).astype(BF16)
    kmean = jnp.concatenate(kmean, axis=0)
    kmean_parts = []
    for _ in range(3):
        part = kmean.astype(BF16)
        kmean_parts.append(part)
        kmean = kmean - part.astype(F32)

    slope = slope_row(blk)
    causal = (lax.broadcasted_iota(jnp.int32, (blk, blk), 1)
              >= lax.broadcasted_iota(jnp.int32, (blk, blk), 0))
    blk_id = lax.broadcasted_iota(jnp.int32, (nb, 1), 0)

    def scores(qb):
        q = q_ref[0, qb * blk:(qb + 1) * blk, :]
        q_aug = jnp.concatenate([(q.astype(F32) * scale).astype(BF16), ones_cols], axis=1)
        gate = sum(lax.dot_general(part, q, NT_DIMS, preferred_element_type=F32)
                   for part in kmean_parts)
        past = blk_id < qb
        offs = []
        m = None
        for n in range(qb + 1):
            t = lax.dot_general(ka_scr[n * blk:(n + 1) * blk, :], q_aug, NT_DIMS,
                                preferred_element_type=F32)
            if n == qb:
                t = jnp.where(causal, t, neg_inf)
                off = jnp.zeros((1, blk), F32)
            else:
                g_n = gate[n:n + 1, :]
                beats = past & ((gate > g_n) | ((gate == g_n) & (blk_id < n)))
                rank = jnp.sum(jnp.where(beats, 1.0, 0.0), axis=0, keepdims=True)
                off = jnp.where(rank < float(MOBA_TOPK), slope * float((n - qb) * blk), neg_inf)
            s_scr[qb % 2, n] = t
            offs.append(off)
            cmax = jnp.max(t, axis=0, keepdims=True) + off
            m = cmax if m is None else jnp.maximum(m, cmax)
        return m, offs

    def attend(qb, m, offs):
        for n in range(qb + 1):
            p = jnp.exp2(s_scr[qb % 2, n] - (m - offs[n]))
            p_scr[qb % 2, n * blk:(n + 1) * blk, :] = p.astype(BF16)
        kk = (qb + 1) * blk
        acc = jnp.dot(vt_scr[:, :kk], p_scr[qb % 2, :kk, :], preferred_element_type=F32)
        o_ref[0, qb * blk:(qb + 1) * blk, :] = (acc[:dh] / acc[dh:dh + 1]).T.astype(o_ref.dtype)

    pending = scores(0)
    for qb in range(nb):
        nxt = scores(qb + 1) if qb + 1 < nb else None
        attend(qb, *pending)
        pending = nxt


def _moba(qkv, n_heads, side, side_lead):
    bsz, s, d3 = qkv.shape
    d = d3 // 3
    dh = d // n_heads
    blk = MOBA_BLOCK
    nb = s // blk
    side_in, side_out = _slab_specs(side, side_lead, 0, ROW_CHUNK, bsz * n_heads,
                                    lambda b, h: b * n_heads + h)
    kern = functools.partial(_moba_kernel, n_heads=n_heads)
    return pl.pallas_call(
        kern,
        grid=(bsz, n_heads),
        in_specs=[pl.BlockSpec((1, s, dh), lambda b, h: (b, 0, h)),
                  pl.BlockSpec((1, s, dh), lambda b, h: (b, 0, n_heads + h)),
                  pl.BlockSpec((1, s, dh), lambda b, h: (b, 0, 2 * n_heads + h)),
                  side_in],
        out_specs=(pl.BlockSpec((1, s, dh), lambda b, h: (b, 0, h)), side_out),
        out_shape=(jax.ShapeDtypeStruct((bsz, s, d), BF16),
                   jax.ShapeDtypeStruct(side.shape[1:], BF16)),
        scratch_shapes=[pltpu.VMEM((s, 2 * dh), BF16),
                        pltpu.VMEM((dh + 16, s), BF16),
                        pltpu.VMEM((2, nb, blk, blk), F32),
                        pltpu.VMEM((2, s, blk), BF16)],
        compiler_params=_params(("arbitrary", "arbitrary")),
        name="moba",
    )(qkv, qkv, qkv, side)


def _pad_cols(w, n):
    return jnp.pad(w, ((0, 0), (0, n - w.shape[1])))


def _pad_rows(w, n, before=0):
    return jnp.pad(w, ((before, n - before - w.shape[0]), (0, 0)))


def kernel(x, c, w_ada, b_ada, g_pre_mix, g_post_mix, g_pre_ffn, g_post_ffn, w_ffn_in, w_ffn_out,
           w_in_ab, w_out_ab, a_v_gain, a_v_bias, a_w_s, a_b_s, b_mu, b_w0, b_w2, b_a0, b_a2, b_g2,
           b_k_k, b_k_a, b_r_k, b_lnx_gain, b_lnx_bias, w_qkv, w_o):
    bsz, s, d = x.shape
    depth = w_ada.shape[0]
    a_width = a_v_gain.shape[1]
    b_width = b_w0.shape[1]
    n_lw = b_w2.shape[1]
    n_la = b_a2.shape[1]
    n_lg = b_g2.shape[1]
    n_heads = d // ATT_HEAD
    assert s % MOBA_BLOCK == 0 and s % 1024 == 0
    assert n_lw + n_la <= LANE and n_lg <= 2 * LANE

    w_in_t = jnp.swapaxes(w_in_ab, 1, 2)
    mod, w_in_h0 = _ada_mod(c, w_ada, b_ada, w_in_t, 0)

    w_qkv_h = None
    for layer in range(depth):
        mod3 = mod[layer].reshape(bsz, 1, 6 * d)
        i = layer // 2
        if layer % 2 == 0:
            lora_w = 3 * LANE
            n_main = 2 * a_width + 3 * b_width
            w_t = w_in_h0 if i == 0 else w_in_t[i]
            w_tail = _pad_rows(w_t[n_main:], lora_w)
            z, z_lora, w_out_h = _norm_mm(x, g_pre_mix[layer], mod3, 1, 0, w_t, BF16, w_out_ab, i,
                                          w_tail=w_tail, transposed=True, name="in_proj_ab")
            y_a = _mixer_a(z, a_v_gain[i], a_v_bias[i], a_w_s[i], a_b_s[i], a_width)

            mu = b_mu[i]
            pvec = jnp.stack([mu[0:b_width], mu[b_width:2 * b_width], mu[2 * b_width:3 * b_width],
                              b_w0[i], b_a0[i], b_k_k[i], b_k_a[i], b_r_k[i].reshape(-1),
                              b_lnx_gain[i], b_lnx_bias[i]])
            pvec = _pad_rows(pvec, _PV_ROWS)
            mu_l = _pad_cols(mu[3 * b_width:].reshape(1, -1), lora_w)
            w2p = _pad_rows(b_w2[i], LANE)
            a2p = _pad_rows(b_a2[i], LANE, before=n_lw)
            g2p = _pad_rows(b_g2[i], 2 * LANE)
            cb = 2 * a_width // LANE
            nb_w = b_width // LANE
            y_b, w_ffn_in_h = _rwkv(z, z_lora, cb, cb + nb_w, cb + 2 * nb_w, pvec, mu_l, w2p, a2p,
                                    g2p, b_width, w_ffn_in, layer)
            x, w_ffn_out_h = _out_proj(y_a, y_b, 0, 0, w_out_h, x, mod3, g_post_mix[layer],
                                       w_ffn_out, layer)
        else:
            w_q = w_qkv[i] if w_qkv_h is None else w_qkv_h
            qkv, w_o_h = _norm_mm(x, g_pre_mix[layer], mod3, 1, 0, w_q, BF16, w_o, i,
                                  name="qkv_proj")
            o, w_ffn_in_h = _moba(qkv, n_heads, w_ffn_in, layer)
            x, w_ffn_out_h = _out_proj(o, o, 0, 1, w_o_h, x, mod3, g_post_mix[layer],
                                       w_ffn_out, layer)
        if layer + 1 < depth and (layer + 1) % 2 == 1:
            x, w_qkv_h = _ffn(x, g_pre_ffn[layer], g_post_ffn[layer], mod3, w_ffn_in_h, w_ffn_out_h,
                              w_qkv, (layer + 1) // 2)
        else:
            x = _ffn(x, g_pre_ffn[layer], g_post_ffn[layer], mod3, w_ffn_in_h, w_ffn_out_h)
            w_qkv_h = None
    return x
```

```python
import functools

import jax
import jax.numpy as jnp
from jax import lax
from jax.experimental import pallas as pl
from jax.experimental.pallas import tpu as pltpu

F32 = jnp.float32
BF16 = jnp.bfloat16

NORM_EPS = 1e-6
LN_EPS = 1e-5
GN_EPS = 64e-5

LANE = 128
SUBLANE = 8
A_GROUPS = 8
A_CHUNK = 128
RW_HEAD = 64
RW_CHUNK = 64
RW_PACK = 2
MOBA_BLOCK = 256
MOBA_TOPK = 3
ATT_HEAD = 128

NT_DIMS = (((1,), (1,)), ((), ()))
TN_DIMS = (((0,), (0,)), ((), ()))

VMEM_LIMIT = 56 * 1024 * 1024


def _params(sem):
    return pltpu.CompilerParams(dimension_semantics=sem, vmem_limit_bytes=VMEM_LIMIT)


def _slab_specs(side, lead, axis, unit, n_steps, flat_index):
    total = side.shape[1 + axis]
    n_slab = max(n for n in range(1, n_steps + 1) if total % (n * unit) == 0)
    shape = list(side.shape[1:])
    shape[axis] = total // n_slab

    def idx(*grid):
        slab = jnp.minimum(flat_index(*grid), n_slab - 1)
        return (slab, 0) if axis == 0 else (0, slab)

    return (pl.BlockSpec((None,) + tuple(shape), lambda *g: (lead,) + idx(*g)),
            pl.BlockSpec(tuple(shape), idx))


ROW_CHUNK = 16
ROW_UNROLL = 8

def _for_row_chunks(n_rows, fn):
    def body(i, carry):
        fn(pl.ds(pl.multiple_of(i * ROW_CHUNK, ROW_CHUNK), ROW_CHUNK))
        return carry
    lax.fori_loop(0, n_rows // ROW_CHUNK, body, 0, unroll=ROW_UNROLL)


def _modulated_norm(x_ref, g_ref, sc_ref, sh_ref, gm_scr, h_scr):
    gm_scr[...] = g_ref[...] * (1.0 + sc_ref[0])

    def rows(r):
        x = x_ref[0, r, :]
        ms = jnp.mean(x * x, axis=-1, keepdims=True)
        h_scr[r, :] = (x * lax.rsqrt(ms + NORM_EPS) * gm_scr[...] + sh_ref[0]).astype(BF16)

    _for_row_chunks(h_scr.shape[0], rows)


def _post_norm_residual(y_scr, x_ref, gt_ref, gpost_ref, gm_scr, o_ref):
    gm_scr[...] = gt_ref[0] * gpost_ref[...]

    def rows(r):
        y = y_scr[r, :]
        ms = jnp.mean(y * y, axis=-1, keepdims=True)
        o_ref[0, r, :] = x_ref[0, r, :] + y * lax.rsqrt(ms + NORM_EPS) * gm_scr[...]

    _for_row_chunks(o_ref.shape[1], rows)


def _ada_kernel(c_ref, w_ref, b_ref, side_ref, o_ref, side_o_ref):
    c = c_ref[...]
    cond = (c * jax.nn.sigmoid(c)).astype(BF16)
    o_ref[0] = jnp.dot(cond, w_ref[0].astype(BF16), preferred_element_type=F32) + b_ref[0]
    side_o_ref[...] = side_ref[...].astype(BF16)


def _ada_mod(c, w_ada, b_ada, side, side_lead):
    depth, d, n = w_ada.shape
    bsz = c.shape[0]
    bp = -(-bsz // SUBLANE) * SUBLANE
    c_p = jnp.pad(c, ((0, bp - bsz), (0, 0)))
    tn = 1024
    nj = n // tn
    side_in, side_out = _slab_specs(side, side_lead, 0, ROW_CHUNK, depth * nj, lambda l, j: l * nj + j)
    out, side_h = pl.pallas_call(
        _ada_kernel,
        grid=(depth, nj),
        in_specs=[pl.BlockSpec((bp, d), lambda l, j: (0, 0)),
                  pl.BlockSpec((1, d, tn), lambda l, j: (l, 0, j)),
                  pl.BlockSpec((1, 1, tn), lambda l, j: (l, 0, j)),
                  side_in],
        out_specs=(pl.BlockSpec((1, bp, tn), lambda l, j: (l, 0, j)), side_out),
        out_shape=(jax.ShapeDtypeStruct((depth, bp, n), F32),
                   jax.ShapeDtypeStruct(side.shape[1:], BF16)),
        compiler_params=_params(("arbitrary", "arbitrary")),
        name="ada_mod",
    )(c_p, w_ada, b_ada.reshape(depth, 1, n), side)
    return out[:, :bsz], side_h


def _norm_mm_kernel(*refs, has_tail, transposed):
    if has_tail:
        (x_ref, g_ref, sc_ref, sh_ref, w_ref, wt_ref, side_ref, o_ref, ot_ref, side_o_ref,
         h_scr, gm_scr) = refs
    else:
        x_ref, g_ref, sc_ref, sh_ref, w_ref, side_ref, o_ref, side_o_ref, h_scr, gm_scr = refs
    j = pl.program_id(2)

    @pl.when(j == 0)
    def _():
        _modulated_norm(x_ref, g_ref, sc_ref, sh_ref, gm_scr, h_scr)

    def project(wr, out_ref):
        w = wr[...].astype(BF16)
        if transposed:
            y = lax.dot_general(h_scr[...], w, NT_DIMS, preferred_element_type=F32)
        else:
            y = jnp.dot(h_scr[...], w, preferred_element_type=F32)
        out_ref[0] = y.astype(out_ref.dtype)

    project(w_ref, o_ref)
    if has_tail:
        pl.when(j == pl.num_programs(2) - 1)(lambda: project(wt_ref, ot_ref))
    side_o_ref[...] = side_ref[...].astype(BF16)


def _norm_mm(x, gain, mod3, sc_idx, sh_idx, w, out_dtype, side, side_lead, w_tail=None,
             transposed=False, tm=1024, tn=1024, name="norm_mm"):
    bsz, s, d = x.shape
    n_w = w.shape[0] if transposed else w.shape[1]
    nm, nj = s // tm, n_w // tn
    n = nj * tn
    side_in, side_out = _slab_specs(side, side_lead, 1, LANE, bsz * nm * nj,
                                    lambda b, m, j: (b * nm + m) * nj + j)
    if transposed:
        w_specs = [pl.BlockSpec((tn, d), lambda b, m, j: (j, 0))]
    else:
        w_specs = [pl.BlockSpec((d, tn), lambda b, m, j: (0, j))]
    w_args = [w]
    out_specs = [pl.BlockSpec((1, tm, tn), lambda b, m, j: (b, m, j))]
    out_shape = [jax.ShapeDtypeStruct((bsz, s, n), out_dtype)]
    if w_tail is not None:
        n_tail = w_tail.shape[0] if transposed else w_tail.shape[1]
        w_specs.append(pl.BlockSpec(w_tail.shape, lambda b, m, j: (0, 0), pipeline_mode=pl.Buffered(1)))
        w_args.append(w_tail)
        out_specs.append(pl.BlockSpec((1, tm, n_tail), lambda b, m, j: (b, m, 0)))
        out_shape.append(jax.ShapeDtypeStruct((bsz, s, n_tail), out_dtype))
    return pl.pallas_call(
        functools.partial(_norm_mm_kernel, has_tail=w_tail is not None, transposed=transposed),
        grid=(bsz, nm, nj),
        in_specs=[pl.BlockSpec((1, tm, d), lambda b, m, j: (b, m, 0)),
                  pl.BlockSpec((1, d), lambda b, m, j: (0, 0)),
                  pl.BlockSpec((1, 1, d), lambda b, m, j: (b, 0, sc_idx)),
                  pl.BlockSpec((1, 1, d), lambda b, m, j: (b, 0, sh_idx)),
                  *w_specs,
                  side_in],
        out_specs=(*out_specs, side_out),
        out_shape=(*out_shape, jax.ShapeDtypeStruct(side.shape[1:], BF16)),
        scratch_shapes=[pltpu.VMEM((tm, d), BF16), pltpu.VMEM((1, d), F32)],
        compiler_params=_params(("arbitrary", "arbitrary", "arbitrary")),
        name=name,
    )(x, gain.reshape(1, d), mod3, mod3, *w_args, side)


def _ffn_kernel(*refs, has_side):
    if has_side:
        (x_ref, gpre_ref, sc_ref, sh_ref, gt_ref, gpost_ref, wg_ref, wu_ref, wo_ref, side_ref,
         o_ref, side_o_ref, h_scr, gm_scr, acc_scr) = refs
        side_o_ref[...] = side_ref[...].astype(BF16)
    else:
        (x_ref, gpre_ref, sc_ref, sh_ref, gt_ref, gpost_ref, wg_ref, wu_ref, wo_ref,
         o_ref, h_scr, gm_scr, acc_scr) = refs
    f = pl.program_id(2)

    @pl.when(f == 0)
    def _():
        _modulated_norm(x_ref, gpre_ref, sc_ref, sh_ref, gm_scr, h_scr)
        acc_scr[...] = jnp.zeros_like(acc_scr)

    h = h_scr[...]
    g = jnp.dot(h, wg_ref[...], preferred_element_type=F32)
    u = jnp.dot(h, wu_ref[...], preferred_element_type=F32)
    a = (g * jax.nn.sigmoid(g) * u).astype(BF16)
    acc_scr[...] += jnp.dot(a, wo_ref[...], preferred_element_type=F32)

    @pl.when(f == pl.num_programs(2) - 1)
    def _():
        _post_norm_residual(acc_scr, x_ref, gt_ref, gpost_ref, gm_scr, o_ref)


def _ffn(x, gpre, gpost, mod3, w_in, w_out, side=None, side_lead=0, tm=1024, tf=512):
    bsz, s, d = x.shape
    fh = w_out.shape[0]
    nf = fh // tf
    nm = s // tm
    out_spec = pl.BlockSpec((1, tm, d), lambda b, m, f: (b, m, 0), pipeline_mode=pl.Buffered(1))
    out_shape = jax.ShapeDtypeStruct((bsz, s, d), F32)
    side_specs, side_args = [], []
    if side is not None:
        side_in, side_out = _slab_specs(side, side_lead, 1, LANE, bsz * nm * nf,
                                        lambda b, m, f: (b * nm + m) * nf + f)
        side_specs, side_args = [side_in], [side]
        out_spec = (out_spec, side_out)
        out_shape = (out_shape, jax.ShapeDtypeStruct(side.shape[1:], BF16))
    return pl.pallas_call(
        functools.partial(_ffn_kernel, has_side=side is not None),
        grid=(bsz, nm, nf),
        in_specs=[pl.BlockSpec((1, tm, d), lambda b, m, f: (b, m, 0), pipeline_mode=pl.Buffered(1)),
                  pl.BlockSpec((1, d), lambda b, m, f: (0, 0)),
                  pl.BlockSpec((1, 1, d), lambda b, m, f: (b, 0, 4)),
                  pl.BlockSpec((1, 1, d), lambda b, m, f: (b, 0, 3)),
                  pl.BlockSpec((1, 1, d), lambda b, m, f: (b, 0, 5)),
                  pl.BlockSpec((1, d), lambda b, m, f: (0, 0)),
                  pl.BlockSpec((d, tf), lambda b, m, f: (0, f)),
                  pl.BlockSpec((d, tf), lambda b, m, f: (0, nf + f)),
                  pl.BlockSpec((tf, d), lambda b, m, f: (f, 0)),
                  *side_specs],
        out_specs=out_spec,
        out_shape=out_shape,
        scratch_shapes=[pltpu.VMEM((tm, d), BF16), pltpu.VMEM((1, d), F32), pltpu.VMEM((tm, d), F32)],
        compiler_params=_params(("arbitrary", "arbitrary", "arbitrary")),
        name="ffn",
    )(x, gpre.reshape(1, d), mod3, mod3, mod3, gpost.reshape(1, d), w_in, w_in, w_out, *side_args)


def _out_proj_kernel(a0_ref, a1_ref, w0_ref, w1_ref, x_ref, gt_ref, gpost_ref, side_ref,
                     o_ref, side_o_ref, gm_scr, y_scr):
    side_o_ref[...] = side_ref[...].astype(BF16)
    y_scr[...] = (jnp.dot(a0_ref[0], w0_ref[...], preferred_element_type=F32)
                  + jnp.dot(a1_ref[0], w1_ref[...], preferred_element_type=F32))
    _post_norm_residual(y_scr, x_ref, gt_ref, gpost_ref, gm_scr, o_ref)


def _out_proj(a0, a1, col0, col1, w, x, mod3, gpost, side, side_lead, tm=512):
    bsz, s, d = x.shape
    kh = w.shape[0] // 2
    nm = s // tm
    side_in, side_out = _slab_specs(side, side_lead, 0, ROW_CHUNK, bsz * nm, lambda b, m: b * nm + m)
    return pl.pallas_call(
        _out_proj_kernel,
        grid=(bsz, nm),
        in_specs=[pl.BlockSpec((1, tm, kh), lambda b, m: (b, m, col0)),
                  pl.BlockSpec((1, tm, kh), lambda b, m: (b, m, col1)),
                  pl.BlockSpec((kh, d), lambda b, m: (0, 0)),
                  pl.BlockSpec((kh, d), lambda b, m: (1, 0)),
                  pl.BlockSpec((1, tm, d), lambda b, m: (b, m, 0)),
                  pl.BlockSpec((1, 1, d), lambda b, m: (b, 0, 2)),
                  pl.BlockSpec((1, d), lambda b, m: (0, 0)),
                  side_in],
        out_specs=(pl.BlockSpec((1, tm, d), lambda b, m: (b, m, 0)), side_out),
        out_shape=(jax.ShapeDtypeStruct((bsz, s, d), F32),
                   jax.ShapeDtypeStruct(side.shape[1:], BF16)),
        scratch_shapes=[pltpu.VMEM((1, d), F32), pltpu.VMEM((tm, d), F32)],
        compiler_params=_params(("arbitrary", "arbitrary")),
        name="out_proj",
    )(a0, a1, w, w, x, mod3, gpost.reshape(1, d), side)


def _mixer_a_kernel(z_ref, vg_ref, vb_ref, ws_ref, bst_ref, o_ref, wm_scr):
    ch = ws_ref.shape[1]

    @pl.when((pl.program_id(0) == 0) & (pl.program_id(1) == 0))
    def _():
        causal = (lax.broadcasted_iota(jnp.int32, (ch, ch), 0)
                  >= lax.broadcasted_iota(jnp.int32, (ch, ch), 1))
        for g in range(A_GROUPS):
            wm_scr[g] = jnp.where(causal, ws_ref[g], 0.0).astype(BF16)

    for c in range(z_ref.shape[1] // ch):
        rows = slice(c * ch, (c + 1) * ch)
        z = jax.nn.gelu(z_ref[0, rows, :].astype(F32))
        wdt = z.shape[1] // 2
        u = z[:, :wdt]
        v = z[:, wdt:]
        mu = jnp.mean(v, axis=-1, keepdims=True)
        dv = v - mu
        var = jnp.mean(dv * dv, axis=-1, keepdims=True)
        vn = (dv * lax.rsqrt(var + LN_EPS) * vg_ref[...] + vb_ref[...]).astype(BF16)
        gd = wdt // A_GROUPS
        for g in range(A_GROUPS):
            sv = jnp.dot(wm_scr[g], vn[:, g * gd:(g + 1) * gd], preferred_element_type=F32)
            sv = sv + bst_ref[:, g:g + 1]
            o_ref[0, rows, g * gd:(g + 1) * gd] = (u[:, g * gd:(g + 1) * gd] * sv).astype(o_ref.dtype)


def _mixer_a(z, v_gain, v_bias, w_s, b_s, width, chunks_per_step=2):
    bsz, s, _ = z.shape
    ch = A_CHUNK
    rows = ch * chunks_per_step
    return pl.pallas_call(
        _mixer_a_kernel,
        grid=(bsz, s // rows),
        in_specs=[pl.BlockSpec((1, rows, 2 * width), lambda b, c: (b, c, 0)),
                  pl.BlockSpec((1, width), lambda b, c: (0, 0)),
                  pl.BlockSpec((1, width), lambda b, c: (0, 0)),
                  pl.BlockSpec((A_GROUPS, ch, ch), lambda b, c: (0, 0, 0)),
                  pl.BlockSpec((ch, A_GROUPS), lambda b, c: (0, 0))],
        out_specs=pl.BlockSpec((1, rows, width), lambda b, c: (b, c, 0)),
        out_shape=jax.ShapeDtypeStruct((bsz, s, width), BF16),
        scratch_shapes=[pltpu.VMEM((A_GROUPS, ch, ch), BF16)],
        compiler_params=_params(("arbitrary", "arbitrary")),
        name="mixer_a",
    )(z, v_gain.reshape(1, width), v_bias.reshape(1, width), w_s, b_s.T)


_PV_MU_R, _PV_MU_K, _PV_MU_V, _PV_W0, _PV_A0, _PV_KK, _PV_KA, _PV_RK, _PV_LG, _PV_LB = range(10)
_PV_ROWS = 16


def _shift_lerp(x, prev_row, mu):
    rolled = pltpu.roll(x, 1, axis=0)
    first = lax.broadcasted_iota(jnp.int32, x.shape, 0) == 0
    xp = jnp.where(first, prev_row, rolled)
    return x + mu * (xp - x)


def _split_bf16(x):
    hi = x.astype(BF16)
    lo = (x - hi.astype(F32)).astype(BF16)
    return hi, lo


def _mm(x, y):
    return jnp.dot(x.astype(BF16), y.astype(BF16), preferred_element_type=F32)


def _mm_nt(x, y):
    return lax.dot_general(x.astype(BF16), y.astype(BF16), NT_DIMS, preferred_element_type=F32)


def _mm_tn(x, y):
    return lax.dot_general(x.astype(BF16), y.astype(BF16), TN_DIMS, preferred_element_type=F32)


def _mm_exact_rhs(x, e_bf16):
    xh, xl = _split_bf16(x)
    return (jnp.dot(xh, e_bf16, preferred_element_type=F32)
            + jnp.dot(xl, e_bf16, preferred_element_type=F32))


def _rwkv_kernel(zr_ref, zk_ref, zv_ref, zl_ref, pva_ref, pvb_ref, mul_ref, w2_ref, a2_ref, g2_ref,
                 side_ref, o_ref, side_o_ref,
                 s_scr, prev_scr, prevl_scr, th_hi_scr, th_lo_scr, xw_scr, sg_scr,
                 ar_scr, kbh_scr, vst_scr, rt_scr, kbar_scr, vb_scr, plp_scr,
                 q_scr, y_scr, gm_scr, cm_scr, pl_scr, bonus_scr, g_scr, *, chunk, n_t, n_p, n_tiles):
    i = pl.program_id(0)
    t_rows = zr_ref.shape[1]
    lanes = zr_ref.shape[2]
    lora = xw_scr.shape[1] + sg_scr.shape[1]
    L = chunk
    SL = RW_PACK * L
    n_chunks = t_rows // L
    side_o_ref[...] = side_ref[...].astype(BF16)

    i1 = jnp.minimum(i, n_tiles - 1)
    i3 = jnp.maximum(i - 2, 0)
    t1, p1 = (i1 // n_p) % n_t, i1 % n_p
    t3, p3 = (i3 // n_p) % n_t, i3 % n_p
    sa = i % 2
    sb = 1 - sa
    s13_w = i % 3
    s13_r = (i + 1) % 3

    @pl.when(i == 0)
    def _():
        for ref in (s_scr, prev_scr, prevl_scr, ar_scr, kbh_scr, vst_scr, rt_scr, kbar_scr, vb_scr,
                    plp_scr, q_scr, y_scr, gm_scr, cm_scr, pl_scr, bonus_scr, g_scr):
            ref[...] = jnp.zeros_like(ref)

    @pl.when(p1 == 0)
    def _():
        zl = zl_ref[0][:, :lora].astype(F32)
        prev = jnp.where(t1 == 0, 0.0, prevl_scr[0:1, :lora])
        zls = _shift_lerp(zl, prev, mul_ref[:, :lora])
        prevl_scr[0:1, :lora] = zl[t_rows - 1:t_rows, :]
        x_wa = zls[:, :LANE]
        th_hi, th_lo = _split_bf16(jnp.tanh(x_wa))
        th_hi_scr[...] = th_hi
        th_lo_scr[...] = th_lo
        xw_scr[...] = x_wa.astype(BF16)
        sg_scr[...] = jax.nn.sigmoid(zls[:, LANE:]).astype(BF16)

    def pva(r):
        return pva_ref[r:r + 1, :]

    li = lax.broadcasted_iota(jnp.int32, (lanes, lanes), 0) // RW_HEAD
    lj = lax.broadcasted_iota(jnp.int32, (lanes, lanes), 1) // RW_HEAD
    same_head = li == lj
    e_head = jnp.where(same_head, 1.0, 0.0).astype(BF16)
    ti = lax.broadcasted_iota(jnp.int32, (L, L), 0)
    tj = lax.broadcasted_iota(jnp.int32, (L, L), 1)
    tri = jnp.where(ti >= tj, 1.0, 0.0).astype(BF16)
    si = lax.broadcasted_iota(jnp.int32, (SL, SL), 0)
    sj = lax.broadcasted_iota(jnp.int32, (SL, SL), 1)
    same_blk = (si // L) == (sj // L)
    m_strict = same_blk & (si > sj)
    m_incl = same_blk & (si >= sj)
    eye = jnp.where(si == sj, 1.0, 0.0)
    lane_head = lax.broadcasted_iota(jnp.int32, (1, lanes), 1) // RW_HEAD
    first_half = lax.broadcasted_iota(jnp.int32, (1, 2 * L), 1) < L
    n_sq = max(L.bit_length() - 2, 0)
    cs = range(n_chunks)
    rows = [slice(c * L, (c + 1) * L) for c in cs]

    def stack(x):
        return jnp.concatenate([jnp.where(lane_head == h, x, 0.0) for h in range(RW_PACK)], axis=0)

    def unstack(x):
        out = x[0:L]
        for h in range(1, RW_PACK):
            out = out + x[h * L:(h + 1) * L]
        return out

    zr = zr_ref[0].astype(F32)
    zk = zk_ref[0].astype(F32)
    zv = zv_ref[0].astype(F32)
    prev = jnp.where(t1 == 0, 0.0, prev_scr[p1])
    r = _shift_lerp(zr, prev[0:1, :], pva(_PV_MU_R))
    k = _shift_lerp(zk, prev[1:2, :], pva(_PV_MU_K))
    v = _shift_lerp(zv, prev[2:3, :], pva(_PV_MU_V))
    prev_scr[p1] = jnp.concatenate([zr[t_rows - 1:t_rows, :], zk[t_rows - 1:t_rows, :],
                                    zv[t_rows - 1:t_rows, :], jnp.zeros((SUBLANE - 3, lanes), F32)],
                                   axis=0)
    prep = {"c": 0, "phase": 0}

    def prep_tile_a():
        w2_hi, w2_lo = _split_bf16(w2_ref[...])
        th_hi = th_hi_scr[...]
        w_pre = (pva(_PV_W0) + jnp.dot(th_hi, w2_hi, preferred_element_type=F32)
                 + jnp.dot(th_lo_scr[...], w2_hi, preferred_element_type=F32)
                 + jnp.dot(th_hi, w2_lo, preferred_element_type=F32))
        t = -w_pre
        softplus = jnp.maximum(t, 0.0) + jnp.log1p(jnp.exp(-jnp.abs(t)))
        prep["log_decay"] = -jnp.exp(-softplus - 0.5)
        prep["a"] = jax.nn.sigmoid(pva(_PV_A0) + _mm(xw_scr[...], a2_ref[...]))
        g_scr[s13_w] = _mm(sg_scr[...], g2_ref[...])

    def prep_tile_b():
        a = prep["a"]
        kk = k * pva(_PV_KK)
        kk = kk / jnp.maximum(jnp.sqrt(_mm(kk * kk, e_head)), 1e-12)
        kn = k * (1.0 + (a - 1.0) * pva(_PV_KA))
        prep["kk"], prep["kn"], prep["bv"] = kk, kn, kk * a
        bonus_scr[s13_w] = _mm(r * kn * pva(_PV_RK), e_head) * v
        lw_hi, lw_lo = _split_bf16(prep["log_decay"])
        prep["cm"] = [jnp.dot(tri, lw_hi[rw], preferred_element_type=F32)
                      + jnp.dot(tri, lw_lo[rw], preferred_element_type=F32) for rw in rows]

    def prep_chunk():
        c = prep["c"]
        if c >= n_chunks:
            return
        prep["c"] = c + 1
        rw = rows[c]
        cm = prep["cm"][c]
        cm_last = cm[L - 1:L, :]
        kc, bvc = prep["kn"][rw], prep["bv"][rw]
        r_t = r[rw] * jnp.exp(cm)
        a_st = stack(-prep["kk"][rw] * jnp.exp(cm - prep["log_decay"][rw])).astype(BF16)
        e_neg = jnp.exp(-cm)
        e_rem = jnp.exp(cm_last - cm)
        ar_scr[sa, c] = jnp.concatenate([a_st, stack(r_t).astype(BF16)], axis=0)
        kbh_scr[sa, c] = jnp.concatenate([(kc * e_neg).astype(BF16), (bvc * e_neg).astype(BF16)], axis=0)
        vst_scr[sa, c] = stack(v[rw]).astype(BF16)
        rt_scr[sa, c] = r_t
        kbar_scr[sa, c] = jnp.concatenate([(kc * e_rem).astype(BF16), (bvc * e_rem).astype(BF16)], axis=0)
        vb_scr[sa, c] = v[rw].astype(BF16)
        plp_scr[sa, c] = jnp.broadcast_to(jnp.exp(cm_last), (SUBLANE, lanes))

    def prep_step():
        phase = prep["phase"]
        prep["phase"] = phase + 1
        if phase == 0:
            prep_tile_a()
        elif phase == 1:
            prep_tile_b()
        else:
            prep_chunk()
            if phase >= 6:
                prep_chunk()

    chain = {"s": jnp.where(t3 == 0, 0.0, s_scr[p3]), "c": 0, "y": []}

    def chain_step():
        c = chain["c"]
        if c >= n_chunks:
            return
        s0 = chain["s"]
        s0b = s0.astype(BF16)
        chain["y"].append(y_scr[sa, c * L:(c + 1) * L, :] + _mm_nt(q_scr[sa, c], s0b))
        chain["s"] = s0 * pl_scr[sa, c][0:1, :] + _mm(s0b, gm_scr[sa, c]) + cm_scr[sa, c]
        chain["c"] = c + 1

    def fill():
        prep_step()
        chain_step()

    ar_st = [ar_scr[sb, c] for c in cs]
    kb_h = [kbh_scr[sb, c] for c in cs]
    v_st = [vst_scr[sb, c] for c in cs]
    prod = [_mm_nt(ar_st[c], kb_h[c]) for c in cs]
    fill()
    swapped = [pltpu.roll(x, L, axis=1) for x in prod]
    prod_k = [jnp.where(first_half, prod[c], swapped[c]) for c in cs]
    prod_b = [jnp.where(first_half, swapped[c], prod[c]) for c in cs]
    a_ak = [jnp.where(m_strict, x[:SL], 0.0).astype(BF16) for x in prod_k]
    a_rk = [jnp.where(m_incl, x[SL:], 0.0).astype(BF16) for x in prod_k]
    a_ab = [jnp.where(m_strict, x[:SL], 0.0) for x in prod_b]
    a_rb = [jnp.where(m_incl, x[SL:], 0.0).astype(BF16) for x in prod_b]

    xp = [x.astype(BF16) for x in a_ab]
    tinv = [eye + x for x in a_ab]
    for _ in range(n_sq):
        xp = [_mm(x, x).astype(BF16) for x in xp]
        tinv = [tinv[c] + _mm(tinv[c], xp[c]) for c in cs]
        fill()
    tinv = [x.astype(BF16) for x in tinv]

    x0 = [_mm(a_ak[c], v_st[c]).astype(BF16) for c in cs]
    fill()
    wu = [_mm(tinv[c], jnp.concatenate([ar_st[c][:SL], x0[c]], axis=1)) for c in cs]
    fill()
    yq = [_mm(a_rb[c], wu[c]) for c in cs]
    y0 = [_mm(a_rk[c], v_st[c]) for c in cs]
    while prep["c"] < n_chunks or chain["c"] < n_chunks:
        fill()
    cmats, gmats = [], []
    for c in cs:
        kbar = kbar_scr[sb, c]
        vu = jnp.concatenate([vb_scr[sb, c], unstack(wu[c][:, lanes:]).astype(BF16)], axis=0)
        cmats.append(jnp.where(same_head, _mm_tn(vu, kbar), 0.0))
        gmats.append(jnp.where(same_head, _mm_tn(unstack(wu[c][:, :lanes]), kbar[L:]), 0.0).astype(BF16))

    s_scr[p3] = chain["s"]
    y = jnp.concatenate(chain["y"], axis=0)
    inv_n = 1.0 / RW_HEAD
    mean = _mm_exact_rhs(y, e_head) * inv_n
    dy = y - mean
    var = _mm(dy * dy, e_head) * inv_n
    yn = dy * lax.rsqrt(var + GN_EPS) * pvb_ref[_PV_LG:_PV_LG + 1, :] + pvb_ref[_PV_LB:_PV_LB + 1, :]
    o_ref[0] = ((yn + bonus_scr[s13_r]) * g_scr[s13_r]).astype(o_ref.dtype)

    for c, rw in enumerate(rows):
        q_scr[sb, c] = (rt_scr[sb, c] + unstack(yq[c][:, :lanes])).astype(BF16)
        y_scr[sb, rw, :] = unstack(yq[c][:, lanes:] + y0[c])
        cm_scr[sb, c] = cmats[c]
        gm_scr[sb, c] = gmats[c]
        pl_scr[sb, c] = plp_scr[sb, c]


def _rwkv(z, z_lora, col_r, col_k, col_v, pvec, mu_l, w2p, a2p, g2p, width, side, side_lead,
          t_rows=512):
    bsz, s, _ = z.shape
    lora_w = z_lora.shape[2]
    n_p = width // LANE
    L = RW_CHUNK
    n_chunks = t_rows // L
    n_t = s // t_rows
    n_tiles = bsz * n_t * n_p
    n_steps = n_tiles + 2
    assert RW_PACK * L == LANE and lora_w == 3 * LANE

    def tile(i, lag):
        it = jnp.clip(i - lag, 0, n_tiles - 1)
        return it // (n_t * n_p), (it // n_p) % n_t, it % n_p

    def z_spec(col):
        def idx(i):
            b, t, p = tile(i, 0)
            return b, t, col + p
        return pl.BlockSpec((1, t_rows, LANE), idx)

    def zl_idx(i):
        b, t, _ = tile(i, 0)
        return b, t, 0

    side_in, side_out = _slab_specs(side, side_lead, 0, ROW_CHUNK, n_steps, lambda i: i)
    kern = functools.partial(_rwkv_kernel, chunk=L, n_t=n_t, n_p=n_p, n_tiles=n_tiles)

    def per_chunk(rows_, dtype):
        return pltpu.VMEM((2, n_chunks, rows_, LANE), dtype)

    return pl.pallas_call(
        kern,
        grid=(n_steps,),
        in_specs=[z_spec(col_r), z_spec(col_k), z_spec(col_v),
                  pl.BlockSpec((1, t_rows, lora_w), zl_idx),
                  pl.BlockSpec((_PV_ROWS, LANE), lambda i: (0, tile(i, 0)[2])),
                  pl.BlockSpec((_PV_ROWS, LANE), lambda i: (0, tile(i, 2)[2])),
                  pl.BlockSpec((1, lora_w), lambda i: (0, 0)),
                  pl.BlockSpec((LANE, LANE), lambda i: (0, tile(i, 0)[2])),
                  pl.BlockSpec((LANE, LANE), lambda i: (0, tile(i, 0)[2])),
                  pl.BlockSpec((2 * LANE, LANE), lambda i: (0, tile(i, 0)[2])),
                  side_in],
        out_specs=(pl.BlockSpec((1, t_rows, LANE), lambda i: tile(i, 2)), side_out),
        out_shape=(jax.ShapeDtypeStruct((bsz, s, width), BF16),
                   jax.ShapeDtypeStruct(side.shape[1:], BF16)),
        scratch_shapes=[pltpu.VMEM((n_p, LANE, LANE), F32),
                        pltpu.VMEM((n_p, SUBLANE, LANE), F32),
                        pltpu.VMEM((SUBLANE, lora_w), F32),
                        pltpu.VMEM((t_rows, LANE), BF16),
                        pltpu.VMEM((t_rows, LANE), BF16),
                        pltpu.VMEM((t_rows, LANE), BF16),
                        pltpu.VMEM((t_rows, lora_w - LANE), BF16),
                        per_chunk(4 * L, BF16),
                        per_chunk(2 * L, BF16),
                        per_chunk(2 * L, BF16),
                        per_chunk(L, F32),
                        per_chunk(2 * L, BF16),
                        per_chunk(L, BF16),
                        per_chunk(SUBLANE, F32),
                        per_chunk(L, BF16),
                        pltpu.VMEM((2, t_rows, LANE), F32),
                        per_chunk(LANE, BF16),
                        per_chunk(LANE, F32),
                        per_chunk(SUBLANE, F32),
                        pltpu.VMEM((3, t_rows, LANE), F32),
                        pltpu.VMEM((3, t_rows, LANE), F32)],
        compiler_params=_params(("arbitrary",)),
        name="rwkv7",
    )(z, z, z, z_lora, pvec, pvec, mu_l, w2p, a2p, g2p, side)


def _moba_kernel(q_ref, k_ref, v_ref, side_ref, o_ref, side_o_ref, ka_scr, vt_scr, s_scr, p_scr, *,
                 n_heads):
    h = pl.program_id(1)
    side_o_ref[...] = side_ref[...].astype(BF16)
    s_len = q_ref.shape[1]
    dh = q_ref.shape[2]
    blk = MOBA_BLOCK
    nb = s_len // blk
    log2e = 1.4426950408889634
    scale = dh ** -0.5 * log2e
    neg_inf = -jnp.inf

    def slope_row(width):
        return log2e * jnp.exp(jnp.full((1, width), -8.0 / n_heads * 0.6931471805599453, F32)
                               * (h + 1).astype(F32))

    lane = lax.broadcasted_iota(jnp.int32, (blk, dh), 1)
    bias = slope_row(dh) * lax.broadcasted_iota(jnp.int32, (blk, dh), 0).astype(F32)
    extra = jnp.zeros((blk, dh), F32)
    for col in range(3):
        part = bias.astype(BF16).astype(F32)
        extra = jnp.where(lane == col, part, extra)
        bias = bias - part
    extra = extra.astype(BF16)
    ones_cols = jnp.where(lane < 3, 1.0, 0.0).astype(BF16)

    kmean = []
    for j in range(nb):
        rows = slice(j * blk, (j + 1) * blk)
        k_j = k_ref[0, rows, :]
        kmean.append(jnp.mean(k_j.astype(F32), axis=0, keepdims=True))
        ka_scr[rows, :dh] = k_j
        ka_scr[rows, dh:] = extra
        vt_scr[:dh, rows] = v_ref[0, rows, :].astype(F32).T.astype(BF16)
    sub = lax.broadcasted_iota(jnp.int32, (vt_scr.shape[0] - dh, s_len), 0)
    vt_scr[dh:, :] = jnp.where(sub == 0, 1.0, 0.0).astype(BF16)
    kmean = jnp.concatenate(kmean, axis=0)
    kmean_parts = []
    for _ in range(3):
        part = kmean.astype(BF16)
        kmean_parts.append(part)
        kmean = kmean - part.astype(F32)

    slope = slope_row(blk)
    causal = (lax.broadcasted_iota(jnp.int32, (blk, blk), 1)
              >= lax.broadcasted_iota(jnp.int32, (blk, blk), 0))
    blk_id = lax.broadcasted_iota(jnp.int32, (nb, 1), 0)

    def scores(qb):
        q = q_ref[0, qb * blk:(qb + 1) * blk, :]
        q_aug = jnp.concatenate([(q.astype(F32) * scale).astype(BF16), ones_cols], axis=1)
        gate = sum(lax.dot_general(part, q, NT_DIMS, preferred_element_type=F32)
                   for part in kmean_parts)
        past = blk_id < qb
        offs = []
        m = None
        for n in range(qb + 1):
            t = lax.dot_general(ka_scr[n * blk:(n + 1) * blk, :], q_aug, NT_DIMS,
                                preferred_element_type=F32)
            if n == qb:
                t = jnp.where(causal, t, neg_inf)
                off = jnp.zeros((1, blk), F32)
            else:
                g_n = gate[n:n + 1, :]
                beats = past & ((gate > g_n) | ((gate == g_n) & (blk_id < n)))
                rank = jnp.sum(jnp.where(beats, 1.0, 0.0), axis=0, keepdims=True)
                off = jnp.where(rank < float(MOBA_TOPK), slope * float((n - qb) * blk), neg_inf)
            s_scr[qb % 2, n] = t
            offs.append(off)
            cmax = jnp.max(t, axis=0, keepdims=True) + off
            m = cmax if m is None else jnp.maximum(m, cmax)
        return m, offs

    def attend(qb, m, offs):
        for n in range(qb + 1):
            p = jnp.exp2(s_scr[qb % 2, n] - (m - offs[n]))
            p_scr[qb % 2, n * blk:(n + 1) * blk, :] = p.astype(BF16)
        kk = (qb + 1) * blk
        acc = jnp.dot(vt_scr[:, :kk], p_scr[qb % 2, :kk, :], preferred_element_type=F32)
        o_ref[0, qb * blk:(qb + 1) * blk, :] = (acc[:dh] / acc[dh:dh + 1]).T.astype(o_ref.dtype)

    pending = scores(0)
    for qb in range(nb):
        nxt = scores(qb + 1) if qb + 1 < nb else None
        attend(qb, *pending)
        pending = nxt


def _moba(qkv, n_heads, side, side_lead):
    bsz, s, d3 = qkv.shape
    d = d3 // 3
    dh = d // n_heads
    blk = MOBA_BLOCK
    nb = s // blk
    side_in, side_out = _slab_specs(side, side_lead, 0, ROW_CHUNK, bsz * n_heads,
                                    lambda b, h: b * n_heads + h)
    kern = functools.partial(_moba_kernel, n_heads=n_heads)
    return pl.pallas_call(
        kern,
        grid=(bsz, n_heads),
        in_specs=[pl.BlockSpec((1, s, dh), lambda b, h: (b, 0, h)),
                  pl.BlockSpec((1, s, dh), lambda b, h: (b, 0, n_heads + h)),
                  pl.BlockSpec((1, s, dh), lambda b, h: (b, 0, 2 * n_heads + h)),
                  side_in],
        out_specs=(pl.BlockSpec((1, s, dh), lambda b, h: (b, 0, h)), side_out),
        out_shape=(jax.ShapeDtypeStruct((bsz, s, d), BF16),
                   jax.ShapeDtypeStruct(side.shape[1:], BF16)),
        scratch_shapes=[pltpu.VMEM((s, 2 * dh), BF16),
                        pltpu.VMEM((dh + 16, s), BF16),
                        pltpu.VMEM((2, nb, blk, blk), F32),
                        pltpu.VMEM((2, s, blk), BF16)],
        compiler_params=_params(("arbitrary", "arbitrary")),
        name="moba",
    )(qkv, qkv, qkv, side)


def _pad_cols(w, n):
    return jnp.pad(w, ((0, 0), (0, n - w.shape[1])))


def _pad_rows(w, n, before=0):
    return jnp.pad(w, ((before, n - before - w.shape[0]), (0, 0)))


def kernel(x, c, w_ada, b_ada, g_pre_mix, g_post_mix, g_pre_ffn, g_post_ffn, w_ffn_in, w_ffn_out,
           w_in_ab, w_out_ab, a_v_gain, a_v_bias, a_w_s, a_b_s, b_mu, b_w0, b_w2, b_a0, b_a2, b_g2,
           b_k_k, b_k_a, b_r_k, b_lnx_gain, b_lnx_bias, w_qkv, w_o):
    bsz, s, d = x.shape
    depth = w_ada.shape[0]
    a_width = a_v_gain.shape[1]
    b_width = b_w0.shape[1]
    n_lw = b_w2.shape[1]
    n_la = b_a2.shape[1]
    n_lg = b_g2.shape[1]
    n_heads = d // ATT_HEAD
    assert s % MOBA_BLOCK == 0 and s % 1024 == 0
    assert n_lw + n_la <= LANE and n_lg <= 2 * LANE

    w_in_t = jnp.swapaxes(w_in_ab, 1, 2)
    mod, w_in_h0 = _ada_mod(c, w_ada, b_ada, w_in_t, 0)

    w_qkv_h = None
    for layer in range(depth):
        mod3 = mod[layer].reshape(bsz, 1, 6 * d)
        i = layer // 2
        if layer % 2 == 0:
            lora_w = 3 * LANE
            n_main = 2 * a_width + 3 * b_width
            w_t = w_in_h0 if i == 0 else w_in_t[i]
            w_tail = _pad_rows(w_t[n_main:], lora_w)
            z, z_lora, w_out_h = _norm_mm(x, g_pre_mix[layer], mod3, 1, 0, w_t, BF16, w_out_ab, i,
                                          w_tail=w_tail, transposed=True, name="in_proj_ab")
            y_a = _mixer_a(z, a_v_gain[i], a_v_bias[i], a_w_s[i], a_b_s[i], a_width)

            mu = b_mu[i]
            pvec = jnp.stack([mu[0:b_width], mu[b_width:2 * b_width], mu[2 * b_width:3 * b_width],
                              b_w0[i], b_a0[i], b_k_k[i], b_k_a[i], b_r_k[i].reshape(-1),
                              b_lnx_gain[i], b_lnx_bias[i]])
            pvec = _pad_rows(pvec, _PV_ROWS)
            mu_l = _pad_cols(mu[3 * b_width:].reshape(1, -1), lora_w)
            w2p = _pad_rows(b_w2[i], LANE)
            a2p = _pad_rows(b_a2[i], LANE, before=n_lw)
            g2p = _pad_rows(b_g2[i], 2 * LANE)
            cb = 2 * a_width // LANE
            nb_w = b_width // LANE
            y_b, w_ffn_in_h = _rwkv(z, z_lora, cb, cb + nb_w, cb + 2 * nb_w, pvec, mu_l, w2p, a2p,
                                    g2p, b_width, w_ffn_in, layer)
            x, w_ffn_out_h = _out_proj(y_a, y_b, 0, 0, w_out_h, x, mod3, g_post_mix[layer],
                                       w_ffn_out, layer)
        else:
            w_q = w_qkv[i] if w_qkv_h is None else w_qkv_h
            qkv, w_o_h = _norm_mm(x, g_pre_mix[layer], mod3, 1, 0, w_q, BF16, w_o, i,
                                  name="qkv_proj")
            o, w_ffn_in_h = _moba(qkv, n_heads, w_ffn_in, layer)
            x, w_ffn_out_h = _out_proj(o, o, 0, 1, w_o_h, x, mod3, g_post_mix[layer],
                                       w_ffn_out, layer)
        if layer + 1 < depth and (layer + 1) % 2 == 1:
            x, w_qkv_h = _ffn(x, g_pre_ffn[layer], g_post_ffn[layer], mod3, w_ffn_in_h, w_ffn_out_h,
                              w_qkv, (layer + 1) // 2)
        else:
            x = _ffn(x, g_pre_ffn[layer], g_post_ffn[layer], mod3, w_ffn_in_h, w_ffn_out_h)
            w_qkv_h = None
    return x
```

```python
import functools

import jax
import jax.numpy as jnp
from jax import lax
from jax.experimental import pallas as pl
from jax.experimental.pallas import tpu as pltpu

F32 = jnp.float32
BF16 = jnp.bfloat16

NORM_EPS = 1e-6
LN_EPS = 1e-5
GN_EPS = 64e-5

LANE = 128
SUBLANE = 8
A_GROUPS = 8
A_CHUNK = 128
RW_HEAD = 64
RW_CHUNK = 64
RW_PACK = 2
MOBA_BLOCK = 256
MOBA_TOPK = 3
ATT_HEAD = 128

NT_DIMS = (((1,), (1,)), ((), ()))
TN_DIMS = (((0,), (0,)), ((), ()))

VMEM_LIMIT = 56 * 1024 * 1024


def _params(sem):
    return pltpu.CompilerParams(dimension_semantics=sem, vmem_limit_bytes=VMEM_LIMIT)


def _slab_specs(side, lead, axis, unit, n_steps, flat_index):
    total = side.shape[1 + axis]
    n_slab = max(n for n in range(1, n_steps + 1) if total % (n * unit) == 0)
    shape = list(side.shape[1:])
    shape[axis] = total // n_slab

    def idx(*grid):
        slab = jnp.minimum(flat_index(*grid), n_slab - 1)
        return (slab, 0) if axis == 0 else (0, slab)

    return (pl.BlockSpec((None,) + tuple(shape), lambda *g: (lead,) + idx(*g)),
            pl.BlockSpec(tuple(shape), idx))


ROW_CHUNK = 16
ROW_UNROLL = 8

def _for_row_chunks(n_rows, fn):
    def body(i, carry):
        fn(pl.ds(pl.multiple_of(i * ROW_CHUNK, ROW_CHUNK), ROW_CHUNK))
        return carry
    lax.fori_loop(0, n_rows // ROW_CHUNK, body, 0, unroll=ROW_UNROLL)


def _modulated_norm(x_ref, g_ref, sc_ref, sh_ref, gm_scr, h_scr):
    gm_scr[...] = g_ref[...] * (1.0 + sc_ref[0])

    def rows(r):
        x = x_ref[0, r, :]
        ms = jnp.mean(x * x, axis=-1, keepdims=True)
        h_scr[r, :] = (x * lax.rsqrt(ms + NORM_EPS) * gm_scr[...] + sh_ref[0]).astype(BF16)

    _for_row_chunks(h_scr.shape[0], rows)


def _post_norm_residual(y_scr, x_ref, gt_ref, gpost_ref, gm_scr, o_ref):
    gm_scr[...] = gt_ref[0] * gpost_ref[...]

    def rows(r):
        y = y_scr[r, :]
        ms = jnp.mean(y * y, axis=-1, keepdims=True)
        o_ref[0, r, :] = x_ref[0, r, :] + y * lax.rsqrt(ms + NORM_EPS) * gm_scr[...]

    _for_row_chunks(o_ref.shape[1], rows)


def _ada_kernel(c_ref, w_ref, b_ref, side_ref, o_ref, side_o_ref):
    c = c_ref[...]
    cond = (c * jax.nn.sigmoid(c)).astype(BF16)
    o_ref[0] = jnp.dot(cond, w_ref[0].astype(BF16), preferred_element_type=F32) + b_ref[0]
    side_o_ref[...] = side_ref[...].astype(BF16)


def _ada_mod(c, w_ada, b_ada, side, side_lead):
    depth, d, n = w_ada.shape
    bsz = c.shape[0]
    bp = -(-bsz // SUBLANE) * SUBLANE
    c_p = jnp.pad(c, ((0, bp - bsz), (0, 0)))
    tn = 1024
    nj = n // tn
    side_in, side_out = _slab_specs(side, side_lead, 0, ROW_CHUNK, depth * nj, lambda l, j: l * nj + j)
    out, side_h = pl.pallas_call(
        _ada_kernel,
        grid=(depth, nj),
        in_specs=[pl.BlockSpec((bp, d), lambda l, j: (0, 0)),
                  pl.BlockSpec((1, d, tn), lambda l, j: (l, 0, j)),
                  pl.BlockSpec((1, 1, tn), lambda l, j: (l, 0, j)),
                  side_in],
        out_specs=(pl.BlockSpec((1, bp, tn), lambda l, j: (l, 0, j)), side_out),
        out_shape=(jax.ShapeDtypeStruct((depth, bp, n), F32),
                   jax.ShapeDtypeStruct(side.shape[1:], BF16)),
        compiler_params=_params(("arbitrary", "arbitrary")),
        name="ada_mod",
    )(c_p, w_ada, b_ada.reshape(depth, 1, n), side)
    return out[:, :bsz], side_h


def _norm_mm_kernel(*refs, has_tail, transposed):
    if has_tail:
        (x_ref, g_ref, sc_ref, sh_ref, w_ref, wt_ref, side_ref, o_ref, ot_ref, side_o_ref,
         h_scr, gm_scr) = refs
    else:
        x_ref, g_ref, sc_ref, sh_ref, w_ref, side_ref, o_ref, side_o_ref, h_scr, gm_scr = refs
    j = pl.program_id(2)

    @pl.when(j == 0)
    def _():
        _modulated_norm(x_ref, g_ref, sc_ref, sh_ref, gm_scr, h_scr)

    def project(wr, out_ref):
        w = wr[...].astype(BF16)
        if transposed:
            y = lax.dot_general(h_scr[...], w, NT_DIMS, preferred_element_type=F32)
        else:
            y = jnp.dot(h_scr[...], w, preferred_element_type=F32)
        out_ref[0] = y.astype(out_ref.dtype)

    project(w_ref, o_ref)
    if has_tail:
        pl.when(j == pl.num_programs(2) - 1)(lambda: project(wt_ref, ot_ref))
    side_o_ref[...] = side_ref[...].astype(BF16)


def _norm_mm(x, gain, mod3, sc_idx, sh_idx, w, out_dtype, side, side_lead, w_tail=None,
             transposed=False, tm=1024, tn=1024, name="norm_mm"):
    bsz, s, d = x.shape
    n_w = w.shape[0] if transposed else w.shape[1]
    nm, nj = s // tm, n_w // tn
    n = nj * tn
    side_in, side_out = _slab_specs(side, side_lead, 1, LANE, bsz * nm * nj,
                                    lambda b, m, j: (b * nm + m) * nj + j)
    if transposed:
        w_specs = [pl.BlockSpec((tn, d), lambda b, m, j: (j, 0))]
    else:
        w_specs = [pl.BlockSpec((d, tn), lambda b, m, j: (0, j))]
    w_args = [w]
    out_specs = [pl.BlockSpec((1, tm, tn), lambda b, m, j: (b, m, j))]
    out_shape = [jax.ShapeDtypeStruct((bsz, s, n), out_dtype)]
    if w_tail is not None:
        n_tail = w_tail.shape[0] if transposed else w_tail.shape[1]
        w_specs.append(pl.BlockSpec(w_tail.shape, lambda b, m, j: (0, 0), pipeline_mode=pl.Buffered(1)))
        w_args.append(w_tail)
        out_specs.append(pl.BlockSpec((1, tm, n_tail), lambda b, m, j: (b, m, 0)))
        out_shape.append(jax.ShapeDtypeStruct((bsz, s, n_tail), out_dtype))
    return pl.pallas_call(
        functools.partial(_norm_mm_kernel, has_tail=w_tail is not None, transposed=transposed),
        grid=(bsz, nm, nj),
        in_specs=[pl.BlockSpec((1, tm, d), lambda b, m, j: (b, m, 0)),
                  pl.BlockSpec((1, d), lambda b, m, j: (0, 0)),
                  pl.BlockSpec((1, 1, d), lambda b, m, j: (b, 0, sc_idx)),
                  pl.BlockSpec((1, 1, d), lambda b, m, j: (b, 0, sh_idx)),
                  *w_specs,
                  side_in],
        out_specs=(*out_specs, side_out),
        out_shape=(*out_shape, jax.ShapeDtypeStruct(side.shape[1:], BF16)),
        scratch_shapes=[pltpu.VMEM((tm, d), BF16), pltpu.VMEM((1, d), F32)],
        compiler_params=_params(("arbitrary", "arbitrary", "arbitrary")),
        name=name,
    )(x, gain.reshape(1, d), mod3, mod3, *w_args, side)


def _ffn_kernel(*refs, has_side):
    if has_side:
        (x_ref, gpre_ref, sc_ref, sh_ref, gt_ref, gpost_ref, wg_ref, wu_ref, wo_ref, side_ref,
         o_ref, side_o_ref, h_scr, gm_scr, acc_scr) = refs
        side_o_ref[...] = side_ref[...].astype(BF16)
    else:
        (x_ref, gpre_ref, sc_ref, sh_ref, gt_ref, gpost_ref, wg_ref, wu_ref, wo_ref,
         o_ref, h_scr, gm_scr, acc_scr) = refs
    f = pl.program_id(2)

    @pl.when(f == 0)
    def _():
        _modulated_norm(x_ref, gpre_ref, sc_ref, sh_ref, gm_scr, h_scr)
        acc_scr[...] = jnp.zeros_like(acc_scr)

    h = h_scr[...]
    g = jnp.dot(h, wg_ref[...], preferred_element_type=F32)
    u = jnp.dot(h, wu_ref[...], preferred_element_type=F32)
    a = (g * jax.nn.sigmoid(g) * u).astype(BF16)
    acc_scr[...] += jnp.dot(a, wo_ref[...], preferred_element_type=F32)

    @pl.when(f == pl.num_programs(2) - 1)
    def _():
        _post_norm_residual(acc_scr, x_ref, gt_ref, gpost_ref, gm_scr, o_ref)


def _ffn(x, gpre, gpost, mod3, w_in, w_out, side=None, side_lead=0, tm=1024, tf=512):
    bsz, s, d = x.shape
    fh = w_out.shape[0]
    nf = fh // tf
    nm = s // tm
    out_spec = pl.BlockSpec((1, tm, d), lambda b, m, f: (b, m, 0), pipeline_mode=pl.Buffered(1))
    out_shape = jax.ShapeDtypeStruct((bsz, s, d), F32)
    side_specs, side_args = [], []
    if side is not None:
        side_in, side_out = _slab_specs(side, side_lead, 1, LANE, bsz * nm * nf,
                                        lambda b, m, f: (b * nm + m) * nf + f)
        side_specs, side_args = [side_in], [side]
        out_spec = (out_spec, side_out)
        out_shape = (out_shape, jax.ShapeDtypeStruct(side.shape[1:], BF16))
    return pl.pallas_call(
        functools.partial(_ffn_kernel, has_side=side is not None),
        grid=(bsz, nm, nf),
        in_specs=[pl.BlockSpec((1, tm, d), lambda b, m, f: (b, m, 0), pipeline_mode=pl.Buffered(1)),
                  pl.BlockSpec((1, d), lambda b, m, f: (0, 0)),
                  pl.BlockSpec((1, 1, d), lambda b, m, f: (b, 0, 4)),
                  pl.BlockSpec((1, 1, d), lambda b, m, f: (b, 0, 3)),
                  pl.BlockSpec((1, 1, d), lambda b, m, f: (b, 0, 5)),
                  pl.BlockSpec((1, d), lambda b, m, f: (0, 0)),
                  pl.BlockSpec((d, tf), lambda b, m, f: (0, f)),
                  pl.BlockSpec((d, tf), lambda b, m, f: (0, nf + f)),
                  pl.BlockSpec((tf, d), lambda b, m, f: (f, 0)),
                  *side_specs],
        out_specs=out_spec,
        out_shape=out_shape,
        scratch_shapes=[pltpu.VMEM((tm, d), BF16), pltpu.VMEM((1, d), F32), pltpu.VMEM((tm, d), F32)],
        compiler_params=_params(("arbitrary", "arbitrary", "arbitrary")),
        name="ffn",
    )(x, gpre.reshape(1, d), mod3, mod3, mod3, gpost.reshape(1, d), w_in, w_in, w_out, *side_args)


def _out_proj_kernel(a0_ref, a1_ref, w0_ref, w1_ref, x_ref, gt_ref, gpost_ref, side_ref,
                     o_ref, side_o_ref, gm_scr, y_scr):
    side_o_ref[...] = side_ref[...].astype(BF16)
    y_scr[...] = (jnp.dot(a0_ref[0], w0_ref[...], preferred_element_type=F32)
                  + jnp.dot(a1_ref[0], w1_ref[...], preferred_element_type=F32))
    _post_norm_residual(y_scr, x_ref, gt_ref, gpost_ref, gm_scr, o_ref)


def _out_proj(a0, a1, col0, col1, w, x, mod3, gpost, side, side_lead, tm=512):
    bsz, s, d = x.shape
    kh = w.shape[0] // 2
    nm = s // tm
    side_in, side_out = _slab_specs(side, side_lead, 0, ROW_CHUNK, bsz * nm, lambda b, m: b * nm + m)
    return pl.pallas_call(
        _out_proj_kernel,
        grid=(bsz, nm),
        in_specs=[pl.BlockSpec((1, tm, kh), lambda b, m: (b, m, col0)),
                  pl.BlockSpec((1, tm, kh), lambda b, m: (b, m, col1)),
                  pl.BlockSpec((kh, d), lambda b, m: (0, 0)),
                  pl.BlockSpec((kh, d), lambda b, m: (1, 0)),
                  pl.BlockSpec((1, tm, d), lambda b, m: (b, m, 0)),
                  pl.BlockSpec((1, 1, d), lambda b, m: (b, 0, 2)),
                  pl.BlockSpec((1, d), lambda b, m: (0, 0)),
                  side_in],
        out_specs=(pl.BlockSpec((1, tm, d), lambda b, m: (b, m, 0)), side_out),
        out_shape=(jax.ShapeDtypeStruct((bsz, s, d), F32),
                   jax.ShapeDtypeStruct(side.shape[1:], BF16)),
        scratch_shapes=[pltpu.VMEM((1, d), F32), pltpu.VMEM((tm, d), F32)],
        compiler_params=_params(("arbitrary", "arbitrary")),
        name="out_proj",
    )(a0, a1, w, w, x, mod3, gpost.reshape(1, d), side)


def _mixer_a_kernel(z_ref, vg_ref, vb_ref, ws_ref, bst_ref, o_ref, wm_scr):
    ch = ws_ref.shape[1]

    @pl.when((pl.program_id(0) == 0) & (pl.program_id(1) == 0))
    def _():
        causal = (lax.broadcasted_iota(jnp.int32, (ch, ch), 0)
                  >= lax.broadcasted_iota(jnp.int32, (ch, ch), 1))
        for g in range(A_GROUPS):
            wm_scr[g] = jnp.where(causal, ws_ref[g], 0.0).astype(BF16)

    for c in range(z_ref.shape[1] // ch):
        rows = slice(c * ch, (c + 1) * ch)
        z = jax.nn.gelu(z_ref[0, rows, :].astype(F32))
        wdt = z.shape[1] // 2
        u = z[:, :wdt]
        v = z[:, wdt:]
        mu = jnp.mean(v, axis=-1, keepdims=True)
        dv = v - mu
        var = jnp.mean(dv * dv, axis=-1, keepdims=True)
        vn = (dv * lax.rsqrt(var + LN_EPS) * vg_ref[...] + vb_ref[...]).astype(BF16)
        gd = wdt // A_GROUPS
        for g in range(A_GROUPS):
            sv = jnp.dot(wm_scr[g], vn[:, g * gd:(g + 1) * gd], preferred_element_type=F32)
            sv = sv + bst_ref[:, g:g + 1]
            o_ref[0, rows, g * gd:(g + 1) * gd] = (u[:, g * gd:(g + 1) * gd] * sv).astype(o_ref.dtype)


def _mixer_a(z, v_gain, v_bias, w_s, b_s, width, chunks_per_step=2):
    bsz, s, _ = z.shape
    ch = A_CHUNK
    rows = ch * chunks_per_step
    return pl.pallas_call(
        _mixer_a_kernel,
        grid=(bsz, s // rows),
        in_specs=[pl.BlockSpec((1, rows, 2 * width), lambda b, c: (b, c, 0)),
                  pl.BlockSpec((1, width), lambda b, c: (0, 0)),
                  pl.BlockSpec((1, width), lambda b, c: (0, 0)),
                  pl.BlockSpec((A_GROUPS, ch, ch), lambda b, c: (0, 0, 0)),
                  pl.BlockSpec((ch, A_GROUPS), lambda b, c: (0, 0))],
        out_specs=pl.BlockSpec((1, rows, width), lambda b, c: (b, c, 0)),
        out_shape=jax.ShapeDtypeStruct((bsz, s, width), BF16),
        scratch_shapes=[pltpu.VMEM((A_GROUPS, ch, ch), BF16)],
        compiler_params=_params(("arbitrary", "arbitrary")),
        name="mixer_a",
    )(z, v_gain.reshape(1, width), v_bias.reshape(1, width), w_s, b_s.T)


_PV_MU_R, _PV_MU_K, _PV_MU_V, _PV_W0, _PV_A0, _PV_KK, _PV_KA, _PV_RK, _PV_LG, _PV_LB = range(10)
_PV_ROWS = 16


def _shift_lerp(x, prev_row, mu):
    rolled = pltpu.roll(x, 1, axis=0)
    first = lax.broadcasted_iota(jnp.int32, x.shape, 0) == 0
    xp = jnp.where(first, prev_row, rolled)
    return x + mu * (xp - x)


def _split_bf16(x):
    hi = x.astype(BF16)
    lo = (x - hi.astype(F32)).astype(BF16)
    return hi, lo


def _mm(x, y):
    return jnp.dot(x.astype(BF16), y.astype(BF16), preferred_element_type=F32)


def _mm_nt(x, y):
    return lax.dot_general(x.astype(BF16), y.astype(BF16), NT_DIMS, preferred_element_type=F32)


def _mm_tn(x, y):
    return lax.dot_general(x.astype(BF16), y.astype(BF16), TN_DIMS, preferred_element_type=F32)


def _mm_exact_rhs(x, e_bf16):
    xh, xl = _split_bf16(x)
    return (jnp.dot(xh, e_bf16, preferred_element_type=F32)
            + jnp.dot(xl, e_bf16, preferred_element_type=F32))


def _rwkv_kernel(zr_ref, zk_ref, zv_ref, zl_ref, pva_ref, pvb_ref, mul_ref, w2_ref, a2_ref, g2_ref,
                 side_ref, o_ref, side_o_ref,
                 s_scr, prev_scr, prevl_scr, th_hi_scr, th_lo_scr, xw_scr, sg_scr,
                 ar_scr, kbh_scr, vst_scr, rt_scr, kbar_scr, vb_scr, plp_scr,
                 q_scr, y_scr, gm_scr, cm_scr, pl_scr, bonus_scr, g_scr, *, chunk, n_t, n_p, n_tiles):
    i = pl.program_id(0)
    t_rows = zr_ref.shape[1]
    lanes = zr_ref.shape[2]
    lora = xw_scr.shape[1] + sg_scr.shape[1]
    L = chunk
    SL = RW_PACK * L
    n_chunks = t_rows // L
    side_o_ref[...] = side_ref[...].astype(BF16)

    i1 = jnp.minimum(i, n_tiles - 1)
    i3 = jnp.maximum(i - 2, 0)
    t1, p1 = (i1 // n_p) % n_t, i1 % n_p
    t3, p3 = (i3 // n_p) % n_t, i3 % n_p
    sa = i % 2
    sb = 1 - sa
    s13_w = i % 3
    s13_r = (i + 1) % 3

    @pl.when(i == 0)
    def _():
        for ref in (s_scr, prev_scr, prevl_scr, ar_scr, kbh_scr, vst_scr, rt_scr, kbar_scr, vb_scr,
                    plp_scr, q_scr, y_scr, gm_scr, cm_scr, pl_scr, bonus_scr, g_scr):
            ref[...] = jnp.zeros_like(ref)

    @pl.when(p1 == 0)
    def _():
        zl = zl_ref[0][:, :lora].astype(F32)
        prev = jnp.where(t1 == 0, 0.0, prevl_scr[0:1, :lora])
        zls = _shift_lerp(zl, prev, mul_ref[:, :lora])
        prevl_scr[0:1, :lora] = zl[t_rows - 1:t_rows, :]
        x_wa = zls[:, :LANE]
        th_hi, th_lo = _split_bf16(jnp.tanh(x_wa))
        th_hi_scr[...] = th_hi
        th_lo_scr[...] = th_lo
        xw_scr[...] = x_wa.astype(BF16)
        sg_scr[...] = jax.nn.sigmoid(zls[:, LANE:]).astype(BF16)

    def pva(r):
        return pva_ref[r:r + 1, :]

    li = lax.broadcasted_iota(jnp.int32, (lanes, lanes), 0) // RW_HEAD
    lj = lax.broadcasted_iota(jnp.int32, (lanes, lanes), 1) // RW_HEAD
    same_head = li == lj
    e_head = jnp.where(same_head, 1.0, 0.0).astype(BF16)
    ti = lax.broadcasted_iota(jnp.int32, (L, L), 0)
    tj = lax.broadcasted_iota(jnp.int32, (L, L), 1)
    tri = jnp.where(ti >= tj, 1.0, 0.0).astype(BF16)
    si = lax.broadcasted_iota(jnp.int32, (SL, SL), 0)
    sj = lax.broadcasted_iota(jnp.int32, (SL, SL), 1)
    same_blk = (si // L) == (sj // L)
    m_strict = same_blk & (si > sj)
    m_incl = same_blk & (si >= sj)
    eye = jnp.where(si == sj, 1.0, 0.0)
    lane_head = lax.broadcasted_iota(jnp.int32, (1, lanes), 1) // RW_HEAD
    first_half = lax.broadcasted_iota(jnp.int32, (1, 2 * L), 1) < L
    n_sq = max(L.bit_length() - 2, 0)
    cs = range(n_chunks)
    rows = [slice(c * L, (c + 1) * L) for c in cs]

    def stack(x):
        return jnp.concatenate([jnp.where(lane_head == h, x, 0.0) for h in range(RW_PACK)], axis=0)

    def unstack(x):
        out = x[0:L]
        for h in range(1, RW_PACK):
            out = out + x[h * L:(h + 1) * L]
        return out

    zr = zr_ref[0].astype(F32)
    zk = zk_ref[0].astype(F32)
    zv = zv_ref[0].astype(F32)
    prev = jnp.where(t1 == 0, 0.0, prev_scr[p1])
    r = _shift_lerp(zr, prev[0:1, :], pva(_PV_MU_R))
    k = _shift_lerp(zk, prev[1:2, :], pva(_PV_MU_K))
    v = _shift_lerp(zv, prev[2:3, :], pva(_PV_MU_V))
    prev_scr[p1] = jnp.concatenate([zr[t_rows - 1:t_rows, :], zk[t_rows - 1:t_rows, :],
                                    zv[t_rows - 1:t_rows, :], jnp.zeros((SUBLANE - 3, lanes), F32)],
                                   axis=0)
    prep = {"c": 0, "phase": 0}

    def prep_tile_a():
        w2_hi, w2_lo = _split_bf16(w2_ref[...])
        th_hi = th_hi_scr[...]
        w_pre = (pva(_PV_W0) + jnp.dot(th_hi, w2_hi, preferred_element_type=F32)
                 + jnp.dot(th_lo_scr[...], w2_hi, preferred_element_type=F32)
                 + jnp.dot(th_hi, w2_lo, preferred_element_type=F32))
        t = -w_pre
        softplus = jnp.maximum(t, 0.0) + jnp.log1p(jnp.exp(-jnp.abs(t)))
        prep["log_decay"] = -jnp.exp(-softplus - 0.5)
        prep["a"] = jax.nn.sigmoid(pva(_PV_A0) + _mm(xw_scr[...], a2_ref[...]))
        g_scr[s13_w] = _mm(sg_scr[...], g2_ref[...])

    def prep_tile_b():
        a = prep["a"]
        kk = k * pva(_PV_KK)
        kk = kk / jnp.maximum(jnp.sqrt(_mm(kk * kk, e_head)), 1e-12)
        kn = k * (1.0 + (a - 1.0) * pva(_PV_KA))
        prep["kk"], prep["kn"], prep["bv"] = kk, kn, kk * a
        bonus_scr[s13_w] = _mm(r * kn * pva(_PV_RK), e_head) * v
        lw_hi, lw_lo = _split_bf16(prep["log_decay"])
        prep["cm"] = [jnp.dot(tri, lw_hi[rw], preferred_element_type=F32)
                      + jnp.dot(tri, lw_lo[rw], preferred_element_type=F32) for rw in rows]

    def prep_chunk():
        c = prep["c"]
        if c >= n_chunks:
            return
        prep["c"] = c + 1
        rw = rows[c]
        cm = prep["cm"][c]
        cm_last = cm[L - 1:L, :]
        kc, bvc = prep["kn"][rw], prep["bv"][rw]
        r_t = r[rw] * jnp.exp(cm)
        a_st = stack(-prep["kk"][rw] * jnp.exp(cm - prep["log_decay"][rw])).astype(BF16)
        e_neg = jnp.exp(-cm)
        e_rem = jnp.exp(cm_last - cm)
        ar_scr[sa, c] = jnp.concatenate([a_st, stack(r_t).astype(BF16)], axis=0)
        kbh_scr[sa, c] = jnp.concatenate([(kc * e_neg).astype(BF16), (bvc * e_neg).astype(BF16)], axis=0)
        vst_scr[sa, c] = stack(v[rw]).astype(BF16)
        rt_scr[sa, c] = r_t
        kbar_scr[sa, c] = jnp.concatenate([(kc * e_rem).astype(BF16), (bvc * e_rem).astype(BF16)], axis=0)
        vb_scr[sa, c] = v[rw].astype(BF16)
        plp_scr[sa, c] = jnp.broadcast_to(jnp.exp(cm_last), (SUBLANE, lanes))

    def prep_step():
        phase = prep["phase"]
        prep["phase"] = phase + 1
        if phase == 0:
            prep_tile_a()
        elif phase == 1:
            prep_tile_b()
        else:
            for _ in range(-(-n_chunks // (n_sq + 1))):
                prep_chunk()

    chain = {"s": jnp.where(t3 == 0, 0.0, s_scr[p3]), "c": 0, "y": []}

    def chain_step():
        c = chain["c"]
        if c >= n_chunks:
            return
        s0 = chain["s"]
        s0b = s0.astype(BF16)
        chain["y"].append(y_scr[sa, c * L:(c + 1) * L, :] + _mm_nt(q_scr[sa, c], s0b))
        chain["s"] = s0 * pl_scr[sa, c][0:1, :] + _mm(s0b, gm_scr[sa, c]) + cm_scr[sa, c]
        chain["c"] = c + 1

    def fill():
        prep_step()
        chain_step()

    ar_st = [ar_scr[sb, c] for c in cs]
    kb_h = [kbh_scr[sb, c] for c in cs]
    v_st = [vst_scr[sb, c] for c in cs]
    prod = [_mm_nt(ar_st[c], kb_h[c]) for c in cs]
    fill()
    swapped = [pltpu.roll(x, L, axis=1) for x in prod]
    prod_k = [jnp.where(first_half, prod[c], swapped[c]) for c in cs]
    prod_b = [jnp.where(first_half, swapped[c], prod[c]) for c in cs]
    a_ak = [jnp.where(m_strict, x[:SL], 0.0).astype(BF16) for x in prod_k]
    a_rk = [jnp.where(m_incl, x[SL:], 0.0).astype(BF16) for x in prod_k]
    a_ab = [jnp.where(m_strict, x[:SL], 0.0) for x in prod_b]
    a_rb = [jnp.where(m_incl, x[SL:], 0.0).astype(BF16) for x in prod_b]

    xp = [x.astype(BF16) for x in a_ab]
    tinv = [eye + x for x in a_ab]
    for _ in range(n_sq):
        xp = [_mm(x, x).astype(BF16) for x in xp]
        tinv = [tinv[c] + _mm(tinv[c], xp[c]) for c in cs]
        fill()
    tinv = [x.astype(BF16) for x in tinv]

    x0 = [_mm(a_ak[c], v_st[c]).astype(BF16) for c in cs]
    fill()
    wu = [_mm(tinv[c], jnp.concatenate([ar_st[c][:SL], x0[c]], axis=1)) for c in cs]
    fill()
    yq = [_mm(a_rb[c], wu[c]) for c in cs]
    y0 = [_mm(a_rk[c], v_st[c]) for c in cs]
    while prep["c"] < n_chunks or chain["c"] < n_chunks:
        fill()
    cmats, gmats = [], []
    for c in cs:
        kbar = kbar_scr[sb, c]
        vu = jnp.concatenate([vb_scr[sb, c], unstack(wu[c][:, lanes:]).astype(BF16)], axis=0)
        cmats.append(jnp.where(same_head, _mm_tn(vu, kbar), 0.0))
        gmats.append(jnp.where(same_head, _mm_tn(unstack(wu[c][:, :lanes]), kbar[L:]), 0.0).astype(BF16))

    s_scr[p3] = chain["s"]
    y = jnp.concatenate(chain["y"], axis=0)
    inv_n = 1.0 / RW_HEAD
    mean = _mm_exact_rhs(y, e_head) * inv_n
    dy = y - mean
    var = _mm(dy * dy, e_head) * inv_n
    yn = dy * lax.rsqrt(var + GN_EPS) * pvb_ref[_PV_LG:_PV_LG + 1, :] + pvb_ref[_PV_LB:_PV_LB + 1, :]
    o_ref[0] = ((yn + bonus_scr[s13_r]) * g_scr[s13_r]).astype(o_ref.dtype)

    for c, rw in enumerate(rows):
        q_scr[sb, c] = (rt_scr[sb, c] + unstack(yq[c][:, :lanes])).astype(BF16)
        y_scr[sb, rw, :] = unstack(yq[c][:, lanes:] + y0[c])
        cm_scr[sb, c] = cmats[c]
        gm_scr[sb, c] = gmats[c]
        pl_scr[sb, c] = plp_scr[sb, c]


def _rwkv(z, z_lora, col_r, col_k, col_v, pvec, mu_l, w2p, a2p, g2p, width, side, side_lead,
          t_rows=1024):
    bsz, s, _ = z.shape
    lora_w = z_lora.shape[2]
    n_p = width // LANE
    L = RW_CHUNK
    n_chunks = t_rows // L
    n_t = s // t_rows
    n_tiles = bsz * n_t * n_p
    n_steps = n_tiles + 2
    assert RW_PACK * L == LANE and lora_w == 3 * LANE

    def tile(i, lag):
        it = jnp.clip(i - lag, 0, n_tiles - 1)
        return it // (n_t * n_p), (it // n_p) % n_t, it % n_p

    def z_spec(col):
        def idx(i):
            b, t, p = tile(i, 0)
            return b, t, col + p
        return pl.BlockSpec((1, t_rows, LANE), idx)

    def zl_idx(i):
        b, t, _ = tile(i, 0)
        return b, t, 0

    side_in, side_out = _slab_specs(side, side_lead, 0, ROW_CHUNK, n_steps, lambda i: i)
    kern = functools.partial(_rwkv_kernel, chunk=L, n_t=n_t, n_p=n_p, n_tiles=n_tiles)

    def per_chunk(rows_, dtype):
        return pltpu.VMEM((2, n_chunks, rows_, LANE), dtype)

    return pl.pallas_call(
        kern,
        grid=(n_steps,),
        in_specs=[z_spec(col_r), z_spec(col_k), z_spec(col_v),
                  pl.BlockSpec((1, t_rows, lora_w), zl_idx),
                  pl.BlockSpec((_PV_ROWS, LANE), lambda i: (0, tile(i, 0)[2])),
                  pl.BlockSpec((_PV_ROWS, LANE), lambda i: (0, tile(i, 2)[2])),
                  pl.BlockSpec((1, lora_w), lambda i: (0, 0)),
                  pl.BlockSpec((LANE, LANE), lambda i: (0, tile(i, 0)[2])),
                  pl.BlockSpec((LANE, LANE), lambda i: (0, tile(i, 0)[2])),
                  pl.BlockSpec((2 * LANE, LANE), lambda i: (0, tile(i, 0)[2])),
                  side_in],
        out_specs=(pl.BlockSpec((1, t_rows, LANE), lambda i: tile(i, 2)), side_out),
        out_shape=(jax.ShapeDtypeStruct((bsz, s, width), BF16),
                   jax.ShapeDtypeStruct(side.shape[1:], BF16)),
        scratch_shapes=[pltpu.VMEM((n_p, LANE, LANE), F32),
                        pltpu.VMEM((n_p, SUBLANE, LANE), F32),
                        pltpu.VMEM((SUBLANE, lora_w), F32),
                        pltpu.VMEM((t_rows, LANE), BF16),
                        pltpu.VMEM((t_rows, LANE), BF16),
                        pltpu.VMEM((t_rows, LANE), BF16),
                        pltpu.VMEM((t_rows, lora_w - LANE), BF16),
                        per_chunk(4 * L, BF16),
                        per_chunk(2 * L, BF16),
                        per_chunk(2 * L, BF16),
                        per_chunk(L, F32),
                        per_chunk(2 * L, BF16),
                        per_chunk(L, BF16),
                        per_chunk(SUBLANE, F32),
                        per_chunk(L, BF16),
                        pltpu.VMEM((2, t_rows, LANE), F32),
                        per_chunk(LANE, BF16),
                        per_chunk(LANE, F32),
                        per_chunk(SUBLANE, F32),
                        pltpu.VMEM((3, t_rows, LANE), F32),
                        pltpu.VMEM((3, t_rows, LANE), F32)],
        compiler_params=_params(("arbitrary",)),
        name="rwkv7",
    )(z, z, z, z_lora, pvec, pvec, mu_l, w2p, a2p, g2p, side)


def _moba_kernel(q_ref, k_ref, v_ref, side_ref, o_ref, side_o_ref, ka_scr, vt_scr, s_scr, p_scr, *,
                 n_heads):
    h = pl.program_id(1)
    side_o_ref[...] = side_ref[...].astype(BF16)
    s_len = q_ref.shape[1]
    dh = q_ref.shape[2]
    blk = MOBA_BLOCK
    nb = s_len // blk
    log2e = 1.4426950408889634
    scale = dh ** -0.5 * log2e
    neg_inf = -jnp.inf

    def slope_row(width):
        return log2e * jnp.exp(jnp.full((1, width), -8.0 / n_heads * 0.6931471805599453, F32)
                               * (h + 1).astype(F32))

    lane = lax.broadcasted_iota(jnp.int32, (blk, dh), 1)
    bias = slope_row(dh) * lax.broadcasted_iota(jnp.int32, (blk, dh), 0).astype(F32)
    extra = jnp.zeros((blk, dh), F32)
    for col in range(3):
        part = bias.astype(BF16).astype(F32)
        extra = jnp.where(lane == col, part, extra)
        bias = bias - part
    extra = extra.astype(BF16)
    ones_cols = jnp.where(lane < 3, 1.0, 0.0).astype(BF16)

    kmean = []
    for j in range(nb):
        rows = slice(j * blk, (j + 1) * blk)
        k_j = k_ref[0, rows, :]
        kmean.append(jnp.mean(k_j.astype(F32), axis=0, keepdims=True))
        ka_scr[rows, :dh] = k_j
        ka_scr[rows, dh:] = extra
        vt_scr[:dh, rows] = v_ref[0, rows, :].astype(F32).T.astype(BF16)
    sub = lax.broadcasted_iota(jnp.int32, (vt_scr.shape[0] - dh, s_len), 0)
    vt_scr[dh:, :] = jnp.where(sub == 0, 1.0, 0.0).astype(BF16)
    kmean = jnp.concatenate(kmean, axis=0)
    kmean_parts = []
    for _ in range(3):
        part = kmean.astype(BF16)
        kmean_parts.append(part)
        kmean = kmean - part.astype(F32)

    slope = slope_row(blk)
    causal = (lax.broadcasted_iota(jnp.int32, (blk, blk), 1)
              >= lax.broadcasted_iota(jnp.int32, (blk, blk), 0))
    blk_id = lax.broadcasted_iota(jnp.int32, (nb, 1), 0)

    def scores(qb):
        q = q_ref[0, qb * blk:(qb + 1) * blk, :]
        q_aug = jnp.concatenate([(q.astype(F32) * scale).astype(BF16), ones_cols], axis=1)
        gate = sum(lax.dot_general(part, q, NT_DIMS, preferred_element_type=F32)
                   for part in kmean_parts)
        past = blk_id < qb
        offs = []
        m = None
        for n in range(qb + 1):
            t = lax.dot_general(ka_scr[n * blk:(n + 1) * blk, :], q_aug, NT_DIMS,
                                preferred_element_type=F32)
            if n == qb:
                t = jnp.where(causal, t, neg_inf)
                off = jnp.zeros((1, blk), F32)
            else:
                g_n = gate[n:n + 1, :]
                beats = past & ((gate > g_n) | ((gate == g_n) & (blk_id < n)))
                rank = jnp.sum(jnp.where(beats, 1.0, 0.0), axis=0, keepdims=True)
                off = jnp.where(rank < float(MOBA_TOPK), slope * float((n - qb) * blk), neg_inf)
            s_scr[qb % 2, n] = t
            offs.append(off)
            cmax = jnp.max(t, axis=0, keepdims=True) + off
            m = cmax if m is None else jnp.maximum(m, cmax)
        return m, offs

    def attend(qb, m, offs):
        for n in range(qb + 1):
            p = jnp.exp2(s_scr[qb % 2, n] - (m - offs[n]))
            p_scr[qb % 2, n * blk:(n + 1) * blk, :] = p.astype(BF16)
        kk = (qb + 1) * blk
        acc = jnp.dot(vt_scr[:, :kk], p_scr[qb % 2, :kk, :], preferred_element_type=F32)
        o_ref[0, qb * blk:(qb + 1) * blk, :] = (acc[:dh] / acc[dh:dh + 1]).T.astype(o_ref.dtype)

    pending = scores(0)
    for qb in range(nb):
        nxt = scores(qb + 1) if qb + 1 < nb else None
        attend(qb, *pending)
        pending = nxt


def _moba(qkv, n_heads, side, side_lead):
    bsz, s, d3 = qkv.shape
    d = d3 // 3
    dh = d // n_heads
    blk = MOBA_BLOCK
    nb = s // blk
    side_in, side_out = _slab_specs(side, side_lead, 0, ROW_CHUNK, bsz * n_heads,
                                    lambda b, h: b * n_heads + h)
    kern = functools.partial(_moba_kernel, n_heads=n_heads)
    return pl.pallas_call(
        kern,
        grid=(bsz, n_heads),
        in_specs=[pl.BlockSpec((1, s, dh), lambda b, h: (b, 0, h)),
                  pl.BlockSpec((1, s, dh), lambda b, h: (b, 0, n_heads + h)),
                  pl.BlockSpec((1, s, dh), lambda b, h: (b, 0, 2 * n_heads + h)),
                  side_in],
        out_specs=(pl.BlockSpec((1, s, dh), lambda b, h: (b, 0, h)), side_out),
        out_shape=(jax.ShapeDtypeStruct((bsz, s, d), BF16),
                   jax.ShapeDtypeStruct(side.shape[1:], BF16)),
        scratch_shapes=[pltpu.VMEM((s, 2 * dh), BF16),
                        pltpu.VMEM((dh + 16, s), BF16),
                        pltpu.VMEM((2, nb, blk, blk), F32),
                        pltpu.VMEM((2, s, blk), BF16)],
        compiler_params=_params(("arbitrary", "arbitrary")),
        name="moba",
    )(qkv, qkv, qkv, side)


def _pad_cols(w, n):
    return jnp.pad(w, ((0, 0), (0, n - w.shape[1])))


def _pad_rows(w, n, before=0):
    return jnp.pad(w, ((before, n - before - w.shape[0]), (0, 0)))


def kernel(x, c, w_ada, b_ada, g_pre_mix, g_post_mix, g_pre_ffn, g_post_ffn, w_ffn_in, w_ffn_out,
           w_in_ab, w_out_ab, a_v_gain, a_v_bias, a_w_s, a_b_s, b_mu, b_w0, b_w2, b_a0, b_a2, b_g2,
           b_k_k, b_k_a, b_r_k, b_lnx_gain, b_lnx_bias, w_qkv, w_o):
    bsz, s, d = x.shape
    depth = w_ada.shape[0]
    a_width = a_v_gain.shape[1]
    b_width = b_w0.shape[1]
    n_lw = b_w2.shape[1]
    n_la = b_a2.shape[1]
    n_lg = b_g2.shape[1]
    n_heads = d // ATT_HEAD
    assert s % MOBA_BLOCK == 0 and s % 1024 == 0
    assert n_lw + n_la <= LANE and n_lg <= 2 * LANE

    w_in_t = jnp.swapaxes(w_in_ab, 1, 2)
    mod, w_in_h0 = _ada_mod(c, w_ada, b_ada, w_in_t, 0)

    w_qkv_h = None
    for layer in range(depth):
        mod3 = mod[layer].reshape(bsz, 1, 6 * d)
        i = layer // 2
        if layer % 2 == 0:
            lora_w = 3 * LANE
            n_main = 2 * a_width + 3 * b_width
            w_t = w_in_h0 if i == 0 else w_in_t[i]
            w_tail = _pad_rows(w_t[n_main:], lora_w)
            z, z_lora, w_out_h = _norm_mm(x, g_pre_mix[layer], mod3, 1, 0, w_t, BF16, w_out_ab, i,
                                          w_tail=w_tail, transposed=True, tn=n_main // 4,
                                          name="in_proj_ab")
            y_a = _mixer_a(z, a_v_gain[i], a_v_bias[i], a_w_s[i], a_b_s[i], a_width)

            mu = b_mu[i]
            pvec = jnp.stack([mu[0:b_width], mu[b_width:2 * b_width], mu[2 * b_width:3 * b_width],
                              b_w0[i], b_a0[i], b_k_k[i], b_k_a[i], b_r_k[i].reshape(-1),
                              b_lnx_gain[i], b_lnx_bias[i]])
            pvec = _pad_rows(pvec, _PV_ROWS)
            mu_l = _pad_cols(mu[3 * b_width:].reshape(1, -1), lora_w)
            w2p = _pad_rows(b_w2[i], LANE)
            a2p = _pad_rows(b_a2[i], LANE, before=n_lw)
            g2p = _pad_rows(b_g2[i], 2 * LANE)
            cb = 2 * a_width // LANE
            nb_w = b_width // LANE
            y_b, w_ffn_in_h = _rwkv(z, z_lora, cb, cb + nb_w, cb + 2 * nb_w, pvec, mu_l, w2p, a2p,
                                    g2p, b_width, w_ffn_in, layer)
            x, w_ffn_out_h = _out_proj(y_a, y_b, 0, 0, w_out_h, x, mod3, g_post_mix[layer],
                                       w_ffn_out, layer)
        else:
            w_q = w_qkv[i] if w_qkv_h is None else w_qkv_h
            qkv, w_o_h = _norm_mm(x, g_pre_mix[layer], mod3, 1, 0, w_q, BF16, w_o, i,
                                  tn=d, name="qkv_proj")
            o, w_ffn_in_h = _moba(qkv, n_heads, w_ffn_in, layer)
            x, w_ffn_out_h = _out_proj(o, o, 0, 1, w_o_h, x, mod3, g_post_mix[layer],
                                       w_ffn_out, layer)
        if layer + 1 < depth and (layer + 1) % 2 == 1:
            x, w_qkv_h = _ffn(x, g_pre_ffn[layer], g_post_ffn[layer], mod3, w_ffn_in_h, w_ffn_out_h,
                              w_qkv, (layer + 1) // 2)
        else:
            x = _ffn(x, g_pre_ffn[layer], g_post_ffn[layer], mod3, w_ffn_in_h, w_ffn_out_h)
            w_qkv_h = None
    return x
```

```python
import functools

import jax
import jax.numpy as jnp
from jax import lax
from jax.experimental import pallas as pl
from jax.experimental.pallas import tpu as pltpu

F32 = jnp.float32
BF16 = jnp.bfloat16

NORM_EPS = 1e-6
LN_EPS = 1e-5
GN_EPS = 64e-5

LANE = 128
SUBLANE = 8
A_GROUPS = 8
A_CHUNK = 128
RW_HEAD = 64
RW_CHUNK = 64
RW_PACK = 2
MOBA_BLOCK = 256
MOBA_TOPK = 3
ATT_HEAD = 128

NT_DIMS = (((1,), (1,)), ((), ()))
TN_DIMS = (((0,), (0,)), ((), ()))

VMEM_LIMIT = 56 * 1024 * 1024


def _params(sem):
    return pltpu.CompilerParams(dimension_semantics=sem, vmem_limit_bytes=VMEM_LIMIT)


def _slab_specs(side, lead, axis, unit, n_steps, flat_index):
    total = side.shape[1 + axis]
    n_slab = max(n for n in range(1, n_steps + 1) if total % (n * unit) == 0)
    shape = list(side.shape[1:])
    shape[axis] = total // n_slab

    def idx(*grid):
        slab = jnp.minimum(flat_index(*grid), n_slab - 1)
        return (slab, 0) if axis == 0 else (0, slab)

    return (pl.BlockSpec((None,) + tuple(shape), lambda *g: (lead,) + idx(*g)),
            pl.BlockSpec(tuple(shape), idx))


ROW_CHUNK = 16
ROW_UNROLL = 8

def _for_row_chunks(n_rows, fn):
    def body(i, carry):
        fn(pl.ds(pl.multiple_of(i * ROW_CHUNK, ROW_CHUNK), ROW_CHUNK))
        return carry
    lax.fori_loop(0, n_rows // ROW_CHUNK, body, 0, unroll=ROW_UNROLL)


def _modulated_norm(x_ref, g_ref, sc_ref, sh_ref, gm_scr, h_scr):
    gm_scr[...] = g_ref[...] * (1.0 + sc_ref[0])

    def rows(r):
        x = x_ref[0, r, :]
        ms = jnp.mean(x * x, axis=-1, keepdims=True)
        h_scr[r, :] = (x * lax.rsqrt(ms + NORM_EPS) * gm_scr[...] + sh_ref[0]).astype(BF16)

    _for_row_chunks(h_scr.shape[0], rows)


def _post_norm_residual(y_scr, x_ref, gt_ref, gpost_ref, gm_scr, o_ref):
    gm_scr[...] = gt_ref[0] * gpost_ref[...]

    def rows(r):
        y = y_scr[r, :]
        ms = jnp.mean(y * y, axis=-1, keepdims=True)
        o_ref[0, r, :] = x_ref[0, r, :] + y * lax.rsqrt(ms + NORM_EPS) * gm_scr[...]

    _for_row_chunks(o_ref.shape[1], rows)


def _ada_kernel(c_ref, w_ref, b_ref, side_ref, o_ref, side_o_ref):
    c = c_ref[...]
    cond = (c * jax.nn.sigmoid(c)).astype(BF16)
    o_ref[0] = jnp.dot(cond, w_ref[0].astype(BF16), preferred_element_type=F32) + b_ref[0]
    side_o_ref[...] = side_ref[...].astype(BF16)


def _ada_mod(c, w_ada, b_ada, side, side_lead):
    depth, d, n = w_ada.shape
    bsz = c.shape[0]
    bp = -(-bsz // SUBLANE) * SUBLANE
    c_p = jnp.pad(c, ((0, bp - bsz), (0, 0)))
    tn = 1024
    nj = n // tn
    side_in, side_out = _slab_specs(side, side_lead, 0, ROW_CHUNK, depth * nj, lambda l, j: l * nj + j)
    out, side_h = pl.pallas_call(
        _ada_kernel,
        grid=(depth, nj),
        in_specs=[pl.BlockSpec((bp, d), lambda l, j: (0, 0)),
                  pl.BlockSpec((1, d, tn), lambda l, j: (l, 0, j)),
                  pl.BlockSpec((1, 1, tn), lambda l, j: (l, 0, j)),
                  side_in],
        out_specs=(pl.BlockSpec((1, bp, tn), lambda l, j: (l, 0, j)), side_out),
        out_shape=(jax.ShapeDtypeStruct((depth, bp, n), F32),
                   jax.ShapeDtypeStruct(side.shape[1:], BF16)),
        compiler_params=_params(("arbitrary", "arbitrary")),
        name="ada_mod",
    )(c_p, w_ada, b_ada.reshape(depth, 1, n), side)
    return out[:, :bsz], side_h


def _norm_mm_kernel(*refs, has_tail, transposed):
    if has_tail:
        (x_ref, g_ref, sc_ref, sh_ref, w_ref, wt_ref, side_ref, o_ref, ot_ref, side_o_ref,
         h_scr, gm_scr) = refs
    else:
        x_ref, g_ref, sc_ref, sh_ref, w_ref, side_ref, o_ref, side_o_ref, h_scr, gm_scr = refs
    j = pl.program_id(2)

    @pl.when(j == 0)
    def _():
        _modulated_norm(x_ref, g_ref, sc_ref, sh_ref, gm_scr, h_scr)

    def project(wr, out_ref):
        w = wr[...].astype(BF16)
        if transposed:
            y = lax.dot_general(h_scr[...], w, NT_DIMS, preferred_element_type=F32)
        else:
            y = jnp.dot(h_scr[...], w, preferred_element_type=F32)
        out_ref[0] = y.astype(out_ref.dtype)

    project(w_ref, o_ref)
    if has_tail:
        pl.when(j == pl.num_programs(2) - 1)(lambda: project(wt_ref, ot_ref))
    side_o_ref[...] = side_ref[...].astype(BF16)


def _norm_mm(x, gain, mod3, sc_idx, sh_idx, w, out_dtype, side, side_lead, w_tail=None,
             transposed=False, tm=1024, tn=1024, name="norm_mm"):
    bsz, s, d = x.shape
    n_w = w.shape[0] if transposed else w.shape[1]
    nm, nj = s // tm, n_w // tn
    n = nj * tn
    side_in, side_out = _slab_specs(side, side_lead, 1, LANE, bsz * nm * nj,
                                    lambda b, m, j: (b * nm + m) * nj + j)
    if transposed:
        w_specs = [pl.BlockSpec((tn, d), lambda b, m, j: (j, 0))]
    else:
        w_specs = [pl.BlockSpec((d, tn), lambda b, m, j: (0, j))]
    w_args = [w]
    out_specs = [pl.BlockSpec((1, tm, tn), lambda b, m, j: (b, m, j))]
    out_shape = [jax.ShapeDtypeStruct((bsz, s, n), out_dtype)]
    if w_tail is not None:
        n_tail = w_tail.shape[0] if transposed else w_tail.shape[1]
        w_specs.append(pl.BlockSpec(w_tail.shape, lambda b, m, j: (0, 0), pipeline_mode=pl.Buffered(1)))
        w_args.append(w_tail)
        out_specs.append(pl.BlockSpec((1, tm, n_tail), lambda b, m, j: (b, m, 0)))
        out_shape.append(jax.ShapeDtypeStruct((bsz, s, n_tail), out_dtype))
    return pl.pallas_call(
        functools.partial(_norm_mm_kernel, has_tail=w_tail is not None, transposed=transposed),
        grid=(bsz, nm, nj),
        in_specs=[pl.BlockSpec((1, tm, d), lambda b, m, j: (b, m, 0)),
                  pl.BlockSpec((1, d), lambda b, m, j: (0, 0)),
                  pl.BlockSpec((1, 1, d), lambda b, m, j: (b, 0, sc_idx)),
                  pl.BlockSpec((1, 1, d), lambda b, m, j: (b, 0, sh_idx)),
                  *w_specs,
                  side_in],
        out_specs=(*out_specs, side_out),
        out_shape=(*out_shape, jax.ShapeDtypeStruct(side.shape[1:], BF16)),
        scratch_shapes=[pltpu.VMEM((tm, d), BF16), pltpu.VMEM((1, d), F32)],
        compiler_params=_params(("arbitrary", "arbitrary", "arbitrary")),
        name=name,
    )(x, gain.reshape(1, d), mod3, mod3, *w_args, side)


def _ffn_kernel(*refs, has_side):
    if has_side:
        (x_ref, gpre_ref, sc_ref, sh_ref, gt_ref, gpost_ref, wg_ref, wu_ref, wo_ref, side_ref,
         o_ref, side_o_ref, h_scr, gm_scr, acc_scr) = refs
        side_o_ref[...] = side_ref[...].astype(BF16)
    else:
        (x_ref, gpre_ref, sc_ref, sh_ref, gt_ref, gpost_ref, wg_ref, wu_ref, wo_ref,
         o_ref, h_scr, gm_scr, acc_scr) = refs
    f = pl.program_id(2)

    @pl.when(f == 0)
    def _():
        _modulated_norm(x_ref, gpre_ref, sc_ref, sh_ref, gm_scr, h_scr)
        acc_scr[...] = jnp.zeros_like(acc_scr)

    h = h_scr[...]
    g = jnp.dot(h, wg_ref[...], preferred_element_type=F32)
    u = jnp.dot(h, wu_ref[...], preferred_element_type=F32)
    a = (g * jax.nn.sigmoid(g) * u).astype(BF16)
    acc_scr[...] += jnp.dot(a, wo_ref[...], preferred_element_type=F32)

    @pl.when(f == pl.num_programs(2) - 1)
    def _():
        _post_norm_residual(acc_scr, x_ref, gt_ref, gpost_ref, gm_scr, o_ref)


def _ffn(x, gpre, gpost, mod3, w_in, w_out, side=None, side_lead=0, tm=1024, tf=512):
    bsz, s, d = x.shape
    fh = w_out.shape[0]
    nf = fh // tf
    nm = s // tm
    out_spec = pl.BlockSpec((1, tm, d), lambda b, m, f: (b, m, 0), pipeline_mode=pl.Buffered(1))
    out_shape = jax.ShapeDtypeStruct((bsz, s, d), F32)
    side_specs, side_args = [], []
    if side is not None:
        side_in, side_out = _slab_specs(side, side_lead, 1, LANE, bsz * nm * nf,
                                        lambda b, m, f: (b * nm + m) * nf + f)
        side_specs, side_args = [side_in], [side]
        out_spec = (out_spec, side_out)
        out_shape = (out_shape, jax.ShapeDtypeStruct(side.shape[1:], BF16))
    return pl.pallas_call(
        functools.partial(_ffn_kernel, has_side=side is not None),
        grid=(bsz, nm, nf),
        in_specs=[pl.BlockSpec((1, tm, d), lambda b, m, f: (b, m, 0), pipeline_mode=pl.Buffered(1)),
                  pl.BlockSpec((1, d), lambda b, m, f: (0, 0)),
                  pl.BlockSpec((1, 1, d), lambda b, m, f: (b, 0, 4)),
                  pl.BlockSpec((1, 1, d), lambda b, m, f: (b, 0, 3)),
                  pl.BlockSpec((1, 1, d), lambda b, m, f: (b, 0, 5)),
                  pl.BlockSpec((1, d), lambda b, m, f: (0, 0)),
                  pl.BlockSpec((d, tf), lambda b, m, f: (0, f)),
                  pl.BlockSpec((d, tf), lambda b, m, f: (0, nf + f)),
                  pl.BlockSpec((tf, d), lambda b, m, f: (f, 0)),
                  *side_specs],
        out_specs=out_spec,
        out_shape=out_shape,
        scratch_shapes=[pltpu.VMEM((tm, d), BF16), pltpu.VMEM((1, d), F32), pltpu.VMEM((tm, d), F32)],
        compiler_params=_params(("arbitrary", "arbitrary", "arbitrary")),
        name="ffn",
    )(x, gpre.reshape(1, d), mod3, mod3, mod3, gpost.reshape(1, d), w_in, w_in, w_out, *side_args)


def _out_proj_kernel(a0_ref, a1_ref, w0_ref, w1_ref, x_ref, gt_ref, gpost_ref, side_ref,
                     o_ref, side_o_ref, gm_scr, y_scr):
    side_o_ref[...] = side_ref[...].astype(BF16)
    y_scr[...] = (jnp.dot(a0_ref[0], w0_ref[...], preferred_element_type=F32)
                  + jnp.dot(a1_ref[0], w1_ref[...], preferred_element_type=F32))
    _post_norm_residual(y_scr, x_ref, gt_ref, gpost_ref, gm_scr, o_ref)


def _out_proj(a0, a1, col0, col1, w, x, mod3, gpost, side, side_lead, tm=512):
    bsz, s, d = x.shape
    kh = w.shape[0] // 2
    nm = s // tm
    side_in, side_out = _slab_specs(side, side_lead, 0, ROW_CHUNK, bsz * nm, lambda b, m: b * nm + m)
    return pl.pallas_call(
        _out_proj_kernel,
        grid=(bsz, nm),
        in_specs=[pl.BlockSpec((1, tm, kh), lambda b, m: (b, m, col0)),
                  pl.BlockSpec((1, tm, kh), lambda b, m: (b, m, col1)),
                  pl.BlockSpec((kh, d), lambda b, m: (0, 0)),
                  pl.BlockSpec((kh, d), lambda b, m: (1, 0)),
                  pl.BlockSpec((1, tm, d), lambda b, m: (b, m, 0)),
                  pl.BlockSpec((1, 1, d), lambda b, m: (b, 0, 2)),
                  pl.BlockSpec((1, d), lambda b, m: (0, 0)),
                  side_in],
        out_specs=(pl.BlockSpec((1, tm, d), lambda b, m: (b, m, 0)), side_out),
        out_shape=(jax.ShapeDtypeStruct((bsz, s, d), F32),
                   jax.ShapeDtypeStruct(side.shape[1:], BF16)),
        scratch_shapes=[pltpu.VMEM((1, d), F32), pltpu.VMEM((tm, d), F32)],
        compiler_params=_params(("arbitrary", "arbitrary")),
        name="out_proj",
    )(a0, a1, w, w, x, mod3, gpost.reshape(1, d), side)


def _mixer_a_kernel(z_ref, vg_ref, vb_ref, ws_ref, bst_ref, o_ref, wm_scr):
    ch = ws_ref.shape[1]

    @pl.when((pl.program_id(0) == 0) & (pl.program_id(1) == 0))
    def _():
        causal = (lax.broadcasted_iota(jnp.int32, (ch, ch), 0)
                  >= lax.broadcasted_iota(jnp.int32, (ch, ch), 1))
        for g in range(A_GROUPS):
            wm_scr[g] = jnp.where(causal, ws_ref[g], 0.0).astype(BF16)

    for c in range(z_ref.shape[1] // ch):
        rows = slice(c * ch, (c + 1) * ch)
        z = jax.nn.gelu(z_ref[0, rows, :].astype(F32))
        wdt = z.shape[1] // 2
        u = z[:, :wdt]
        v = z[:, wdt:]
        mu = jnp.mean(v, axis=-1, keepdims=True)
        dv = v - mu
        var = jnp.mean(dv * dv, axis=-1, keepdims=True)
        vn = (dv * lax.rsqrt(var + LN_EPS) * vg_ref[...] + vb_ref[...]).astype(BF16)
        gd = wdt // A_GROUPS
        for g in range(A_GROUPS):
            sv = jnp.dot(wm_scr[g], vn[:, g * gd:(g + 1) * gd], preferred_element_type=F32)
            sv = sv + bst_ref[:, g:g + 1]
            o_ref[0, rows, g * gd:(g + 1) * gd] = (u[:, g * gd:(g + 1) * gd] * sv).astype(o_ref.dtype)


def _mixer_a(z, v_gain, v_bias, w_s, b_s, width, chunks_per_step=4):
    bsz, s, _ = z.shape
    ch = A_CHUNK
    rows = ch * chunks_per_step
    return pl.pallas_call(
        _mixer_a_kernel,
        grid=(bsz, s // rows),
        in_specs=[pl.BlockSpec((1, rows, 2 * width), lambda b, c: (b, c, 0)),
                  pl.BlockSpec((1, width), lambda b, c: (0, 0)),
                  pl.BlockSpec((1, width), lambda b, c: (0, 0)),
                  pl.BlockSpec((A_GROUPS, ch, ch), lambda b, c: (0, 0, 0)),
                  pl.BlockSpec((ch, A_GROUPS), lambda b, c: (0, 0))],
        out_specs=pl.BlockSpec((1, rows, width), lambda b, c: (b, c, 0)),
        out_shape=jax.ShapeDtypeStruct((bsz, s, width), BF16),
        scratch_shapes=[pltpu.VMEM((A_GROUPS, ch, ch), BF16)],
        compiler_params=_params(("arbitrary", "arbitrary")),
        name="mixer_a",
    )(z, v_gain.reshape(1, width), v_bias.reshape(1, width), w_s, b_s.T)


_PV_MU_R, _PV_MU_K, _PV_MU_V, _PV_W0, _PV_A0, _PV_KK, _PV_KA, _PV_RK, _PV_LG, _PV_LB = range(10)
_PV_ROWS = 16


def _shift_lerp(x, prev_row, mu):
    rolled = pltpu.roll(x, 1, axis=0)
    first = lax.broadcasted_iota(jnp.int32, x.shape, 0) == 0
    xp = jnp.where(first, prev_row, rolled)
    return x + mu * (xp - x)


def _split_bf16(x):
    hi = x.astype(BF16)
    lo = (x - hi.astype(F32)).astype(BF16)
    return hi, lo


def _mm(x, y):
    return jnp.dot(x.astype(BF16), y.astype(BF16), preferred_element_type=F32)


def _mm_nt(x, y):
    return lax.dot_general(x.astype(BF16), y.astype(BF16), NT_DIMS, preferred_element_type=F32)


def _mm_tn(x, y):
    return lax.dot_general(x.astype(BF16), y.astype(BF16), TN_DIMS, preferred_element_type=F32)


def _mm_exact_rhs(x, e_bf16):
    xh, xl = _split_bf16(x)
    return (jnp.dot(xh, e_bf16, preferred_element_type=F32)
            + jnp.dot(xl, e_bf16, preferred_element_type=F32))


def _rwkv_kernel(zr_ref, zk_ref, zv_ref, zl_ref, pva_ref, pvb_ref, mul_ref, w2_ref, a2_ref, g2_ref,
                 side_ref, o_ref, side_o_ref,
                 s_scr, prev_scr, prevl_scr, th_hi_scr, th_lo_scr, xw_scr, sg_scr,
                 ar_scr, kbh_scr, vst_scr, rt_scr, kbar_scr, vb_scr, plp_scr,
                 q_scr, y_scr, gm_scr, cm_scr, pl_scr, bonus_scr, g_scr, *, chunk, n_t, n_p, n_tiles):
    i = pl.program_id(0)
    t_rows = zr_ref.shape[1]
    lanes = zr_ref.shape[2]
    lora = xw_scr.shape[1] + sg_scr.shape[1]
    L = chunk
    SL = RW_PACK * L
    n_chunks = t_rows // L
    side_o_ref[...] = side_ref[...].astype(BF16)

    i1 = jnp.minimum(i, n_tiles - 1)
    i3 = jnp.maximum(i - 2, 0)
    t1, p1 = (i1 // n_p) % n_t, i1 % n_p
    t3, p3 = (i3 // n_p) % n_t, i3 % n_p
    sa = i % 2
    sb = 1 - sa
    s13_w = i % 3
    s13_r = (i + 1) % 3

    @pl.when(i == 0)
    def _():
        for ref in (s_scr, prev_scr, prevl_scr, ar_scr, kbh_scr, vst_scr, rt_scr, kbar_scr, vb_scr,
                    plp_scr, q_scr, y_scr, gm_scr, cm_scr, pl_scr, bonus_scr, g_scr):
            ref[...] = jnp.zeros_like(ref)

    @pl.when(p1 == 0)
    def _():
        zl = zl_ref[0][:, :lora].astype(F32)
        prev = jnp.where(t1 == 0, 0.0, prevl_scr[0:1, :lora])
        zls = _shift_lerp(zl, prev, mul_ref[:, :lora])
        prevl_scr[0:1, :lora] = zl[t_rows - 1:t_rows, :]
        x_wa = zls[:, :LANE]
        th_hi, th_lo = _split_bf16(jnp.tanh(x_wa))
        th_hi_scr[...] = th_hi
        th_lo_scr[...] = th_lo
        xw_scr[...] = x_wa.astype(BF16)
        sg_scr[...] = jax.nn.sigmoid(zls[:, LANE:]).astype(BF16)

    def pva(r):
        return pva_ref[r:r + 1, :]

    li = lax.broadcasted_iota(jnp.int32, (lanes, lanes), 0) // RW_HEAD
    lj = lax.broadcasted_iota(jnp.int32, (lanes, lanes), 1) // RW_HEAD
    same_head = li == lj
    e_head = jnp.where(same_head, 1.0, 0.0).astype(BF16)
    ti = lax.broadcasted_iota(jnp.int32, (L, L), 0)
    tj = lax.broadcasted_iota(jnp.int32, (L, L), 1)
    tri = jnp.where(ti >= tj, 1.0, 0.0).astype(BF16)
    si = lax.broadcasted_iota(jnp.int32, (SL, SL), 0)
    sj = lax.broadcasted_iota(jnp.int32, (SL, SL), 1)
    same_blk = (si // L) == (sj // L)
    m_strict = same_blk & (si > sj)
    m_incl = same_blk & (si >= sj)
    eye = jnp.where(si == sj, 1.0, 0.0)
    lane_head = lax.broadcasted_iota(jnp.int32, (1, lanes), 1) // RW_HEAD
    first_half = lax.broadcasted_iota(jnp.int32, (1, 2 * L), 1) < L
    n_sq = max(L.bit_length() - 2, 0)
    cs = range(n_chunks)
    rows = [slice(c * L, (c + 1) * L) for c in cs]

    def stack(x):
        return jnp.concatenate([jnp.where(lane_head == h, x, 0.0) for h in range(RW_PACK)], axis=0)

    def unstack(x):
        out = x[0:L]
        for h in range(1, RW_PACK):
            out = out + x[h * L:(h + 1) * L]
        return out

    zr = zr_ref[0].astype(F32)
    zk = zk_ref[0].astype(F32)
    zv = zv_ref[0].astype(F32)
    prev = jnp.where(t1 == 0, 0.0, prev_scr[p1])
    r = _shift_lerp(zr, prev[0:1, :], pva(_PV_MU_R))
    k = _shift_lerp(zk, prev[1:2, :], pva(_PV_MU_K))
    v = _shift_lerp(zv, prev[2:3, :], pva(_PV_MU_V))
    prev_scr[p1] = jnp.concatenate([zr[t_rows - 1:t_rows, :], zk[t_rows - 1:t_rows, :],
                                    zv[t_rows - 1:t_rows, :], jnp.zeros((SUBLANE - 3, lanes), F32)],
                                   axis=0)
    prep = {"c": 0, "phase": 0}

    def prep_tile_a():
        w2_hi, w2_lo = _split_bf16(w2_ref[...])
        th_hi = th_hi_scr[...]
        w_pre = (pva(_PV_W0) + jnp.dot(th_hi, w2_hi, preferred_element_type=F32)
                 + jnp.dot(th_lo_scr[...], w2_hi, preferred_element_type=F32)
                 + jnp.dot(th_hi, w2_lo, preferred_element_type=F32))
        t = -w_pre
        softplus = jnp.maximum(t, 0.0) + jnp.log1p(jnp.exp(-jnp.abs(t)))
        prep["log_decay"] = -jnp.exp(-softplus - 0.5)
        prep["a"] = jax.nn.sigmoid(pva(_PV_A0) + _mm(xw_scr[...], a2_ref[...]))
        g_scr[s13_w] = _mm(sg_scr[...], g2_ref[...])

    def prep_tile_b():
        a = prep["a"]
        kk = k * pva(_PV_KK)
        kk = kk / jnp.maximum(jnp.sqrt(_mm(kk * kk, e_head)), 1e-12)
        kn = k * (1.0 + (a - 1.0) * pva(_PV_KA))
        prep["kk"], prep["kn"], prep["bv"] = kk, kn, kk * a
        bonus_scr[s13_w] = _mm(r * kn * pva(_PV_RK), e_head) * v
        lw_hi, lw_lo = _split_bf16(prep["log_decay"])
        prep["cm"] = [jnp.dot(tri, lw_hi[rw], preferred_element_type=F32)
                      + jnp.dot(tri, lw_lo[rw], preferred_element_type=F32) for rw in rows]

    def prep_chunk():
        c = prep["c"]
        if c >= n_chunks:
            return
        prep["c"] = c + 1
        rw = rows[c]
        cm = prep["cm"][c]
        cm_last = cm[L - 1:L, :]
        kc, bvc = prep["kn"][rw], prep["bv"][rw]
        r_t = r[rw] * jnp.exp(cm)
        a_st = stack(-prep["kk"][rw] * jnp.exp(cm - prep["log_decay"][rw])).astype(BF16)
        e_neg = jnp.exp(-cm)
        e_rem = jnp.exp(cm_last - cm)
        ar_scr[sa, c] = jnp.concatenate([a_st, stack(r_t).astype(BF16)], axis=0)
        kbh_scr[sa, c] = jnp.concatenate([(kc * e_neg).astype(BF16), (bvc * e_neg).astype(BF16)], axis=0)
        vst_scr[sa, c] = stack(v[rw]).astype(BF16)
        rt_scr[sa, c] = r_t
        kbar_scr[sa, c] = jnp.concatenate([(kc * e_rem).astype(BF16), (bvc * e_rem).astype(BF16)], axis=0)
        vb_scr[sa, c] = v[rw].astype(BF16)
        plp_scr[sa, c] = jnp.broadcast_to(jnp.exp(cm_last), (SUBLANE, lanes))

    def prep_step():
        phase = prep["phase"]
        prep["phase"] = phase + 1
        if phase == 0:
            prep_tile_a()
        elif phase == 1:
            prep_tile_b()
        else:
            for _ in range(-(-n_chunks // (n_sq + 1))):
                prep_chunk()

    chain = {"s": jnp.where(t3 == 0, 0.0, s_scr[p3]), "c": 0, "y": []}

    def chain_step():
        c = chain["c"]
        if c >= n_chunks:
            return
        s0 = chain["s"]
        s0b = s0.astype(BF16)
        chain["y"].append(y_scr[sa, c * L:(c + 1) * L, :] + _mm_nt(q_scr[sa, c], s0b))
        chain["s"] = s0 * pl_scr[sa, c][0:1, :] + _mm(s0b, gm_scr[sa, c]) + cm_scr[sa, c]
        chain["c"] = c + 1

    def fill():
        prep_step()
        chain_step()

    ar_st = [ar_scr[sb, c] for c in cs]
    kb_h = [kbh_scr[sb, c] for c in cs]
    v_st = [vst_scr[sb, c] for c in cs]
    prod = [_mm_nt(ar_st[c], kb_h[c]) for c in cs]
    fill()
    swapped = [pltpu.roll(x, L, axis=1) for x in prod]
    prod_k = [jnp.where(first_half, prod[c], swapped[c]) for c in cs]
    prod_b = [jnp.where(first_half, swapped[c], prod[c]) for c in cs]
    a_ak = [jnp.where(m_strict, x[:SL], 0.0).astype(BF16) for x in prod_k]
    a_rk = [jnp.where(m_incl, x[SL:], 0.0).astype(BF16) for x in prod_k]
    a_ab = [jnp.where(m_strict, x[:SL], 0.0) for x in prod_b]
    a_rb = [jnp.where(m_incl, x[SL:], 0.0).astype(BF16) for x in prod_b]

    xp = [x.astype(BF16) for x in a_ab]
    tinv = [eye + x for x in a_ab]
    for _ in range(n_sq):
        xp = [_mm(x, x).astype(BF16) for x in xp]
        tinv = [tinv[c] + _mm(tinv[c], xp[c]) for c in cs]
        fill()
    tinv = [x.astype(BF16) for x in tinv]

    x0 = [_mm(a_ak[c], v_st[c]).astype(BF16) for c in cs]
    fill()
    wu = [_mm(tinv[c], jnp.concatenate([ar_st[c][:SL], x0[c]], axis=1)) for c in cs]
    fill()
    yq = [_mm(a_rb[c], wu[c]) for c in cs]
    y0 = [_mm(a_rk[c], v_st[c]) for c in cs]
    while prep["c"] < n_chunks or chain["c"] < n_chunks:
        fill()
    cmats, gmats = [], []
    for c in cs:
        kbar = kbar_scr[sb, c]
        vu = jnp.concatenate([vb_scr[sb, c], unstack(wu[c][:, lanes:]).astype(BF16)], axis=0)
        cmats.append(jnp.where(same_head, _mm_tn(vu, kbar), 0.0))
        gmats.append(jnp.where(same_head, _mm_tn(unstack(wu[c][:, :lanes]), kbar[L:]), 0.0).astype(BF16))

    s_scr[p3] = chain["s"]
    y = jnp.concatenate(chain["y"], axis=0)
    inv_n = 1.0 / RW_HEAD
    mean = _mm_exact_rhs(y, e_head) * inv_n
    dy = y - mean
    var = _mm(dy * dy, e_head) * inv_n
    yn = dy * lax.rsqrt(var + GN_EPS) * pvb_ref[_PV_LG:_PV_LG + 1, :] + pvb_ref[_PV_LB:_PV_LB + 1, :]
    o_ref[0] = ((yn + bonus_scr[s13_r]) * g_scr[s13_r]).astype(o_ref.dtype)

    for c, rw in enumerate(rows):
        q_scr[sb, c] = (rt_scr[sb, c] + unstack(yq[c][:, :lanes])).astype(BF16)
        y_scr[sb, rw, :] = unstack(yq[c][:, lanes:] + y0[c])
        cm_scr[sb, c] = cmats[c]
        gm_scr[sb, c] = gmats[c]
        pl_scr[sb, c] = plp_scr[sb, c]


def _rwkv(z, z_lora, col_r, col_k, col_v, pvec, mu_l, w2p, a2p, g2p, width, side, side_lead,
          t_rows=1024):
    bsz, s, _ = z.shape
    lora_w = z_lora.shape[2]
    n_p = width // LANE
    L = RW_CHUNK
    n_chunks = t_rows // L
    n_t = s // t_rows
    n_tiles = bsz * n_t * n_p
    n_steps = n_tiles + 2
    assert RW_PACK * L == LANE and lora_w == 3 * LANE

    def tile(i, lag):
        it = jnp.clip(i - lag, 0, n_tiles - 1)
        return it // (n_t * n_p), (it // n_p) % n_t, it % n_p

    def z_spec(col):
        def idx(i):
            b, t, p = tile(i, 0)
            return b, t, col + p
        return pl.BlockSpec((1, t_rows, LANE), idx)

    def zl_idx(i):
        b, t, _ = tile(i, 0)
        return b, t, 0

    side_in, side_out = _slab_specs(side, side_lead, 0, ROW_CHUNK, n_steps, lambda i: i)
    kern = functools.partial(_rwkv_kernel, chunk=L, n_t=n_t, n_p=n_p, n_tiles=n_tiles)

    def per_chunk(rows_, dtype):
        return pltpu.VMEM((2, n_chunks, rows_, LANE), dtype)

    return pl.pallas_call(
        kern,
        grid=(n_steps,),
        in_specs=[z_spec(col_r), z_spec(col_k), z_spec(col_v),
                  pl.BlockSpec((1, t_rows, lora_w), zl_idx),
                  pl.BlockSpec((_PV_ROWS, LANE), lambda i: (0, tile(i, 0)[2])),
                  pl.BlockSpec((_PV_ROWS, LANE), lambda i: (0, tile(i, 2)[2])),
                  pl.BlockSpec((1, lora_w), lambda i: (0, 0)),
                  pl.BlockSpec((LANE, LANE), lambda i: (0, tile(i, 0)[2])),
                  pl.BlockSpec((LANE, LANE), lambda i: (0, tile(i, 0)[2])),
                  pl.BlockSpec((2 * LANE, LANE), lambda i: (0, tile(i, 0)[2])),
                  side_in],
        out_specs=(pl.BlockSpec((1, t_rows, LANE), lambda i: tile(i, 2)), side_out),
        out_shape=(jax.ShapeDtypeStruct((bsz, s, width), BF16),
                   jax.ShapeDtypeStruct(side.shape[1:], BF16)),
        scratch_shapes=[pltpu.VMEM((n_p, LANE, LANE), F32),
                        pltpu.VMEM((n_p, SUBLANE, LANE), F32),
                        pltpu.VMEM((SUBLANE, lora_w), F32),
                        pltpu.VMEM((t_rows, LANE), BF16),
                        pltpu.VMEM((t_rows, LANE), BF16),
                        pltpu.VMEM((t_rows, LANE), BF16),
                        pltpu.VMEM((t_rows, lora_w - LANE), BF16),
                        per_chunk(4 * L, BF16),
                        per_chunk(2 * L, BF16),
                        per_chunk(2 * L, BF16),
                        per_chunk(L, F32),
                        per_chunk(2 * L, BF16),
                        per_chunk(L, BF16),
                        per_chunk(SUBLANE, F32),
                        per_chunk(L, BF16),
                        pltpu.VMEM((2, t_rows, LANE), F32),
                        per_chunk(LANE, BF16),
                        per_chunk(LANE, F32),
                        per_chunk(SUBLANE, F32),
                        pltpu.VMEM((3, t_rows, LANE), F32),
                        pltpu.VMEM((3, t_rows, LANE), F32)],
        compiler_params=_params(("arbitrary",)),
        name="rwkv7",
    )(z, z, z, z_lora, pvec, pvec, mu_l, w2p, a2p, g2p, side)


def _moba_kernel(q_ref, k_ref, v_ref, side_ref, o_ref, side_o_ref, ka_scr, vt_scr, s_scr, p_scr, *,
                 n_heads):
    h = pl.program_id(1)
    side_o_ref[...] = side_ref[...].astype(BF16)
    s_len = q_ref.shape[1]
    dh = q_ref.shape[2]
    blk = MOBA_BLOCK
    nb = s_len // blk
    log2e = 1.4426950408889634
    scale = dh ** -0.5 * log2e
    neg_inf = -jnp.inf

    def slope_row(width):
        return log2e * jnp.exp(jnp.full((1, width), -8.0 / n_heads * 0.6931471805599453, F32)
                               * (h + 1).astype(F32))

    lane = lax.broadcasted_iota(jnp.int32, (blk, dh), 1)
    bias = slope_row(dh) * lax.broadcasted_iota(jnp.int32, (blk, dh), 0).astype(F32)
    extra = jnp.zeros((blk, dh), F32)
    for col in range(3):
        part = bias.astype(BF16).astype(F32)
        extra = jnp.where(lane == col, part, extra)
        bias = bias - part
    extra = extra.astype(BF16)
    ones_cols = jnp.where(lane < 3, 1.0, 0.0).astype(BF16)

    kmean = []
    for j in range(nb):
        rows = slice(j * blk, (j + 1) * blk)
        k_j = k_ref[0, rows, :]
        kmean.append(jnp.mean(k_j.astype(F32), axis=0, keepdims=True))
        ka_scr[rows, :dh] = k_j
        ka_scr[rows, dh:] = extra
        vt_scr[:dh, rows] = v_ref[0, rows, :].astype(F32).T.astype(BF16)
    sub = lax.broadcasted_iota(jnp.int32, (vt_scr.shape[0] - dh, s_len), 0)
    vt_scr[dh:, :] = jnp.where(sub == 0, 1.0, 0.0).astype(BF16)
    kmean = jnp.concatenate(kmean, axis=0)
    kmean_parts = []
    for _ in range(3):
        part = kmean.astype(BF16)
        kmean_parts.append(part)
        kmean = kmean - part.astype(F32)

    slope = slope_row(blk)
    causal = (lax.broadcasted_iota(jnp.int32, (blk, blk), 1)
              >= lax.broadcasted_iota(jnp.int32, (blk, blk), 0))
    blk_id = lax.broadcasted_iota(jnp.int32, (nb, 1), 0)

    def scores(qb):
        q = q_ref[0, qb * blk:(qb + 1) * blk, :]
        q_aug = jnp.concatenate([(q.astype(F32) * scale).astype(BF16), ones_cols], axis=1)
        gate = sum(lax.dot_general(part, q, NT_DIMS, preferred_element_type=F32)
                   for part in kmean_parts)
        past = blk_id < qb
        offs = []
        m = None
        for n in range(qb + 1):
            t = lax.dot_general(ka_scr[n * blk:(n + 1) * blk, :], q_aug, NT_DIMS,
                                preferred_element_type=F32)
            if n == qb:
                t = jnp.where(causal, t, neg_inf)
                off = jnp.zeros((1, blk), F32)
            else:
                g_n = gate[n:n + 1, :]
                beats = past & ((gate > g_n) | ((gate == g_n) & (blk_id < n)))
                rank = jnp.sum(jnp.where(beats, 1.0, 0.0), axis=0, keepdims=True)
                off = jnp.where(rank < float(MOBA_TOPK), slope * float((n - qb) * blk), neg_inf)
            s_scr[qb % 2, n] = t
            offs.append(off)
            cmax = jnp.max(t, axis=0, keepdims=True) + off
            m = cmax if m is None else jnp.maximum(m, cmax)
        return m, offs

    def attend(qb, m, offs):
        for n in range(qb + 1):
            p = jnp.exp2(s_scr[qb % 2, n] - (m - offs[n]))
            p_scr[qb % 2, n * blk:(n + 1) * blk, :] = p.astype(BF16)
        kk = (qb + 1) * blk
        acc = jnp.dot(vt_scr[:, :kk], p_scr[qb % 2, :kk, :], preferred_element_type=F32)
        o_ref[0, qb * blk:(qb + 1) * blk, :] = (acc[:dh] / acc[dh:dh + 1]).T.astype(o_ref.dtype)

    pending = scores(0)
    for qb in range(nb):
        nxt = scores(qb + 1) if qb + 1 < nb else None
        attend(qb, *pending)
        pending = nxt


def _moba(qkv, n_heads, side, side_lead):
    bsz, s, d3 = qkv.shape
    d = d3 // 3
    dh = d // n_heads
    blk = MOBA_BLOCK
    nb = s // blk
    side_in, side_out = _slab_specs(side, side_lead, 0, ROW_CHUNK, bsz * n_heads,
                                    lambda b, h: b * n_heads + h)
    kern = functools.partial(_moba_kernel, n_heads=n_heads)
    return pl.pallas_call(
        kern,
        grid=(bsz, n_heads),
        in_specs=[pl.BlockSpec((1, s, dh), lambda b, h: (b, 0, h)),
                  pl.BlockSpec((1, s, dh), lambda b, h: (b, 0, n_heads + h)),
                  pl.BlockSpec((1, s, dh), lambda b, h: (b, 0, 2 * n_heads + h)),
                  side_in],
        out_specs=(pl.BlockSpec((1, s, dh), lambda b, h: (b, 0, h)), side_out),
        out_shape=(jax.ShapeDtypeStruct((bsz, s, d), BF16),
                   jax.ShapeDtypeStruct(side.shape[1:], BF16)),
        scratch_shapes=[pltpu.VMEM((s, 2 * dh), BF16),
                        pltpu.VMEM((dh + 16, s), BF16),
                        pltpu.VMEM((2, nb, blk, blk), F32),
                        pltpu.VMEM((2, s, blk), BF16)],
        compiler_params=_params(("arbitrary", "arbitrary")),
        name="moba",
    )(qkv, qkv, qkv, side)


def _pad_cols(w, n):
    return jnp.pad(w, ((0, 0), (0, n - w.shape[1])))


def _pad_rows(w, n, before=0):
    return jnp.pad(w, ((before, n - before - w.shape[0]), (0, 0)))


def kernel(x, c, w_ada, b_ada, g_pre_mix, g_post_mix, g_pre_ffn, g_post_ffn, w_ffn_in, w_ffn_out,
           w_in_ab, w_out_ab, a_v_gain, a_v_bias, a_w_s, a_b_s, b_mu, b_w0, b_w2, b_a0, b_a2, b_g2,
           b_k_k, b_k_a, b_r_k, b_lnx_gain, b_lnx_bias, w_qkv, w_o):
    bsz, s, d = x.shape
    depth = w_ada.shape[0]
    a_width = a_v_gain.shape[1]
    b_width = b_w0.shape[1]
    n_lw = b_w2.shape[1]
    n_la = b_a2.shape[1]
    n_lg = b_g2.shape[1]
    n_heads = d // ATT_HEAD
    assert s % MOBA_BLOCK == 0 and s % 1024 == 0
    assert n_lw + n_la <= LANE and n_lg <= 2 * LANE

    w_in_t = jnp.swapaxes(w_in_ab, 1, 2)
    mod, w_in_h0 = _ada_mod(c, w_ada, b_ada, w_in_t, 0)

    w_qkv_h = None
    for layer in range(depth):
        mod3 = mod[layer].reshape(bsz, 1, 6 * d)
        i = layer // 2
        if layer % 2 == 0:
            lora_w = 3 * LANE
            n_main = 2 * a_width + 3 * b_width
            w_t = w_in_h0 if i == 0 else w_in_t[i]
            w_tail = _pad_rows(w_t[n_main:], lora_w)
            z, z_lora, w_out_h = _norm_mm(x, g_pre_mix[layer], mod3, 1, 0, w_t, BF16, w_out_ab, i,
                                          w_tail=w_tail, transposed=True, tn=n_main // 4,
                                          name="in_proj_ab")
            y_a = _mixer_a(z, a_v_gain[i], a_v_bias[i], a_w_s[i], a_b_s[i], a_width)

            mu = b_mu[i]
            pvec = jnp.stack([mu[0:b_width], mu[b_width:2 * b_width], mu[2 * b_width:3 * b_width],
                              b_w0[i], b_a0[i], b_k_k[i], b_k_a[i], b_r_k[i].reshape(-1),
                              b_lnx_gain[i], b_lnx_bias[i]])
            pvec = _pad_rows(pvec, _PV_ROWS)
            mu_l = _pad_cols(mu[3 * b_width:].reshape(1, -1), lora_w)
            w2p = _pad_rows(b_w2[i], LANE)
            a2p = _pad_rows(b_a2[i], LANE, before=n_lw)
            g2p = _pad_rows(b_g2[i], 2 * LANE)
            cb = 2 * a_width // LANE
            nb_w = b_width // LANE
            y_b, w_ffn_in_h = _rwkv(z, z_lora, cb, cb + nb_w, cb + 2 * nb_w, pvec, mu_l, w2p, a2p,
                                    g2p, b_width, w_ffn_in, layer)
            x, w_ffn_out_h = _out_proj(y_a, y_b, 0, 0, w_out_h, x, mod3, g_post_mix[layer],
                                       w_ffn_out, layer)
        else:
            w_q = w_qkv[i] if w_qkv_h is None else w_qkv_h
            qkv, w_o_h = _norm_mm(x, g_pre_mix[layer], mod3, 1, 0, w_q, BF16, w_o, i,
                                  tn=d, name="qkv_proj")
            o, w_ffn_in_h = _moba(qkv, n_heads, w_ffn_in, layer)
            x, w_ffn_out_h = _out_proj(o, o, 0, 1, w_o_h, x, mod3, g_post_mix[layer],
                                       w_ffn_out, layer)
        if layer + 1 < depth and (layer + 1) % 2 == 1:
            x, w_qkv_h = _ffn(x, g_pre_ffn[layer], g_post_ffn[layer], mod3, w_ffn_in_h, w_ffn_out_h,
                              w_qkv, (layer + 1) // 2)
        else:
            x = _ffn(x, g_pre_ffn[layer], g_post_ffn[layer], mod3, w_ffn_in_h, w_ffn_out_h)
            w_qkv_h = None
    return x
```

```python
import functools

import jax
import jax.numpy as jnp
from jax import lax
from jax.experimental import pallas as pl
from jax.experimental.pallas import tpu as pltpu

F32 = jnp.float32
BF16 = jnp.bfloat16

NORM_EPS = 1e-6
LN_EPS = 1e-5
GN_EPS = 64e-5

LANE = 128
SUBLANE = 8
A_GROUPS = 8
A_CHUNK = 128
RW_HEAD = 64
RW_CHUNK = 64
RW_PACK = 2
MOBA_BLOCK = 256
MOBA_TOPK = 3
ATT_HEAD = 128

NT_DIMS = (((1,), (1,)), ((), ()))
TN_DIMS = (((0,), (0,)), ((), ()))

VMEM_LIMIT = 56 * 1024 * 1024


def _params(sem):
    return pltpu.CompilerParams(dimension_semantics=sem, vmem_limit_bytes=VMEM_LIMIT)


def _slab_specs(side, lead, axis, unit, n_steps, flat_index):
    total = side.shape[1 + axis]
    n_slab = max(n for n in range(1, n_steps + 1) if total % (n * unit) == 0)
    shape = list(side.shape[1:])
    shape[axis] = total // n_slab

    def idx(*grid):
        slab = jnp.minimum(flat_index(*grid), n_slab - 1)
        return (slab, 0) if axis == 0 else (0, slab)

    return (pl.BlockSpec((None,) + tuple(shape), lambda *g: (lead,) + idx(*g)),
            pl.BlockSpec(tuple(shape), idx))


ROW_CHUNK = 16
ROW_UNROLL = 8

def _for_row_chunks(n_rows, fn):
    def body(i, carry):
        fn(pl.ds(pl.multiple_of(i * ROW_CHUNK, ROW_CHUNK), ROW_CHUNK))
        return carry
    lax.fori_loop(0, n_rows // ROW_CHUNK, body, 0, unroll=ROW_UNROLL)


def _modulated_norm(x_ref, g_ref, sc_ref, sh_ref, gm_scr, h_scr):
    gm_scr[...] = g_ref[...] * (1.0 + sc_ref[0])

    def rows(r):
        x = x_ref[0, r, :]
        ms = jnp.mean(x * x, axis=-1, keepdims=True)
        h_scr[r, :] = (x * lax.rsqrt(ms + NORM_EPS) * gm_scr[...] + sh_ref[0]).astype(BF16)

    _for_row_chunks(h_scr.shape[0], rows)


def _post_norm_residual(y_scr, x_ref, gt_ref, gpost_ref, gm_scr, o_ref):
    gm_scr[...] = gt_ref[0] * gpost_ref[...]

    def rows(r):
        y = y_scr[r, :]
        ms = jnp.mean(y * y, axis=-1, keepdims=True)
        o_ref[0, r, :] = x_ref[0, r, :] + y * lax.rsqrt(ms + NORM_EPS) * gm_scr[...]

    _for_row_chunks(o_ref.shape[1], rows)


def _ada_kernel(c_ref, w_ref, b_ref, side_ref, o_ref, side_o_ref):
    c = c_ref[...]
    cond = (c * jax.nn.sigmoid(c)).astype(BF16)
    o_ref[0] = jnp.dot(cond, w_ref[0].astype(BF16), preferred_element_type=F32) + b_ref[0]
    side_o_ref[...] = side_ref[...].astype(BF16)


def _ada_mod(c, w_ada, b_ada, side, side_lead):
    depth, d, n = w_ada.shape
    bsz = c.shape[0]
    bp = -(-bsz // SUBLANE) * SUBLANE
    c_p = jnp.pad(c, ((0, bp - bsz), (0, 0)))
    tn = 1024
    nj = n // tn
    side_in, side_out = _slab_specs(side, side_lead, 0, ROW_CHUNK, depth * nj, lambda l, j: l * nj + j)
    out, side_h = pl.pallas_call(
        _ada_kernel,
        grid=(depth, nj),
        in_specs=[pl.BlockSpec((bp, d), lambda l, j: (0, 0)),
                  pl.BlockSpec((1, d, tn), lambda l, j: (l, 0, j)),
                  pl.BlockSpec((1, 1, tn), lambda l, j: (l, 0, j)),
                  side_in],
        out_specs=(pl.BlockSpec((1, bp, tn), lambda l, j: (l, 0, j)), side_out),
        out_shape=(jax.ShapeDtypeStruct((depth, bp, n), F32),
                   jax.ShapeDtypeStruct(side.shape[1:], BF16)),
        compiler_params=_params(("arbitrary", "arbitrary")),
        name="ada_mod",
    )(c_p, w_ada, b_ada.reshape(depth, 1, n), side)
    return out[:, :bsz], side_h


def _norm_mm_kernel(*refs, has_tail, transposed):
    if has_tail:
        (x_ref, g_ref, sc_ref, sh_ref, w_ref, wt_ref, side_ref, o_ref, ot_ref, side_o_ref,
         h_scr, gm_scr) = refs
    else:
        x_ref, g_ref, sc_ref, sh_ref, w_ref, side_ref, o_ref, side_o_ref, h_scr, gm_scr = refs
    j = pl.program_id(2)

    @pl.when(j == 0)
    def _():
        _modulated_norm(x_ref, g_ref, sc_ref, sh_ref, gm_scr, h_scr)

    def project(wr, out_ref):
        w = wr[...].astype(BF16)
        if transposed:
            y = lax.dot_general(h_scr[...], w, NT_DIMS, preferred_element_type=F32)
        else:
            y = jnp.dot(h_scr[...], w, preferred_element_type=F32)
        out_ref[0] = y.astype(out_ref.dtype)

    project(w_ref, o_ref)
    if has_tail:
        pl.when(j == pl.num_programs(2) - 1)(lambda: project(wt_ref, ot_ref))
    side_o_ref[...] = side_ref[...].astype(BF16)


def _norm_mm(x, gain, mod3, sc_idx, sh_idx, w, out_dtype, side, side_lead, w_tail=None,
             transposed=False, tm=1024, tn=1024, name="norm_mm"):
    bsz, s, d = x.shape
    n_w = w.shape[0] if transposed else w.shape[1]
    nm, nj = s // tm, n_w // tn
    n = nj * tn
    side_in, side_out = _slab_specs(side, side_lead, 1, LANE, bsz * nm * nj,
                                    lambda b, m, j: (b * nm + m) * nj + j)
    if transposed:
        w_specs = [pl.BlockSpec((tn, d), lambda b, m, j: (j, 0))]
    else:
        w_specs = [pl.BlockSpec((d, tn), lambda b, m, j: (0, j))]
    w_args = [w]
    out_specs = [pl.BlockSpec((1, tm, tn), lambda b, m, j: (b, m, j))]
    out_shape = [jax.ShapeDtypeStruct((bsz, s, n), out_dtype)]
    if w_tail is not None:
        n_tail = w_tail.shape[0] if transposed else w_tail.shape[1]
        w_specs.append(pl.BlockSpec(w_tail.shape, lambda b, m, j: (0, 0), pipeline_mode=pl.Buffered(1)))
        w_args.append(w_tail)
        out_specs.append(pl.BlockSpec((1, tm, n_tail), lambda b, m, j: (b, m, 0)))
        out_shape.append(jax.ShapeDtypeStruct((bsz, s, n_tail), out_dtype))
    return pl.pallas_call(
        functools.partial(_norm_mm_kernel, has_tail=w_tail is not None, transposed=transposed),
        grid=(bsz, nm, nj),
        in_specs=[pl.BlockSpec((1, tm, d), lambda b, m, j: (b, m, 0)),
                  pl.BlockSpec((1, d), lambda b, m, j: (0, 0)),
                  pl.BlockSpec((1, 1, d), lambda b, m, j: (b, 0, sc_idx)),
                  pl.BlockSpec((1, 1, d), lambda b, m, j: (b, 0, sh_idx)),
                  *w_specs,
                  side_in],
        out_specs=(*out_specs, side_out),
        out_shape=(*out_shape, jax.ShapeDtypeStruct(side.shape[1:], BF16)),
        scratch_shapes=[pltpu.VMEM((tm, d), BF16), pltpu.VMEM((1, d), F32)],
        compiler_params=_params(("arbitrary", "arbitrary", "arbitrary")),
        name=name,
    )(x, gain.reshape(1, d), mod3, mod3, *w_args, side)


def _ffn_kernel(*refs, has_side):
    if has_side:
        (x_ref, gpre_ref, sc_ref, sh_ref, gt_ref, gpost_ref, wg_ref, wu_ref, wo_ref, side_ref,
         o_ref, side_o_ref, h_scr, gm_scr, acc_scr) = refs
        side_o_ref[...] = side_ref[...].astype(BF16)
    else:
        (x_ref, gpre_ref, sc_ref, sh_ref, gt_ref, gpost_ref, wg_ref, wu_ref, wo_ref,
         o_ref, h_scr, gm_scr, acc_scr) = refs
    f = pl.program_id(2)

    @pl.when(f == 0)
    def _():
        _modulated_norm(x_ref, gpre_ref, sc_ref, sh_ref, gm_scr, h_scr)
        acc_scr[...] = jnp.zeros_like(acc_scr)

    h = h_scr[...]
    g = jnp.dot(h, wg_ref[...], preferred_element_type=F32)
    u = jnp.dot(h, wu_ref[...], preferred_element_type=F32)
    a = (g * jax.nn.sigmoid(g) * u).astype(BF16)
    acc_scr[...] += jnp.dot(a, wo_ref[...], preferred_element_type=F32)

    @pl.when(f == pl.num_programs(2) - 1)
    def _():
        _post_norm_residual(acc_scr, x_ref, gt_ref, gpost_ref, gm_scr, o_ref)


def _ffn(x, gpre, gpost, mod3, w_in, w_out, side=None, side_lead=0, tm=1024, tf=512):
    bsz, s, d = x.shape
    fh = w_out.shape[0]
    nf = fh // tf
    nm = s // tm
    out_spec = pl.BlockSpec((1, tm, d), lambda b, m, f: (b, m, 0), pipeline_mode=pl.Buffered(1))
    out_shape = jax.ShapeDtypeStruct((bsz, s, d), F32)
    side_specs, side_args = [], []
    if side is not None:
        side_in, side_out = _slab_specs(side, side_lead, 1, LANE, bsz * nm * nf,
                                        lambda b, m, f: (b * nm + m) * nf + f)
        side_specs, side_args = [side_in], [side]
        out_spec = (out_spec, side_out)
        out_shape = (out_shape, jax.ShapeDtypeStruct(side.shape[1:], BF16))
    return pl.pallas_call(
        functools.partial(_ffn_kernel, has_side=side is not None),
        grid=(bsz, nm, nf),
        in_specs=[pl.BlockSpec((1, tm, d), lambda b, m, f: (b, m, 0), pipeline_mode=pl.Buffered(1)),
                  pl.BlockSpec((1, d), lambda b, m, f: (0, 0)),
                  pl.BlockSpec((1, 1, d), lambda b, m, f: (b, 0, 4)),
                  pl.BlockSpec((1, 1, d), lambda b, m, f: (b, 0, 3)),
                  pl.BlockSpec((1, 1, d), lambda b, m, f: (b, 0, 5)),
                  pl.BlockSpec((1, d), lambda b, m, f: (0, 0)),
                  pl.BlockSpec((d, tf), lambda b, m, f: (0, f)),
                  pl.BlockSpec((d, tf), lambda b, m, f: (0, nf + f)),
                  pl.BlockSpec((tf, d), lambda b, m, f: (f, 0)),
                  *side_specs],
        out_specs=out_spec,
        out_shape=out_shape,
        scratch_shapes=[pltpu.VMEM((tm, d), BF16), pltpu.VMEM((1, d), F32), pltpu.VMEM((tm, d), F32)],
        compiler_params=_params(("arbitrary", "arbitrary", "arbitrary")),
        name="ffn",
    )(x, gpre.reshape(1, d), mod3, mod3, mod3, gpost.reshape(1, d), w_in, w_in, w_out, *side_args)


def _out_proj_kernel(a0_ref, a1_ref, w0_ref, w1_ref, x_ref, gt_ref, gpost_ref, side_ref,
                     o_ref, side_o_ref, gm_scr, y_scr):
    side_o_ref[...] = side_ref[...].astype(BF16)
    y_scr[...] = (jnp.dot(a0_ref[0], w0_ref[...], preferred_element_type=F32)
                  + jnp.dot(a1_ref[0], w1_ref[...], preferred_element_type=F32))
    _post_norm_residual(y_scr, x_ref, gt_ref, gpost_ref, gm_scr, o_ref)


def _out_proj(a0, a1, col0, col1, w, x, mod3, gpost, side, side_lead, tm=512):
    bsz, s, d = x.shape
    kh = w.shape[0] // 2
    nm = s // tm
    side_in, side_out = _slab_specs(side, side_lead, 0, ROW_CHUNK, bsz * nm, lambda b, m: b * nm + m)
    return pl.pallas_call(
        _out_proj_kernel,
        grid=(bsz, nm),
        in_specs=[pl.BlockSpec((1, tm, kh), lambda b, m: (b, m, col0)),
                  pl.BlockSpec((1, tm, kh), lambda b, m: (b, m, col1)),
                  pl.BlockSpec((kh, d), lambda b, m: (0, 0)),
                  pl.BlockSpec((kh, d), lambda b, m: (1, 0)),
                  pl.BlockSpec((1, tm, d), lambda b, m: (b, m, 0)),
                  pl.BlockSpec((1, 1, d), lambda b, m: (b, 0, 2)),
                  pl.BlockSpec((1, d), lambda b, m: (0, 0)),
                  side_in],
        out_specs=(pl.BlockSpec((1, tm, d), lambda b, m: (b, m, 0)), side_out),
        out_shape=(jax.ShapeDtypeStruct((bsz, s, d), F32),
                   jax.ShapeDtypeStruct(side.shape[1:], BF16)),
        scratch_shapes=[pltpu.VMEM((1, d), F32), pltpu.VMEM((tm, d), F32)],
        compiler_params=_params(("arbitrary", "arbitrary")),
        name="out_proj",
    )(a0, a1, w, w, x, mod3, gpost.reshape(1, d), side)


def _mixer_a_kernel(z_ref, vg_ref, vb_ref, ws_ref, bst_ref, o_ref, wm_scr):
    ch = ws_ref.shape[1]

    @pl.when((pl.program_id(0) == 0) & (pl.program_id(1) == 0))
    def _():
        causal = (lax.broadcasted_iota(jnp.int32, (ch, ch), 0)
                  >= lax.broadcasted_iota(jnp.int32, (ch, ch), 1))
        for g in range(A_GROUPS):
            wm_scr[g] = jnp.where(causal, ws_ref[g], 0.0).astype(BF16)

    for c in range(z_ref.shape[1] // ch):
        rows = slice(c * ch, (c + 1) * ch)
        z = jax.nn.gelu(z_ref[0, rows, :].astype(F32))
        wdt = z.shape[1] // 2
        u = z[:, :wdt]
        v = z[:, wdt:]
        mu = jnp.mean(v, axis=-1, keepdims=True)
        dv = v - mu
        var = jnp.mean(dv * dv, axis=-1, keepdims=True)
        vn = (dv * lax.rsqrt(var + LN_EPS) * vg_ref[...] + vb_ref[...]).astype(BF16)
        gd = wdt // A_GROUPS
        for g in range(A_GROUPS):
            sv = jnp.dot(wm_scr[g], vn[:, g * gd:(g + 1) * gd], preferred_element_type=F32)
            sv = sv + bst_ref[:, g:g + 1]
            o_ref[0, rows, g * gd:(g + 1) * gd] = (u[:, g * gd:(g + 1) * gd] * sv).astype(o_ref.dtype)


def _mixer_a(z, v_gain, v_bias, w_s, b_s, width, chunks_per_step=4):
    bsz, s, _ = z.shape
    ch = A_CHUNK
    rows = ch * chunks_per_step
    return pl.pallas_call(
        _mixer_a_kernel,
        grid=(bsz, s // rows),
        in_specs=[pl.BlockSpec((1, rows, 2 * width), lambda b, c: (b, c, 0)),
                  pl.BlockSpec((1, width), lambda b, c: (0, 0)),
                  pl.BlockSpec((1, width), lambda b, c: (0, 0)),
                  pl.BlockSpec((A_GROUPS, ch, ch), lambda b, c: (0, 0, 0)),
                  pl.BlockSpec((ch, A_GROUPS), lambda b, c: (0, 0))],
        out_specs=pl.BlockSpec((1, rows, width), lambda b, c: (b, c, 0)),
        out_shape=jax.ShapeDtypeStruct((bsz, s, width), BF16),
        scratch_shapes=[pltpu.VMEM((A_GROUPS, ch, ch), BF16)],
        compiler_params=_params(("arbitrary", "arbitrary")),
        name="mixer_a",
    )(z, v_gain.reshape(1, width), v_bias.reshape(1, width), w_s, b_s.T)


_PV_MU_R, _PV_MU_K, _PV_MU_V, _PV_W0, _PV_A0, _PV_KK, _PV_KA, _PV_RK, _PV_LG, _PV_LB = range(10)
_PV_ROWS = 16


def _shift_lerp(x, prev_row, mu):
    rolled = pltpu.roll(x, 1, axis=0)
    first = lax.broadcasted_iota(jnp.int32, x.shape, 0) == 0
    xp = jnp.where(first, prev_row, rolled)
    return x + mu * (xp - x)


def _split_bf16(x):
    hi = x.astype(BF16)
    lo = (x - hi.astype(F32)).astype(BF16)
    return hi, lo


def _mm(x, y):
    return jnp.dot(x.astype(BF16), y.astype(BF16), preferred_element_type=F32)


def _mm_nt(x, y):
    return lax.dot_general(x.astype(BF16), y.astype(BF16), NT_DIMS, preferred_element_type=F32)


def _mm_tn(x, y):
    return lax.dot_general(x.astype(BF16), y.astype(BF16), TN_DIMS, preferred_element_type=F32)


def _mm_exact_rhs(x, e_bf16):
    xh, xl = _split_bf16(x)
    return (jnp.dot(xh, e_bf16, preferred_element_type=F32)
            + jnp.dot(xl, e_bf16, preferred_element_type=F32))


def _rwkv_kernel(zr_ref, zk_ref, zv_ref, zl_ref, pva_ref, pvb_ref, mul_ref, w2_ref, a2_ref, g2_ref,
                 side_ref, o_ref, side_o_ref,
                 s_scr, prev_scr, prevl_scr, th_hi_scr, th_lo_scr, xw_scr, sg_scr,
                 ar_scr, kbh_scr, vst_scr, rt_scr, kbar_scr, vb_scr, plp_scr,
                 q_scr, y_scr, gm_scr, cm_scr, pl_scr, bonus_scr, g_scr, *, chunk, n_t, n_p, n_tiles):
    i = pl.program_id(0)
    t_rows = zr_ref.shape[1]
    lanes = zr_ref.shape[2]
    lora = xw_scr.shape[1] + sg_scr.shape[1]
    L = chunk
    SL = RW_PACK * L
    n_chunks = t_rows // L
    side_o_ref[...] = side_ref[...].astype(BF16)

    i1 = jnp.minimum(i, n_tiles - 1)
    i3 = jnp.maximum(i - 2, 0)
    t1, p1 = (i1 // n_p) % n_t, i1 % n_p
    t3, p3 = (i3 // n_p) % n_t, i3 % n_p
    sa = i % 2
    sb = 1 - sa
    s13_w = i % 3
    s13_r = (i + 1) % 3

    @pl.when(i == 0)
    def _():
        for ref in (s_scr, prev_scr, prevl_scr, ar_scr, kbh_scr, vst_scr, rt_scr, kbar_scr, vb_scr,
                    plp_scr, q_scr, y_scr, gm_scr, cm_scr, pl_scr, bonus_scr, g_scr):
            ref[...] = jnp.zeros_like(ref)

    @pl.when(p1 == 0)
    def _():
        zl = zl_ref[0][:, :lora].astype(F32)
        prev = jnp.where(t1 == 0, 0.0, prevl_scr[0:1, :lora])
        zls = _shift_lerp(zl, prev, mul_ref[:, :lora])
        prevl_scr[0:1, :lora] = zl[t_rows - 1:t_rows, :]
        x_wa = zls[:, :LANE]
        th_hi, th_lo = _split_bf16(jnp.tanh(x_wa))
        th_hi_scr[...] = th_hi
        th_lo_scr[...] = th_lo
        xw_scr[...] = x_wa.astype(BF16)
        sg_scr[...] = jax.nn.sigmoid(zls[:, LANE:]).astype(BF16)

    def pva(r):
        return pva_ref[r:r + 1, :]

    li = lax.broadcasted_iota(jnp.int32, (lanes, lanes), 0) // RW_HEAD
    lj = lax.broadcasted_iota(jnp.int32, (lanes, lanes), 1) // RW_HEAD
    same_head = li == lj
    e_head = jnp.where(same_head, 1.0, 0.0).astype(BF16)
    ti = lax.broadcasted_iota(jnp.int32, (L, L), 0)
    tj = lax.broadcasted_iota(jnp.int32, (L, L), 1)
    tri = jnp.where(ti >= tj, 1.0, 0.0).astype(BF16)
    si = lax.broadcasted_iota(jnp.int32, (SL, SL), 0)
    sj = lax.broadcasted_iota(jnp.int32, (SL, SL), 1)
    same_blk = (si // L) == (sj // L)
    m_strict = same_blk & (si > sj)
    m_incl = same_blk & (si >= sj)
    eye = jnp.where(si == sj, 1.0, 0.0)
    lane_head = lax.broadcasted_iota(jnp.int32, (1, lanes), 1) // RW_HEAD
    first_half = lax.broadcasted_iota(jnp.int32, (1, 2 * L), 1) < L
    n_sq = max(L.bit_length() - 2, 0)
    cs = range(n_chunks)
    rows = [slice(c * L, (c + 1) * L) for c in cs]

    def stack(x):
        return jnp.concatenate([jnp.where(lane_head == h, x, 0.0) for h in range(RW_PACK)], axis=0)

    def unstack(x):
        out = x[0:L]
        for h in range(1, RW_PACK):
            out = out + x[h * L:(h + 1) * L]
        return out

    zr = zr_ref[0].astype(F32)
    zk = zk_ref[0].astype(F32)
    zv = zv_ref[0].astype(F32)
    prev = jnp.where(t1 == 0, 0.0, prev_scr[p1])
    r = _shift_lerp(zr, prev[0:1, :], pva(_PV_MU_R))
    k = _shift_lerp(zk, prev[1:2, :], pva(_PV_MU_K))
    v = _shift_lerp(zv, prev[2:3, :], pva(_PV_MU_V))
    prev_scr[p1] = jnp.concatenate([zr[t_rows - 1:t_rows, :], zk[t_rows - 1:t_rows, :],
                                    zv[t_rows - 1:t_rows, :], jnp.zeros((SUBLANE - 3, lanes), F32)],
                                   axis=0)
    prep = {"c": 0, "phase": 0}

    def prep_tile_a():
        w2_hi, w2_lo = _split_bf16(w2_ref[...])
        th_hi = th_hi_scr[...]
        w_pre = (pva(_PV_W0) + jnp.dot(th_hi, w2_hi, preferred_element_type=F32)
                 + jnp.dot(th_lo_scr[...], w2_hi, preferred_element_type=F32)
                 + jnp.dot(th_hi, w2_lo, preferred_element_type=F32))
        t = -w_pre
        softplus = jnp.maximum(t, 0.0) + jnp.log1p(jnp.exp(-jnp.abs(t)))
        prep["log_decay"] = -jnp.exp(-softplus - 0.5)
        prep["a"] = jax.nn.sigmoid(pva(_PV_A0) + _mm(xw_scr[...], a2_ref[...]))
        g_scr[s13_w] = _mm(sg_scr[...], g2_ref[...])

    def prep_tile_b():
        a = prep["a"]
        kk = k * pva(_PV_KK)
        kk = kk / jnp.maximum(jnp.sqrt(_mm(kk * kk, e_head)), 1e-12)
        kn = k * (1.0 + (a - 1.0) * pva(_PV_KA))
        prep["kk"], prep["kn"], prep["bv"] = kk, kn, kk * a
        bonus_scr[s13_w] = _mm(r * kn * pva(_PV_RK), e_head) * v
        lw_hi, lw_lo = _split_bf16(prep["log_decay"])
        prep["cm"] = [jnp.dot(tri, lw_hi[rw], preferred_element_type=F32)
                      + jnp.dot(tri, lw_lo[rw], preferred_element_type=F32) for rw in rows]

    def prep_chunk():
        c = prep["c"]
        if c >= n_chunks:
            return
        prep["c"] = c + 1
        rw = rows[c]
        cm = prep["cm"][c]
        cm_last = cm[L - 1:L, :]
        kc, bvc = prep["kn"][rw], prep["bv"][rw]
        r_t = r[rw] * jnp.exp(cm)
        a_st = stack(-prep["kk"][rw] * jnp.exp(cm - prep["log_decay"][rw])).astype(BF16)
        e_neg = jnp.exp(-cm)
        e_rem = jnp.exp(cm_last - cm)
        ar_scr[sa, c] = jnp.concatenate([a_st, stack(r_t).astype(BF16)], axis=0)
        kbh_scr[sa, c] = jnp.concatenate([(kc * e_neg).astype(BF16), (bvc * e_neg).astype(BF16)], axis=0)
        vst_scr[sa, c] = stack(v[rw]).astype(BF16)
        rt_scr[sa, c] = r_t
        kbar_scr[sa, c] = jnp.concatenate([(kc * e_rem).astype(BF16), (bvc * e_rem).astype(BF16)], axis=0)
        vb_scr[sa, c] = v[rw].astype(BF16)
        plp_scr[sa, c] = jnp.broadcast_to(jnp.exp(cm_last), (SUBLANE, lanes))

    def prep_step():
        phase = prep["phase"]
        prep["phase"] = phase + 1
        if phase == 0:
            prep_tile_a()
        elif phase == 1:
            prep_tile_b()
        else:
            for _ in range(-(-n_chunks // (n_sq + 1))):
                prep_chunk()

    chain = {"s": jnp.where(t3 == 0, 0.0, s_scr[p3]), "c": 0, "y": []}

    def chain_step():
        c = chain["c"]
        if c >= n_chunks:
            return
        s0 = chain["s"]
        s0b = s0.astype(BF16)
        chain["y"].append(y_scr[sa, c * L:(c + 1) * L, :] + _mm_nt(q_scr[sa, c], s0b))
        chain["s"] = s0 * pl_scr[sa, c][0:1, :] + _mm(s0b, gm_scr[sa, c]) + cm_scr[sa, c]
        chain["c"] = c + 1

    def fill():
        prep_step()
        chain_step()

    ar_st = [ar_scr[sb, c] for c in cs]
    kb_h = [kbh_scr[sb, c] for c in cs]
    v_st = [vst_scr[sb, c] for c in cs]
    prod = [_mm_nt(ar_st[c], kb_h[c]) for c in cs]
    fill()
    swapped = [pltpu.roll(x, L, axis=1) for x in prod]
    prod_k = [jnp.where(first_half, prod[c], swapped[c]) for c in cs]
    prod_b = [jnp.where(first_half, swapped[c], prod[c]) for c in cs]
    a_ak = [jnp.where(m_strict, x[:SL], 0.0).astype(BF16) for x in prod_k]
    a_rk = [jnp.where(m_incl, x[SL:], 0.0).astype(BF16) for x in prod_k]
    a_ab = [jnp.where(m_strict, x[:SL], 0.0) for x in prod_b]
    a_rb = [jnp.where(m_incl, x[SL:], 0.0).astype(BF16) for x in prod_b]

    xp = [x.astype(BF16) for x in a_ab]
    tinv = [eye + x for x in a_ab]
    for _ in range(n_sq):
        xp = [_mm(x, x).astype(BF16) for x in xp]
        tinv = [tinv[c] + _mm(tinv[c], xp[c]) for c in cs]
        fill()
    tinv = [x.astype(BF16) for x in tinv]

    x0 = [_mm(a_ak[c], v_st[c]).astype(BF16) for c in cs]
    fill()
    wu = [_mm(tinv[c], jnp.concatenate([ar_st[c][:SL], x0[c]], axis=1)) for c in cs]
    fill()
    yq = [_mm(a_rb[c], wu[c]) for c in cs]
    y0 = [_mm(a_rk[c], v_st[c]) for c in cs]
    while prep["c"] < n_chunks or chain["c"] < n_chunks:
        fill()
    cmats, gmats = [], []
    for c in cs:
        kbar = kbar_scr[sb, c]
        vu = jnp.concatenate([vb_scr[sb, c], unstack(wu[c][:, lanes:]).astype(BF16)], axis=0)
        cmats.append(jnp.where(same_head, _mm_tn(vu, kbar), 0.0))
        gmats.append(jnp.where(same_head, _mm_tn(unstack(wu[c][:, :lanes]), kbar[L:]), 0.0).astype(BF16))

    s_scr[p3] = chain["s"]
    y = jnp.concatenate(chain["y"], axis=0)
    inv_n = 1.0 / RW_HEAD
    mean = _mm_exact_rhs(y, e_head) * inv_n
    dy = y - mean
    var = _mm(dy * dy, e_head) * inv_n
    yn = dy * lax.rsqrt(var + GN_EPS) * pvb_ref[_PV_LG:_PV_LG + 1, :] + pvb_ref[_PV_LB:_PV_LB + 1, :]
    o_ref[0] = ((yn + bonus_scr[s13_r]) * g_scr[s13_r]).astype(o_ref.dtype)

    for c, rw in enumerate(rows):
        q_scr[sb, c] = (rt_scr[sb, c] + unstack(yq[c][:, :lanes])).astype(BF16)
        y_scr[sb, rw, :] = unstack(yq[c][:, lanes:] + y0[c])
        cm_scr[sb, c] = cmats[c]
        gm_scr[sb, c] = gmats[c]
        pl_scr[sb, c] = plp_scr[sb, c]


def _rwkv(z, z_lora, col_r, col_k, col_v, pvec, mu_l, w2p, a2p, g2p, width, side, side_lead,
          t_rows=1024):
    bsz, s, _ = z.shape
    lora_w = z_lora.shape[2]
    n_p = width // LANE
    L = RW_CHUNK
    n_chunks = t_rows // L
    n_t = s // t_rows
    n_tiles = bsz * n_t * n_p
    n_steps = n_tiles + 2
    assert RW_PACK * L == LANE and lora_w == 3 * LANE

    def tile(i, lag):
        it = jnp.clip(i - lag, 0, n_tiles - 1)
        return it // (n_t * n_p), (it // n_p) % n_t, it % n_p

    def z_spec(col):
        def idx(i):
            b, t, p = tile(i, 0)
            return b, t, col + p
        return pl.BlockSpec((1, t_rows, LANE), idx)

    def zl_idx(i):
        b, t, _ = tile(i, 0)
        return b, t, 0

    side_in, side_out = _slab_specs(side, side_lead, 0, ROW_CHUNK, n_steps, lambda i: i)
    kern = functools.partial(_rwkv_kernel, chunk=L, n_t=n_t, n_p=n_p, n_tiles=n_tiles)

    def per_chunk(rows_, dtype):
        return pltpu.VMEM((2, n_chunks, rows_, LANE), dtype)

    return pl.pallas_call(
        kern,
        grid=(n_steps,),
        in_specs=[z_spec(col_r), z_spec(col_k), z_spec(col_v),
                  pl.BlockSpec((1, t_rows, lora_w), zl_idx),
                  pl.BlockSpec((_PV_ROWS, LANE), lambda i: (0, tile(i, 0)[2])),
                  pl.BlockSpec((_PV_ROWS, LANE), lambda i: (0, tile(i, 2)[2])),
                  pl.BlockSpec((1, lora_w), lambda i: (0, 0)),
                  pl.BlockSpec((LANE, LANE), lambda i: (0, tile(i, 0)[2])),
                  pl.BlockSpec((LANE, LANE), lambda i: (0, tile(i, 0)[2])),
                  pl.BlockSpec((2 * LANE, LANE), lambda i: (0, tile(i, 0)[2])),
                  side_in],
        out_specs=(pl.BlockSpec((1, t_rows, LANE), lambda i: tile(i, 2)), side_out),
        out_shape=(jax.ShapeDtypeStruct((bsz, s, width), BF16),
                   jax.ShapeDtypeStruct(side.shape[1:], BF16)),
        scratch_shapes=[pltpu.VMEM((n_p, LANE, LANE), F32),
                        pltpu.VMEM((n_p, SUBLANE, LANE), F32),
                        pltpu.VMEM((SUBLANE, lora_w), F32),
                        pltpu.VMEM((t_rows, LANE), BF16),
                        pltpu.VMEM((t_rows, LANE), BF16),
                        pltpu.VMEM((t_rows, LANE), BF16),
                        pltpu.VMEM((t_rows, lora_w - LANE), BF16),
                        per_chunk(4 * L, BF16),
                        per_chunk(2 * L, BF16),
                        per_chunk(2 * L, BF16),
                        per_chunk(L, F32),
                        per_chunk(2 * L, BF16),
                        per_chunk(L, BF16),
                        per_chunk(SUBLANE, F32),
                        per_chunk(L, BF16),
                        pltpu.VMEM((2, t_rows, LANE), F32),
                        per_chunk(LANE, BF16),
                        per_chunk(LANE, F32),
                        per_chunk(SUBLANE, F32),
                        pltpu.VMEM((3, t_rows, LANE), F32),
                        pltpu.VMEM((3, t_rows, LANE), F32)],
        compiler_params=_params(("arbitrary",)),
        name="rwkv7",
    )(z, z, z, z_lora, pvec, pvec, mu_l, w2p, a2p, g2p, side)


def _moba_kernel(q_ref, k_ref, v_ref, side_ref, o_ref, side_o_ref, ka_scr, vt_scr, s_scr, p_scr, *,
                 n_heads, heads_per_step):
    side_o_ref[...] = side_ref[...].astype(BF16)
    heads = [_moba_head(q_ref, k_ref, v_ref, o_ref, ka_scr.at[j], vt_scr.at[j], s_scr.at[j],
                        p_scr.at[j], j, pl.program_id(1) * heads_per_step + j, n_heads, heads_per_step)
             for j in range(heads_per_step)]
    nb = q_ref.shape[1] // MOBA_BLOCK
    pending = [scores(0) for scores, _ in heads]
    for qb in range(nb):
        nxt = [scores(qb + 1) if qb + 1 < nb else None for scores, _ in heads]
        for (_, attend), pend in zip(heads, pending):
            attend(qb, *pend)
        pending = nxt


def _moba_head(q_ref, k_ref, v_ref, o_ref, ka_scr, vt_scr, s_scr, p_scr, j, h, n_heads, heads_per_step):
    s_len = q_ref.shape[1]
    dh = q_ref.shape[2] // heads_per_step
    cols = slice(j * dh, (j + 1) * dh)
    blk = MOBA_BLOCK
    nb = s_len // blk
    log2e = 1.4426950408889634
    scale = dh ** -0.5 * log2e
    neg_inf = -jnp.inf

    def slope_row(width):
        return log2e * jnp.exp(jnp.full((1, width), -8.0 / n_heads * 0.6931471805599453, F32)
                               * (h + 1).astype(F32))

    lane = lax.broadcasted_iota(jnp.int32, (blk, dh), 1)
    bias = slope_row(dh) * lax.broadcasted_iota(jnp.int32, (blk, dh), 0).astype(F32)
    extra = jnp.zeros((blk, dh), F32)
    for col in range(3):
        part = bias.astype(BF16).astype(F32)
        extra = jnp.where(lane == col, part, extra)
        bias = bias - part
    extra = extra.astype(BF16)
    ones_cols = jnp.where(lane < 3, 1.0, 0.0).astype(BF16)

    kmean = []
    for j in range(nb):
        rows = slice(j * blk, (j + 1) * blk)
        k_j = k_ref[0, rows, cols]
        kmean.append(jnp.mean(k_j.astype(F32), axis=0, keepdims=True))
        ka_scr[rows, :dh] = k_j
        ka_scr[rows, dh:] = extra
        vt_scr[:dh, rows] = v_ref[0, rows, cols].astype(F32).T.astype(BF16)
    sub = lax.broadcasted_iota(jnp.int32, (vt_scr.shape[0] - dh, s_len), 0)
    vt_scr[dh:, :] = jnp.where(sub == 0, 1.0, 0.0).astype(BF16)
    kmean = jnp.concatenate(kmean, axis=0)
    kmean_parts = []
    for _ in range(3):
        part = kmean.astype(BF16)
        kmean_parts.append(part)
        kmean = kmean - part.astype(F32)

    slope = slope_row(blk)
    causal = (lax.broadcasted_iota(jnp.int32, (blk, blk), 1)
              >= lax.broadcasted_iota(jnp.int32, (blk, blk), 0))
    blk_id = lax.broadcasted_iota(jnp.int32, (nb, 1), 0)

    def scores(qb):
        q = q_ref[0, qb * blk:(qb + 1) * blk, cols]
        q_aug = jnp.concatenate([(q.astype(F32) * scale).astype(BF16), ones_cols], axis=1)
        gate = sum(lax.dot_general(part, q, NT_DIMS, preferred_element_type=F32)
                   for part in kmean_parts)
        past = blk_id < qb
        offs = []
        m = None
        for n in range(qb + 1):
            t = lax.dot_general(ka_scr[n * blk:(n + 1) * blk, :], q_aug, NT_DIMS,
                                preferred_element_type=F32)
            if n == qb:
                t = jnp.where(causal, t, neg_inf)
                off = jnp.zeros((1, blk), F32)
            else:
                g_n = gate[n:n + 1, :]
                beats = past & ((gate > g_n) | ((gate == g_n) & (blk_id < n)))
                rank = jnp.sum(jnp.where(beats, 1.0, 0.0), axis=0, keepdims=True)
                off = jnp.where(rank < float(MOBA_TOPK), slope * float((n - qb) * blk), neg_inf)
            s_scr[qb % 2, n] = t
            offs.append(off)
            cmax = jnp.max(t, axis=0, keepdims=True) + off
            m = cmax if m is None else jnp.maximum(m, cmax)
        return m, offs

    def attend(qb, m, offs):
        for n in range(qb + 1):
            p = jnp.exp2(s_scr[qb % 2, n] - (m - offs[n]))
            p_scr[qb % 2, n * blk:(n + 1) * blk, :] = p.astype(BF16)
        kk = (qb + 1) * blk
        acc = jnp.dot(vt_scr[:, :kk], p_scr[qb % 2, :kk, :], preferred_element_type=F32)
        o_ref[0, qb * blk:(qb + 1) * blk, cols] = (acc[:dh] / acc[dh:dh + 1]).T.astype(o_ref.dtype)

    return scores, attend


def _moba(qkv, n_heads, side, side_lead, heads_per_step=2):
    bsz, s, d3 = qkv.shape
    d = d3 // 3
    dh = d // n_heads
    blk = MOBA_BLOCK
    nb = s // blk
    n_hp = n_heads // heads_per_step
    wdt = heads_per_step * dh
    side_in, side_out = _slab_specs(side, side_lead, 0, ROW_CHUNK, bsz * n_hp,
                                    lambda b, h: b * n_hp + h)
    kern = functools.partial(_moba_kernel, n_heads=n_heads, heads_per_step=heads_per_step)
    return pl.pallas_call(
        kern,
        grid=(bsz, n_hp),
        in_specs=[pl.BlockSpec((1, s, wdt), lambda b, h: (b, 0, h)),
                  pl.BlockSpec((1, s, wdt), lambda b, h: (b, 0, n_hp + h)),
                  pl.BlockSpec((1, s, wdt), lambda b, h: (b, 0, 2 * n_hp + h)),
                  side_in],
        out_specs=(pl.BlockSpec((1, s, wdt), lambda b, h: (b, 0, h)), side_out),
        out_shape=(jax.ShapeDtypeStruct((bsz, s, d), BF16),
                   jax.ShapeDtypeStruct(side.shape[1:], BF16)),
        scratch_shapes=[pltpu.VMEM((heads_per_step, s, 2 * dh), BF16),
                        pltpu.VMEM((heads_per_step, dh + 16, s), BF16),
                        pltpu.VMEM((heads_per_step, 2, nb, blk, blk), F32),
                        pltpu.VMEM((heads_per_step, 2, s, blk), BF16)],
        compiler_params=_params(("arbitrary", "arbitrary")),
        name="moba",
    )(qkv, qkv, qkv, side)


def _pad_cols(w, n):
    return jnp.pad(w, ((0, 0), (0, n - w.shape[1])))


def _pad_rows(w, n, before=0):
    return jnp.pad(w, ((before, n - before - w.shape[0]), (0, 0)))


def kernel(x, c, w_ada, b_ada, g_pre_mix, g_post_mix, g_pre_ffn, g_post_ffn, w_ffn_in, w_ffn_out,
           w_in_ab, w_out_ab, a_v_gain, a_v_bias, a_w_s, a_b_s, b_mu, b_w0, b_w2, b_a0, b_a2, b_g2,
           b_k_k, b_k_a, b_r_k, b_lnx_gain, b_lnx_bias, w_qkv, w_o):
    bsz, s, d = x.shape
    depth = w_ada.shape[0]
    a_width = a_v_gain.shape[1]
    b_width = b_w0.shape[1]
    n_lw = b_w2.shape[1]
    n_la = b_a2.shape[1]
    n_lg = b_g2.shape[1]
    n_heads = d // ATT_HEAD
    assert s % MOBA_BLOCK == 0 and s % 1024 == 0
    assert n_lw + n_la <= LANE and n_lg <= 2 * LANE

    w_in_t = jnp.swapaxes(w_in_ab, 1, 2)
    mod, w_in_h0 = _ada_mod(c, w_ada, b_ada, w_in_t, 0)

    w_qkv_h = None
    for layer in range(depth):
        mod3 = mod[layer].reshape(bsz, 1, 6 * d)
        i = layer // 2
        if layer % 2 == 0:
            lora_w = 3 * LANE
            n_main = 2 * a_width + 3 * b_width
            w_t = w_in_h0 if i == 0 else w_in_t[i]
            w_tail = _pad_rows(w_t[n_main:], lora_w)
            z, z_lora, w_out_h = _norm_mm(x, g_pre_mix[layer], mod3, 1, 0, w_t, BF16, w_out_ab, i,
                                          w_tail=w_tail, transposed=True, tn=n_main // 4,
                                          name="in_proj_ab")
            y_a = _mixer_a(z, a_v_gain[i], a_v_bias[i], a_w_s[i], a_b_s[i], a_width)

            mu = b_mu[i]
            pvec = jnp.stack([mu[0:b_width], mu[b_width:2 * b_width], mu[2 * b_width:3 * b_width],
                              b_w0[i], b_a0[i], b_k_k[i], b_k_a[i], b_r_k[i].reshape(-1),
                              b_lnx_gain[i], b_lnx_bias[i]])
            pvec = _pad_rows(pvec, _PV_ROWS)
            mu_l = _pad_cols(mu[3 * b_width:].reshape(1, -1), lora_w)
            w2p = _pad_rows(b_w2[i], LANE)
            a2p = _pad_rows(b_a2[i], LANE, before=n_lw)
            g2p = _pad_rows(b_g2[i], 2 * LANE)
            cb = 2 * a_width // LANE
            nb_w = b_width // LANE
            y_b, w_ffn_in_h = _rwkv(z, z_lora, cb, cb + nb_w, cb + 2 * nb_w, pvec, mu_l, w2p, a2p,
                                    g2p, b_width, w_ffn_in, layer)
            x, w_ffn_out_h = _out_proj(y_a, y_b, 0, 0, w_out_h, x, mod3, g_post_mix[layer],
                                       w_ffn_out, layer)
        else:
            w_q = w_qkv[i] if w_qkv_h is None else w_qkv_h
            qkv, w_o_h = _norm_mm(x, g_pre_mix[layer], mod3, 1, 0, w_q, BF16, w_o, i,
                                  tn=d, name="qkv_proj")
            o, w_ffn_in_h = _moba(qkv, n_heads, w_ffn_in, layer)
            x, w_ffn_out_h = _out_proj(o, o, 0, 1, w_o_h, x, mod3, g_post_mix[layer],
                                       w_ffn_out, layer)
        if layer + 1 < depth and (layer + 1) % 2 == 1:
            x, w_qkv_h = _ffn(x, g_pre_ffn[layer], g_post_ffn[layer], mod3, w_ffn_in_h, w_ffn_out_h,
                              w_qkv, (layer + 1) // 2)
        else:
            x = _ffn(x, g_pre_ffn[layer], g_post_ffn[layer], mod3, w_ffn_in_h, w_ffn_out_h)
            w_qkv_h = None
    return x
```

```python
import functools

import jax
import jax.numpy as jnp
from jax import lax
from jax.experimental import pallas as pl
from jax.experimental.pallas import tpu as pltpu

F32 = jnp.float32
BF16 = jnp.bfloat16

NORM_EPS = 1e-6
LN_EPS = 1e-5
GN_EPS = 64e-5

LANE = 128
SUBLANE = 8
A_GROUPS = 8
A_CHUNK = 128
RW_HEAD = 64
RW_CHUNK = 64
RW_PACK = 2
MOBA_BLOCK = 256
MOBA_TOPK = 3
ATT_HEAD = 128

NT_DIMS = (((1,), (1,)), ((), ()))
TN_DIMS = (((0,), (0,)), ((), ()))

VMEM_LIMIT = 56 * 1024 * 1024


def _params(sem):
    return pltpu.CompilerParams(dimension_semantics=sem, vmem_limit_bytes=VMEM_LIMIT)


def _slab_specs(side, lead, axis, unit, n_steps, flat_index):
    total = side.shape[1 + axis]
    n_slab = max(n for n in range(1, n_steps + 1) if total % (n * unit) == 0)
    shape = list(side.shape[1:])
    shape[axis] = total // n_slab

    def idx(*grid):
        slab = jnp.minimum(flat_index(*grid), n_slab - 1)
        return (slab, 0) if axis == 0 else (0, slab)

    return (pl.BlockSpec((None,) + tuple(shape), lambda *g: (lead,) + idx(*g)),
            pl.BlockSpec(tuple(shape), idx))


ROW_CHUNK = 16
ROW_UNROLL = 8

def _for_row_chunks(n_rows, fn):
    def body(i, carry):
        fn(pl.ds(pl.multiple_of(i * ROW_CHUNK, ROW_CHUNK), ROW_CHUNK))
        return carry
    lax.fori_loop(0, n_rows // ROW_CHUNK, body, 0, unroll=ROW_UNROLL)


def _modulated_norm(x_ref, g_ref, sc_ref, sh_ref, gm_scr, h_scr):
    gm_scr[...] = g_ref[...] * (1.0 + sc_ref[0])

    def rows(r):
        x = x_ref[0, r, :]
        ms = jnp.mean(x * x, axis=-1, keepdims=True)
        h_scr[r, :] = (x * lax.rsqrt(ms + NORM_EPS) * gm_scr[...] + sh_ref[0]).astype(BF16)

    _for_row_chunks(h_scr.shape[0], rows)


def _post_norm_residual(y_scr, x_ref, gt_ref, gpost_ref, gm_scr, o_ref):
    gm_scr[...] = gt_ref[0] * gpost_ref[...]

    def rows(r):
        y = y_scr[r, :]
        ms = jnp.mean(y * y, axis=-1, keepdims=True)
        o_ref[0, r, :] = x_ref[0, r, :] + y * lax.rsqrt(ms + NORM_EPS) * gm_scr[...]

    _for_row_chunks(o_ref.shape[1], rows)


def _ada_kernel(c_ref, w_ref, b_ref, side_ref, o_ref, side_o_ref):
    c = c_ref[...]
    cond = (c * jax.nn.sigmoid(c)).astype(BF16)
    o_ref[0] = jnp.dot(cond, w_ref[0].astype(BF16), preferred_element_type=F32) + b_ref[0]
    side_o_ref[...] = side_ref[...].astype(BF16)


def _ada_mod(c, w_ada, b_ada, side, side_lead):
    depth, d, n = w_ada.shape
    bsz = c.shape[0]
    bp = -(-bsz // SUBLANE) * SUBLANE
    c_p = jnp.pad(c, ((0, bp - bsz), (0, 0)))
    tn = 1024
    nj = n // tn
    side_in, side_out = _slab_specs(side, side_lead, 0, ROW_CHUNK, depth * nj, lambda l, j: l * nj + j)
    out, side_h = pl.pallas_call(
        _ada_kernel,
        grid=(depth, nj),
        in_specs=[pl.BlockSpec((bp, d), lambda l, j: (0, 0)),
                  pl.BlockSpec((1, d, tn), lambda l, j: (l, 0, j)),
                  pl.BlockSpec((1, 1, tn), lambda l, j: (l, 0, j)),
                  side_in],
        out_specs=(pl.BlockSpec((1, bp, tn), lambda l, j: (l, 0, j)), side_out),
        out_shape=(jax.ShapeDtypeStruct((depth, bp, n), F32),
                   jax.ShapeDtypeStruct(side.shape[1:], BF16)),
        compiler_params=_params(("arbitrary", "arbitrary")),
        name="ada_mod",
    )(c_p, w_ada, b_ada.reshape(depth, 1, n), side)
    return out[:, :bsz], side_h


def _norm_mm_kernel(*refs, has_tail, transposed):
    if has_tail:
        (x_ref, g_ref, sc_ref, sh_ref, w_ref, wt_ref, side_ref, o_ref, ot_ref, side_o_ref,
         h_scr, gm_scr) = refs
    else:
        x_ref, g_ref, sc_ref, sh_ref, w_ref, side_ref, o_ref, side_o_ref, h_scr, gm_scr = refs
    j = pl.program_id(2)

    @pl.when(j == 0)
    def _():
        _modulated_norm(x_ref, g_ref, sc_ref, sh_ref, gm_scr, h_scr)

    def project(wr, out_ref):
        w = wr[...].astype(BF16)
        if transposed:
            y = lax.dot_general(h_scr[...], w, NT_DIMS, preferred_element_type=F32)
        else:
            y = jnp.dot(h_scr[...], w, preferred_element_type=F32)
        out_ref[0] = y.astype(out_ref.dtype)

    project(w_ref, o_ref)
    if has_tail:
        pl.when(j == pl.num_programs(2) - 1)(lambda: project(wt_ref, ot_ref))
    side_o_ref[...] = side_ref[...].astype(BF16)


def _norm_mm(x, gain, mod3, sc_idx, sh_idx, w, out_dtype, side, side_lead, w_tail=None,
             transposed=False, tm=1024, tn=1024, name="norm_mm"):
    bsz, s, d = x.shape
    n_w = w.shape[0] if transposed else w.shape[1]
    nm, nj = s // tm, n_w // tn
    n = nj * tn
    side_in, side_out = _slab_specs(side, side_lead, 1, LANE, bsz * nm * nj,
                                    lambda b, m, j: (b * nm + m) * nj + j)
    if transposed:
        w_specs = [pl.BlockSpec((tn, d), lambda b, m, j: (j, 0))]
    else:
        w_specs = [pl.BlockSpec((d, tn), lambda b, m, j: (0, j))]
    w_args = [w]
    out_specs = [pl.BlockSpec((1, tm, tn), lambda b, m, j: (b, m, j))]
    out_shape = [jax.ShapeDtypeStruct((bsz, s, n), out_dtype)]
    if w_tail is not None:
        n_tail = w_tail.shape[0] if transposed else w_tail.shape[1]
        w_specs.append(pl.BlockSpec(w_tail.shape, lambda b, m, j: (0, 0), pipeline_mode=pl.Buffered(1)))
        w_args.append(w_tail)
        out_specs.append(pl.BlockSpec((1, tm, n_tail), lambda b, m, j: (b, m, 0)))
        out_shape.append(jax.ShapeDtypeStruct((bsz, s, n_tail), out_dtype))
    return pl.pallas_call(
        functools.partial(_norm_mm_kernel, has_tail=w_tail is not None, transposed=transposed),
        grid=(bsz, nm, nj),
        in_specs=[pl.BlockSpec((1, tm, d), lambda b, m, j: (b, m, 0)),
                  pl.BlockSpec((1, d), lambda b, m, j: (0, 0)),
                  pl.BlockSpec((1, 1, d), lambda b, m, j: (b, 0, sc_idx)),
                  pl.BlockSpec((1, 1, d), lambda b, m, j: (b, 0, sh_idx)),
                  *w_specs,
                  side_in],
        out_specs=(*out_specs, side_out),
        out_shape=(*out_shape, jax.ShapeDtypeStruct(side.shape[1:], BF16)),
        scratch_shapes=[pltpu.VMEM((tm, d), BF16), pltpu.VMEM((1, d), F32)],
        compiler_params=_params(("arbitrary", "arbitrary", "arbitrary")),
        name=name,
    )(x, gain.reshape(1, d), mod3, mod3, *w_args, side)


def _ffn_kernel(*refs, has_side):
    if has_side:
        (x_ref, gpre_ref, sc_ref, sh_ref, gt_ref, gpost_ref, wg_ref, wu_ref, wo_ref, side_ref,
         o_ref, side_o_ref, h_scr, gm_scr, acc_scr) = refs
        side_o_ref[...] = side_ref[...].astype(BF16)
    else:
        (x_ref, gpre_ref, sc_ref, sh_ref, gt_ref, gpost_ref, wg_ref, wu_ref, wo_ref,
         o_ref, h_scr, gm_scr, acc_scr) = refs
    f = pl.program_id(2)

    @pl.when((pl.program_id(0) == 0) & (pl.program_id(1) == 0) & (f == 0))
    def _():
        acc_scr[...] = jnp.zeros_like(acc_scr)

    @pl.when(f == 0)
    def _():
        _modulated_norm(x_ref, gpre_ref, sc_ref, sh_ref, gm_scr, h_scr)

    h = h_scr[...]
    g = jnp.dot(h, wg_ref[...], preferred_element_type=F32)
    u = jnp.dot(h, wu_ref[...], preferred_element_type=F32)
    a = (g * jax.nn.sigmoid(g) * u).astype(BF16)
    acc_scr[...] = (jnp.where(f == 0, 0.0, acc_scr[...])
                    + jnp.dot(a, wo_ref[...], preferred_element_type=F32))

    @pl.when(f == pl.num_programs(2) - 1)
    def _():
        _post_norm_residual(acc_scr, x_ref, gt_ref, gpost_ref, gm_scr, o_ref)


def _ffn(x, gpre, gpost, mod3, w_in, w_out, side=None, side_lead=0, tm=1024, tf=512):
    bsz, s, d = x.shape
    fh = w_out.shape[0]
    nf = fh // tf
    nm = s // tm
    out_spec = pl.BlockSpec((1, tm, d), lambda b, m, f: (b, m, 0), pipeline_mode=pl.Buffered(1))
    out_shape = jax.ShapeDtypeStruct((bsz, s, d), F32)
    side_specs, side_args = [], []
    if side is not None:
        side_in, side_out = _slab_specs(side, side_lead, 1, LANE, bsz * nm * nf,
                                        lambda b, m, f: (b * nm + m) * nf + f)
        side_specs, side_args = [side_in], [side]
        out_spec = (out_spec, side_out)
        out_shape = (out_shape, jax.ShapeDtypeStruct(side.shape[1:], BF16))
    return pl.pallas_call(
        functools.partial(_ffn_kernel, has_side=side is not None),
        grid=(bsz, nm, nf),
        in_specs=[pl.BlockSpec((1, tm, d), lambda b, m, f: (b, m, 0), pipeline_mode=pl.Buffered(1)),
                  pl.BlockSpec((1, d), lambda b, m, f: (0, 0)),
                  pl.BlockSpec((1, 1, d), lambda b, m, f: (b, 0, 4)),
                  pl.BlockSpec((1, 1, d), lambda b, m, f: (b, 0, 3)),
                  pl.BlockSpec((1, 1, d), lambda b, m, f: (b, 0, 5)),
                  pl.BlockSpec((1, d), lambda b, m, f: (0, 0)),
                  pl.BlockSpec((d, tf), lambda b, m, f: (0, f)),
                  pl.BlockSpec((d, tf), lambda b, m, f: (0, nf + f)),
                  pl.BlockSpec((tf, d), lambda b, m, f: (f, 0)),
                  *side_specs],
        out_specs=out_spec,
        out_shape=out_shape,
        scratch_shapes=[pltpu.VMEM((tm, d), BF16), pltpu.VMEM((1, d), F32), pltpu.VMEM((tm, d), F32)],
        compiler_params=_params(("arbitrary", "arbitrary", "arbitrary")),
        name="ffn",
    )(x, gpre.reshape(1, d), mod3, mod3, mod3, gpost.reshape(1, d), w_in, w_in, w_out, *side_args)


def _out_proj_kernel(a0_ref, a1_ref, w0_ref, w1_ref, x_ref, gt_ref, gpost_ref, side_ref,
                     o_ref, side_o_ref, gm_scr, y_scr):
    side_o_ref[...] = side_ref[...].astype(BF16)
    y_scr[...] = (jnp.dot(a0_ref[0], w0_ref[...], preferred_element_type=F32)
                  + jnp.dot(a1_ref[0], w1_ref[...], preferred_element_type=F32))
    _post_norm_residual(y_scr, x_ref, gt_ref, gpost_ref, gm_scr, o_ref)


def _out_proj(a0, a1, col0, col1, w, x, mod3, gpost, side, side_lead, tm=512):
    bsz, s, d = x.shape
    kh = w.shape[0] // 2
    nm = s // tm
    side_in, side_out = _slab_specs(side, side_lead, 0, ROW_CHUNK, bsz * nm, lambda b, m: b * nm + m)
    return pl.pallas_call(
        _out_proj_kernel,
        grid=(bsz, nm),
        in_specs=[pl.BlockSpec((1, tm, kh), lambda b, m: (b, m, col0)),
                  pl.BlockSpec((1, tm, kh), lambda b, m: (b, m, col1)),
                  pl.BlockSpec((kh, d), lambda b, m: (0, 0)),
                  pl.BlockSpec((kh, d), lambda b, m: (1, 0)),
                  pl.BlockSpec((1, tm, d), lambda b, m: (b, m, 0)),
                  pl.BlockSpec((1, 1, d), lambda b, m: (b, 0, 2)),
                  pl.BlockSpec((1, d), lambda b, m: (0, 0)),
                  side_in],
        out_specs=(pl.BlockSpec((1, tm, d), lambda b, m: (b, m, 0)), side_out),
        out_shape=(jax.ShapeDtypeStruct((bsz, s, d), F32),
                   jax.ShapeDtypeStruct(side.shape[1:], BF16)),
        scratch_shapes=[pltpu.VMEM((1, d), F32), pltpu.VMEM((tm, d), F32)],
        compiler_params=_params(("arbitrary", "arbitrary")),
        name="out_proj",
    )(a0, a1, w, w, x, mod3, gpost.reshape(1, d), side)


def _mixer_a_kernel(z_ref, vg_ref, vb_ref, ws_ref, bst_ref, o_ref, wm_scr):
    ch = ws_ref.shape[1]

    @pl.when((pl.program_id(0) == 0) & (pl.program_id(1) == 0))
    def _():
        causal = (lax.broadcasted_iota(jnp.int32, (ch, ch), 0)
                  >= lax.broadcasted_iota(jnp.int32, (ch, ch), 1))
        for g in range(A_GROUPS):
            wm_scr[g] = jnp.where(causal, ws_ref[g], 0.0).astype(BF16)

    for c in range(z_ref.shape[1] // ch):
        rows = slice(c * ch, (c + 1) * ch)
        z = jax.nn.gelu(z_ref[0, rows, :].astype(F32))
        wdt = z.shape[1] // 2
        u = z[:, :wdt]
        v = z[:, wdt:]
        mu = jnp.mean(v, axis=-1, keepdims=True)
        dv = v - mu
        var = jnp.mean(dv * dv, axis=-1, keepdims=True)
        vn = (dv * lax.rsqrt(var + LN_EPS) * vg_ref[...] + vb_ref[...]).astype(BF16)
        gd = wdt // A_GROUPS
        for g in range(A_GROUPS):
            sv = jnp.dot(wm_scr[g], vn[:, g * gd:(g + 1) * gd], preferred_element_type=F32)
            sv = sv + bst_ref[:, g:g + 1]
            o_ref[0, rows, g * gd:(g + 1) * gd] = (u[:, g * gd:(g + 1) * gd] * sv).astype(o_ref.dtype)


def _mixer_a(z, v_gain, v_bias, w_s, b_s, width, chunks_per_step=4):
    bsz, s, _ = z.shape
    ch = A_CHUNK
    rows = ch * chunks_per_step
    return pl.pallas_call(
        _mixer_a_kernel,
        grid=(bsz, s // rows),
        in_specs=[pl.BlockSpec((1, rows, 2 * width), lambda b, c: (b, c, 0)),
                  pl.BlockSpec((1, width), lambda b, c: (0, 0)),
                  pl.BlockSpec((1, width), lambda b, c: (0, 0)),
                  pl.BlockSpec((A_GROUPS, ch, ch), lambda b, c: (0, 0, 0)),
                  pl.BlockSpec((ch, A_GROUPS), lambda b, c: (0, 0))],
        out_specs=pl.BlockSpec((1, rows, width), lambda b, c: (b, c, 0)),
        out_shape=jax.ShapeDtypeStruct((bsz, s, width), BF16),
        scratch_shapes=[pltpu.VMEM((A_GROUPS, ch, ch), BF16)],
        compiler_params=_params(("arbitrary", "arbitrary")),
        name="mixer_a",
    )(z, v_gain.reshape(1, width), v_bias.reshape(1, width), w_s, b_s.T)


_PV_MU_R, _PV_MU_K, _PV_MU_V, _PV_W0, _PV_A0, _PV_KK, _PV_KA, _PV_RK, _PV_LG, _PV_LB = range(10)
_PV_ROWS = 16


def _shift_lerp(x, prev_row, mu):
    rolled = pltpu.roll(x, 1, axis=0)
    first = lax.broadcasted_iota(jnp.int32, x.shape, 0) == 0
    xp = jnp.where(first, prev_row, rolled)
    return x + mu * (xp - x)


def _split_bf16(x):
    hi = x.astype(BF16)
    lo = (x - hi.astype(F32)).astype(BF16)
    return hi, lo


def _mm(x, y):
    return jnp.dot(x.astype(BF16), y.astype(BF16), preferred_element_type=F32)


def _mm_nt(x, y):
    return lax.dot_general(x.astype(BF16), y.astype(BF16), NT_DIMS, preferred_element_type=F32)


def _mm_tn(x, y):
    return lax.dot_general(x.astype(BF16), y.astype(BF16), TN_DIMS, preferred_element_type=F32)


def _mm_exact_rhs(x, e_bf16):
    xh, xl = _split_bf16(x)
    return (jnp.dot(xh, e_bf16, preferred_element_type=F32)
            + jnp.dot(xl, e_bf16, preferred_element_type=F32))


def _rwkv_kernel(zr_ref, zk_ref, zv_ref, zl_ref, pva_ref, pvb_ref, mul_ref, w2_ref, a2_ref, g2_ref,
                 side_ref, o_ref, side_o_ref,
                 s_scr, prev_scr, prevl_scr, th_hi_scr, th_lo_scr, xw_scr, sg_scr,
                 ar_scr, kbh_scr, vst_scr, rt_scr, kbar_scr, vb_scr, plp_scr,
                 q_scr, y_scr, gm_scr, cm_scr, pl_scr, bonus_scr, g_scr, *, chunk, n_t, n_p, n_tiles):
    i = pl.program_id(0)
    t_rows = zr_ref.shape[1]
    lanes = zr_ref.shape[2]
    lora = xw_scr.shape[1] + sg_scr.shape[1]
    L = chunk
    SL = RW_PACK * L
    n_chunks = t_rows // L
    side_o_ref[...] = side_ref[...].astype(BF16)

    i1 = jnp.minimum(i, n_tiles - 1)
    i3 = jnp.maximum(i - 2, 0)
    t1, p1 = (i1 // n_p) % n_t, i1 % n_p
    t3, p3 = (i3 // n_p) % n_t, i3 % n_p
    sa = i % 2
    sb = 1 - sa
    s13_w = i % 3
    s13_r = (i + 1) % 3

    @pl.when(i == 0)
    def _():
        for ref in (s_scr, prev_scr, prevl_scr, ar_scr, kbh_scr, vst_scr, rt_scr, kbar_scr, vb_scr,
                    plp_scr, q_scr, y_scr, gm_scr, cm_scr, pl_scr, bonus_scr, g_scr):
            ref[...] = jnp.zeros_like(ref)

    @pl.when(p1 == 0)
    def _():
        zl = zl_ref[0][:, :lora].astype(F32)
        prev = jnp.where(t1 == 0, 0.0, prevl_scr[0:1, :lora])
        zls = _shift_lerp(zl, prev, mul_ref[:, :lora])
        prevl_scr[0:1, :lora] = zl[t_rows - 1:t_rows, :]
        x_wa = zls[:, :LANE]
        th_hi, th_lo = _split_bf16(jnp.tanh(x_wa))
        th_hi_scr[...] = th_hi
        th_lo_scr[...] = th_lo
        xw_scr[...] = x_wa.astype(BF16)
        sg_scr[...] = jax.nn.sigmoid(zls[:, LANE:]).astype(BF16)

    def pva(r):
        return pva_ref[r:r + 1, :]

    li = lax.broadcasted_iota(jnp.int32, (lanes, lanes), 0) // RW_HEAD
    lj = lax.broadcasted_iota(jnp.int32, (lanes, lanes), 1) // RW_HEAD
    same_head = li == lj
    e_head = jnp.where(same_head, 1.0, 0.0).astype(BF16)
    ti = lax.broadcasted_iota(jnp.int32, (L, L), 0)
    tj = lax.broadcasted_iota(jnp.int32, (L, L), 1)
    tri = jnp.where(ti >= tj, 1.0, 0.0).astype(BF16)
    si = lax.broadcasted_iota(jnp.int32, (SL, SL), 0)
    sj = lax.broadcasted_iota(jnp.int32, (SL, SL), 1)
    same_blk = (si // L) == (sj // L)
    m_strict = same_blk & (si > sj)
    m_incl = same_blk & (si >= sj)
    eye = jnp.where(si == sj, 1.0, 0.0)
    lane_head = lax.broadcasted_iota(jnp.int32, (1, lanes), 1) // RW_HEAD
    first_half = lax.broadcasted_iota(jnp.int32, (1, 2 * L), 1) < L
    n_sq = max(L.bit_length() - 2, 0)
    cs = range(n_chunks)
    rows = [slice(c * L, (c + 1) * L) for c in cs]

    def stack(x):
        return jnp.concatenate([jnp.where(lane_head == h, x, 0.0) for h in range(RW_PACK)], axis=0)

    def unstack(x):
        out = x[0:L]
        for h in range(1, RW_PACK):
            out = out + x[h * L:(h + 1) * L]
        return out

    zr = zr_ref[0].astype(F32)
    zk = zk_ref[0].astype(F32)
    zv = zv_ref[0].astype(F32)
    prev = jnp.where(t1 == 0, 0.0, prev_scr[p1])
    r = _shift_lerp(zr, prev[0:1, :], pva(_PV_MU_R))
    k = _shift_lerp(zk, prev[1:2, :], pva(_PV_MU_K))
    v = _shift_lerp(zv, prev[2:3, :], pva(_PV_MU_V))
    prev_scr[p1] = jnp.concatenate([zr[t_rows - 1:t_rows, :], zk[t_rows - 1:t_rows, :],
                                    zv[t_rows - 1:t_rows, :], jnp.zeros((SUBLANE - 3, lanes), F32)],
                                   axis=0)
    prep = {"c": 0, "phase": 0}

    def prep_tile_a():
        w2_hi, w2_lo = _split_bf16(w2_ref[...])
        th_hi = th_hi_scr[...]
        w_pre = (pva(_PV_W0) + jnp.dot(th_hi, w2_hi, preferred_element_type=F32)
                 + jnp.dot(th_lo_scr[...], w2_hi, preferred_element_type=F32)
                 + jnp.dot(th_hi, w2_lo, preferred_element_type=F32))
        t = -w_pre
        softplus = jnp.maximum(t, 0.0) + jnp.log1p(jnp.exp(-jnp.abs(t)))
        prep["log_decay"] = -jnp.exp(-softplus - 0.5)
        prep["a"] = jax.nn.sigmoid(pva(_PV_A0) + _mm(xw_scr[...], a2_ref[...]))
        g_scr[s13_w] = _mm(sg_scr[...], g2_ref[...])

    def prep_tile_b():
        a = prep["a"]
        kk = k * pva(_PV_KK)
        kk = kk / jnp.maximum(jnp.sqrt(_mm(kk * kk, e_head)), 1e-12)
        kn = k * (1.0 + (a - 1.0) * pva(_PV_KA))
        prep["kk"], prep["kn"], prep["bv"] = kk, kn, kk * a
        bonus_scr[s13_w] = _mm(r * kn * pva(_PV_RK), e_head) * v
        lw_hi, lw_lo = _split_bf16(prep["log_decay"])
        prep["cm"] = [jnp.dot(tri, lw_hi[rw], preferred_element_type=F32)
                      + jnp.dot(tri, lw_lo[rw], preferred_element_type=F32) for rw in rows]

    def prep_chunk():
        c = prep["c"]
        if c >= n_chunks:
            return
        prep["c"] = c + 1
        rw = rows[c]
        cm = prep["cm"][c]
        cm_last = cm[L - 1:L, :]
        kc, bvc = prep["kn"][rw], prep["bv"][rw]
        r_t = r[rw] * jnp.exp(cm)
        a_st = stack(-prep["kk"][rw] * jnp.exp(cm - prep["log_decay"][rw])).astype(BF16)
        e_neg = jnp.exp(-cm)
        e_rem = jnp.exp(cm_last - cm)
        ar_scr[sa, c] = jnp.concatenate([a_st, stack(r_t).astype(BF16)], axis=0)
        kbh_scr[sa, c] = jnp.concatenate([(kc * e_neg).astype(BF16), (bvc * e_neg).astype(BF16)], axis=0)
        vst_scr[sa, c] = stack(v[rw]).astype(BF16)
        rt_scr[sa, c] = r_t
        kbar_scr[sa, c] = jnp.concatenate([(kc * e_rem).astype(BF16), (bvc * e_rem).astype(BF16)], axis=0)
        vb_scr[sa, c] = v[rw].astype(BF16)
        plp_scr[sa, c] = jnp.broadcast_to(jnp.exp(cm_last), (SUBLANE, lanes))

    def prep_step():
        phase = prep["phase"]
        prep["phase"] = phase + 1
        if phase == 0:
            prep_tile_a()
        elif phase == 1:
            prep_tile_b()
        else:
            for _ in range(-(-n_chunks // (n_sq + 1))):
                prep_chunk()

    chain = {"s": jnp.where(t3 == 0, 0.0, s_scr[p3]), "c": 0, "y": []}

    def chain_step():
        c = chain["c"]
        if c >= n_chunks:
            return
        s0 = chain["s"]
        s0b = s0.astype(BF16)
        chain["y"].append(y_scr[sa, c * L:(c + 1) * L, :] + _mm_nt(q_scr[sa, c], s0b))
        chain["s"] = s0 * pl_scr[sa, c][0:1, :] + _mm(s0b, gm_scr[sa, c]) + cm_scr[sa, c]
        chain["c"] = c + 1

    def fill():
        prep_step()
        chain_step()

    ar_st = [ar_scr[sb, c] for c in cs]
    kb_h = [kbh_scr[sb, c] for c in cs]
    v_st = [vst_scr[sb, c] for c in cs]
    prod = [_mm_nt(ar_st[c], kb_h[c]) for c in cs]
    fill()
    swapped = [pltpu.roll(x, L, axis=1) for x in prod]
    prod_k = [jnp.where(first_half, prod[c], swapped[c]) for c in cs]
    prod_b = [jnp.where(first_half, swapped[c], prod[c]) for c in cs]
    a_ak = [jnp.where(m_strict, x[:SL], 0.0).astype(BF16) for x in prod_k]
    a_rk = [jnp.where(m_incl, x[SL:], 0.0).astype(BF16) for x in prod_k]
    a_ab = [jnp.where(m_strict, x[:SL], 0.0) for x in prod_b]
    a_rb = [jnp.where(m_incl, x[SL:], 0.0).astype(BF16) for x in prod_b]

    xp = [x.astype(BF16) for x in a_ab]
    tinv = [eye + x for x in a_ab]
    for _ in range(n_sq):
        xp = [_mm(x, x).astype(BF16) for x in xp]
        tinv = [tinv[c] + _mm(tinv[c], xp[c]) for c in cs]
        fill()
    tinv = [x.astype(BF16) for x in tinv]

    x0 = [_mm(a_ak[c], v_st[c]).astype(BF16) for c in cs]
    fill()
    wu = [_mm(tinv[c], jnp.concatenate([ar_st[c][:SL], x0[c]], axis=1)) for c in cs]
    fill()
    yq = [_mm(a_rb[c], wu[c]) for c in cs]
    y0 = [_mm(a_rk[c], v_st[c]) for c in cs]
    while prep["c"] < n_chunks or chain["c"] < n_chunks:
        fill()
    cmats, gmats = [], []
    for c in cs:
        kbar = kbar_scr[sb, c]
        vu = jnp.concatenate([vb_scr[sb, c], unstack(wu[c][:, lanes:]).astype(BF16)], axis=0)
        cmats.append(jnp.where(same_head, _mm_tn(vu, kbar), 0.0))
        gmats.append(jnp.where(same_head, _mm_tn(unstack(wu[c][:, :lanes]), kbar[L:]), 0.0).astype(BF16))

    s_scr[p3] = chain["s"]
    y = jnp.concatenate(chain["y"], axis=0)
    inv_n = 1.0 / RW_HEAD
    mean = _mm_exact_rhs(y, e_head) * inv_n
    dy = y - mean
    var = _mm(dy * dy, e_head) * inv_n
    yn = dy * lax.rsqrt(var + GN_EPS) * pvb_ref[_PV_LG:_PV_LG + 1, :] + pvb_ref[_PV_LB:_PV_LB + 1, :]
    o_ref[0] = ((yn + bonus_scr[s13_r]) * g_scr[s13_r]).astype(o_ref.dtype)

    for c, rw in enumerate(rows):
        q_scr[sb, c] = (rt_scr[sb, c] + unstack(yq[c][:, :lanes])).astype(BF16)
        y_scr[sb, rw, :] = unstack(yq[c][:, lanes:] + y0[c])
        cm_scr[sb, c] = cmats[c]
        gm_scr[sb, c] = gmats[c]
        pl_scr[sb, c] = plp_scr[sb, c]


def _rwkv(z, z_lora, col_r, col_k, col_v, pvec, mu_l, w2p, a2p, g2p, width, side, side_lead,
          t_rows=1024):
    bsz, s, _ = z.shape
    lora_w = z_lora.shape[2]
    n_p = width // LANE
    L = RW_CHUNK
    n_chunks = t_rows // L
    n_t = s // t_rows
    n_tiles = bsz * n_t * n_p
    n_steps = n_tiles + 2
    assert RW_PACK * L == LANE and lora_w == 3 * LANE

    def tile(i, lag):
        it = jnp.clip(i - lag, 0, n_tiles - 1)
        return it // (n_t * n_p), (it // n_p) % n_t, it % n_p

    def z_spec(col):
        def idx(i):
            b, t, p = tile(i, 0)
            return b, t, col + p
        return pl.BlockSpec((1, t_rows, LANE), idx)

    def zl_idx(i):
        b, t, _ = tile(i, 0)
        return b, t, 0

    side_in, side_out = _slab_specs(side, side_lead, 0, ROW_CHUNK, n_steps, lambda i: i)
    kern = functools.partial(_rwkv_kernel, chunk=L, n_t=n_t, n_p=n_p, n_tiles=n_tiles)

    def per_chunk(rows_, dtype):
        return pltpu.VMEM((2, n_chunks, rows_, LANE), dtype)

    return pl.pallas_call(
        kern,
        grid=(n_steps,),
        in_specs=[z_spec(col_r), z_spec(col_k), z_spec(col_v),
                  pl.BlockSpec((1, t_rows, lora_w), zl_idx),
                  pl.BlockSpec((_PV_ROWS, LANE), lambda i: (0, tile(i, 0)[2])),
                  pl.BlockSpec((_PV_ROWS, LANE), lambda i: (0, tile(i, 2)[2])),
                  pl.BlockSpec((1, lora_w), lambda i: (0, 0)),
                  pl.BlockSpec((LANE, LANE), lambda i: (0, tile(i, 0)[2])),
                  pl.BlockSpec((LANE, LANE), lambda i: (0, tile(i, 0)[2])),
                  pl.BlockSpec((2 * LANE, LANE), lambda i: (0, tile(i, 0)[2])),
                  side_in],
        out_specs=(pl.BlockSpec((1, t_rows, LANE), lambda i: tile(i, 2)), side_out),
        out_shape=(jax.ShapeDtypeStruct((bsz, s, width), BF16),
                   jax.ShapeDtypeStruct(side.shape[1:], BF16)),
        scratch_shapes=[pltpu.VMEM((n_p, LANE, LANE), F32),
                        pltpu.VMEM((n_p, SUBLANE, LANE), F32),
                        pltpu.VMEM((SUBLANE, lora_w), F32),
                        pltpu.VMEM((t_rows, LANE), BF16),
                        pltpu.VMEM((t_rows, LANE), BF16),
                        pltpu.VMEM((t_rows, LANE), BF16),
                        pltpu.VMEM((t_rows, lora_w - LANE), BF16),
                        per_chunk(4 * L, BF16),
                        per_chunk(2 * L, BF16),
                        per_chunk(2 * L, BF16),
                        per_chunk(L, F32),
                        per_chunk(2 * L, BF16),
                        per_chunk(L, BF16),
                        per_chunk(SUBLANE, F32),
                        per_chunk(L, BF16),
                        pltpu.VMEM((2, t_rows, LANE), F32),
                        per_chunk(LANE, BF16),
                        per_chunk(LANE, F32),
                        per_chunk(SUBLANE, F32),
                        pltpu.VMEM((3, t_rows, LANE), F32),
                        pltpu.VMEM((3, t_rows, LANE), F32)],
        compiler_params=_params(("arbitrary",)),
        name="rwkv7",
    )(z, z, z, z_lora, pvec, pvec, mu_l, w2p, a2p, g2p, side)


def _moba_kernel(q_ref, k_ref, v_ref, side_ref, o_ref, side_o_ref, ka_scr, vt_scr, s_scr, p_scr, *,
                 n_heads, heads_per_step):
    side_o_ref[...] = side_ref[...].astype(BF16)
    heads = [_moba_head(q_ref, k_ref, v_ref, o_ref, ka_scr.at[j], vt_scr.at[j], s_scr.at[j],
                        p_scr.at[j], j, pl.program_id(1) * heads_per_step + j, n_heads, heads_per_step)
             for j in range(heads_per_step)]
    nb = q_ref.shape[1] // MOBA_BLOCK
    pending = [scores(0) for scores, _ in heads]
    for qb in range(nb):
        nxt = [scores(qb + 1) if qb + 1 < nb else None for scores, _ in heads]
        for (_, attend), pend in zip(heads, pending):
            attend(qb, *pend)
        pending = nxt


def _moba_head(q_ref, k_ref, v_ref, o_ref, ka_scr, vt_scr, s_scr, p_scr, j, h, n_heads, heads_per_step):
    s_len = q_ref.shape[1]
    dh = q_ref.shape[2] // heads_per_step
    cols = slice(j * dh, (j + 1) * dh)
    blk = MOBA_BLOCK
    nb = s_len // blk
    log2e = 1.4426950408889634
    scale = dh ** -0.5 * log2e
    neg_inf = -jnp.inf

    def slope_row(width):
        return log2e * jnp.exp(jnp.full((1, width), -8.0 / n_heads * 0.6931471805599453, F32)
                               * (h + 1).astype(F32))

    lane = lax.broadcasted_iota(jnp.int32, (blk, dh), 1)
    bias = slope_row(dh) * lax.broadcasted_iota(jnp.int32, (blk, dh), 0).astype(F32)
    extra = jnp.zeros((blk, dh), F32)
    for col in range(3):
        part = bias.astype(BF16).astype(F32)
        extra = jnp.where(lane == col, part, extra)
        bias = bias - part
    extra = extra.astype(BF16)
    ones_cols = jnp.where(lane < 3, 1.0, 0.0).astype(BF16)

    kmean = []
    for j in range(nb):
        rows = slice(j * blk, (j + 1) * blk)
        k_j = k_ref[0, rows, cols]
        kmean.append(jnp.mean(k_j.astype(F32), axis=0, keepdims=True))
        ka_scr[rows, :dh] = k_j
        ka_scr[rows, dh:] = extra
        vt_scr[:dh, rows] = v_ref[0, rows, cols].astype(F32).T.astype(BF16)
    sub = lax.broadcasted_iota(jnp.int32, (vt_scr.shape[0] - dh, s_len), 0)
    vt_scr[dh:, :] = jnp.where(sub == 0, 1.0, 0.0).astype(BF16)
    kmean = jnp.concatenate(kmean, axis=0)
    kmean_parts = []
    for _ in range(3):
        part = kmean.astype(BF16)
        kmean_parts.append(part)
        kmean = kmean - part.astype(F32)

    slope = slope_row(blk)
    causal = (lax.broadcasted_iota(jnp.int32, (blk, blk), 1)
              >= lax.broadcasted_iota(jnp.int32, (blk, blk), 0))
    blk_id = lax.broadcasted_iota(jnp.int32, (nb, 1), 0)

    def scores(qb):
        q = q_ref[0, qb * blk:(qb + 1) * blk, cols]
        q_aug = jnp.concatenate([(q.astype(F32) * scale).astype(BF16), ones_cols], axis=1)
        gate = sum(lax.dot_general(part, q, NT_DIMS, preferred_element_type=F32)
                   for part in kmean_parts)
        past = blk_id < qb
        offs = []
        m = None
        for n in range(qb + 1):
            t = lax.dot_general(ka_scr[n * blk:(n + 1) * blk, :], q_aug, NT_DIMS,
                                preferred_element_type=F32)
            if n == qb:
                t = jnp.where(causal, t, neg_inf)
                off = jnp.zeros((1, blk), F32)
            else:
                g_n = gate[n:n + 1, :]
                beats = past & ((gate > g_n) | ((gate == g_n) & (blk_id < n)))
                rank = jnp.sum(jnp.where(beats, 1.0, 0.0), axis=0, keepdims=True)
                off = jnp.where(rank < float(MOBA_TOPK), slope * float((n - qb) * blk), neg_inf)
            s_scr[qb % 2, n] = t
            offs.append(off)
            cmax = jnp.max(t, axis=0, keepdims=True) + off
            m = cmax if m is None else jnp.maximum(m, cmax)
        return m, offs

    def attend(qb, m, offs):
        for n in range(qb + 1):
            p = jnp.exp2(s_scr[qb % 2, n] - (m - offs[n]))
            p_scr[qb % 2, n * blk:(n + 1) * blk, :] = p.astype(BF16)
        kk = (qb + 1) * blk
        acc = jnp.dot(vt_scr[:, :kk], p_scr[qb % 2, :kk, :], preferred_element_type=F32)
        o_ref[0, qb * blk:(qb + 1) * blk, cols] = (acc[:dh] / acc[dh:dh + 1]).T.astype(o_ref.dtype)

    return scores, attend


def _moba(qkv, n_heads, side, side_lead, heads_per_step=2):
    bsz, s, d3 = qkv.shape
    d = d3 // 3
    dh = d // n_heads
    blk = MOBA_BLOCK
    nb = s // blk
    n_hp = n_heads // heads_per_step
    wdt = heads_per_step * dh
    side_in, side_out = _slab_specs(side, side_lead, 0, ROW_CHUNK, bsz * n_hp,
                                    lambda b, h: b * n_hp + h)
    kern = functools.partial(_moba_kernel, n_heads=n_heads, heads_per_step=heads_per_step)
    return pl.pallas_call(
        kern,
        grid=(bsz, n_hp),
        in_specs=[pl.BlockSpec((1, s, wdt), lambda b, h: (b, 0, h)),
                  pl.BlockSpec((1, s, wdt), lambda b, h: (b, 0, n_hp + h)),
                  pl.BlockSpec((1, s, wdt), lambda b, h: (b, 0, 2 * n_hp + h)),
                  side_in],
        out_specs=(pl.BlockSpec((1, s, wdt), lambda b, h: (b, 0, h)), side_out),
        out_shape=(jax.ShapeDtypeStruct((bsz, s, d), BF16),
                   jax.ShapeDtypeStruct(side.shape[1:], BF16)),
        scratch_shapes=[pltpu.VMEM((heads_per_step, s, 2 * dh), BF16),
                        pltpu.VMEM((heads_per_step, dh + 16, s), BF16),
                        pltpu.VMEM((heads_per_step, 2, nb, blk, blk), F32),
                        pltpu.VMEM((heads_per_step, 2, s, blk), BF16)],
        compiler_params=_params(("arbitrary", "arbitrary")),
        name="moba",
    )(qkv, qkv, qkv, side)


def _pad_cols(w, n):
    return jnp.pad(w, ((0, 0), (0, n - w.shape[1])))


def _pad_rows(w, n, before=0):
    return jnp.pad(w, ((before, n - before - w.shape[0]), (0, 0)))


def kernel(x, c, w_ada, b_ada, g_pre_mix, g_post_mix, g_pre_ffn, g_post_ffn, w_ffn_in, w_ffn_out,
           w_in_ab, w_out_ab, a_v_gain, a_v_bias, a_w_s, a_b_s, b_mu, b_w0, b_w2, b_a0, b_a2, b_g2,
           b_k_k, b_k_a, b_r_k, b_lnx_gain, b_lnx_bias, w_qkv, w_o):
    bsz, s, d = x.shape
    depth = w_ada.shape[0]
    a_width = a_v_gain.shape[1]
    b_width = b_w0.shape[1]
    n_lw = b_w2.shape[1]
    n_la = b_a2.shape[1]
    n_lg = b_g2.shape[1]
    n_heads = d // ATT_HEAD
    assert s % MOBA_BLOCK == 0 and s % 1024 == 0
    assert n_lw + n_la <= LANE and n_lg <= 2 * LANE

    w_in_t = jnp.swapaxes(w_in_ab, 1, 2)
    mod, w_in_h0 = _ada_mod(c, w_ada, b_ada, w_in_t, 0)

    w_qkv_h = None
    for layer in range(depth):
        mod3 = mod[layer].reshape(bsz, 1, 6 * d)
        i = layer // 2
        if layer % 2 == 0:
            lora_w = 3 * LANE
            n_main = 2 * a_width + 3 * b_width
            w_t = w_in_h0 if i == 0 else w_in_t[i]
            w_tail = _pad_rows(w_t[n_main:], lora_w)
            z, z_lora, w_out_h = _norm_mm(x, g_pre_mix[layer], mod3, 1, 0, w_t, BF16, w_out_ab, i,
                                          w_tail=w_tail, transposed=True, tn=n_main // 4,
                                          name="in_proj_ab")
            y_a = _mixer_a(z, a_v_gain[i], a_v_bias[i], a_w_s[i], a_b_s[i], a_width)

            mu = b_mu[i]
            pvec = jnp.stack([mu[0:b_width], mu[b_width:2 * b_width], mu[2 * b_width:3 * b_width],
                              b_w0[i], b_a0[i], b_k_k[i], b_k_a[i], b_r_k[i].reshape(-1),
                              b_lnx_gain[i], b_lnx_bias[i]])
            pvec = _pad_rows(pvec, _PV_ROWS)
            mu_l = _pad_cols(mu[3 * b_width:].reshape(1, -1), lora_w)
            w2p = _pad_rows(b_w2[i], LANE)
            a2p = _pad_rows(b_a2[i], LANE, before=n_lw)
            g2p = _pad_rows(b_g2[i], 2 * LANE)
            cb = 2 * a_width // LANE
            nb_w = b_width // LANE
            y_b, w_ffn_in_h = _rwkv(z, z_lora, cb, cb + nb_w, cb + 2 * nb_w, pvec, mu_l, w2p, a2p,
                                    g2p, b_width, w_ffn_in, layer)
            x, w_ffn_out_h = _out_proj(y_a, y_b, 0, 0, w_out_h, x, mod3, g_post_mix[layer],
                                       w_ffn_out, layer)
        else:
            w_q = w_qkv[i] if w_qkv_h is None else w_qkv_h
            qkv, w_o_h = _norm_mm(x, g_pre_mix[layer], mod3, 1, 0, w_q, BF16, w_o, i,
                                  tn=d, name="qkv_proj")
            o, w_ffn_in_h = _moba(qkv, n_heads, w_ffn_in, layer)
            x, w_ffn_out_h = _out_proj(o, o, 0, 1, w_o_h, x, mod3, g_post_mix[layer],
                                       w_ffn_out, layer)
        if layer + 1 < depth and (layer + 1) % 2 == 1:
            x, w_qkv_h = _ffn(x, g_pre_ffn[layer], g_post_ffn[layer], mod3, w_ffn_in_h, w_ffn_out_h,
                              w_qkv, (layer + 1) // 2)
        else:
            x = _ffn(x, g_pre_ffn[layer], g_post_ffn[layer], mod3, w_ffn_in_h, w_ffn_out_h)
            w_qkv_h = None
    return x
```

```python
import functools

import jax
import jax.numpy as jnp
from jax import lax
from jax.experimental import pallas as pl
from jax.experimental.pallas import tpu as pltpu

F32 = jnp.float32
BF16 = jnp.bfloat16

NORM_EPS = 1e-6
LN_EPS = 1e-5
GN_EPS = 64e-5

LANE = 128
SUBLANE = 8
A_GROUPS = 8
A_CHUNK = 128
RW_HEAD = 64
RW_CHUNK = 64
RW_PACK = 2
MOBA_BLOCK = 256
MOBA_TOPK = 3
MOBA_DEPTH = 3
ATT_HEAD = 128

NT_DIMS = (((1,), (1,)), ((), ()))
TN_DIMS = (((0,), (0,)), ((), ()))

VMEM_LIMIT = 56 * 1024 * 1024


def _params(sem):
    return pltpu.CompilerParams(dimension_semantics=sem, vmem_limit_bytes=VMEM_LIMIT)


def _slab_specs(side, lead, axis, unit, n_steps, flat_index):
    total = side.shape[1 + axis]
    n_slab = max(n for n in range(1, n_steps + 1) if total % (n * unit) == 0)
    shape = list(side.shape[1:])
    shape[axis] = total // n_slab

    def idx(*grid):
        slab = jnp.minimum(flat_index(*grid), n_slab - 1)
        return (slab, 0) if axis == 0 else (0, slab)

    return (pl.BlockSpec((None,) + tuple(shape), lambda *g: (lead,) + idx(*g)),
            pl.BlockSpec(tuple(shape), idx))


ROW_CHUNK = 16
ROW_UNROLL = 8

def _for_row_chunks(n_rows, fn):
    def body(i, carry):
        fn(pl.ds(pl.multiple_of(i * ROW_CHUNK, ROW_CHUNK), ROW_CHUNK))
        return carry
    lax.fori_loop(0, n_rows // ROW_CHUNK, body, 0, unroll=ROW_UNROLL)


def _modulated_norm(x_ref, g_ref, sc_ref, sh_ref, gm_scr, h_scr):
    gm_scr[...] = g_ref[...] * (1.0 + sc_ref[0])

    def rows(r):
        x = x_ref[0, r, :]
        ms = jnp.mean(x * x, axis=-1, keepdims=True)
        h_scr[r, :] = (x * lax.rsqrt(ms + NORM_EPS) * gm_scr[...] + sh_ref[0]).astype(BF16)

    _for_row_chunks(h_scr.shape[0], rows)


def _post_norm_residual(y_scr, x_ref, gt_ref, gpost_ref, gm_scr, o_ref):
    gm_scr[...] = gt_ref[0] * gpost_ref[...]

    def rows(r):
        y = y_scr[r, :]
        ms = jnp.mean(y * y, axis=-1, keepdims=True)
        o_ref[0, r, :] = x_ref[0, r, :] + y * lax.rsqrt(ms + NORM_EPS) * gm_scr[...]

    _for_row_chunks(o_ref.shape[1], rows)


def _ada_kernel(c_ref, w_ref, b_ref, side_ref, o_ref, side_o_ref):
    c = c_ref[...]
    cond = (c * jax.nn.sigmoid(c)).astype(BF16)
    o_ref[0] = jnp.dot(cond, w_ref[0].astype(BF16), preferred_element_type=F32) + b_ref[0]
    side_o_ref[...] = side_ref[...].astype(BF16)


def _ada_mod(c, w_ada, b_ada, side, side_lead):
    depth, d, n = w_ada.shape
    bsz = c.shape[0]
    bp = -(-bsz // SUBLANE) * SUBLANE
    c_p = jnp.pad(c, ((0, bp - bsz), (0, 0)))
    tn = 1024
    nj = n // tn
    side_in, side_out = _slab_specs(side, side_lead, 0, ROW_CHUNK, depth * nj, lambda l, j: l * nj + j)
    out, side_h = pl.pallas_call(
        _ada_kernel,
        grid=(depth, nj),
        in_specs=[pl.BlockSpec((bp, d), lambda l, j: (0, 0)),
                  pl.BlockSpec((1, d, tn), lambda l, j: (l, 0, j)),
                  pl.BlockSpec((1, 1, tn), lambda l, j: (l, 0, j)),
                  side_in],
        out_specs=(pl.BlockSpec((1, bp, tn), lambda l, j: (l, 0, j)), side_out),
        out_shape=(jax.ShapeDtypeStruct((depth, bp, n), F32),
                   jax.ShapeDtypeStruct(side.shape[1:], BF16)),
        compiler_params=_params(("arbitrary", "arbitrary")),
        name="ada_mod",
    )(c_p, w_ada, b_ada.reshape(depth, 1, n), side)
    return out[:, :bsz], side_h


def _norm_mm_kernel(*refs, has_tail, transposed):
    if has_tail:
        (x_ref, g_ref, sc_ref, sh_ref, w_ref, wt_ref, side_ref, o_ref, ot_ref, side_o_ref,
         h_scr, gm_scr) = refs
    else:
        x_ref, g_ref, sc_ref, sh_ref, w_ref, side_ref, o_ref, side_o_ref, h_scr, gm_scr = refs
    j = pl.program_id(2)

    @pl.when(j == 0)
    def _():
        _modulated_norm(x_ref, g_ref, sc_ref, sh_ref, gm_scr, h_scr)

    def project(wr, out_ref):
        w = wr[...].astype(BF16)
        if transposed:
            y = lax.dot_general(h_scr[...], w, NT_DIMS, preferred_element_type=F32)
        else:
            y = jnp.dot(h_scr[...], w, preferred_element_type=F32)
        out_ref[0] = y.astype(out_ref.dtype)

    project(w_ref, o_ref)
    if has_tail:
        pl.when(j == pl.num_programs(2) - 1)(lambda: project(wt_ref, ot_ref))
    side_o_ref[...] = side_ref[...].astype(BF16)


def _norm_mm(x, gain, mod3, sc_idx, sh_idx, w, out_dtype, side, side_lead, w_tail=None,
             transposed=False, tm=1024, tn=1024, name="norm_mm"):
    bsz, s, d = x.shape
    n_w = w.shape[0] if transposed else w.shape[1]
    nm, nj = s // tm, n_w // tn
    n = nj * tn
    side_in, side_out = _slab_specs(side, side_lead, 1, LANE, bsz * nm * nj,
                                    lambda b, m, j: (b * nm + m) * nj + j)
    if transposed:
        w_specs = [pl.BlockSpec((tn, d), lambda b, m, j: (j, 0))]
    else:
        w_specs = [pl.BlockSpec((d, tn), lambda b, m, j: (0, j))]
    w_args = [w]
    out_specs = [pl.BlockSpec((1, tm, tn), lambda b, m, j: (b, m, j))]
    out_shape = [jax.ShapeDtypeStruct((bsz, s, n), out_dtype)]
    if w_tail is not None:
        n_tail = w_tail.shape[0] if transposed else w_tail.shape[1]
        w_specs.append(pl.BlockSpec(w_tail.shape, lambda b, m, j: (0, 0), pipeline_mode=pl.Buffered(1)))
        w_args.append(w_tail)
        out_specs.append(pl.BlockSpec((1, tm, n_tail), lambda b, m, j: (b, m, 0)))
        out_shape.append(jax.ShapeDtypeStruct((bsz, s, n_tail), out_dtype))
    return pl.pallas_call(
        functools.partial(_norm_mm_kernel, has_tail=w_tail is not None, transposed=transposed),
        grid=(bsz, nm, nj),
        in_specs=[pl.BlockSpec((1, tm, d), lambda b, m, j: (b, m, 0)),
                  pl.BlockSpec((1, d), lambda b, m, j: (0, 0)),
                  pl.BlockSpec((1, 1, d), lambda b, m, j: (b, 0, sc_idx)),
                  pl.BlockSpec((1, 1, d), lambda b, m, j: (b, 0, sh_idx)),
                  *w_specs,
                  side_in],
        out_specs=(*out_specs, side_out),
        out_shape=(*out_shape, jax.ShapeDtypeStruct(side.shape[1:], BF16)),
        scratch_shapes=[pltpu.VMEM((tm, d), BF16), pltpu.VMEM((1, d), F32)],
        compiler_params=_params(("arbitrary", "arbitrary", "arbitrary")),
        name=name,
    )(x, gain.reshape(1, d), mod3, mod3, *w_args, side)


def _ffn_kernel(*refs, has_side):
    if has_side:
        (x_ref, gpre_ref, sc_ref, sh_ref, gt_ref, gpost_ref, wg_ref, wu_ref, wo_ref, side_ref,
         o_ref, side_o_ref, h_scr, gm_scr, acc_scr) = refs
        side_o_ref[...] = side_ref[...].astype(BF16)
    else:
        (x_ref, gpre_ref, sc_ref, sh_ref, gt_ref, gpost_ref, wg_ref, wu_ref, wo_ref,
         o_ref, h_scr, gm_scr, acc_scr) = refs
    f = pl.program_id(2)

    @pl.when((pl.program_id(0) == 0) & (pl.program_id(1) == 0) & (f == 0))
    def _():
        acc_scr[...] = jnp.zeros_like(acc_scr)

    @pl.when(f == 0)
    def _():
        _modulated_norm(x_ref, gpre_ref, sc_ref, sh_ref, gm_scr, h_scr)

    h = h_scr[...]
    g = jnp.dot(h, wg_ref[...], preferred_element_type=F32)
    u = jnp.dot(h, wu_ref[...], preferred_element_type=F32)
    a = (g * jax.nn.sigmoid(g) * u).astype(BF16)
    acc_scr[...] = (jnp.where(f == 0, 0.0, acc_scr[...])
                    + jnp.dot(a, wo_ref[...], preferred_element_type=F32))

    @pl.when(f == pl.num_programs(2) - 1)
    def _():
        _post_norm_residual(acc_scr, x_ref, gt_ref, gpost_ref, gm_scr, o_ref)


def _ffn(x, gpre, gpost, mod3, w_in, w_out, side=None, side_lead=0, tm=1024, tf=512):
    bsz, s, d = x.shape
    fh = w_out.shape[0]
    nf = fh // tf
    nm = s // tm
    out_spec = pl.BlockSpec((1, tm, d), lambda b, m, f: (b, m, 0), pipeline_mode=pl.Buffered(1))
    out_shape = jax.ShapeDtypeStruct((bsz, s, d), F32)
    side_specs, side_args = [], []
    if side is not None:
        side_in, side_out = _slab_specs(side, side_lead, 1, LANE, bsz * nm * nf,
                                        lambda b, m, f: (b * nm + m) * nf + f)
        side_specs, side_args = [side_in], [side]
        out_spec = (out_spec, side_out)
        out_shape = (out_shape, jax.ShapeDtypeStruct(side.shape[1:], BF16))
    return pl.pallas_call(
        functools.partial(_ffn_kernel, has_side=side is not None),
        grid=(bsz, nm, nf),
        in_specs=[pl.BlockSpec((1, tm, d), lambda b, m, f: (b, m, 0), pipeline_mode=pl.Buffered(1)),
                  pl.BlockSpec((1, d), lambda b, m, f: (0, 0)),
                  pl.BlockSpec((1, 1, d), lambda b, m, f: (b, 0, 4)),
                  pl.BlockSpec((1, 1, d), lambda b, m, f: (b, 0, 3)),
                  pl.BlockSpec((1, 1, d), lambda b, m, f: (b, 0, 5)),
                  pl.BlockSpec((1, d), lambda b, m, f: (0, 0)),
                  pl.BlockSpec((d, tf), lambda b, m, f: (0, f)),
                  pl.BlockSpec((d, tf), lambda b, m, f: (0, nf + f)),
                  pl.BlockSpec((tf, d), lambda b, m, f: (f, 0)),
                  *side_specs],
        out_specs=out_spec,
        out_shape=out_shape,
        scratch_shapes=[pltpu.VMEM((tm, d), BF16), pltpu.VMEM((1, d), F32), pltpu.VMEM((tm, d), F32)],
        compiler_params=_params(("arbitrary", "arbitrary", "arbitrary")),
        name="ffn",
    )(x, gpre.reshape(1, d), mod3, mod3, mod3, gpost.reshape(1, d), w_in, w_in, w_out, *side_args)


def _out_proj_kernel(a0_ref, a1_ref, w0_ref, w1_ref, x_ref, gt_ref, gpost_ref, side_ref,
                     o_ref, side_o_ref, gm_scr, y_scr):
    side_o_ref[...] = side_ref[...].astype(BF16)
    y_scr[...] = (jnp.dot(a0_ref[0], w0_ref[...], preferred_element_type=F32)
                  + jnp.dot(a1_ref[0], w1_ref[...], preferred_element_type=F32))
    _post_norm_residual(y_scr, x_ref, gt_ref, gpost_ref, gm_scr, o_ref)


def _out_proj(a0, a1, col0, col1, w, x, mod3, gpost, side, side_lead, tm=512):
    bsz, s, d = x.shape
    kh = w.shape[0] // 2
    nm = s // tm
    side_in, side_out = _slab_specs(side, side_lead, 0, ROW_CHUNK, bsz * nm, lambda b, m: b * nm + m)
    return pl.pallas_call(
        _out_proj_kernel,
        grid=(bsz, nm),
        in_specs=[pl.BlockSpec((1, tm, kh), lambda b, m: (b, m, col0)),
                  pl.BlockSpec((1, tm, kh), lambda b, m: (b, m, col1)),
                  pl.BlockSpec((kh, d), lambda b, m: (0, 0)),
                  pl.BlockSpec((kh, d), lambda b, m: (1, 0)),
                  pl.BlockSpec((1, tm, d), lambda b, m: (b, m, 0)),
                  pl.BlockSpec((1, 1, d), lambda b, m: (b, 0, 2)),
                  pl.BlockSpec((1, d), lambda b, m: (0, 0)),
                  side_in],
        out_specs=(pl.BlockSpec((1, tm, d), lambda b, m: (b, m, 0)), side_out),
        out_shape=(jax.ShapeDtypeStruct((bsz, s, d), F32),
                   jax.ShapeDtypeStruct(side.shape[1:], BF16)),
        scratch_shapes=[pltpu.VMEM((1, d), F32), pltpu.VMEM((tm, d), F32)],
        compiler_params=_params(("arbitrary", "arbitrary")),
        name="out_proj",
    )(a0, a1, w, w, x, mod3, gpost.reshape(1, d), side)


def _mixer_a_kernel(z_ref, vg_ref, vb_ref, ws_ref, bst_ref, o_ref, wm_scr):
    ch = ws_ref.shape[1]

    @pl.when((pl.program_id(0) == 0) & (pl.program_id(1) == 0))
    def _():
        causal = (lax.broadcasted_iota(jnp.int32, (ch, ch), 0)
                  >= lax.broadcasted_iota(jnp.int32, (ch, ch), 1))
        for g in range(A_GROUPS):
            wm_scr[g] = jnp.where(causal, ws_ref[g], 0.0).astype(BF16)

    for c in range(z_ref.shape[1] // ch):
        rows = slice(c * ch, (c + 1) * ch)
        z = jax.nn.gelu(z_ref[0, rows, :].astype(F32))
        wdt = z.shape[1] // 2
        u = z[:, :wdt]
        v = z[:, wdt:]
        mu = jnp.mean(v, axis=-1, keepdims=True)
        dv = v - mu
        var = jnp.mean(dv * dv, axis=-1, keepdims=True)
        vn = (dv * lax.rsqrt(var + LN_EPS) * vg_ref[...] + vb_ref[...]).astype(BF16)
        gd = wdt // A_GROUPS
        for g in range(A_GROUPS):
            sv = jnp.dot(wm_scr[g], vn[:, g * gd:(g + 1) * gd], preferred_element_type=F32)
            sv = sv + bst_ref[:, g:g + 1]
            o_ref[0, rows, g * gd:(g + 1) * gd] = (u[:, g * gd:(g + 1) * gd] * sv).astype(o_ref.dtype)


def _mixer_a(z, v_gain, v_bias, w_s, b_s, width, chunks_per_step=4):
    bsz, s, _ = z.shape
    ch = A_CHUNK
    rows = ch * chunks_per_step
    return pl.pallas_call(
        _mixer_a_kernel,
        grid=(bsz, s // rows),
        in_specs=[pl.BlockSpec((1, rows, 2 * width), lambda b, c: (b, c, 0)),
                  pl.BlockSpec((1, width), lambda b, c: (0, 0)),
                  pl.BlockSpec((1, width), lambda b, c: (0, 0)),
                  pl.BlockSpec((A_GROUPS, ch, ch), lambda b, c: (0, 0, 0)),
                  pl.BlockSpec((ch, A_GROUPS), lambda b, c: (0, 0))],
        out_specs=pl.BlockSpec((1, rows, width), lambda b, c: (b, c, 0)),
        out_shape=jax.ShapeDtypeStruct((bsz, s, width), BF16),
        scratch_shapes=[pltpu.VMEM((A_GROUPS, ch, ch), BF16)],
        compiler_params=_params(("arbitrary", "arbitrary")),
        name="mixer_a",
    )(z, v_gain.reshape(1, width), v_bias.reshape(1, width), w_s, b_s.T)


_PV_MU_R, _PV_MU_K, _PV_MU_V, _PV_W0, _PV_A0, _PV_KK, _PV_KA, _PV_RK, _PV_LG, _PV_LB = range(10)
_PV_ROWS = 16


def _shift_lerp(x, prev_row, mu):
    rolled = pltpu.roll(x, 1, axis=0)
    first = lax.broadcasted_iota(jnp.int32, x.shape, 0) == 0
    xp = jnp.where(first, prev_row, rolled)
    return x + mu * (xp - x)


def _split_bf16(x):
    hi = x.astype(BF16)
    lo = (x - hi.astype(F32)).astype(BF16)
    return hi, lo


def _mm(x, y):
    return jnp.dot(x.astype(BF16), y.astype(BF16), preferred_element_type=F32)


def _mm_nt(x, y):
    return lax.dot_general(x.astype(BF16), y.astype(BF16), NT_DIMS, preferred_element_type=F32)


def _mm_tn(x, y):
    return lax.dot_general(x.astype(BF16), y.astype(BF16), TN_DIMS, preferred_element_type=F32)


def _mm_exact_rhs(x, e_bf16):
    xh, xl = _split_bf16(x)
    return (jnp.dot(xh, e_bf16, preferred_element_type=F32)
            + jnp.dot(xl, e_bf16, preferred_element_type=F32))


def _rwkv_kernel(zr_ref, zk_ref, zv_ref, zl_ref, pva_ref, pvb_ref, mul_ref, w2_ref, a2_ref, g2_ref,
                 side_ref, o_ref, side_o_ref,
                 s_scr, prev_scr, prevl_scr, th_hi_scr, th_lo_scr, xw_scr, sg_scr,
                 ar_scr, kbh_scr, vst_scr, rt_scr, kbar_scr, vb_scr, plp_scr,
                 q_scr, y_scr, gm_scr, cm_scr, pl_scr, bonus_scr, g_scr, *, chunk, n_t, n_p, n_tiles):
    i = pl.program_id(0)
    t_rows = zr_ref.shape[1]
    lanes = zr_ref.shape[2]
    lora = xw_scr.shape[1] + sg_scr.shape[1]
    L = chunk
    SL = RW_PACK * L
    n_chunks = t_rows // L
    side_o_ref[...] = side_ref[...].astype(BF16)

    i1 = jnp.minimum(i, n_tiles - 1)
    i3 = jnp.maximum(i - 2, 0)
    t1, p1 = (i1 // n_p) % n_t, i1 % n_p
    t3, p3 = (i3 // n_p) % n_t, i3 % n_p
    sa = i % 2
    sb = 1 - sa
    s13_w = i % 3
    s13_r = (i + 1) % 3

    @pl.when(i == 0)
    def _():
        for ref in (s_scr, prev_scr, prevl_scr, ar_scr, kbh_scr, vst_scr, rt_scr, kbar_scr, vb_scr,
                    plp_scr, q_scr, y_scr, gm_scr, cm_scr, pl_scr, bonus_scr, g_scr):
            ref[...] = jnp.zeros_like(ref)

    @pl.when(p1 == 0)
    def _():
        zl = zl_ref[0][:, :lora].astype(F32)
        prev = jnp.where(t1 == 0, 0.0, prevl_scr[0:1, :lora])
        zls = _shift_lerp(zl, prev, mul_ref[:, :lora])
        prevl_scr[0:1, :lora] = zl[t_rows - 1:t_rows, :]
        x_wa = zls[:, :LANE]
        th_hi, th_lo = _split_bf16(jnp.tanh(x_wa))
        th_hi_scr[...] = th_hi
        th_lo_scr[...] = th_lo
        xw_scr[...] = x_wa.astype(BF16)
        sg_scr[...] = jax.nn.sigmoid(zls[:, LANE:]).astype(BF16)

    def pva(r):
        return pva_ref[r:r + 1, :]

    li = lax.broadcasted_iota(jnp.int32, (lanes, lanes), 0) // RW_HEAD
    lj = lax.broadcasted_iota(jnp.int32, (lanes, lanes), 1) // RW_HEAD
    same_head = li == lj
    e_head = jnp.where(same_head, 1.0, 0.0).astype(BF16)
    ti = lax.broadcasted_iota(jnp.int32, (L, L), 0)
    tj = lax.broadcasted_iota(jnp.int32, (L, L), 1)
    tri = jnp.where(ti >= tj, 1.0, 0.0).astype(BF16)
    si = lax.broadcasted_iota(jnp.int32, (SL, SL), 0)
    sj = lax.broadcasted_iota(jnp.int32, (SL, SL), 1)
    same_blk = (si // L) == (sj // L)
    m_strict = same_blk & (si > sj)
    m_incl = same_blk & (si >= sj)
    eye = jnp.where(si == sj, 1.0, 0.0)
    lane_head = lax.broadcasted_iota(jnp.int32, (1, lanes), 1) // RW_HEAD
    first_half = lax.broadcasted_iota(jnp.int32, (1, 2 * L), 1) < L
    n_sq = max(L.bit_length() - 2, 0)
    cs = range(n_chunks)
    rows = [slice(c * L, (c + 1) * L) for c in cs]

    def stack(x):
        return jnp.concatenate([jnp.where(lane_head == h, x, 0.0) for h in range(RW_PACK)], axis=0)

    def unstack(x):
        out = x[0:L]
        for h in range(1, RW_PACK):
            out = out + x[h * L:(h + 1) * L]
        return out

    zr = zr_ref[0].astype(F32)
    zk = zk_ref[0].astype(F32)
    zv = zv_ref[0].astype(F32)
    prev = jnp.where(t1 == 0, 0.0, prev_scr[p1])
    r = _shift_lerp(zr, prev[0:1, :], pva(_PV_MU_R))
    k = _shift_lerp(zk, prev[1:2, :], pva(_PV_MU_K))
    v = _shift_lerp(zv, prev[2:3, :], pva(_PV_MU_V))
    prev_scr[p1] = jnp.concatenate([zr[t_rows - 1:t_rows, :], zk[t_rows - 1:t_rows, :],
                                    zv[t_rows - 1:t_rows, :], jnp.zeros((SUBLANE - 3, lanes), F32)],
                                   axis=0)
    prep = {"c": 0, "phase": 0}

    def prep_tile_a():
        w2_hi, w2_lo = _split_bf16(w2_ref[...])
        th_hi = th_hi_scr[...]
        w_pre = (pva(_PV_W0) + jnp.dot(th_hi, w2_hi, preferred_element_type=F32)
                 + jnp.dot(th_lo_scr[...], w2_hi, preferred_element_type=F32)
                 + jnp.dot(th_hi, w2_lo, preferred_element_type=F32))
        t = -w_pre
        softplus = jnp.maximum(t, 0.0) + jnp.log1p(jnp.exp(-jnp.abs(t)))
        prep["log_decay"] = -jnp.exp(-softplus - 0.5)
        prep["a"] = jax.nn.sigmoid(pva(_PV_A0) + _mm(xw_scr[...], a2_ref[...]))
        g_scr[s13_w] = _mm(sg_scr[...], g2_ref[...])

    def prep_tile_b():
        a = prep["a"]
        kk = k * pva(_PV_KK)
        kk = kk / jnp.maximum(jnp.sqrt(_mm(kk * kk, e_head)), 1e-12)
        kn = k * (1.0 + (a - 1.0) * pva(_PV_KA))
        prep["kk"], prep["kn"], prep["bv"] = kk, kn, kk * a
        bonus_scr[s13_w] = _mm(r * kn * pva(_PV_RK), e_head) * v
        lw_hi, lw_lo = _split_bf16(prep["log_decay"])
        prep["cm"] = [jnp.dot(tri, lw_hi[rw], preferred_element_type=F32)
                      + jnp.dot(tri, lw_lo[rw], preferred_element_type=F32) for rw in rows]

    def prep_chunk():
        c = prep["c"]
        if c >= n_chunks:
            return
        prep["c"] = c + 1
        rw = rows[c]
        cm = prep["cm"][c]
        cm_last = cm[L - 1:L, :]
        kc, bvc = prep["kn"][rw], prep["bv"][rw]
        r_t = r[rw] * jnp.exp(cm)
        a_st = stack(-prep["kk"][rw] * jnp.exp(cm - prep["log_decay"][rw])).astype(BF16)
        e_neg = jnp.exp(-cm)
        e_rem = jnp.exp(cm_last - cm)
        ar_scr[sa, c] = jnp.concatenate([a_st, stack(r_t).astype(BF16)], axis=0)
        kbh_scr[sa, c] = jnp.concatenate([(kc * e_neg).astype(BF16), (bvc * e_neg).astype(BF16)], axis=0)
        vst_scr[sa, c] = stack(v[rw]).astype(BF16)
        rt_scr[sa, c] = r_t
        kbar_scr[sa, c] = jnp.concatenate([(kc * e_rem).astype(BF16), (bvc * e_rem).astype(BF16)], axis=0)
        vb_scr[sa, c] = v[rw].astype(BF16)
        plp_scr[sa, c] = jnp.broadcast_to(jnp.exp(cm_last), (SUBLANE, lanes))

    def prep_step():
        phase = prep["phase"]
        prep["phase"] = phase + 1
        if phase == 0:
            prep_tile_a()
        elif phase == 1:
            prep_tile_b()
        else:
            for _ in range(-(-n_chunks // (n_sq + 1))):
                prep_chunk()

    chain = {"s": jnp.where(t3 == 0, 0.0, s_scr[p3]), "c": 0, "y": []}

    def chain_step():
        c = chain["c"]
        if c >= n_chunks:
            return
        s0 = chain["s"]
        s0b = s0.astype(BF16)
        chain["y"].append(y_scr[sa, c * L:(c + 1) * L, :] + _mm_nt(q_scr[sa, c], s0b))
        chain["s"] = s0 * pl_scr[sa, c][0:1, :] + _mm(s0b, gm_scr[sa, c]) + cm_scr[sa, c]
        chain["c"] = c + 1

    def fill():
        prep_step()
        chain_step()

    ar_st = [ar_scr[sb, c] for c in cs]
    kb_h = [kbh_scr[sb, c] for c in cs]
    v_st = [vst_scr[sb, c] for c in cs]
    prod = [_mm_nt(ar_st[c], kb_h[c]) for c in cs]
    fill()
    swapped = [pltpu.roll(x, L, axis=1) for x in prod]
    prod_k = [jnp.where(first_half, prod[c], swapped[c]) for c in cs]
    prod_b = [jnp.where(first_half, swapped[c], prod[c]) for c in cs]
    a_ak = [jnp.where(m_strict, x[:SL], 0.0).astype(BF16) for x in prod_k]
    a_rk = [jnp.where(m_incl, x[SL:], 0.0).astype(BF16) for x in prod_k]
    a_ab = [jnp.where(m_strict, x[:SL], 0.0) for x in prod_b]
    a_rb = [jnp.where(m_incl, x[SL:], 0.0).astype(BF16) for x in prod_b]

    xp = [x.astype(BF16) for x in a_ab]
    tinv = [eye + x for x in a_ab]
    def group(fn):
        out = []
        for c in cs:
            out.append(fn(c))
            if n_chunks > n_sq + 3 and c == n_chunks // 2 - 1:
                chain_step()
        return out

    for _ in range(n_sq):
        xp = group(lambda c: _mm(xp[c], xp[c]).astype(BF16))
        tinv = group(lambda c: tinv[c] + _mm(tinv[c], xp[c]))
        fill()
    tinv = [x.astype(BF16) for x in tinv]

    x0 = [_mm(a_ak[c], v_st[c]).astype(BF16) for c in cs]
    fill()
    wu = [_mm(tinv[c], jnp.concatenate([ar_st[c][:SL], x0[c]], axis=1)) for c in cs]
    fill()
    yq = [_mm(a_rb[c], wu[c]) for c in cs]
    y0 = [_mm(a_rk[c], v_st[c]) for c in cs]
    while prep["c"] < n_chunks or chain["c"] < n_chunks:
        fill()
    cmats, gmats = [], []
    for c in cs:
        kbar = kbar_scr[sb, c]
        vu = jnp.concatenate([vb_scr[sb, c], unstack(wu[c][:, lanes:]).astype(BF16)], axis=0)
        cmats.append(jnp.where(same_head, _mm_tn(vu, kbar), 0.0))
        gmats.append(jnp.where(same_head, _mm_tn(unstack(wu[c][:, :lanes]), kbar[L:]), 0.0).astype(BF16))

    s_scr[p3] = chain["s"]
    y = jnp.concatenate(chain["y"], axis=0)
    inv_n = 1.0 / RW_HEAD
    mean = _mm_exact_rhs(y, e_head) * inv_n
    dy = y - mean
    var = _mm(dy * dy, e_head) * inv_n
    yn = dy * lax.rsqrt(var + GN_EPS) * pvb_ref[_PV_LG:_PV_LG + 1, :] + pvb_ref[_PV_LB:_PV_LB + 1, :]
    o_ref[0] = ((yn + bonus_scr[s13_r]) * g_scr[s13_r]).astype(o_ref.dtype)

    for c, rw in enumerate(rows):
        q_scr[sb, c] = (rt_scr[sb, c] + unstack(yq[c][:, :lanes])).astype(BF16)
        y_scr[sb, rw, :] = unstack(yq[c][:, lanes:] + y0[c])
        cm_scr[sb, c] = cmats[c]
        gm_scr[sb, c] = gmats[c]
        pl_scr[sb, c] = plp_scr[sb, c]


def _rwkv(z, z_lora, col_r, col_k, col_v, pvec, mu_l, w2p, a2p, g2p, width, side, side_lead,
          t_rows=1024):
    bsz, s, _ = z.shape
    lora_w = z_lora.shape[2]
    n_p = width // LANE
    L = RW_CHUNK
    n_chunks = t_rows // L
    n_t = s // t_rows
    n_tiles = bsz * n_t * n_p
    n_steps = n_tiles + 2
    assert RW_PACK * L == LANE and lora_w == 3 * LANE

    def tile(i, lag):
        it = jnp.clip(i - lag, 0, n_tiles - 1)
        return it // (n_t * n_p), (it // n_p) % n_t, it % n_p

    def z_spec(col):
        def idx(i):
            b, t, p = tile(i, 0)
            return b, t, col + p
        return pl.BlockSpec((1, t_rows, LANE), idx)

    def zl_idx(i):
        b, t, _ = tile(i, 0)
        return b, t, 0

    side_in, side_out = _slab_specs(side, side_lead, 0, ROW_CHUNK, n_steps, lambda i: i)
    kern = functools.partial(_rwkv_kernel, chunk=L, n_t=n_t, n_p=n_p, n_tiles=n_tiles)

    def per_chunk(rows_, dtype):
        return pltpu.VMEM((2, n_chunks, rows_, LANE), dtype)

    return pl.pallas_call(
        kern,
        grid=(n_steps,),
        in_specs=[z_spec(col_r), z_spec(col_k), z_spec(col_v),
                  pl.BlockSpec((1, t_rows, lora_w), zl_idx),
                  pl.BlockSpec((_PV_ROWS, LANE), lambda i: (0, tile(i, 0)[2])),
                  pl.BlockSpec((_PV_ROWS, LANE), lambda i: (0, tile(i, 2)[2])),
                  pl.BlockSpec((1, lora_w), lambda i: (0, 0)),
                  pl.BlockSpec((LANE, LANE), lambda i: (0, tile(i, 0)[2])),
                  pl.BlockSpec((LANE, LANE), lambda i: (0, tile(i, 0)[2])),
                  pl.BlockSpec((2 * LANE, LANE), lambda i: (0, tile(i, 0)[2])),
                  side_in],
        out_specs=(pl.BlockSpec((1, t_rows, LANE), lambda i: tile(i, 2)), side_out),
        out_shape=(jax.ShapeDtypeStruct((bsz, s, width), BF16),
                   jax.ShapeDtypeStruct(side.shape[1:], BF16)),
        scratch_shapes=[pltpu.VMEM((n_p, LANE, LANE), F32),
                        pltpu.VMEM((n_p, SUBLANE, LANE), F32),
                        pltpu.VMEM((SUBLANE, lora_w), F32),
                        pltpu.VMEM((t_rows, LANE), BF16),
                        pltpu.VMEM((t_rows, LANE), BF16),
                        pltpu.VMEM((t_rows, LANE), BF16),
                        pltpu.VMEM((t_rows, lora_w - LANE), BF16),
                        per_chunk(4 * L, BF16),
                        per_chunk(2 * L, BF16),
                        per_chunk(2 * L, BF16),
                        per_chunk(L, F32),
                        per_chunk(2 * L, BF16),
                        per_chunk(L, BF16),
                        per_chunk(SUBLANE, F32),
                        per_chunk(L, BF16),
                        pltpu.VMEM((2, t_rows, LANE), F32),
                        per_chunk(LANE, BF16),
                        per_chunk(LANE, F32),
                        per_chunk(SUBLANE, F32),
                        pltpu.VMEM((3, t_rows, LANE), F32),
                        pltpu.VMEM((3, t_rows, LANE), F32)],
        compiler_params=_params(("arbitrary",)),
        name="rwkv7",
    )(z, z, z, z_lora, pvec, pvec, mu_l, w2p, a2p, g2p, side)


def _moba_kernel(q_ref, k_ref, v_ref, side_ref, o_ref, side_o_ref, ka_scr, vt_scr, s_scr, p_scr, *,
                 n_heads, heads_per_step):
    side_o_ref[...] = side_ref[...].astype(BF16)
    heads = [_moba_head(q_ref, k_ref, v_ref, o_ref, ka_scr.at[j], vt_scr.at[j], s_scr.at[j],
                        p_scr.at[j], j, pl.program_id(1) * heads_per_step + j, n_heads, heads_per_step)
             for j in range(heads_per_step)]
    nb = q_ref.shape[1] // MOBA_BLOCK
    ahead = MOBA_DEPTH - 1
    pending = {q: [scores(q) for scores, _ in heads] for q in range(min(ahead, nb))}
    for qb in range(nb):
        if qb + ahead < nb:
            pending[qb + ahead] = [scores(qb + ahead) for scores, _ in heads]
        for (_, attend), pend in zip(heads, pending.pop(qb)):
            attend(qb, *pend)


def _moba_head(q_ref, k_ref, v_ref, o_ref, ka_scr, vt_scr, s_scr, p_scr, j, h, n_heads, heads_per_step):
    s_len = q_ref.shape[1]
    dh = q_ref.shape[2] // heads_per_step
    cols = slice(j * dh, (j + 1) * dh)
    blk = MOBA_BLOCK
    nb = s_len // blk
    log2e = 1.4426950408889634
    scale = dh ** -0.5 * log2e
    neg_inf = -jnp.inf

    def slope_row(width):
        return log2e * jnp.exp(jnp.full((1, width), -8.0 / n_heads * 0.6931471805599453, F32)
                               * (h + 1).astype(F32))

    lane = lax.broadcasted_iota(jnp.int32, (blk, dh), 1)
    bias = slope_row(dh) * lax.broadcasted_iota(jnp.int32, (blk, dh), 0).astype(F32)
    extra = jnp.zeros((blk, dh), F32)
    for col in range(3):
        part = bias.astype(BF16).astype(F32)
        extra = jnp.where(lane == col, part, extra)
        bias = bias - part
    extra = extra.astype(BF16)
    ones_cols = jnp.where(lane < 3, 1.0, 0.0).astype(BF16)

    kmean = []
    for j in range(nb):
        rows = slice(j * blk, (j + 1) * blk)
        k_j = k_ref[0, rows, cols]
        kmean.append(jnp.mean(k_j.astype(F32), axis=0, keepdims=True))
        ka_scr[rows, :dh] = k_j
        ka_scr[rows, dh:] = extra
        vt_scr[:dh, rows] = v_ref[0, rows, cols].astype(F32).T.astype(BF16)
    sub = lax.broadcasted_iota(jnp.int32, (vt_scr.shape[0] - dh, s_len), 0)
    vt_scr[dh:, :] = jnp.where(sub == 0, 1.0, 0.0).astype(BF16)
    kmean = jnp.concatenate(kmean, axis=0)
    kmean_parts = []
    for _ in range(3):
        part = kmean.astype(BF16)
        kmean_parts.append(part)
        kmean = kmean - part.astype(F32)

    slope = slope_row(blk)
    causal = (lax.broadcasted_iota(jnp.int32, (blk, blk), 1)
              >= lax.broadcasted_iota(jnp.int32, (blk, blk), 0))
    blk_id = lax.broadcasted_iota(jnp.int32, (nb, 1), 0)

    def scores(qb):
        q = q_ref[0, qb * blk:(qb + 1) * blk, cols]
        q_aug = jnp.concatenate([(q.astype(F32) * scale).astype(BF16), ones_cols], axis=1)
        gate = sum(lax.dot_general(part, q, NT_DIMS, preferred_element_type=F32)
                   for part in kmean_parts)
        past = blk_id < qb
        offs = []
        m = None
        for n in range(qb + 1):
            t = lax.dot_general(ka_scr[n * blk:(n + 1) * blk, :], q_aug, NT_DIMS,
                                preferred_element_type=F32)
            if n == qb:
                t = jnp.where(causal, t, neg_inf)
                off = jnp.zeros((1, blk), F32)
            else:
                g_n = gate[n:n + 1, :]
                beats = past & ((gate > g_n) | ((gate == g_n) & (blk_id < n)))
                rank = jnp.sum(jnp.where(beats, 1.0, 0.0), axis=0, keepdims=True)
                off = jnp.where(rank < float(MOBA_TOPK), slope * float((n - qb) * blk), neg_inf)
            s_scr[qb % MOBA_DEPTH, n] = t
            offs.append(off)
            cmax = jnp.max(t, axis=0, keepdims=True) + off
            m = cmax if m is None else jnp.maximum(m, cmax)
        return m, offs

    def attend(qb, m, offs):
        for n in range(qb + 1):
            p = jnp.exp2(s_scr[qb % MOBA_DEPTH, n] - (m - offs[n]))
            p_scr[qb % MOBA_DEPTH, n * blk:(n + 1) * blk, :] = p.astype(BF16)
        kk = (qb + 1) * blk
        acc = jnp.dot(vt_scr[:, :kk], p_scr[qb % MOBA_DEPTH, :kk, :], preferred_element_type=F32)
        o_ref[0, qb * blk:(qb + 1) * blk, cols] = (acc[:dh] / acc[dh:dh + 1]).T.astype(o_ref.dtype)

    return scores, attend


def _moba(qkv, n_heads, side, side_lead, heads_per_step=2):
    bsz, s, d3 = qkv.shape
    d = d3 // 3
    dh = d // n_heads
    blk = MOBA_BLOCK
    nb = s // blk
    n_hp = n_heads // heads_per_step
    wdt = heads_per_step * dh
    side_in, side_out = _slab_specs(side, side_lead, 0, ROW_CHUNK, bsz * n_hp,
                                    lambda b, h: b * n_hp + h)
    kern = functools.partial(_moba_kernel, n_heads=n_heads, heads_per_step=heads_per_step)
    return pl.pallas_call(
        kern,
        grid=(bsz, n_hp),
        in_specs=[pl.BlockSpec((1, s, wdt), lambda b, h: (b, 0, h)),
                  pl.BlockSpec((1, s, wdt), lambda b, h: (b, 0, n_hp + h)),
                  pl.BlockSpec((1, s, wdt), lambda b, h: (b, 0, 2 * n_hp + h)),
                  side_in],
        out_specs=(pl.BlockSpec((1, s, wdt), lambda b, h: (b, 0, h)), side_out),
        out_shape=(jax.ShapeDtypeStruct((bsz, s, d), BF16),
                   jax.ShapeDtypeStruct(side.shape[1:], BF16)),
        scratch_shapes=[pltpu.VMEM((heads_per_step, s, 2 * dh), BF16),
                        pltpu.VMEM((heads_per_step, dh + 16, s), BF16),
                        pltpu.VMEM((heads_per_step, MOBA_DEPTH, nb, blk, blk), F32),
                        pltpu.VMEM((heads_per_step, MOBA_DEPTH, s, blk), BF16)],
        compiler_params=_params(("arbitrary", "arbitrary")),
        name="moba",
    )(qkv, qkv, qkv, side)


def _pad_cols(w, n):
    return jnp.pad(w, ((0, 0), (0, n - w.shape[1])))


def _pad_rows(w, n, before=0):
    return jnp.pad(w, ((before, n - before - w.shape[0]), (0, 0)))


def kernel(x, c, w_ada, b_ada, g_pre_mix, g_post_mix, g_pre_ffn, g_post_ffn, w_ffn_in, w_ffn_out,
           w_in_ab, w_out_ab, a_v_gain, a_v_bias, a_w_s, a_b_s, b_mu, b_w0, b_w2, b_a0, b_a2, b_g2,
           b_k_k, b_k_a, b_r_k, b_lnx_gain, b_lnx_bias, w_qkv, w_o):
    bsz, s, d = x.shape
    depth = w_ada.shape[0]
    a_width = a_v_gain.shape[1]
    b_width = b_w0.shape[1]
    n_lw = b_w2.shape[1]
    n_la = b_a2.shape[1]
    n_lg = b_g2.shape[1]
    n_heads = d // ATT_HEAD
    assert s % MOBA_BLOCK == 0 and s % 1024 == 0
    assert n_lw + n_la <= LANE and n_lg <= 2 * LANE

    w_in_t = jnp.swapaxes(w_in_ab, 1, 2)
    mod, w_in_h0 = _ada_mod(c, w_ada, b_ada, w_in_t, 0)

    w_qkv_h = None
    for layer in range(depth):
        mod3 = mod[layer].reshape(bsz, 1, 6 * d)
        i = layer // 2
        if layer % 2 == 0:
            lora_w = 3 * LANE
            n_main = 2 * a_width + 3 * b_width
            w_t = w_in_h0 if i == 0 else w_in_t[i]
            w_tail = _pad_rows(w_t[n_main:], lora_w)
            z, z_lora, w_out_h = _norm_mm(x, g_pre_mix[layer], mod3, 1, 0, w_t, BF16, w_out_ab, i,
                                          w_tail=w_tail, transposed=True, tn=n_main // 4,
                                          name="in_proj_ab")
            y_a = _mixer_a(z, a_v_gain[i], a_v_bias[i], a_w_s[i], a_b_s[i], a_width)

            mu = b_mu[i]
            pvec = jnp.stack([mu[0:b_width], mu[b_width:2 * b_width], mu[2 * b_width:3 * b_width],
                              b_w0[i], b_a0[i], b_k_k[i], b_k_a[i], b_r_k[i].reshape(-1),
                              b_lnx_gain[i], b_lnx_bias[i]])
            pvec = _pad_rows(pvec, _PV_ROWS)
            mu_l = _pad_cols(mu[3 * b_width:].reshape(1, -1), lora_w)
            w2p = _pad_rows(b_w2[i], LANE)
            a2p = _pad_rows(b_a2[i], LANE, before=n_lw)
            g2p = _pad_rows(b_g2[i], 2 * LANE)
            cb = 2 * a_width // LANE
            nb_w = b_width // LANE
            y_b, w_ffn_in_h = _rwkv(z, z_lora, cb, cb + nb_w, cb + 2 * nb_w, pvec, mu_l, w2p, a2p,
                                    g2p, b_width, w_ffn_in, layer)
            x, w_ffn_out_h = _out_proj(y_a, y_b, 0, 0, w_out_h, x, mod3, g_post_mix[layer],
                                       w_ffn_out, layer)
        else:
            w_q = w_qkv[i] if w_qkv_h is None else w_qkv_h
            qkv, w_o_h = _norm_mm(x, g_pre_mix[layer], mod3, 1, 0, w_q, BF16, w_o, i,
                                  tn=d, name="qkv_proj")
            o, w_ffn_in_h = _moba(qkv, n_heads, w_ffn_in, layer)
            x, w_ffn_out_h = _out_proj(o, o, 0, 1, w_o_h, x, mod3, g_post_mix[layer],
                                       w_ffn_out, layer)
        if layer + 1 < depth and (layer + 1) % 2 == 1:
            x, w_qkv_h = _ffn(x, g_pre_ffn[layer], g_post_ffn[layer], mod3, w_ffn_in_h, w_ffn_out_h,
                              w_qkv, (layer + 1) // 2)
        else:
            x = _ffn(x, g_pre_ffn[layer], g_post_ffn[layer], mod3, w_ffn_in_h, w_ffn_out_h)
            w_qkv_h = None
    return x
```

```python
import functools

import jax
import jax.numpy as jnp
from jax import lax
from jax.experimental import pallas as pl
from jax.experimental.pallas import tpu as pltpu

F32 = jnp.float32
BF16 = jnp.bfloat16

NORM_EPS = 1e-6
LN_EPS = 1e-5
GN_EPS = 64e-5

LANE = 128
SUBLANE = 8
A_GROUPS = 8
A_CHUNK = 128
RW_HEAD = 64
RW_CHUNK = 64
RW_PACK = 2
MOBA_BLOCK = 256
MOBA_TOPK = 3
MOBA_DEPTH = 3
ATT_HEAD = 128

NT_DIMS = (((1,), (1,)), ((), ()))
TN_DIMS = (((0,), (0,)), ((), ()))

VMEM_LIMIT = 56 * 1024 * 1024


def _params(sem):
    return pltpu.CompilerParams(dimension_semantics=sem, vmem_limit_bytes=VMEM_LIMIT)


def _slab_specs(side, lead, axis, unit, n_steps, flat_index):
    total = side.shape[1 + axis]
    n_slab = max(n for n in range(1, n_steps + 1) if total % (n * unit) == 0)
    shape = list(side.shape[1:])
    shape[axis] = total // n_slab

    def idx(*grid):
        slab = jnp.minimum(flat_index(*grid), n_slab - 1)
        return (slab, 0) if axis == 0 else (0, slab)

    return (pl.BlockSpec((None,) + tuple(shape), lambda *g: (lead,) + idx(*g)),
            pl.BlockSpec(tuple(shape), idx))


ROW_CHUNK = 16
ROW_UNROLL = 8

def _for_row_chunks(n_rows, fn):
    def body(i, carry):
        fn(pl.ds(pl.multiple_of(i * ROW_CHUNK, ROW_CHUNK), ROW_CHUNK))
        return carry
    lax.fori_loop(0, n_rows // ROW_CHUNK, body, 0, unroll=ROW_UNROLL)


def _modulated_norm(x_ref, g_ref, sc_ref, sh_ref, gm_scr, h_scr):
    gm_scr[...] = g_ref[...] * (1.0 + sc_ref[0])

    def rows(r):
        x = x_ref[0, r, :]
        ms = jnp.mean(x * x, axis=-1, keepdims=True)
        h_scr[r, :] = (x * lax.rsqrt(ms + NORM_EPS) * gm_scr[...] + sh_ref[0]).astype(BF16)

    _for_row_chunks(h_scr.shape[0], rows)


def _post_norm_residual(y_scr, x_ref, gt_ref, gpost_ref, gm_scr, o_ref):
    gm_scr[...] = gt_ref[0] * gpost_ref[...]

    def rows(r):
        y = y_scr[r, :]
        ms = jnp.mean(y * y, axis=-1, keepdims=True)
        o_ref[0, r, :] = x_ref[0, r, :] + y * lax.rsqrt(ms + NORM_EPS) * gm_scr[...]

    _for_row_chunks(o_ref.shape[1], rows)


def _ada_kernel(c_ref, w_ref, b_ref, side_ref, o_ref, side_o_ref):
    c = c_ref[...]
    cond = (c * jax.nn.sigmoid(c)).astype(BF16)
    o_ref[0] = jnp.dot(cond, w_ref[0].astype(BF16), preferred_element_type=F32) + b_ref[0]
    side_o_ref[...] = side_ref[...].astype(BF16)


def _ada_mod(c, w_ada, b_ada, side, side_lead):
    depth, d, n = w_ada.shape
    bsz = c.shape[0]
    bp = -(-bsz // SUBLANE) * SUBLANE
    c_p = jnp.pad(c, ((0, bp - bsz), (0, 0)))
    tn = 1024
    nj = n // tn
    side_in, side_out = _slab_specs(side, side_lead, 0, ROW_CHUNK, depth * nj, lambda l, j: l * nj + j)
    out, side_h = pl.pallas_call(
        _ada_kernel,
        grid=(depth, nj),
        in_specs=[pl.BlockSpec((bp, d), lambda l, j: (0, 0)),
                  pl.BlockSpec((1, d, tn), lambda l, j: (l, 0, j)),
                  pl.BlockSpec((1, 1, tn), lambda l, j: (l, 0, j)),
                  side_in],
        out_specs=(pl.BlockSpec((1, bp, tn), lambda l, j: (l, 0, j)), side_out),
        out_shape=(jax.ShapeDtypeStruct((depth, bp, n), F32),
                   jax.ShapeDtypeStruct(side.shape[1:], BF16)),
        compiler_params=_params(("arbitrary", "arbitrary")),
        name="ada_mod",
    )(c_p, w_ada, b_ada.reshape(depth, 1, n), side)
    return out[:, :bsz], side_h


def _norm_mm_kernel(*refs, has_tail, transposed):
    if has_tail:
        (x_ref, g_ref, sc_ref, sh_ref, w_ref, wt_ref, side_ref, o_ref, ot_ref, side_o_ref,
         h_scr, gm_scr) = refs
    else:
        x_ref, g_ref, sc_ref, sh_ref, w_ref, side_ref, o_ref, side_o_ref, h_scr, gm_scr = refs
    j = pl.program_id(2)

    @pl.when(j == 0)
    def _():
        _modulated_norm(x_ref, g_ref, sc_ref, sh_ref, gm_scr, h_scr)

    def project(wr, out_ref):
        w = wr[...].astype(BF16)
        if transposed:
            y = lax.dot_general(h_scr[...], w, NT_DIMS, preferred_element_type=F32)
        else:
            y = jnp.dot(h_scr[...], w, preferred_element_type=F32)
        out_ref[0] = y.astype(out_ref.dtype)

    project(w_ref, o_ref)
    if has_tail:
        pl.when(j == pl.num_programs(2) - 1)(lambda: project(wt_ref, ot_ref))
    side_o_ref[...] = side_ref[...].astype(BF16)


def _norm_mm(x, gain, mod3, sc_idx, sh_idx, w, out_dtype, side, side_lead, w_tail=None,
             transposed=False, tm=1024, tn=1024, name="norm_mm"):
    bsz, s, d = x.shape
    n_w = w.shape[0] if transposed else w.shape[1]
    nm, nj = s // tm, n_w // tn
    n = nj * tn
    side_in, side_out = _slab_specs(side, side_lead, 1, LANE, bsz * nm * nj,
                                    lambda b, m, j: (b * nm + m) * nj + j)
    if transposed:
        w_specs = [pl.BlockSpec((tn, d), lambda b, m, j: (j, 0))]
    else:
        w_specs = [pl.BlockSpec((d, tn), lambda b, m, j: (0, j))]
    w_args = [w]
    out_specs = [pl.BlockSpec((1, tm, tn), lambda b, m, j: (b, m, j))]
    out_shape = [jax.ShapeDtypeStruct((bsz, s, n), out_dtype)]
    if w_tail is not None:
        n_tail = w_tail.shape[0] if transposed else w_tail.shape[1]
        w_specs.append(pl.BlockSpec(w_tail.shape, lambda b, m, j: (0, 0), pipeline_mode=pl.Buffered(1)))
        w_args.append(w_tail)
        out_specs.append(pl.BlockSpec((1, tm, n_tail), lambda b, m, j: (b, m, 0)))
        out_shape.append(jax.ShapeDtypeStruct((bsz, s, n_tail), out_dtype))
    return pl.pallas_call(
        functools.partial(_norm_mm_kernel, has_tail=w_tail is not None, transposed=transposed),
        grid=(bsz, nm, nj),
        in_specs=[pl.BlockSpec((1, tm, d), lambda b, m, j: (b, m, 0)),
                  pl.BlockSpec((1, d), lambda b, m, j: (0, 0)),
                  pl.BlockSpec((1, 1, d), lambda b, m, j: (b, 0, sc_idx)),
                  pl.BlockSpec((1, 1, d), lambda b, m, j: (b, 0, sh_idx)),
                  *w_specs,
                  side_in],
        out_specs=(*out_specs, side_out),
        out_shape=(*out_shape, jax.ShapeDtypeStruct(side.shape[1:], BF16)),
        scratch_shapes=[pltpu.VMEM((tm, d), BF16), pltpu.VMEM((1, d), F32)],
        compiler_params=_params(("arbitrary", "arbitrary", "arbitrary")),
        name=name,
    )(x, gain.reshape(1, d), mod3, mod3, *w_args, side)


def _ffn_kernel(*refs, has_side):
    if has_side:
        (x_ref, gpre_ref, sc_ref, sh_ref, gt_ref, gpost_ref, wg_ref, wu_ref, wo_ref, side_ref,
         o_ref, side_o_ref, h_scr, gm_scr, rs_scr) = refs
        side_o_ref[...] = side_ref[...].astype(BF16)
    else:
        (x_ref, gpre_ref, sc_ref, sh_ref, gt_ref, gpost_ref, wg_ref, wu_ref, wo_ref,
         o_ref, h_scr, gm_scr, rs_scr) = refs
    f = pl.program_id(2)

    @pl.when((pl.program_id(0) == 0) & (pl.program_id(1) == 0) & (f == 0))
    def _():
        o_ref[0] = jnp.zeros(o_ref.shape[1:], F32)

    @pl.when(f == 0)
    def _():
        _modulated_norm(x_ref, gpre_ref, sc_ref, sh_ref, gm_scr, h_scr)

    h = h_scr[...]
    g = jnp.dot(h, wg_ref[...], preferred_element_type=F32)
    u = jnp.dot(h, wu_ref[...], preferred_element_type=F32)
    a = (g * jax.nn.sigmoid(g) * u).astype(BF16)
    o_ref[0] = (jnp.where(f == 0, 0.0, o_ref[0])
                + jnp.dot(a, wo_ref[...], preferred_element_type=F32))

    @pl.when(f == pl.num_programs(2) - 1)
    def _():
        gm_scr[...] = gt_ref[0] * gpost_ref[...]

        def row_scale(r):
            y = o_ref[0, r, :]
            ms = jnp.mean(y * y, axis=-1, keepdims=True)
            rs_scr[r, :] = jnp.broadcast_to(lax.rsqrt(ms + NORM_EPS), (ROW_CHUNK, LANE))

        def finish(r):
            o_ref[0, r, :] = x_ref[0, r, :] + o_ref[0, r, :] * rs_scr[r, :][:, :1] * gm_scr[...]

        _for_row_chunks(o_ref.shape[1], row_scale)
        _for_row_chunks(o_ref.shape[1], finish)


def _ffn(x, gpre, gpost, mod3, w_in, w_out, side=None, side_lead=0, tm=1024, tf=512):
    bsz, s, d = x.shape
    fh = w_out.shape[0]
    nf = fh // tf
    nm = s // tm
    out_spec = pl.BlockSpec((1, tm, d), lambda b, m, f: (b, m, 0), pipeline_mode=pl.Buffered(1))
    out_shape = jax.ShapeDtypeStruct((bsz, s, d), F32)
    side_specs, side_args = [], []
    if side is not None:
        side_in, side_out = _slab_specs(side, side_lead, 1, LANE, bsz * nm * nf,
                                        lambda b, m, f: (b * nm + m) * nf + f)
        side_specs, side_args = [side_in], [side]
        out_spec = (out_spec, side_out)
        out_shape = (out_shape, jax.ShapeDtypeStruct(side.shape[1:], BF16))
    return pl.pallas_call(
        functools.partial(_ffn_kernel, has_side=side is not None),
        grid=(bsz, nm, nf),
        in_specs=[pl.BlockSpec((1, tm, d), lambda b, m, f: (b, m, 0)),
                  pl.BlockSpec((1, d), lambda b, m, f: (0, 0)),
                  pl.BlockSpec((1, 1, d), lambda b, m, f: (b, 0, 4)),
                  pl.BlockSpec((1, 1, d), lambda b, m, f: (b, 0, 3)),
                  pl.BlockSpec((1, 1, d), lambda b, m, f: (b, 0, 5)),
                  pl.BlockSpec((1, d), lambda b, m, f: (0, 0)),
                  pl.BlockSpec((d, tf), lambda b, m, f: (0, f)),
                  pl.BlockSpec((d, tf), lambda b, m, f: (0, nf + f)),
                  pl.BlockSpec((tf, d), lambda b, m, f: (f, 0)),
                  *side_specs],
        out_specs=out_spec,
        out_shape=out_shape,
        scratch_shapes=[pltpu.VMEM((tm, d), BF16), pltpu.VMEM((1, d), F32), pltpu.VMEM((tm, LANE), F32)],
        compiler_params=_params(("arbitrary", "arbitrary", "arbitrary")),
        name="ffn",
    )(x, gpre.reshape(1, d), mod3, mod3, mod3, gpost.reshape(1, d), w_in, w_in, w_out, *side_args)


def _out_proj_kernel(a0_ref, a1_ref, w0_ref, w1_ref, x_ref, gt_ref, gpost_ref, side_ref,
                     o_ref, side_o_ref, gm_scr, y_scr):
    side_o_ref[...] = side_ref[...].astype(BF16)
    y_scr[...] = (jnp.dot(a0_ref[0], w0_ref[...], preferred_element_type=F32)
                  + jnp.dot(a1_ref[0], w1_ref[...], preferred_element_type=F32))
    _post_norm_residual(y_scr, x_ref, gt_ref, gpost_ref, gm_scr, o_ref)


def _out_proj(a0, a1, col0, col1, w, x, mod3, gpost, side, side_lead, tm=512):
    bsz, s, d = x.shape
    kh = w.shape[0] // 2
    nm = s // tm
    side_in, side_out = _slab_specs(side, side_lead, 0, ROW_CHUNK, bsz * nm, lambda b, m: b * nm + m)
    return pl.pallas_call(
        _out_proj_kernel,
        grid=(bsz, nm),
        in_specs=[pl.BlockSpec((1, tm, kh), lambda b, m: (b, m, col0)),
                  pl.BlockSpec((1, tm, kh), lambda b, m: (b, m, col1)),
                  pl.BlockSpec((kh, d), lambda b, m: (0, 0)),
                  pl.BlockSpec((kh, d), lambda b, m: (1, 0)),
                  pl.BlockSpec((1, tm, d), lambda b, m: (b, m, 0)),
                  pl.BlockSpec((1, 1, d), lambda b, m: (b, 0, 2)),
                  pl.BlockSpec((1, d), lambda b, m: (0, 0)),
                  side_in],
        out_specs=(pl.BlockSpec((1, tm, d), lambda b, m: (b, m, 0)), side_out),
        out_shape=(jax.ShapeDtypeStruct((bsz, s, d), F32),
                   jax.ShapeDtypeStruct(side.shape[1:], BF16)),
        scratch_shapes=[pltpu.VMEM((1, d), F32), pltpu.VMEM((tm, d), F32)],
        compiler_params=_params(("arbitrary", "arbitrary")),
        name="out_proj",
    )(a0, a1, w, w, x, mod3, gpost.reshape(1, d), side)


def _mixer_a_kernel(z_ref, vg_ref, vb_ref, ws_ref, bst_ref, o_ref, wm_scr):
    ch = ws_ref.shape[1]

    @pl.when((pl.program_id(0) == 0) & (pl.program_id(1) == 0))
    def _():
        causal = (lax.broadcasted_iota(jnp.int32, (ch, ch), 0)
                  >= lax.broadcasted_iota(jnp.int32, (ch, ch), 1))
        for g in range(A_GROUPS):
            wm_scr[g] = jnp.where(causal, ws_ref[g], 0.0).astype(BF16)

    for c in range(z_ref.shape[1] // ch):
        rows = slice(c * ch, (c + 1) * ch)
        z = jax.nn.gelu(z_ref[0, rows, :].astype(F32))
        wdt = z.shape[1] // 2
        u = z[:, :wdt]
        v = z[:, wdt:]
        mu = jnp.mean(v, axis=-1, keepdims=True)
        dv = v - mu
        var = jnp.mean(dv * dv, axis=-1, keepdims=True)
        vn = (dv * lax.rsqrt(var + LN_EPS) * vg_ref[...] + vb_ref[...]).astype(BF16)
        gd = wdt // A_GROUPS
        for g in range(A_GROUPS):
            sv = jnp.dot(wm_scr[g], vn[:, g * gd:(g + 1) * gd], preferred_element_type=F32)
            sv = sv + bst_ref[:, g:g + 1]
            o_ref[0, rows, g * gd:(g + 1) * gd] = (u[:, g * gd:(g + 1) * gd] * sv).astype(o_ref.dtype)


def _mixer_a(z, v_gain, v_bias, w_s, b_s, width, chunks_per_step=4):
    bsz, s, _ = z.shape
    ch = A_CHUNK
    rows = ch * chunks_per_step
    return pl.pallas_call(
        _mixer_a_kernel,
        grid=(bsz, s // rows),
        in_specs=[pl.BlockSpec((1, rows, 2 * width), lambda b, c: (b, c, 0)),
                  pl.BlockSpec((1, width), lambda b, c: (0, 0)),
                  pl.BlockSpec((1, width), lambda b, c: (0, 0)),
                  pl.BlockSpec((A_GROUPS, ch, ch), lambda b, c: (0, 0, 0)),
                  pl.BlockSpec((ch, A_GROUPS), lambda b, c: (0, 0))],
        out_specs=pl.BlockSpec((1, rows, width), lambda b, c: (b, c, 0)),
        out_shape=jax.ShapeDtypeStruct((bsz, s, width), BF16),
        scratch_shapes=[pltpu.VMEM((A_GROUPS, ch, ch), BF16)],
        compiler_params=_params(("arbitrary", "arbitrary")),
        name="mixer_a",
    )(z, v_gain.reshape(1, width), v_bias.reshape(1, width), w_s, b_s.T)


_PV_MU_R, _PV_MU_K, _PV_MU_V, _PV_W0, _PV_A0, _PV_KK, _PV_KA, _PV_RK, _PV_LG, _PV_LB = range(10)
_PV_ROWS = 16


def _shift_lerp(x, prev_row, mu):
    rolled = pltpu.roll(x, 1, axis=0)
    first = lax.broadcasted_iota(jnp.int32, x.shape, 0) == 0
    xp = jnp.where(first, prev_row, rolled)
    return x + mu * (xp - x)


def _split_bf16(x):
    hi = x.astype(BF16)
    lo = (x - hi.astype(F32)).astype(BF16)
    return hi, lo


def _mm(x, y):
    return jnp.dot(x.astype(BF16), y.astype(BF16), preferred_element_type=F32)


def _mm_nt(x, y):
    return lax.dot_general(x.astype(BF16), y.astype(BF16), NT_DIMS, preferred_element_type=F32)


def _mm_tn(x, y):
    return lax.dot_general(x.astype(BF16), y.astype(BF16), TN_DIMS, preferred_element_type=F32)


def _mm_exact_rhs(x, e_bf16):
    xh, xl = _split_bf16(x)
    return (jnp.dot(xh, e_bf16, preferred_element_type=F32)
            + jnp.dot(xl, e_bf16, preferred_element_type=F32))


def _rwkv_kernel(zr_ref, zk_ref, zv_ref, zl_ref, pva_ref, pvb_ref, mul_ref, w2_ref, a2_ref, g2_ref,
                 side_ref, o_ref, side_o_ref,
                 s_scr, prev_scr, prevl_scr, th_hi_scr, th_lo_scr, xw_scr, sg_scr,
                 ar_scr, kbh_scr, vst_scr, rt_scr, kbar_scr, vb_scr, plp_scr,
                 q_scr, y_scr, gm_scr, cm_scr, pl_scr, bonus_scr, g_scr, *, chunk, n_t, n_p, n_tiles):
    i = pl.program_id(0)
    t_rows = zr_ref.shape[1]
    lanes = zr_ref.shape[2]
    lora = xw_scr.shape[1] + sg_scr.shape[1]
    L = chunk
    SL = RW_PACK * L
    n_chunks = t_rows // L
    side_o_ref[...] = side_ref[...].astype(BF16)

    i1 = jnp.minimum(i, n_tiles - 1)
    i3 = jnp.maximum(i - 2, 0)
    t1, p1 = (i1 // n_p) % n_t, i1 % n_p
    t3, p3 = (i3 // n_p) % n_t, i3 % n_p
    sa = i % 2
    sb = 1 - sa
    s13_w = i % 3
    s13_r = (i + 1) % 3

    @pl.when(i == 0)
    def _():
        for ref in (s_scr, prev_scr, prevl_scr, ar_scr, kbh_scr, vst_scr, rt_scr, kbar_scr, vb_scr,
                    plp_scr, q_scr, y_scr, gm_scr, cm_scr, pl_scr, bonus_scr, g_scr):
            ref[...] = jnp.zeros_like(ref)

    @pl.when(p1 == 0)
    def _():
        zl = zl_ref[0][:, :lora].astype(F32)
        prev = jnp.where(t1 == 0, 0.0, prevl_scr[0:1, :lora])
        zls = _shift_lerp(zl, prev, mul_ref[:, :lora])
        prevl_scr[0:1, :lora] = zl[t_rows - 1:t_rows, :]
        x_wa = zls[:, :LANE]
        th_hi, th_lo = _split_bf16(jnp.tanh(x_wa))
        th_hi_scr[...] = th_hi
        th_lo_scr[...] = th_lo
        xw_scr[...] = x_wa.astype(BF16)
        sg_scr[...] = jax.nn.sigmoid(zls[:, LANE:]).astype(BF16)

    def pva(r):
        return pva_ref[r:r + 1, :]

    li = lax.broadcasted_iota(jnp.int32, (lanes, lanes), 0) // RW_HEAD
    lj = lax.broadcasted_iota(jnp.int32, (lanes, lanes), 1) // RW_HEAD
    same_head = li == lj
    e_head = jnp.where(same_head, 1.0, 0.0).astype(BF16)
    ti = lax.broadcasted_iota(jnp.int32, (L, L), 0)
    tj = lax.broadcasted_iota(jnp.int32, (L, L), 1)
    tri = jnp.where(ti >= tj, 1.0, 0.0).astype(BF16)
    si = lax.broadcasted_iota(jnp.int32, (SL, SL), 0)
    sj = lax.broadcasted_iota(jnp.int32, (SL, SL), 1)
    same_blk = (si // L) == (sj // L)
    m_strict = same_blk & (si > sj)
    m_incl = same_blk & (si >= sj)
    eye = jnp.where(si == sj, 1.0, 0.0)
    lane_head = lax.broadcasted_iota(jnp.int32, (1, lanes), 1) // RW_HEAD
    first_half = lax.broadcasted_iota(jnp.int32, (1, 2 * L), 1) < L
    n_sq = max(L.bit_length() - 2, 0)
    cs = range(n_chunks)
    rows = [slice(c * L, (c + 1) * L) for c in cs]

    def stack(x):
        return jnp.concatenate([jnp.where(lane_head == h, x, 0.0) for h in range(RW_PACK)], axis=0)

    def unstack(x):
        out = x[0:L]
        for h in range(1, RW_PACK):
            out = out + x[h * L:(h + 1) * L]
        return out

    zr = zr_ref[0].astype(F32)
    zk = zk_ref[0].astype(F32)
    zv = zv_ref[0].astype(F32)
    prev = jnp.where(t1 == 0, 0.0, prev_scr[p1])
    r = _shift_lerp(zr, prev[0:1, :], pva(_PV_MU_R))
    k = _shift_lerp(zk, prev[1:2, :], pva(_PV_MU_K))
    v = _shift_lerp(zv, prev[2:3, :], pva(_PV_MU_V))
    prev_scr[p1] = jnp.concatenate([zr[t_rows - 1:t_rows, :], zk[t_rows - 1:t_rows, :],
                                    zv[t_rows - 1:t_rows, :], jnp.zeros((SUBLANE - 3, lanes), F32)],
                                   axis=0)
    prep = {"c": 0, "phase": 0}

    def prep_tile_a():
        w2_hi, w2_lo = _split_bf16(w2_ref[...])
        th_hi = th_hi_scr[...]
        w_pre = (pva(_PV_W0) + jnp.dot(th_hi, w2_hi, preferred_element_type=F32)
                 + jnp.dot(th_lo_scr[...], w2_hi, preferred_element_type=F32)
                 + jnp.dot(th_hi, w2_lo, preferred_element_type=F32))
        t = -w_pre
        softplus = jnp.maximum(t, 0.0) + jnp.log1p(jnp.exp(-jnp.abs(t)))
        prep["log_decay"] = -jnp.exp(-softplus - 0.5)
        prep["a"] = jax.nn.sigmoid(pva(_PV_A0) + _mm(xw_scr[...], a2_ref[...]))
        g_scr[s13_w] = _mm(sg_scr[...], g2_ref[...])

    def prep_tile_b():
        a = prep["a"]
        kk = k * pva(_PV_KK)
        kk = kk / jnp.maximum(jnp.sqrt(_mm(kk * kk, e_head)), 1e-12)
        kn = k * (1.0 + (a - 1.0) * pva(_PV_KA))
        prep["kk"], prep["kn"], prep["bv"] = kk, kn, kk * a
        bonus_scr[s13_w] = _mm(r * kn * pva(_PV_RK), e_head) * v
        lw_hi, lw_lo = _split_bf16(prep["log_decay"])
        prep["cm"] = [jnp.dot(tri, lw_hi[rw], preferred_element_type=F32)
                      + jnp.dot(tri, lw_lo[rw], preferred_element_type=F32) for rw in rows]

    def prep_chunk():
        c = prep["c"]
        if c >= n_chunks:
            return
        prep["c"] = c + 1
        rw = rows[c]
        cm = prep["cm"][c]
        cm_last = cm[L - 1:L, :]
        kc, bvc = prep["kn"][rw], prep["bv"][rw]
        r_t = r[rw] * jnp.exp(cm)
        a_st = stack(-prep["kk"][rw] * jnp.exp(cm - prep["log_decay"][rw])).astype(BF16)
        e_neg = jnp.exp(-cm)
        e_rem = jnp.exp(cm_last - cm)
        ar_scr[sa, c] = jnp.concatenate([a_st, stack(r_t).astype(BF16)], axis=0)
        kbh_scr[sa, c] = jnp.concatenate([(kc * e_neg).astype(BF16), (bvc * e_neg).astype(BF16)], axis=0)
        vst_scr[sa, c] = stack(v[rw]).astype(BF16)
        rt_scr[sa, c] = r_t
        kbar_scr[sa, c] = jnp.concatenate([(kc * e_rem).astype(BF16), (bvc * e_rem).astype(BF16)], axis=0)
        vb_scr[sa, c] = v[rw].astype(BF16)
        plp_scr[sa, c] = jnp.broadcast_to(jnp.exp(cm_last), (SUBLANE, lanes))

    def prep_step():
        phase = prep["phase"]
        prep["phase"] = phase + 1
        if phase == 0:
            prep_tile_a()
        elif phase == 1:
            prep_tile_b()
        else:
            for _ in range(-(-n_chunks // (n_sq + 1))):
                prep_chunk()

    chain = {"s": jnp.where(t3 == 0, 0.0, s_scr[p3]), "c": 0, "y": []}

    def chain_step():
        c = chain["c"]
        if c >= n_chunks:
            return
        s0 = chain["s"]
        s0b = s0.astype(BF16)
        chain["y"].append(y_scr[sa, c * L:(c + 1) * L, :] + _mm_nt(q_scr[sa, c], s0b))
        chain["s"] = s0 * pl_scr[sa, c][0:1, :] + _mm(s0b, gm_scr[sa, c]) + cm_scr[sa, c]
        chain["c"] = c + 1

    def fill():
        prep_step()
        chain_step()

    ar_st = [ar_scr[sb, c] for c in cs]
    kb_h = [kbh_scr[sb, c] for c in cs]
    v_st = [vst_scr[sb, c] for c in cs]
    prod = [_mm_nt(ar_st[c], kb_h[c]) for c in cs]
    fill()
    swapped = [pltpu.roll(x, L, axis=1) for x in prod]
    prod_k = [jnp.where(first_half, prod[c], swapped[c]) for c in cs]
    prod_b = [jnp.where(first_half, swapped[c], prod[c]) for c in cs]
    a_ak = [jnp.where(m_strict, x[:SL], 0.0).astype(BF16) for x in prod_k]
    a_rk = [jnp.where(m_incl, x[SL:], 0.0).astype(BF16) for x in prod_k]
    a_ab = [jnp.where(m_strict, x[:SL], 0.0) for x in prod_b]
    a_rb = [jnp.where(m_incl, x[SL:], 0.0).astype(BF16) for x in prod_b]

    xp = [x.astype(BF16) for x in a_ab]
    tinv = [eye + x for x in a_ab]
    def group(fn):
        out = []
        for c in cs:
            out.append(fn(c))
            if n_chunks > n_sq + 3 and c == n_chunks // 2 - 1:
                chain_step()
        return out

    for _ in range(n_sq):
        xp = group(lambda c: _mm(xp[c], xp[c]).astype(BF16))
        tinv = group(lambda c: tinv[c] + _mm(tinv[c], xp[c]))
        fill()
    tinv = [x.astype(BF16) for x in tinv]

    x0 = [_mm(a_ak[c], v_st[c]).astype(BF16) for c in cs]
    fill()
    wu = [_mm(tinv[c], jnp.concatenate([ar_st[c][:SL], x0[c]], axis=1)) for c in cs]
    fill()
    yq = [_mm(a_rb[c], wu[c]) for c in cs]
    y0 = [_mm(a_rk[c], v_st[c]) for c in cs]
    while prep["c"] < n_chunks or chain["c"] < n_chunks:
        fill()
    cmats, gmats = [], []
    for c in cs:
        kbar = kbar_scr[sb, c]
        vu = jnp.concatenate([vb_scr[sb, c], unstack(wu[c][:, lanes:]).astype(BF16)], axis=0)
        cmats.append(jnp.where(same_head, _mm_tn(vu, kbar), 0.0))
        gmats.append(jnp.where(same_head, _mm_tn(unstack(wu[c][:, :lanes]), kbar[L:]), 0.0).astype(BF16))

    s_scr[p3] = chain["s"]
    y = jnp.concatenate(chain["y"], axis=0)
    inv_n = 1.0 / RW_HEAD
    mean = _mm_exact_rhs(y, e_head) * inv_n
    dy = y - mean
    var = _mm(dy * dy, e_head) * inv_n
    yn = dy * lax.rsqrt(var + GN_EPS) * pvb_ref[_PV_LG:_PV_LG + 1, :] + pvb_ref[_PV_LB:_PV_LB + 1, :]
    o_ref[0] = ((yn + bonus_scr[s13_r]) * g_scr[s13_r]).astype(o_ref.dtype)

    for c, rw in enumerate(rows):
        q_scr[sb, c] = (rt_scr[sb, c] + unstack(yq[c][:, :lanes])).astype(BF16)
        y_scr[sb, rw, :] = unstack(yq[c][:, lanes:] + y0[c])
        cm_scr[sb, c] = cmats[c]
        gm_scr[sb, c] = gmats[c]
        pl_scr[sb, c] = plp_scr[sb, c]


def _rwkv(z, z_lora, col_r, col_k, col_v, pvec, mu_l, w2p, a2p, g2p, width, side, side_lead,
          t_rows=1024):
    bsz, s, _ = z.shape
    lora_w = z_lora.shape[2]
    n_p = width // LANE
    L = RW_CHUNK
    n_chunks = t_rows // L
    n_t = s // t_rows
    n_tiles = bsz * n_t * n_p
    n_steps = n_tiles + 2
    assert RW_PACK * L == LANE and lora_w == 3 * LANE

    def tile(i, lag):
        it = jnp.clip(i - lag, 0, n_tiles - 1)
        return it // (n_t * n_p), (it // n_p) % n_t, it % n_p

    def z_spec(col):
        def idx(i):
            b, t, p = tile(i, 0)
            return b, t, col + p
        return pl.BlockSpec((1, t_rows, LANE), idx)

    def zl_idx(i):
        b, t, _ = tile(i, 0)
        return b, t, 0

    side_in, side_out = _slab_specs(side, side_lead, 0, ROW_CHUNK, n_steps, lambda i: i)
    kern = functools.partial(_rwkv_kernel, chunk=L, n_t=n_t, n_p=n_p, n_tiles=n_tiles)

    def per_chunk(rows_, dtype):
        return pltpu.VMEM((2, n_chunks, rows_, LANE), dtype)

    return pl.pallas_call(
        kern,
        grid=(n_steps,),
        in_specs=[z_spec(col_r), z_spec(col_k), z_spec(col_v),
                  pl.BlockSpec((1, t_rows, lora_w), zl_idx),
                  pl.BlockSpec((_PV_ROWS, LANE), lambda i: (0, tile(i, 0)[2])),
                  pl.BlockSpec((_PV_ROWS, LANE), lambda i: (0, tile(i, 2)[2])),
                  pl.BlockSpec((1, lora_w), lambda i: (0, 0)),
                  pl.BlockSpec((LANE, LANE), lambda i: (0, tile(i, 0)[2])),
                  pl.BlockSpec((LANE, LANE), lambda i: (0, tile(i, 0)[2])),
                  pl.BlockSpec((2 * LANE, LANE), lambda i: (0, tile(i, 0)[2])),
                  side_in],
        out_specs=(pl.BlockSpec((1, t_rows, LANE), lambda i: tile(i, 2)), side_out),
        out_shape=(jax.ShapeDtypeStruct((bsz, s, width), BF16),
                   jax.ShapeDtypeStruct(side.shape[1:], BF16)),
        scratch_shapes=[pltpu.VMEM((n_p, LANE, LANE), F32),
                        pltpu.VMEM((n_p, SUBLANE, LANE), F32),
                        pltpu.VMEM((SUBLANE, lora_w), F32),
                        pltpu.VMEM((t_rows, LANE), BF16),
                        pltpu.VMEM((t_rows, LANE), BF16),
                        pltpu.VMEM((t_rows, LANE), BF16),
                        pltpu.VMEM((t_rows, lora_w - LANE), BF16),
                        per_chunk(4 * L, BF16),
                        per_chunk(2 * L, BF16),
                        per_chunk(2 * L, BF16),
                        per_chunk(L, F32),
                        per_chunk(2 * L, BF16),
                        per_chunk(L, BF16),
                        per_chunk(SUBLANE, F32),
                        per_chunk(L, BF16),
                        pltpu.VMEM((2, t_rows, LANE), F32),
                        per_chunk(LANE, BF16),
                        per_chunk(LANE, F32),
                        per_chunk(SUBLANE, F32),
                        pltpu.VMEM((3, t_rows, LANE), F32),
                        pltpu.VMEM((3, t_rows, LANE), F32)],
        compiler_params=_params(("arbitrary",)),
        name="rwkv7",
    )(z, z, z, z_lora, pvec, pvec, mu_l, w2p, a2p, g2p, side)


def _moba_kernel(q_ref, k_ref, v_ref, side_ref, o_ref, side_o_ref, ka_scr, vt_scr, s_scr, p_scr, *,
                 n_heads, heads_per_step):
    side_o_ref[...] = side_ref[...].astype(BF16)
    heads = [_moba_head(q_ref, k_ref, v_ref, o_ref, ka_scr.at[j], vt_scr.at[j], s_scr.at[j],
                        p_scr.at[j], j, pl.program_id(1) * heads_per_step + j, n_heads, heads_per_step)
             for j in range(heads_per_step)]
    nb = q_ref.shape[1] // MOBA_BLOCK
    ahead = MOBA_DEPTH - 1
    pending = {q: [scores(q) for scores, _ in heads] for q in range(min(ahead, nb))}
    for qb in range(nb):
        if qb + ahead < nb:
            pending[qb + ahead] = [scores(qb + ahead) for scores, _ in heads]
        for (_, attend), pend in zip(heads, pending.pop(qb)):
            attend(qb, *pend)


def _moba_head(q_ref, k_ref, v_ref, o_ref, ka_scr, vt_scr, s_scr, p_scr, j, h, n_heads, heads_per_step):
    s_len = q_ref.shape[1]
    dh = q_ref.shape[2] // heads_per_step
    cols = slice(j * dh, (j + 1) * dh)
    blk = MOBA_BLOCK
    nb = s_len // blk
    log2e = 1.4426950408889634
    scale = dh ** -0.5 * log2e
    neg_inf = -jnp.inf

    def slope_row(width):
        return log2e * jnp.exp(jnp.full((1, width), -8.0 / n_heads * 0.6931471805599453, F32)
                               * (h + 1).astype(F32))

    lane = lax.broadcasted_iota(jnp.int32, (blk, dh), 1)
    bias = slope_row(dh) * lax.broadcasted_iota(jnp.int32, (blk, dh), 0).astype(F32)
    extra = jnp.zeros((blk, dh), F32)
    for col in range(3):
        part = bias.astype(BF16).astype(F32)
        extra = jnp.where(lane == col, part, extra)
        bias = bias - part
    extra = extra.astype(BF16)
    ones_cols = jnp.where(lane < 3, 1.0, 0.0).astype(BF16)

    kmean = []
    for j in range(nb):
        rows = slice(j * blk, (j + 1) * blk)
        k_j = k_ref[0, rows, cols]
        kmean.append(jnp.mean(k_j.astype(F32), axis=0, keepdims=True))
        ka_scr[rows, :dh] = k_j
        ka_scr[rows, dh:] = extra
        vt_scr[:dh, rows] = v_ref[0, rows, cols].astype(F32).T.astype(BF16)
    sub = lax.broadcasted_iota(jnp.int32, (vt_scr.shape[0] - dh, s_len), 0)
    vt_scr[dh:, :] = jnp.where(sub == 0, 1.0, 0.0).astype(BF16)
    kmean = jnp.concatenate(kmean, axis=0)
    kmean_parts = []
    for _ in range(3):
        part = kmean.astype(BF16)
        kmean_parts.append(part)
        kmean = kmean - part.astype(F32)

    slope = slope_row(blk)
    causal = (lax.broadcasted_iota(jnp.int32, (blk, blk), 1)
              >= lax.broadcasted_iota(jnp.int32, (blk, blk), 0))
    blk_id = lax.broadcasted_iota(jnp.int32, (nb, 1), 0)

    def scores(qb):
        q = q_ref[0, qb * blk:(qb + 1) * blk, cols]
        q_aug = jnp.concatenate([(q.astype(F32) * scale).astype(BF16), ones_cols], axis=1)
        gate = sum(lax.dot_general(part, q, NT_DIMS, preferred_element_type=F32)
                   for part in kmean_parts)
        past = blk_id < qb
        offs = []
        m = None
        for n in range(qb + 1):
            t = lax.dot_general(ka_scr[n * blk:(n + 1) * blk, :], q_aug, NT_DIMS,
                                preferred_element_type=F32)
            if n == qb:
                t = jnp.where(causal, t, neg_inf)
                off = jnp.zeros((1, blk), F32)
            else:
                g_n = gate[n:n + 1, :]
                beats = past & ((gate > g_n) | ((gate == g_n) & (blk_id < n)))
                rank = jnp.sum(jnp.where(beats, 1.0, 0.0), axis=0, keepdims=True)
                off = jnp.where(rank < float(MOBA_TOPK), slope * float((n - qb) * blk), neg_inf)
            s_scr[qb % MOBA_DEPTH, n] = t
            offs.append(off)
            cmax = jnp.max(t, axis=0, keepdims=True) + off
            m = cmax if m is None else jnp.maximum(m, cmax)
        return m, offs

    def attend(qb, m, offs):
        for n in range(qb + 1):
            p = jnp.exp2(s_scr[qb % MOBA_DEPTH, n] - (m - offs[n]))
            p_scr[qb % MOBA_DEPTH, n * blk:(n + 1) * blk, :] = p.astype(BF16)
        kk = (qb + 1) * blk
        acc = jnp.dot(vt_scr[:, :kk], p_scr[qb % MOBA_DEPTH, :kk, :], preferred_element_type=F32)
        o_ref[0, qb * blk:(qb + 1) * blk, cols] = (acc[:dh] / acc[dh:dh + 1]).T.astype(o_ref.dtype)

    return scores, attend


def _moba(qkv, n_heads, side, side_lead, heads_per_step=2):
    bsz, s, d3 = qkv.shape
    d = d3 // 3
    dh = d // n_heads
    blk = MOBA_BLOCK
    nb = s // blk
    n_hp = n_heads // heads_per_step
    wdt = heads_per_step * dh
    side_in, side_out = _slab_specs(side, side_lead, 0, ROW_CHUNK, bsz * n_hp,
                                    lambda b, h: b * n_hp + h)
    kern = functools.partial(_moba_kernel, n_heads=n_heads, heads_per_step=heads_per_step)
    return pl.pallas_call(
        kern,
        grid=(bsz, n_hp),
        in_specs=[pl.BlockSpec((1, s, wdt), lambda b, h: (b, 0, h)),
                  pl.BlockSpec((1, s, wdt), lambda b, h: (b, 0, n_hp + h)),
                  pl.BlockSpec((1, s, wdt), lambda b, h: (b, 0, 2 * n_hp + h)),
                  side_in],
        out_specs=(pl.BlockSpec((1, s, wdt), lambda b, h: (b, 0, h)), side_out),
        out_shape=(jax.ShapeDtypeStruct((bsz, s, d), BF16),
                   jax.ShapeDtypeStruct(side.shape[1:], BF16)),
        scratch_shapes=[pltpu.VMEM((heads_per_step, s, 2 * dh), BF16),
                        pltpu.VMEM((heads_per_step, dh + 16, s), BF16),
                        pltpu.VMEM((heads_per_step, MOBA_DEPTH, nb, blk, blk), F32),
                        pltpu.VMEM((heads_per_step, MOBA_DEPTH, s, blk), BF16)],
        compiler_params=_params(("arbitrary", "arbitrary")),
        name="moba",
    )(qkv, qkv, qkv, side)


def _pad_cols(w, n):
    return jnp.pad(w, ((0, 0), (0, n - w.shape[1])))


def _pad_rows(w, n, before=0):
    return jnp.pad(w, ((before, n - before - w.shape[0]), (0, 0)))


def kernel(x, c, w_ada, b_ada, g_pre_mix, g_post_mix, g_pre_ffn, g_post_ffn, w_ffn_in, w_ffn_out,
           w_in_ab, w_out_ab, a_v_gain, a_v_bias, a_w_s, a_b_s, b_mu, b_w0, b_w2, b_a0, b_a2, b_g2,
           b_k_k, b_k_a, b_r_k, b_lnx_gain, b_lnx_bias, w_qkv, w_o):
    bsz, s, d = x.shape
    depth = w_ada.shape[0]
    a_width = a_v_gain.shape[1]
    b_width = b_w0.shape[1]
    n_lw = b_w2.shape[1]
    n_la = b_a2.shape[1]
    n_lg = b_g2.shape[1]
    n_heads = d // ATT_HEAD
    assert s % MOBA_BLOCK == 0 and s % 1024 == 0
    assert n_lw + n_la <= LANE and n_lg <= 2 * LANE

    w_in_t = jnp.swapaxes(w_in_ab, 1, 2)
    mod, w_in_h0 = _ada_mod(c, w_ada, b_ada, w_in_t, 0)

    w_qkv_h = None
    for layer in range(depth):
        mod3 = mod[layer].reshape(bsz, 1, 6 * d)
        i = layer // 2
        if layer % 2 == 0:
            lora_w = 3 * LANE
            n_main = 2 * a_width + 3 * b_width
            w_t = w_in_h0 if i == 0 else w_in_t[i]
            w_tail = _pad_rows(w_t[n_main:], lora_w)
            z, z_lora, w_out_h = _norm_mm(x, g_pre_mix[layer], mod3, 1, 0, w_t, BF16, w_out_ab, i,
                                          w_tail=w_tail, transposed=True, tn=n_main // 4,
                                          name="in_proj_ab")
            y_a = _mixer_a(z, a_v_gain[i], a_v_bias[i], a_w_s[i], a_b_s[i], a_width)

            mu = b_mu[i]
            pvec = jnp.stack([mu[0:b_width], mu[b_width:2 * b_width], mu[2 * b_width:3 * b_width],
                              b_w0[i], b_a0[i], b_k_k[i], b_k_a[i], b_r_k[i].reshape(-1),
                              b_lnx_gain[i], b_lnx_bias[i]])
            pvec = _pad_rows(pvec, _PV_ROWS)
            mu_l = _pad_cols(mu[3 * b_width:].reshape(1, -1), lora_w)
            w2p = _pad_rows(b_w2[i], LANE)
            a2p = _pad_rows(b_a2[i], LANE, before=n_lw)
            g2p = _pad_rows(b_g2[i], 2 * LANE)
            cb = 2 * a_width // LANE
            nb_w = b_width // LANE
            y_b, w_ffn_in_h = _rwkv(z, z_lora, cb, cb + nb_w, cb + 2 * nb_w, pvec, mu_l, w2p, a2p,
                                    g2p, b_width, w_ffn_in, layer)
            x, w_ffn_out_h = _out_proj(y_a, y_b, 0, 0, w_out_h, x, mod3, g_post_mix[layer],
                                       w_ffn_out, layer)
        else:
            w_q = w_qkv[i] if w_qkv_h is None else w_qkv_h
            qkv, w_o_h = _norm_mm(x, g_pre_mix[layer], mod3, 1, 0, w_q, BF16, w_o, i,
                                  tn=d, name="qkv_proj")
            o, w_ffn_in_h = _moba(qkv, n_heads, w_ffn_in, layer)
            x, w_ffn_out_h = _out_proj(o, o, 0, 1, w_o_h, x, mod3, g_post_mix[layer],
                                       w_ffn_out, layer)
        if layer + 1 < depth and (layer + 1) % 2 == 1:
            x, w_qkv_h = _ffn(x, g_pre_ffn[layer], g_post_ffn[layer], mod3, w_ffn_in_h, w_ffn_out_h,
                              w_qkv, (layer + 1) // 2)
        else:
            x = _ffn(x, g_pre_ffn[layer], g_post_ffn[layer], mod3, w_ffn_in_h, w_ffn_out_h)
            w_qkv_h = None
    return x
```

```python
import functools

import jax
import jax.numpy as jnp
from jax import lax
from jax.experimental import pallas as pl
from jax.experimental.pallas import tpu as pltpu

F32 = jnp.float32
BF16 = jnp.bfloat16

NORM_EPS = 1e-6
LN_EPS = 1e-5
GN_EPS = 64e-5

LANE = 128
SUBLANE = 8
A_GROUPS = 8
A_CHUNK = 128
RW_HEAD = 64
RW_CHUNK = 64
RW_PACK = 2
MOBA_BLOCK = 256
MOBA_TOPK = 3
MOBA_DEPTH = 3
ATT_HEAD = 128

NT_DIMS = (((1,), (1,)), ((), ()))
TN_DIMS = (((0,), (0,)), ((), ()))

VMEM_LIMIT = 56 * 1024 * 1024


def _params(sem):
    return pltpu.CompilerParams(dimension_semantics=sem, vmem_limit_bytes=VMEM_LIMIT)


def _slab_specs(side, lead, axis, unit, n_steps, flat_index):
    total = side.shape[1 + axis]
    n_slab = max(n for n in range(1, n_steps + 1) if total % (n * unit) == 0)
    shape = list(side.shape[1:])
    shape[axis] = total // n_slab

    def idx(*grid):
        slab = jnp.minimum(flat_index(*grid), n_slab - 1)
        return (slab, 0) if axis == 0 else (0, slab)

    return (pl.BlockSpec((None,) + tuple(shape), lambda *g: (lead,) + idx(*g)),
            pl.BlockSpec(tuple(shape), idx))


ROW_CHUNK = 16
ROW_UNROLL = 8

def _for_row_chunks(n_rows, fn):
    def body(i, carry):
        fn(pl.ds(pl.multiple_of(i * ROW_CHUNK, ROW_CHUNK), ROW_CHUNK))
        return carry
    lax.fori_loop(0, n_rows // ROW_CHUNK, body, 0, unroll=ROW_UNROLL)


def _modulated_norm(x_ref, g_ref, sc_ref, sh_ref, gm_scr, h_scr):
    gm_scr[...] = g_ref[...] * (1.0 + sc_ref[0])

    def rows(r):
        x = x_ref[0, r, :]
        ms = jnp.mean(x * x, axis=-1, keepdims=True)
        h_scr[r, :] = (x * lax.rsqrt(ms + NORM_EPS) * gm_scr[...] + sh_ref[0]).astype(BF16)

    _for_row_chunks(h_scr.shape[0], rows)


def _post_norm_residual(y_scr, x_ref, gt_ref, gpost_ref, gm_scr, o_ref):
    gm_scr[...] = gt_ref[0] * gpost_ref[...]

    def rows(r):
        y = y_scr[r, :]
        ms = jnp.mean(y * y, axis=-1, keepdims=True)
        o_ref[0, r, :] = x_ref[0, r, :] + y * lax.rsqrt(ms + NORM_EPS) * gm_scr[...]

    _for_row_chunks(o_ref.shape[1], rows)


def _ada_kernel(c_ref, w_ref, b_ref, side_ref, o_ref, side_o_ref):
    c = c_ref[...]
    cond = (c * jax.nn.sigmoid(c)).astype(BF16)
    o_ref[0] = jnp.dot(cond, w_ref[0].astype(BF16), preferred_element_type=F32) + b_ref[0]
    side_o_ref[...] = side_ref[...].astype(BF16)


def _ada_mod(c, w_ada, b_ada, side, side_lead):
    depth, d, n = w_ada.shape
    bsz = c.shape[0]
    bp = -(-bsz // SUBLANE) * SUBLANE
    c_p = jnp.pad(c, ((0, bp - bsz), (0, 0)))
    tn = 1024
    nj = n // tn
    side_in, side_out = _slab_specs(side, side_lead, 0, ROW_CHUNK, depth * nj, lambda l, j: l * nj + j)
    out, side_h = pl.pallas_call(
        _ada_kernel,
        grid=(depth, nj),
        in_specs=[pl.BlockSpec((bp, d), lambda l, j: (0, 0)),
                  pl.BlockSpec((1, d, tn), lambda l, j: (l, 0, j)),
                  pl.BlockSpec((1, 1, tn), lambda l, j: (l, 0, j)),
                  side_in],
        out_specs=(pl.BlockSpec((1, bp, tn), lambda l, j: (l, 0, j)), side_out),
        out_shape=(jax.ShapeDtypeStruct((depth, bp, n), F32),
                   jax.ShapeDtypeStruct(side.shape[1:], BF16)),
        compiler_params=_params(("arbitrary", "arbitrary")),
        name="ada_mod",
    )(c_p, w_ada, b_ada.reshape(depth, 1, n), side)
    return out[:, :bsz], side_h


def _norm_mm_kernel(*refs, has_tail, transposed):
    if has_tail:
        (x_ref, g_ref, sc_ref, sh_ref, w_ref, wt_ref, side_ref, o_ref, ot_ref, side_o_ref,
         h_scr, gm_scr) = refs
    else:
        x_ref, g_ref, sc_ref, sh_ref, w_ref, side_ref, o_ref, side_o_ref, h_scr, gm_scr = refs
    j = pl.program_id(2)

    @pl.when(j == 0)
    def _():
        _modulated_norm(x_ref, g_ref, sc_ref, sh_ref, gm_scr, h_scr)

    def project(wr, out_ref):
        w = wr[...].astype(BF16)
        if transposed:
            y = lax.dot_general(h_scr[...], w, NT_DIMS, preferred_element_type=F32)
        else:
            y = jnp.dot(h_scr[...], w, preferred_element_type=F32)
        out_ref[0] = y.astype(out_ref.dtype)

    project(w_ref, o_ref)
    if has_tail:
        pl.when(j == pl.num_programs(2) - 1)(lambda: project(wt_ref, ot_ref))
    side_o_ref[...] = side_ref[...].astype(BF16)


def _norm_mm(x, gain, mod3, sc_idx, sh_idx, w, out_dtype, side, side_lead, w_tail=None,
             transposed=False, tm=1024, tn=1024, name="norm_mm"):
    bsz, s, d = x.shape
    n_w = w.shape[0] if transposed else w.shape[1]
    nm, nj = s // tm, n_w // tn
    n = nj * tn
    side_in, side_out = _slab_specs(side, side_lead, 1, LANE, bsz * nm * nj,
                                    lambda b, m, j: (b * nm + m) * nj + j)
    if transposed:
        w_specs = [pl.BlockSpec((tn, d), lambda b, m, j: (j, 0))]
    else:
        w_specs = [pl.BlockSpec((d, tn), lambda b, m, j: (0, j))]
    w_args = [w]
    out_specs = [pl.BlockSpec((1, tm, tn), lambda b, m, j: (b, m, j))]
    out_shape = [jax.ShapeDtypeStruct((bsz, s, n), out_dtype)]
    if w_tail is not None:
        n_tail = w_tail.shape[0] if transposed else w_tail.shape[1]
        w_specs.append(pl.BlockSpec(w_tail.shape, lambda b, m, j: (0, 0), pipeline_mode=pl.Buffered(1)))
        w_args.append(w_tail)
        out_specs.append(pl.BlockSpec((1, tm, n_tail), lambda b, m, j: (b, m, 0)))
        out_shape.append(jax.ShapeDtypeStruct((bsz, s, n_tail), out_dtype))
    return pl.pallas_call(
        functools.partial(_norm_mm_kernel, has_tail=w_tail is not None, transposed=transposed),
        grid=(bsz, nm, nj),
        in_specs=[pl.BlockSpec((1, tm, d), lambda b, m, j: (b, m, 0)),
                  pl.BlockSpec((1, d), lambda b, m, j: (0, 0)),
                  pl.BlockSpec((1, 1, d), lambda b, m, j: (b, 0, sc_idx)),
                  pl.BlockSpec((1, 1, d), lambda b, m, j: (b, 0, sh_idx)),
                  *w_specs,
                  side_in],
        out_specs=(*out_specs, side_out),
        out_shape=(*out_shape, jax.ShapeDtypeStruct(side.shape[1:], BF16)),
        scratch_shapes=[pltpu.VMEM((tm, d), BF16), pltpu.VMEM((1, d), F32)],
        compiler_params=_params(("arbitrary", "arbitrary", "arbitrary")),
        name=name,
    )(x, gain.reshape(1, d), mod3, mod3, *w_args, side)


def _ffn_kernel(*refs, has_side):
    if has_side:
        (x_ref, gpre_ref, sc_ref, sh_ref, gt_ref, gpost_ref, wg_ref, wu_ref, wo_ref, side_ref,
         o_ref, side_o_ref, h_scr, gm_scr, rs_scr) = refs
        side_o_ref[...] = side_ref[...].astype(BF16)
    else:
        (x_ref, gpre_ref, sc_ref, sh_ref, gt_ref, gpost_ref, wg_ref, wu_ref, wo_ref,
         o_ref, h_scr, gm_scr, rs_scr) = refs
    f = pl.program_id(2)

    @pl.when((pl.program_id(0) == 0) & (pl.program_id(1) < 2) & (f == 0))
    def _():
        o_ref[0] = jnp.zeros(o_ref.shape[1:], F32)

    @pl.when(f == 0)
    def _():
        _modulated_norm(x_ref, gpre_ref, sc_ref, sh_ref, gm_scr, h_scr)

    h = h_scr[...]
    g = jnp.dot(h, wg_ref[...], preferred_element_type=F32)
    u = jnp.dot(h, wu_ref[...], preferred_element_type=F32)
    a = (g * jax.nn.sigmoid(g) * u).astype(BF16)
    o_ref[0] = (jnp.where(f == 0, 0.0, o_ref[0])
                + jnp.dot(a, wo_ref[...], preferred_element_type=F32))

    @pl.when(f == pl.num_programs(2) - 1)
    def _():
        gm_scr[...] = gt_ref[0] * gpost_ref[...]

        def row_scale(r):
            y = o_ref[0, r, :]
            ms = jnp.mean(y * y, axis=-1, keepdims=True)
            rs_scr[r, :] = jnp.broadcast_to(lax.rsqrt(ms + NORM_EPS), (ROW_CHUNK, LANE))

        def finish(r):
            o_ref[0, r, :] = x_ref[0, r, :] + o_ref[0, r, :] * rs_scr[r, :][:, :1] * gm_scr[...]

        _for_row_chunks(o_ref.shape[1], row_scale)
        _for_row_chunks(o_ref.shape[1], finish)


def _ffn(x, gpre, gpost, mod3, w_in, w_out, side=None, side_lead=0, tm=1024, tf=256):
    bsz, s, d = x.shape
    fh = w_out.shape[0]
    nf = fh // tf
    nm = s // tm
    out_spec = pl.BlockSpec((1, tm, d), lambda b, m, f: (b, m, 0))
    out_shape = jax.ShapeDtypeStruct((bsz, s, d), F32)
    side_specs, side_args = [], []
    if side is not None:
        side_in, side_out = _slab_specs(side, side_lead, 1, LANE, bsz * nm * nf,
                                        lambda b, m, f: (b * nm + m) * nf + f)
        side_specs, side_args = [side_in], [side]
        out_spec = (out_spec, side_out)
        out_shape = (out_shape, jax.ShapeDtypeStruct(side.shape[1:], BF16))
    return pl.pallas_call(
        functools.partial(_ffn_kernel, has_side=side is not None),
        grid=(bsz, nm, nf),
        in_specs=[pl.BlockSpec((1, tm, d), lambda b, m, f: (b, m, 0)),
                  pl.BlockSpec((1, d), lambda b, m, f: (0, 0)),
                  pl.BlockSpec((1, 1, d), lambda b, m, f: (b, 0, 4)),
                  pl.BlockSpec((1, 1, d), lambda b, m, f: (b, 0, 3)),
                  pl.BlockSpec((1, 1, d), lambda b, m, f: (b, 0, 5)),
                  pl.BlockSpec((1, d), lambda b, m, f: (0, 0)),
                  pl.BlockSpec((d, tf), lambda b, m, f: (0, f)),
                  pl.BlockSpec((d, tf), lambda b, m, f: (0, nf + f)),
                  pl.BlockSpec((tf, d), lambda b, m, f: (f, 0)),
                  *side_specs],
        out_specs=out_spec,
        out_shape=out_shape,
        scratch_shapes=[pltpu.VMEM((tm, d), BF16), pltpu.VMEM((1, d), F32), pltpu.VMEM((tm, LANE), F32)],
        compiler_params=_params(("arbitrary", "arbitrary", "arbitrary")),
        name="ffn",
    )(x, gpre.reshape(1, d), mod3, mod3, mod3, gpost.reshape(1, d), w_in, w_in, w_out, *side_args)


def _out_proj_kernel(a0_ref, a1_ref, w0_ref, w1_ref, x_ref, gt_ref, gpost_ref, side_ref,
                     o_ref, side_o_ref, gm_scr, y_scr):
    side_o_ref[...] = side_ref[...].astype(BF16)
    y_scr[...] = (jnp.dot(a0_ref[0], w0_ref[...], preferred_element_type=F32)
                  + jnp.dot(a1_ref[0], w1_ref[...], preferred_element_type=F32))
    _post_norm_residual(y_scr, x_ref, gt_ref, gpost_ref, gm_scr, o_ref)


def _out_proj(a0, a1, col0, col1, w, x, mod3, gpost, side, side_lead, tm=512):
    bsz, s, d = x.shape
    kh = w.shape[0] // 2
    nm = s // tm
    side_in, side_out = _slab_specs(side, side_lead, 0, ROW_CHUNK, bsz * nm, lambda b, m: b * nm + m)
    return pl.pallas_call(
        _out_proj_kernel,
        grid=(bsz, nm),
        in_specs=[pl.BlockSpec((1, tm, kh), lambda b, m: (b, m, col0)),
                  pl.BlockSpec((1, tm, kh), lambda b, m: (b, m, col1)),
                  pl.BlockSpec((kh, d), lambda b, m: (0, 0)),
                  pl.BlockSpec((kh, d), lambda b, m: (1, 0)),
                  pl.BlockSpec((1, tm, d), lambda b, m: (b, m, 0)),
                  pl.BlockSpec((1, 1, d), lambda b, m: (b, 0, 2)),
                  pl.BlockSpec((1, d), lambda b, m: (0, 0)),
                  side_in],
        out_specs=(pl.BlockSpec((1, tm, d), lambda b, m: (b, m, 0)), side_out),
        out_shape=(jax.ShapeDtypeStruct((bsz, s, d), F32),
                   jax.ShapeDtypeStruct(side.shape[1:], BF16)),
        scratch_shapes=[pltpu.VMEM((1, d), F32), pltpu.VMEM((tm, d), F32)],
        compiler_params=_params(("arbitrary", "arbitrary")),
        name="out_proj",
    )(a0, a1, w, w, x, mod3, gpost.reshape(1, d), side)


def _mixer_a_kernel(z_ref, vg_ref, vb_ref, ws_ref, bst_ref, o_ref, wm_scr):
    ch = ws_ref.shape[1]

    @pl.when((pl.program_id(0) == 0) & (pl.program_id(1) == 0))
    def _():
        causal = (lax.broadcasted_iota(jnp.int32, (ch, ch), 0)
                  >= lax.broadcasted_iota(jnp.int32, (ch, ch), 1))
        for g in range(A_GROUPS):
            wm_scr[g] = jnp.where(causal, ws_ref[g], 0.0).astype(BF16)

    for c in range(z_ref.shape[1] // ch):
        rows = slice(c * ch, (c + 1) * ch)
        z = jax.nn.gelu(z_ref[0, rows, :].astype(F32))
        wdt = z.shape[1] // 2
        u = z[:, :wdt]
        v = z[:, wdt:]
        mu = jnp.mean(v, axis=-1, keepdims=True)
        dv = v - mu
        var = jnp.mean(dv * dv, axis=-1, keepdims=True)
        vn = (dv * lax.rsqrt(var + LN_EPS) * vg_ref[...] + vb_ref[...]).astype(BF16)
        gd = wdt // A_GROUPS
        for g in range(A_GROUPS):
            sv = jnp.dot(wm_scr[g], vn[:, g * gd:(g + 1) * gd], preferred_element_type=F32)
            sv = sv + bst_ref[:, g:g + 1]
            o_ref[0, rows, g * gd:(g + 1) * gd] = (u[:, g * gd:(g + 1) * gd] * sv).astype(o_ref.dtype)


def _mixer_a(z, v_gain, v_bias, w_s, b_s, width, chunks_per_step=4):
    bsz, s, _ = z.shape
    ch = A_CHUNK
    rows = ch * chunks_per_step
    return pl.pallas_call(
        _mixer_a_kernel,
        grid=(bsz, s // rows),
        in_specs=[pl.BlockSpec((1, rows, 2 * width), lambda b, c: (b, c, 0)),
                  pl.BlockSpec((1, width), lambda b, c: (0, 0)),
                  pl.BlockSpec((1, width), lambda b, c: (0, 0)),
                  pl.BlockSpec((A_GROUPS, ch, ch), lambda b, c: (0, 0, 0)),
                  pl.BlockSpec((ch, A_GROUPS), lambda b, c: (0, 0))],
        out_specs=pl.BlockSpec((1, rows, width), lambda b, c: (b, c, 0)),
        out_shape=jax.ShapeDtypeStruct((bsz, s, width), BF16),
        scratch_shapes=[pltpu.VMEM((A_GROUPS, ch, ch), BF16)],
        compiler_params=_params(("arbitrary", "arbitrary")),
        name="mixer_a",
    )(z, v_gain.reshape(1, width), v_bias.reshape(1, width), w_s, b_s.T)


_PV_MU_R, _PV_MU_K, _PV_MU_V, _PV_W0, _PV_A0, _PV_KK, _PV_KA, _PV_RK, _PV_LG, _PV_LB = range(10)
_PV_ROWS = 16


def _shift_lerp(x, prev_row, mu):
    rolled = pltpu.roll(x, 1, axis=0)
    first = lax.broadcasted_iota(jnp.int32, x.shape, 0) == 0
    xp = jnp.where(first, prev_row, rolled)
    return x + mu * (xp - x)


def _split_bf16(x):
    hi = x.astype(BF16)
    lo = (x - hi.astype(F32)).astype(BF16)
    return hi, lo


def _mm(x, y):
    return jnp.dot(x.astype(BF16), y.astype(BF16), preferred_element_type=F32)


def _mm_nt(x, y):
    return lax.dot_general(x.astype(BF16), y.astype(BF16), NT_DIMS, preferred_element_type=F32)


def _mm_tn(x, y):
    return lax.dot_general(x.astype(BF16), y.astype(BF16), TN_DIMS, preferred_element_type=F32)


def _mm_exact_rhs(x, e_bf16):
    xh, xl = _split_bf16(x)
    return (jnp.dot(xh, e_bf16, preferred_element_type=F32)
            + jnp.dot(xl, e_bf16, preferred_element_type=F32))


def _rwkv_kernel(zr_ref, zk_ref, zv_ref, zl_ref, pva_ref, pvb_ref, mul_ref, w2_ref, a2_ref, g2_ref,
                 side_ref, o_ref, side_o_ref,
                 s_scr, prev_scr, prevl_scr, th_hi_scr, th_lo_scr, xw_scr, sg_scr,
                 ar_scr, kbh_scr, vst_scr, rt_scr, kbar_scr, vb_scr, plp_scr,
                 q_scr, y_scr, gm_scr, cm_scr, pl_scr, bonus_scr, g_scr, *, chunk, n_t, n_p, n_tiles):
    i = pl.program_id(0)
    t_rows = zr_ref.shape[1]
    lanes = zr_ref.shape[2]
    lora = xw_scr.shape[1] + sg_scr.shape[1]
    L = chunk
    SL = RW_PACK * L
    n_chunks = t_rows // L
    side_o_ref[...] = side_ref[...].astype(BF16)

    i1 = jnp.minimum(i, n_tiles - 1)
    i3 = jnp.maximum(i - 2, 0)
    t1, p1 = (i1 // n_p) % n_t, i1 % n_p
    t3, p3 = (i3 // n_p) % n_t, i3 % n_p
    sa = i % 2
    sb = 1 - sa
    s13_w = i % 3
    s13_r = (i + 1) % 3

    @pl.when(i == 0)
    def _():
        for ref in (s_scr, prev_scr, prevl_scr, ar_scr, kbh_scr, vst_scr, rt_scr, kbar_scr, vb_scr,
                    plp_scr, q_scr, y_scr, gm_scr, cm_scr, pl_scr, bonus_scr, g_scr):
            ref[...] = jnp.zeros_like(ref)

    @pl.when(p1 == 0)
    def _():
        zl = zl_ref[0][:, :lora].astype(F32)
        prev = jnp.where(t1 == 0, 0.0, prevl_scr[0:1, :lora])
        zls = _shift_lerp(zl, prev, mul_ref[:, :lora])
        prevl_scr[0:1, :lora] = zl[t_rows - 1:t_rows, :]
        x_wa = zls[:, :LANE]
        th_hi, th_lo = _split_bf16(jnp.tanh(x_wa))
        th_hi_scr[...] = th_hi
        th_lo_scr[...] = th_lo
        xw_scr[...] = x_wa.astype(BF16)
        sg_scr[...] = jax.nn.sigmoid(zls[:, LANE:]).astype(BF16)

    def pva(r):
        return pva_ref[r:r + 1, :]

    li = lax.broadcasted_iota(jnp.int32, (lanes, lanes), 0) // RW_HEAD
    lj = lax.broadcasted_iota(jnp.int32, (lanes, lanes), 1) // RW_HEAD
    same_head = li == lj
    e_head = jnp.where(same_head, 1.0, 0.0).astype(BF16)
    ti = lax.broadcasted_iota(jnp.int32, (L, L), 0)
    tj = lax.broadcasted_iota(jnp.int32, (L, L), 1)
    tri = jnp.where(ti >= tj, 1.0, 0.0).astype(BF16)
    si = lax.broadcasted_iota(jnp.int32, (SL, SL), 0)
    sj = lax.broadcasted_iota(jnp.int32, (SL, SL), 1)
    same_blk = (si // L) == (sj // L)
    m_strict = same_blk & (si > sj)
    m_incl = same_blk & (si >= sj)
    eye = jnp.where(si == sj, 1.0, 0.0)
    lane_head = lax.broadcasted_iota(jnp.int32, (1, lanes), 1) // RW_HEAD
    first_half = lax.broadcasted_iota(jnp.int32, (1, 2 * L), 1) < L
    n_sq = max(L.bit_length() - 2, 0)
    cs = range(n_chunks)
    rows = [slice(c * L, (c + 1) * L) for c in cs]

    def stack(x):
        return jnp.concatenate([jnp.where(lane_head == h, x, 0.0) for h in range(RW_PACK)], axis=0)

    def unstack(x):
        out = x[0:L]
        for h in range(1, RW_PACK):
            out = out + x[h * L:(h + 1) * L]
        return out

    zr = zr_ref[0].astype(F32)
    zk = zk_ref[0].astype(F32)
    zv = zv_ref[0].astype(F32)
    prev = jnp.where(t1 == 0, 0.0, prev_scr[p1])
    r = _shift_lerp(zr, prev[0:1, :], pva(_PV_MU_R))
    k = _shift_lerp(zk, prev[1:2, :], pva(_PV_MU_K))
    v = _shift_lerp(zv, prev[2:3, :], pva(_PV_MU_V))
    prev_scr[p1] = jnp.concatenate([zr[t_rows - 1:t_rows, :], zk[t_rows - 1:t_rows, :],
                                    zv[t_rows - 1:t_rows, :], jnp.zeros((SUBLANE - 3, lanes), F32)],
                                   axis=0)
    prep = {"c": 0, "phase": 0}

    def prep_tile_a():
        w2_hi, w2_lo = _split_bf16(w2_ref[...])
        th_hi = th_hi_scr[...]
        w_pre = (pva(_PV_W0) + jnp.dot(th_hi, w2_hi, preferred_element_type=F32)
                 + jnp.dot(th_lo_scr[...], w2_hi, preferred_element_type=F32)
                 + jnp.dot(th_hi, w2_lo, preferred_element_type=F32))
        t = -w_pre
        softplus = jnp.maximum(t, 0.0) + jnp.log1p(jnp.exp(-jnp.abs(t)))
        prep["log_decay"] = -jnp.exp(-softplus - 0.5)
        prep["a"] = jax.nn.sigmoid(pva(_PV_A0) + _mm(xw_scr[...], a2_ref[...]))
        g_scr[s13_w] = _mm(sg_scr[...], g2_ref[...])

    def prep_tile_b():
        a = prep["a"]
        kk = k * pva(_PV_KK)
        kk = kk / jnp.maximum(jnp.sqrt(_mm(kk * kk, e_head)), 1e-12)
        kn = k * (1.0 + (a - 1.0) * pva(_PV_KA))
        prep["kk"], prep["kn"], prep["bv"] = kk, kn, kk * a
        bonus_scr[s13_w] = _mm(r * kn * pva(_PV_RK), e_head) * v
        lw_hi, lw_lo = _split_bf16(prep["log_decay"])
        prep["cm"] = [jnp.dot(tri, lw_hi[rw], preferred_element_type=F32)
                      + jnp.dot(tri, lw_lo[rw], preferred_element_type=F32) for rw in rows]

    def prep_chunk():
        c = prep["c"]
        if c >= n_chunks:
            return
        prep["c"] = c + 1
        rw = rows[c]
        cm = prep["cm"][c]
        cm_last = cm[L - 1:L, :]
        kc, bvc = prep["kn"][rw], prep["bv"][rw]
        r_t = r[rw] * jnp.exp(cm)
        a_st = stack(-prep["kk"][rw] * jnp.exp(cm - prep["log_decay"][rw])).astype(BF16)
        e_neg = jnp.exp(-cm)
        e_rem = jnp.exp(cm_last - cm)
        ar_scr[sa, c] = jnp.concatenate([a_st, stack(r_t).astype(BF16)], axis=0)
        kbh_scr[sa, c] = jnp.concatenate([(kc * e_neg).astype(BF16), (bvc * e_neg).astype(BF16)], axis=0)
        vst_scr[sa, c] = stack(v[rw]).astype(BF16)
        rt_scr[sa, c] = r_t
        kbar_scr[sa, c] = jnp.concatenate([(kc * e_rem).astype(BF16), (bvc * e_rem).astype(BF16)], axis=0)
        vb_scr[sa, c] = v[rw].astype(BF16)
        plp_scr[sa, c] = jnp.broadcast_to(jnp.exp(cm_last), (SUBLANE, lanes))

    def prep_step():
        phase = prep["phase"]
        prep["phase"] = phase + 1
        if phase == 0:
            prep_tile_a()
        elif phase == 1:
            prep_tile_b()
        else:
            for _ in range(-(-n_chunks // (n_sq + 1))):
                prep_chunk()

    chain = {"s": jnp.where(t3 == 0, 0.0, s_scr[p3]), "c": 0, "y": []}

    def chain_step():
        c = chain["c"]
        if c >= n_chunks:
            return
        s0 = chain["s"]
        s0b = s0.astype(BF16)
        chain["y"].append(y_scr[sa, c * L:(c + 1) * L, :] + _mm_nt(q_scr[sa, c], s0b))
        chain["s"] = s0 * pl_scr[sa, c][0:1, :] + _mm(s0b, gm_scr[sa, c]) + cm_scr[sa, c]
        chain["c"] = c + 1

    def fill():
        prep_step()
        chain_step()

    ar_st = [ar_scr[sb, c] for c in cs]
    kb_h = [kbh_scr[sb, c] for c in cs]
    v_st = [vst_scr[sb, c] for c in cs]
    prod = [_mm_nt(ar_st[c], kb_h[c]) for c in cs]
    fill()
    swapped = [pltpu.roll(x, L, axis=1) for x in prod]
    prod_k = [jnp.where(first_half, prod[c], swapped[c]) for c in cs]
    prod_b = [jnp.where(first_half, swapped[c], prod[c]) for c in cs]
    a_ak = [jnp.where(m_strict, x[:SL], 0.0).astype(BF16) for x in prod_k]
    a_rk = [jnp.where(m_incl, x[SL:], 0.0).astype(BF16) for x in prod_k]
    a_ab = [jnp.where(m_strict, x[:SL], 0.0) for x in prod_b]
    a_rb = [jnp.where(m_incl, x[SL:], 0.0).astype(BF16) for x in prod_b]

    xp = [x.astype(BF16) for x in a_ab]
    tinv = [eye + x for x in a_ab]
    def group(fn):
        out = []
        for c in cs:
            out.append(fn(c))
            if n_chunks > n_sq + 3 and c == n_chunks // 2 - 1:
                chain_step()
        return out

    for _ in range(n_sq):
        xp = group(lambda c: _mm(xp[c], xp[c]).astype(BF16))
        tinv = group(lambda c: tinv[c] + _mm(tinv[c], xp[c]))
        fill()
    tinv = [x.astype(BF16) for x in tinv]

    x0 = [_mm(a_ak[c], v_st[c]).astype(BF16) for c in cs]
    fill()
    wu = [_mm(tinv[c], jnp.concatenate([ar_st[c][:SL], x0[c]], axis=1)) for c in cs]
    fill()
    yq = [_mm(a_rb[c], wu[c]) for c in cs]
    y0 = [_mm(a_rk[c], v_st[c]) for c in cs]
    while prep["c"] < n_chunks or chain["c"] < n_chunks:
        fill()
    cmats, gmats = [], []
    for c in cs:
        kbar = kbar_scr[sb, c]
        vu = jnp.concatenate([vb_scr[sb, c], unstack(wu[c][:, lanes:]).astype(BF16)], axis=0)
        cmats.append(jnp.where(same_head, _mm_tn(vu, kbar), 0.0))
        gmats.append(jnp.where(same_head, _mm_tn(unstack(wu[c][:, :lanes]), kbar[L:]), 0.0).astype(BF16))

    s_scr[p3] = chain["s"]
    y = jnp.concatenate(chain["y"], axis=0)
    inv_n = 1.0 / RW_HEAD
    mean = _mm_exact_rhs(y, e_head) * inv_n
    dy = y - mean
    var = _mm(dy * dy, e_head) * inv_n
    yn = dy * lax.rsqrt(var + GN_EPS) * pvb_ref[_PV_LG:_PV_LG + 1, :] + pvb_ref[_PV_LB:_PV_LB + 1, :]
    o_ref[0] = ((yn + bonus_scr[s13_r]) * g_scr[s13_r]).astype(o_ref.dtype)

    for c, rw in enumerate(rows):
        q_scr[sb, c] = (rt_scr[sb, c] + unstack(yq[c][:, :lanes])).astype(BF16)
        y_scr[sb, rw, :] = unstack(yq[c][:, lanes:] + y0[c])
        cm_scr[sb, c] = cmats[c]
        gm_scr[sb, c] = gmats[c]
        pl_scr[sb, c] = plp_scr[sb, c]


def _rwkv(z, z_lora, col_r, col_k, col_v, pvec, mu_l, w2p, a2p, g2p, width, side, side_lead,
          t_rows=1024):
    bsz, s, _ = z.shape
    lora_w = z_lora.shape[2]
    n_p = width // LANE
    L = RW_CHUNK
    n_chunks = t_rows // L
    n_t = s // t_rows
    n_tiles = bsz * n_t * n_p
    n_steps = n_tiles + 2
    assert RW_PACK * L == LANE and lora_w == 3 * LANE

    def tile(i, lag):
        it = jnp.clip(i - lag, 0, n_tiles - 1)
        return it // (n_t * n_p), (it // n_p) % n_t, it % n_p

    def z_spec(col):
        def idx(i):
            b, t, p = tile(i, 0)
            return b, t, col + p
        return pl.BlockSpec((1, t_rows, LANE), idx)

    def zl_idx(i):
        b, t, _ = tile(i, 0)
        return b, t, 0

    side_in, side_out = _slab_specs(side, side_lead, 0, ROW_CHUNK, n_steps, lambda i: i)
    kern = functools.partial(_rwkv_kernel, chunk=L, n_t=n_t, n_p=n_p, n_tiles=n_tiles)

    def per_chunk(rows_, dtype):
        return pltpu.VMEM((2, n_chunks, rows_, LANE), dtype)

    return pl.pallas_call(
        kern,
        grid=(n_steps,),
        in_specs=[z_spec(col_r), z_spec(col_k), z_spec(col_v),
                  pl.BlockSpec((1, t_rows, lora_w), zl_idx),
                  pl.BlockSpec((_PV_ROWS, LANE), lambda i: (0, tile(i, 0)[2])),
                  pl.BlockSpec((_PV_ROWS, LANE), lambda i: (0, tile(i, 2)[2])),
                  pl.BlockSpec((1, lora_w), lambda i: (0, 0)),
                  pl.BlockSpec((LANE, LANE), lambda i: (0, tile(i, 0)[2])),
                  pl.BlockSpec((LANE, LANE), lambda i: (0, tile(i, 0)[2])),
                  pl.BlockSpec((2 * LANE, LANE), lambda i: (0, tile(i, 0)[2])),
                  side_in],
        out_specs=(pl.BlockSpec((1, t_rows, LANE), lambda i: tile(i, 2)), side_out),
        out_shape=(jax.ShapeDtypeStruct((bsz, s, width), BF16),
                   jax.ShapeDtypeStruct(side.shape[1:], BF16)),
        scratch_shapes=[pltpu.VMEM((n_p, LANE, LANE), F32),
                        pltpu.VMEM((n_p, SUBLANE, LANE), F32),
                        pltpu.VMEM((SUBLANE, lora_w), F32),
                        pltpu.VMEM((t_rows, LANE), BF16),
                        pltpu.VMEM((t_rows, LANE), BF16),
                        pltpu.VMEM((t_rows, LANE), BF16),
                        pltpu.VMEM((t_rows, lora_w - LANE), BF16),
                        per_chunk(4 * L, BF16),
                        per_chunk(2 * L, BF16),
                        per_chunk(2 * L, BF16),
                        per_chunk(L, F32),
                        per_chunk(2 * L, BF16),
                        per_chunk(L, BF16),
                        per_chunk(SUBLANE, F32),
                        per_chunk(L, BF16),
                        pltpu.VMEM((2, t_rows, LANE), F32),
                        per_chunk(LANE, BF16),
                        per_chunk(LANE, F32),
                        per_chunk(SUBLANE, F32),
                        pltpu.VMEM((3, t_rows, LANE), F32),
                        pltpu.VMEM((3, t_rows, LANE), F32)],
        compiler_params=_params(("arbitrary",)),
        name="rwkv7",
    )(z, z, z, z_lora, pvec, pvec, mu_l, w2p, a2p, g2p, side)


def _moba_kernel(q_ref, k_ref, v_ref, side_ref, o_ref, side_o_ref, ka_scr, vt_scr, s_scr, p_scr, *,
                 n_heads, heads_per_step):
    side_o_ref[...] = side_ref[...].astype(BF16)
    heads = [_moba_head(q_ref, k_ref, v_ref, o_ref, ka_scr.at[j], vt_scr.at[j], s_scr.at[j],
                        p_scr.at[j], j, pl.program_id(1) * heads_per_step + j, n_heads, heads_per_step)
             for j in range(heads_per_step)]
    nb = q_ref.shape[1] // MOBA_BLOCK
    ahead = MOBA_DEPTH - 1
    pending = {q: [scores(q) for scores, _ in heads] for q in range(min(ahead, nb))}
    for qb in range(nb):
        if qb + ahead < nb:
            pending[qb + ahead] = [scores(qb + ahead) for scores, _ in heads]
        for (_, attend), pend in zip(heads, pending.pop(qb)):
            attend(qb, *pend)


def _moba_head(q_ref, k_ref, v_ref, o_ref, ka_scr, vt_scr, s_scr, p_scr, j, h, n_heads, heads_per_step):
    s_len = q_ref.shape[1]
    dh = q_ref.shape[2] // heads_per_step
    cols = slice(j * dh, (j + 1) * dh)
    blk = MOBA_BLOCK
    nb = s_len // blk
    log2e = 1.4426950408889634
    scale = dh ** -0.5 * log2e
    neg_inf = -jnp.inf

    def slope_row(width):
        return log2e * jnp.exp(jnp.full((1, width), -8.0 / n_heads * 0.6931471805599453, F32)
                               * (h + 1).astype(F32))

    lane = lax.broadcasted_iota(jnp.int32, (blk, dh), 1)
    bias = slope_row(dh) * lax.broadcasted_iota(jnp.int32, (blk, dh), 0).astype(F32)
    extra = jnp.zeros((blk, dh), F32)
    for col in range(3):
        part = bias.astype(BF16).astype(F32)
        extra = jnp.where(lane == col, part, extra)
        bias = bias - part
    extra = extra.astype(BF16)
    ones_cols = jnp.where(lane < 3, 1.0, 0.0).astype(BF16)

    kmean = []
    for j in range(nb):
        rows = slice(j * blk, (j + 1) * blk)
        k_j = k_ref[0, rows, cols]
        kmean.append(jnp.mean(k_j.astype(F32), axis=0, keepdims=True))
        ka_scr[rows, :dh] = k_j
        ka_scr[rows, dh:] = extra
        vt_scr[:dh, rows] = v_ref[0, rows, cols].astype(F32).T.astype(BF16)
    sub = lax.broadcasted_iota(jnp.int32, (vt_scr.shape[0] - dh, s_len), 0)
    vt_scr[dh:, :] = jnp.where(sub == 0, 1.0, 0.0).astype(BF16)
    kmean = jnp.concatenate(kmean, axis=0)
    kmean_parts = []
    for _ in range(3):
        part = kmean.astype(BF16)
        kmean_parts.append(part)
        kmean = kmean - part.astype(F32)

    slope = slope_row(blk)
    causal = (lax.broadcasted_iota(jnp.int32, (blk, blk), 1)
              >= lax.broadcasted_iota(jnp.int32, (blk, blk), 0))
    blk_id = lax.broadcasted_iota(jnp.int32, (nb, 1), 0)

    def scores(qb):
        q = q_ref[0, qb * blk:(qb + 1) * blk, cols]
        q_aug = jnp.concatenate([(q.astype(F32) * scale).astype(BF16), ones_cols], axis=1)
        gate = sum(lax.dot_general(part, q, NT_DIMS, preferred_element_type=F32)
                   for part in kmean_parts)
        past = blk_id < qb
        offs = []
        m = None
        for n in range(qb + 1):
            t = lax.dot_general(ka_scr[n * blk:(n + 1) * blk, :], q_aug, NT_DIMS,
                                preferred_element_type=F32)
            if n == qb:
                t = jnp.where(causal, t, neg_inf)
                off = jnp.zeros((1, blk), F32)
            else:
                g_n = gate[n:n + 1, :]
                beats = past & ((gate > g_n) | ((gate == g_n) & (blk_id < n)))
                rank = jnp.sum(jnp.where(beats, 1.0, 0.0), axis=0, keepdims=True)
                off = jnp.where(rank < float(MOBA_TOPK), slope * float((n - qb) * blk), neg_inf)
            s_scr[qb % MOBA_DEPTH, n] = t
            offs.append(off)
            cmax = jnp.max(t, axis=0, keepdims=True) + off
            m = cmax if m is None else jnp.maximum(m, cmax)
        return m, offs

    def attend(qb, m, offs):
        for n in range(qb + 1):
            p = jnp.exp2(s_scr[qb % MOBA_DEPTH, n] - (m - offs[n]))
            p_scr[qb % MOBA_DEPTH, n * blk:(n + 1) * blk, :] = p.astype(BF16)
        kk = (qb + 1) * blk
        acc = jnp.dot(vt_scr[:, :kk], p_scr[qb % MOBA_DEPTH, :kk, :], preferred_element_type=F32)
        o_ref[0, qb * blk:(qb + 1) * blk, cols] = (acc[:dh] / acc[dh:dh + 1]).T.astype(o_ref.dtype)

    return scores, attend


def _moba(qkv, n_heads, side, side_lead, heads_per_step=2):
    bsz, s, d3 = qkv.shape
    d = d3 // 3
    dh = d // n_heads
    blk = MOBA_BLOCK
    nb = s // blk
    n_hp = n_heads // heads_per_step
    wdt = heads_per_step * dh
    side_in, side_out = _slab_specs(side, side_lead, 0, ROW_CHUNK, bsz * n_hp,
                                    lambda b, h: b * n_hp + h)
    kern = functools.partial(_moba_kernel, n_heads=n_heads, heads_per_step=heads_per_step)
    return pl.pallas_call(
        kern,
        grid=(bsz, n_hp),
        in_specs=[pl.BlockSpec((1, s, wdt), lambda b, h: (b, 0, h)),
                  pl.BlockSpec((1, s, wdt), lambda b, h: (b, 0, n_hp + h)),
                  pl.BlockSpec((1, s, wdt), lambda b, h: (b, 0, 2 * n_hp + h)),
                  side_in],
        out_specs=(pl.BlockSpec((1, s, wdt), lambda b, h: (b, 0, h)), side_out),
        out_shape=(jax.ShapeDtypeStruct((bsz, s, d), BF16),
                   jax.ShapeDtypeStruct(side.shape[1:], BF16)),
        scratch_shapes=[pltpu.VMEM((heads_per_step, s, 2 * dh), BF16),
                        pltpu.VMEM((heads_per_step, dh + 16, s), BF16),
                        pltpu.VMEM((heads_per_step, MOBA_DEPTH, nb, blk, blk), F32),
                        pltpu.VMEM((heads_per_step, MOBA_DEPTH, s, blk), BF16)],
        compiler_params=_params(("arbitrary", "arbitrary")),
        name="moba",
    )(qkv, qkv, qkv, side)


def _pad_cols(w, n):
    return jnp.pad(w, ((0, 0), (0, n - w.shape[1])))


def _pad_rows(w, n, before=0):
    return jnp.pad(w, ((before, n - before - w.shape[0]), (0, 0)))


def kernel(x, c, w_ada, b_ada, g_pre_mix, g_post_mix, g_pre_ffn, g_post_ffn, w_ffn_in, w_ffn_out,
           w_in_ab, w_out_ab, a_v_gain, a_v_bias, a_w_s, a_b_s, b_mu, b_w0, b_w2, b_a0, b_a2, b_g2,
           b_k_k, b_k_a, b_r_k, b_lnx_gain, b_lnx_bias, w_qkv, w_o):
    bsz, s, d = x.shape
    depth = w_ada.shape[0]
    a_width = a_v_gain.shape[1]
    b_width = b_w0.shape[1]
    n_lw = b_w2.shape[1]
    n_la = b_a2.shape[1]
    n_lg = b_g2.shape[1]
    n_heads = d // ATT_HEAD
    assert s % MOBA_BLOCK == 0 and s % 1024 == 0
    assert n_lw + n_la <= LANE and n_lg <= 2 * LANE

    w_in_t = jnp.swapaxes(w_in_ab, 1, 2)
    mod, w_in_h0 = _ada_mod(c, w_ada, b_ada, w_in_t, 0)

    w_qkv_h = None
    for layer in range(depth):
        mod3 = mod[layer].reshape(bsz, 1, 6 * d)
        i = layer // 2
        if layer % 2 == 0:
            lora_w = 3 * LANE
            n_main = 2 * a_width + 3 * b_width
            w_t = w_in_h0 if i == 0 else w_in_t[i]
            w_tail = _pad_rows(w_t[n_main:], lora_w)
            z, z_lora, w_out_h = _norm_mm(x, g_pre_mix[layer], mod3, 1, 0, w_t, BF16, w_out_ab, i,
                                          w_tail=w_tail, transposed=True, tn=n_main // 4,
                                          name="in_proj_ab")
            y_a = _mixer_a(z, a_v_gain[i], a_v_bias[i], a_w_s[i], a_b_s[i], a_width)

            mu = b_mu[i]
            pvec = jnp.stack([mu[0:b_width], mu[b_width:2 * b_width], mu[2 * b_width:3 * b_width],
                              b_w0[i], b_a0[i], b_k_k[i], b_k_a[i], b_r_k[i].reshape(-1),
                              b_lnx_gain[i], b_lnx_bias[i]])
            pvec = _pad_rows(pvec, _PV_ROWS)
            mu_l = _pad_cols(mu[3 * b_width:].reshape(1, -1), lora_w)
            w2p = _pad_rows(b_w2[i], LANE)
            a2p = _pad_rows(b_a2[i], LANE, before=n_lw)
            g2p = _pad_rows(b_g2[i], 2 * LANE)
            cb = 2 * a_width // LANE
            nb_w = b_width // LANE
            y_b, w_ffn_in_h = _rwkv(z, z_lora, cb, cb + nb_w, cb + 2 * nb_w, pvec, mu_l, w2p, a2p,
                                    g2p, b_width, w_ffn_in, layer)
            x, w_ffn_out_h = _out_proj(y_a, y_b, 0, 0, w_out_h, x, mod3, g_post_mix[layer],
                                       w_ffn_out, layer)
        else:
            w_q = w_qkv[i] if w_qkv_h is None else w_qkv_h
            qkv, w_o_h = _norm_mm(x, g_pre_mix[layer], mod3, 1, 0, w_q, BF16, w_o, i,
                                  tn=d, name="qkv_proj")
            o, w_ffn_in_h = _moba(qkv, n_heads, w_ffn_in, layer)
            x, w_ffn_out_h = _out_proj(o, o, 0, 1, w_o_h, x, mod3, g_post_mix[layer],
                                       w_ffn_out, layer)
        if layer + 1 < depth and (layer + 1) % 2 == 1:
            x, w_qkv_h = _ffn(x, g_pre_ffn[layer], g_post_ffn[layer], mod3, w_ffn_in_h, w_ffn_out_h,
                              w_qkv, (layer + 1) // 2)
        else:
            x = _ffn(x, g_pre_ffn[layer], g_post_ffn[layer], mod3, w_ffn_in_h, w_ffn_out_h)
            w_qkv_h = None
    return x
```

```python
import functools

import jax
import jax.numpy as jnp
from jax import lax
from jax.experimental import pallas as pl
from jax.experimental.pallas import tpu as pltpu

F32 = jnp.float32
BF16 = jnp.bfloat16

NORM_EPS = 1e-6
LN_EPS = 1e-5
GN_EPS = 64e-5

LANE = 128
SUBLANE = 8
A_GROUPS = 8
A_CHUNK = 128
RW_HEAD = 64
RW_CHUNK = 64
RW_PACK = 2
MOBA_BLOCK = 256
MOBA_TOPK = 3
MOBA_DEPTH = 3
ATT_HEAD = 128

NT_DIMS = (((1,), (1,)), ((), ()))
TN_DIMS = (((0,), (0,)), ((), ()))

VMEM_LIMIT = 56 * 1024 * 1024


def _params(sem):
    return pltpu.CompilerParams(dimension_semantics=sem, vmem_limit_bytes=VMEM_LIMIT)


def _slab_specs(side, lead, axis, unit, n_steps, flat_index):
    total = side.shape[1 + axis]
    n_slab = max(n for n in range(1, n_steps + 1) if total % (n * unit) == 0)
    shape = list(side.shape[1:])
    shape[axis] = total // n_slab

    def idx(*grid):
        slab = jnp.minimum(flat_index(*grid), n_slab - 1)
        return (slab, 0) if axis == 0 else (0, slab)

    return (pl.BlockSpec((None,) + tuple(shape), lambda *g: (lead,) + idx(*g)),
            pl.BlockSpec(tuple(shape), idx))


ROW_CHUNK = 16
ROW_UNROLL = 8

def _for_row_chunks(n_rows, fn):
    def body(i, carry):
        fn(pl.ds(pl.multiple_of(i * ROW_CHUNK, ROW_CHUNK), ROW_CHUNK))
        return carry
    lax.fori_loop(0, n_rows // ROW_CHUNK, body, 0, unroll=ROW_UNROLL)


def _modulated_norm(x_ref, g_ref, sc_ref, sh_ref, gm_scr, h_scr):
    gm_scr[...] = g_ref[...] * (1.0 + sc_ref[0])

    def rows(r):
        x = x_ref[0, r, :]
        ms = jnp.mean(x * x, axis=-1, keepdims=True)
        h_scr[r, :] = (x * lax.rsqrt(ms + NORM_EPS) * gm_scr[...] + sh_ref[0]).astype(BF16)

    _for_row_chunks(h_scr.shape[0], rows)


def _post_norm_residual(y_scr, x_ref, gt_ref, gpost_ref, gm_scr, o_ref):
    gm_scr[...] = gt_ref[0] * gpost_ref[...]

    def rows(r):
        y = y_scr[r, :]
        ms = jnp.mean(y * y, axis=-1, keepdims=True)
        o_ref[0, r, :] = x_ref[0, r, :] + y * lax.rsqrt(ms + NORM_EPS) * gm_scr[...]

    _for_row_chunks(o_ref.shape[1], rows)


def _ada_kernel(c_ref, w_ref, b_ref, side_ref, o_ref, side_o_ref):
    c = c_ref[...]
    cond = (c * jax.nn.sigmoid(c)).astype(BF16)
    o_ref[0] = jnp.dot(cond, w_ref[0].astype(BF16), preferred_element_type=F32) + b_ref[0]
    side_o_ref[...] = side_ref[...].astype(BF16)


def _ada_mod(c, w_ada, b_ada, side, side_lead):
    depth, d, n = w_ada.shape
    bsz = c.shape[0]
    bp = -(-bsz // SUBLANE) * SUBLANE
    c_p = jnp.pad(c, ((0, bp - bsz), (0, 0)))
    tn = 1024
    nj = n // tn
    side_in, side_out = _slab_specs(side, side_lead, 0, ROW_CHUNK, depth * nj, lambda l, j: l * nj + j)
    out, side_h = pl.pallas_call(
        _ada_kernel,
        grid=(depth, nj),
        in_specs=[pl.BlockSpec((bp, d), lambda l, j: (0, 0)),
                  pl.BlockSpec((1, d, tn), lambda l, j: (l, 0, j)),
                  pl.BlockSpec((1, 1, tn), lambda l, j: (l, 0, j)),
                  side_in],
        out_specs=(pl.BlockSpec((1, bp, tn), lambda l, j: (l, 0, j)), side_out),
        out_shape=(jax.ShapeDtypeStruct((depth, bp, n), F32),
                   jax.ShapeDtypeStruct(side.shape[1:], BF16)),
        compiler_params=_params(("arbitrary", "arbitrary")),
        name="ada_mod",
    )(c_p, w_ada, b_ada.reshape(depth, 1, n), side)
    return out[:, :bsz], side_h


def _norm_mm_kernel(*refs, has_tail, transposed):
    if has_tail:
        (x_ref, g_ref, sc_ref, sh_ref, w_ref, wt_ref, side_ref, o_ref, ot_ref, side_o_ref,
         h_scr, gm_scr) = refs
    else:
        x_ref, g_ref, sc_ref, sh_ref, w_ref, side_ref, o_ref, side_o_ref, h_scr, gm_scr = refs
    j = pl.program_id(2)

    @pl.when(j == 0)
    def _():
        _modulated_norm(x_ref, g_ref, sc_ref, sh_ref, gm_scr, h_scr)

    def project(wr, out_ref):
        w = wr[...].astype(BF16)
        if transposed:
            y = lax.dot_general(h_scr[...], w, NT_DIMS, preferred_element_type=F32)
        else:
            y = jnp.dot(h_scr[...], w, preferred_element_type=F32)
        out_ref[0] = y.astype(out_ref.dtype)

    project(w_ref, o_ref)
    if has_tail:
        pl.when(j == pl.num_programs(2) - 1)(lambda: project(wt_ref, ot_ref))
    side_o_ref[...] = side_ref[...].astype(BF16)


def _norm_mm(x, gain, mod3, sc_idx, sh_idx, w, out_dtype, side, side_lead, w_tail=None,
             transposed=False, tm=1024, tn=1024, name="norm_mm"):
    bsz, s, d = x.shape
    n_w = w.shape[0] if transposed else w.shape[1]
    nm, nj = s // tm, n_w // tn
    n = nj * tn
    side_in, side_out = _slab_specs(side, side_lead, 1, LANE, bsz * nm * nj,
                                    lambda b, m, j: (b * nm + m) * nj + j)
    if transposed:
        w_specs = [pl.BlockSpec((tn, d), lambda b, m, j: (j, 0))]
    else:
        w_specs = [pl.BlockSpec((d, tn), lambda b, m, j: (0, j))]
    w_args = [w]
    out_specs = [pl.BlockSpec((1, tm, tn), lambda b, m, j: (b, m, j))]
    out_shape = [jax.ShapeDtypeStruct((bsz, s, n), out_dtype)]
    if w_tail is not None:
        n_tail = w_tail.shape[0] if transposed else w_tail.shape[1]
        w_specs.append(pl.BlockSpec(w_tail.shape, lambda b, m, j: (0, 0), pipeline_mode=pl.Buffered(1)))
        w_args.append(w_tail)
        out_specs.append(pl.BlockSpec((1, tm, n_tail), lambda b, m, j: (b, m, 0)))
        out_shape.append(jax.ShapeDtypeStruct((bsz, s, n_tail), out_dtype))
    return pl.pallas_call(
        functools.partial(_norm_mm_kernel, has_tail=w_tail is not None, transposed=transposed),
        grid=(bsz, nm, nj),
        in_specs=[pl.BlockSpec((1, tm, d), lambda b, m, j: (b, m, 0)),
                  pl.BlockSpec((1, d), lambda b, m, j: (0, 0)),
                  pl.BlockSpec((1, 1, d), lambda b, m, j: (b, 0, sc_idx)),
                  pl.BlockSpec((1, 1, d), lambda b, m, j: (b, 0, sh_idx)),
                  *w_specs,
                  side_in],
        out_specs=(*out_specs, side_out),
        out_shape=(*out_shape, jax.ShapeDtypeStruct(side.shape[1:], BF16)),
        scratch_shapes=[pltpu.VMEM((tm, d), BF16), pltpu.VMEM((1, d), F32)],
        compiler_params=_params(("arbitrary", "arbitrary", "arbitrary")),
        name=name,
    )(x, gain.reshape(1, d), mod3, mod3, *w_args, side)


FFN_OUT_PARTS = 4

def _ffn_kernel(*refs, has_side):
    if has_side:
        (x_ref, gpre_ref, sc_ref, sh_ref, gt_ref, gpost_ref, wg_ref, wu_ref, wo_ref, side_ref,
         o_hbm, side_o_ref, h_scr, gm_scr, rs_scr, acc_scr, sems) = refs
        side_o_ref[...] = side_ref[...].astype(BF16)
    else:
        (x_ref, gpre_ref, sc_ref, sh_ref, gt_ref, gpost_ref, wg_ref, wu_ref, wo_ref,
         o_hbm, h_scr, gm_scr, rs_scr, acc_scr, sems) = refs
    b, m, f = pl.program_id(0), pl.program_id(1), pl.program_id(2)
    tm = acc_scr.shape[0]
    part = tm // FFN_OUT_PARTS
    first_tile = (b == 0) & (m == 0)

    def writeback(q, bb, mm):
        rows = pl.ds(pl.multiple_of(mm * tm + q * part, part), part)
        return pltpu.make_async_copy(acc_scr.at[pl.ds(q * part, part), :], o_hbm.at[bb, rows, :],
                                     sems.at[q])

    @pl.when(first_tile & (f == 0))
    def _():
        acc_scr[...] = jnp.zeros(acc_scr.shape, F32)

    @pl.when(f == 0)
    def _():
        _modulated_norm(x_ref, gpre_ref, sc_ref, sh_ref, gm_scr, h_scr)

    @pl.when((f == 0) & jnp.logical_not(first_tile))
    def _():
        prev_m = jnp.where(m == 0, pl.num_programs(1) - 1, m - 1)
        prev_b = jnp.where(m == 0, b - 1, b)
        for q in range(FFN_OUT_PARTS):
            writeback(q, prev_b, prev_m).wait()

    h = h_scr[...]
    g = jnp.dot(h, wg_ref[...], preferred_element_type=F32)
    u = jnp.dot(h, wu_ref[...], preferred_element_type=F32)
    a = (g * jax.nn.sigmoid(g) * u).astype(BF16)
    acc_scr[...] = (jnp.where(f == 0, 0.0, acc_scr[...])
                    + jnp.dot(a, wo_ref[...], preferred_element_type=F32))

    @pl.when(f == pl.num_programs(2) - 1)
    def _():
        gm_scr[...] = gt_ref[0] * gpost_ref[...]

        def row_scale(r):
            y = acc_scr[r, :]
            ms = jnp.mean(y * y, axis=-1, keepdims=True)
            rs_scr[r, :] = jnp.broadcast_to(lax.rsqrt(ms + NORM_EPS), (ROW_CHUNK, LANE))

        _for_row_chunks(tm, row_scale)
        for q in range(FFN_OUT_PARTS):
            def finish(r, q=q):
                rr = pl.ds(pl.multiple_of(q * part + r.start, ROW_CHUNK), ROW_CHUNK)
                acc_scr[rr, :] = x_ref[0, rr, :] + acc_scr[rr, :] * rs_scr[rr, :][:, :1] * gm_scr[...]
            _for_row_chunks(part, finish)
            writeback(q, b, m).start()

    @pl.when((b == pl.num_programs(0) - 1) & (m == pl.num_programs(1) - 1) & (f == pl.num_programs(2) - 1))
    def _():
        for q in range(FFN_OUT_PARTS):
            writeback(q, b, m).wait()


def _ffn(x, gpre, gpost, mod3, w_in, w_out, side=None, side_lead=0, tm=1024, tf=512):
    bsz, s, d = x.shape
    fh = w_out.shape[0]
    nf = fh // tf
    nm = s // tm
    out_spec = pl.BlockSpec(memory_space=pl.ANY)
    out_shape = jax.ShapeDtypeStruct((bsz, s, d), F32)
    side_specs, side_args = [], []
    if side is not None:
        side_in, side_out = _slab_specs(side, side_lead, 1, LANE, bsz * nm * nf,
                                        lambda b, m, f: (b * nm + m) * nf + f)
        side_specs, side_args = [side_in], [side]
        out_spec = (out_spec, side_out)
        out_shape = (out_shape, jax.ShapeDtypeStruct(side.shape[1:], BF16))
    return pl.pallas_call(
        functools.partial(_ffn_kernel, has_side=side is not None),
        grid=(bsz, nm, nf),
        in_specs=[pl.BlockSpec((1, tm, d), lambda b, m, f: (b, m, 0)),
                  pl.BlockSpec((1, d), lambda b, m, f: (0, 0)),
                  pl.BlockSpec((1, 1, d), lambda b, m, f: (b, 0, 4)),
                  pl.BlockSpec((1, 1, d), lambda b, m, f: (b, 0, 3)),
                  pl.BlockSpec((1, 1, d), lambda b, m, f: (b, 0, 5)),
                  pl.BlockSpec((1, d), lambda b, m, f: (0, 0)),
                  pl.BlockSpec((d, tf), lambda b, m, f: (0, f)),
                  pl.BlockSpec((d, tf), lambda b, m, f: (0, nf + f)),
                  pl.BlockSpec((tf, d), lambda b, m, f: (f, 0)),
                  *side_specs],
        out_specs=out_spec,
        out_shape=out_shape,
        scratch_shapes=[pltpu.VMEM((tm, d), BF16), pltpu.VMEM((1, d), F32), pltpu.VMEM((tm, LANE), F32),
                        pltpu.VMEM((tm, d), F32), pltpu.SemaphoreType.DMA((FFN_OUT_PARTS,))],
        compiler_params=_params(("arbitrary", "arbitrary", "arbitrary")),
        name="ffn",
    )(x, gpre.reshape(1, d), mod3, mod3, mod3, gpost.reshape(1, d), w_in, w_in, w_out, *side_args)


def _out_proj_kernel(a0_ref, a1_ref, w0_ref, w1_ref, x_ref, gt_ref, gpost_ref, side_ref,
                     o_ref, side_o_ref, gm_scr, y_scr):
    side_o_ref[...] = side_ref[...].astype(BF16)
    y_scr[...] = (jnp.dot(a0_ref[0], w0_ref[...], preferred_element_type=F32)
                  + jnp.dot(a1_ref[0], w1_ref[...], preferred_element_type=F32))
    _post_norm_residual(y_scr, x_ref, gt_ref, gpost_ref, gm_scr, o_ref)


def _out_proj(a0, a1, col0, col1, w, x, mod3, gpost, side, side_lead, tm=512):
    bsz, s, d = x.shape
    kh = w.shape[0] // 2
    nm = s // tm
    side_in, side_out = _slab_specs(side, side_lead, 0, ROW_CHUNK, bsz * nm, lambda b, m: b * nm + m)
    return pl.pallas_call(
        _out_proj_kernel,
        grid=(bsz, nm),
        in_specs=[pl.BlockSpec((1, tm, kh), lambda b, m: (b, m, col0)),
                  pl.BlockSpec((1, tm, kh), lambda b, m: (b, m, col1)),
                  pl.BlockSpec((kh, d), lambda b, m: (0, 0)),
                  pl.BlockSpec((kh, d), lambda b, m: (1, 0)),
                  pl.BlockSpec((1, tm, d), lambda b, m: (b, m, 0)),
                  pl.BlockSpec((1, 1, d), lambda b, m: (b, 0, 2)),
                  pl.BlockSpec((1, d), lambda b, m: (0, 0)),
                  side_in],
        out_specs=(pl.BlockSpec((1, tm, d), lambda b, m: (b, m, 0)), side_out),
        out_shape=(jax.ShapeDtypeStruct((bsz, s, d), F32),
                   jax.ShapeDtypeStruct(side.shape[1:], BF16)),
        scratch_shapes=[pltpu.VMEM((1, d), F32), pltpu.VMEM((tm, d), F32)],
        compiler_params=_params(("arbitrary", "arbitrary")),
        name="out_proj",
    )(a0, a1, w, w, x, mod3, gpost.reshape(1, d), side)


def _mixer_a_kernel(z_ref, vg_ref, vb_ref, ws_ref, bst_ref, o_ref, wm_scr):
    ch = ws_ref.shape[1]

    @pl.when((pl.program_id(0) == 0) & (pl.program_id(1) == 0))
    def _():
        causal = (lax.broadcasted_iota(jnp.int32, (ch, ch), 0)
                  >= lax.broadcasted_iota(jnp.int32, (ch, ch), 1))
        for g in range(A_GROUPS):
            wm_scr[g] = jnp.where(causal, ws_ref[g], 0.0).astype(BF16)

    for c in range(z_ref.shape[1] // ch):
        rows = slice(c * ch, (c + 1) * ch)
        z = jax.nn.gelu(z_ref[0, rows, :].astype(F32))
        wdt = z.shape[1] // 2
        u = z[:, :wdt]
        v = z[:, wdt:]
        mu = jnp.mean(v, axis=-1, keepdims=True)
        dv = v - mu
        var = jnp.mean(dv * dv, axis=-1, keepdims=True)
        vn = (dv * lax.rsqrt(var + LN_EPS) * vg_ref[...] + vb_ref[...]).astype(BF16)
        gd = wdt // A_GROUPS
        for g in range(A_GROUPS):
            sv = jnp.dot(wm_scr[g], vn[:, g * gd:(g + 1) * gd], preferred_element_type=F32)
            sv = sv + bst_ref[:, g:g + 1]
            o_ref[0, rows, g * gd:(g + 1) * gd] = (u[:, g * gd:(g + 1) * gd] * sv).astype(o_ref.dtype)


def _mixer_a(z, v_gain, v_bias, w_s, b_s, width, chunks_per_step=4):
    bsz, s, _ = z.shape
    ch = A_CHUNK
    rows = ch * chunks_per_step
    return pl.pallas_call(
        _mixer_a_kernel,
        grid=(bsz, s // rows),
        in_specs=[pl.BlockSpec((1, rows, 2 * width), lambda b, c: (b, c, 0)),
                  pl.BlockSpec((1, width), lambda b, c: (0, 0)),
                  pl.BlockSpec((1, width), lambda b, c: (0, 0)),
                  pl.BlockSpec((A_GROUPS, ch, ch), lambda b, c: (0, 0, 0)),
                  pl.BlockSpec((ch, A_GROUPS), lambda b, c: (0, 0))],
        out_specs=pl.BlockSpec((1, rows, width), lambda b, c: (b, c, 0)),
        out_shape=jax.ShapeDtypeStruct((bsz, s, width), BF16),
        scratch_shapes=[pltpu.VMEM((A_GROUPS, ch, ch), BF16)],
        compiler_params=_params(("arbitrary", "arbitrary")),
        name="mixer_a",
    )(z, v_gain.reshape(1, width), v_bias.reshape(1, width), w_s, b_s.T)


_PV_MU_R, _PV_MU_K, _PV_MU_V, _PV_W0, _PV_A0, _PV_KK, _PV_KA, _PV_RK, _PV_LG, _PV_LB = range(10)
_PV_ROWS = 16


def _shift_lerp(x, prev_row, mu):
    rolled = pltpu.roll(x, 1, axis=0)
    first = lax.broadcasted_iota(jnp.int32, x.shape, 0) == 0
    xp = jnp.where(first, prev_row, rolled)
    return x + mu * (xp - x)


def _split_bf16(x):
    hi = x.astype(BF16)
    lo = (x - hi.astype(F32)).astype(BF16)
    return hi, lo


def _mm(x, y):
    return jnp.dot(x.astype(BF16), y.astype(BF16), preferred_element_type=F32)


def _mm_nt(x, y):
    return lax.dot_general(x.astype(BF16), y.astype(BF16), NT_DIMS, preferred_element_type=F32)


def _mm_tn(x, y):
    return lax.dot_general(x.astype(BF16), y.astype(BF16), TN_DIMS, preferred_element_type=F32)


def _mm_exact_rhs(x, e_bf16):
    xh, xl = _split_bf16(x)
    return (jnp.dot(xh, e_bf16, preferred_element_type=F32)
            + jnp.dot(xl, e_bf16, preferred_element_type=F32))


def _rwkv_kernel(zr_ref, zk_ref, zv_ref, zl_ref, pva_ref, pvb_ref, mul_ref, w2_ref, a2_ref, g2_ref,
                 side_ref, o_ref, side_o_ref,
                 s_scr, prev_scr, prevl_scr, th_hi_scr, th_lo_scr, xw_scr, sg_scr,
                 ar_scr, kbh_scr, vst_scr, rt_scr, kbar_scr, vb_scr, plp_scr,
                 q_scr, y_scr, gm_scr, cm_scr, pl_scr, bonus_scr, g_scr, *, chunk, n_t, n_p, n_tiles):
    i = pl.program_id(0)
    t_rows = zr_ref.shape[1]
    lanes = zr_ref.shape[2]
    lora = xw_scr.shape[1] + sg_scr.shape[1]
    L = chunk
    SL = RW_PACK * L
    n_chunks = t_rows // L
    side_o_ref[...] = side_ref[...].astype(BF16)

    i1 = jnp.minimum(i, n_tiles - 1)
    i3 = jnp.maximum(i - 2, 0)
    t1, p1 = (i1 // n_p) % n_t, i1 % n_p
    t3, p3 = (i3 // n_p) % n_t, i3 % n_p
    sa = i % 2
    sb = 1 - sa
    s13_w = i % 3
    s13_r = (i + 1) % 3

    @pl.when(i == 0)
    def _():
        for ref in (s_scr, prev_scr, prevl_scr, ar_scr, kbh_scr, vst_scr, rt_scr, kbar_scr, vb_scr,
                    plp_scr, q_scr, y_scr, gm_scr, cm_scr, pl_scr, bonus_scr, g_scr):
            ref[...] = jnp.zeros_like(ref)

    @pl.when(p1 == 0)
    def _():
        zl = zl_ref[0][:, :lora].astype(F32)
        prev = jnp.where(t1 == 0, 0.0, prevl_scr[0:1, :lora])
        zls = _shift_lerp(zl, prev, mul_ref[:, :lora])
        prevl_scr[0:1, :lora] = zl[t_rows - 1:t_rows, :]
        x_wa = zls[:, :LANE]
        th_hi, th_lo = _split_bf16(jnp.tanh(x_wa))
        th_hi_scr[...] = th_hi
        th_lo_scr[...] = th_lo
        xw_scr[...] = x_wa.astype(BF16)
        sg_scr[...] = jax.nn.sigmoid(zls[:, LANE:]).astype(BF16)

    def pva(r):
        return pva_ref[r:r + 1, :]

    li = lax.broadcasted_iota(jnp.int32, (lanes, lanes), 0) // RW_HEAD
    lj = lax.broadcasted_iota(jnp.int32, (lanes, lanes), 1) // RW_HEAD
    same_head = li == lj
    e_head = jnp.where(same_head, 1.0, 0.0).astype(BF16)
    ti = lax.broadcasted_iota(jnp.int32, (L, L), 0)
    tj = lax.broadcasted_iota(jnp.int32, (L, L), 1)
    tri = jnp.where(ti >= tj, 1.0, 0.0).astype(BF16)
    si = lax.broadcasted_iota(jnp.int32, (SL, SL), 0)
    sj = lax.broadcasted_iota(jnp.int32, (SL, SL), 1)
    same_blk = (si // L) == (sj // L)
    m_strict = same_blk & (si > sj)
    m_incl = same_blk & (si >= sj)
    eye = jnp.where(si == sj, 1.0, 0.0)
    lane_head = lax.broadcasted_iota(jnp.int32, (1, lanes), 1) // RW_HEAD
    first_half = lax.broadcasted_iota(jnp.int32, (1, 2 * L), 1) < L
    n_sq = max(L.bit_length() - 2, 0)
    cs = range(n_chunks)
    rows = [slice(c * L, (c + 1) * L) for c in cs]

    def stack(x):
        return jnp.concatenate([jnp.where(lane_head == h, x, 0.0) for h in range(RW_PACK)], axis=0)

    def unstack(x):
        out = x[0:L]
        for h in range(1, RW_PACK):
            out = out + x[h * L:(h + 1) * L]
        return out

    zr = zr_ref[0].astype(F32)
    zk = zk_ref[0].astype(F32)
    zv = zv_ref[0].astype(F32)
    prev = jnp.where(t1 == 0, 0.0, prev_scr[p1])
    r = _shift_lerp(zr, prev[0:1, :], pva(_PV_MU_R))
    k = _shift_lerp(zk, prev[1:2, :], pva(_PV_MU_K))
    v = _shift_lerp(zv, prev[2:3, :], pva(_PV_MU_V))
    prev_scr[p1] = jnp.concatenate([zr[t_rows - 1:t_rows, :], zk[t_rows - 1:t_rows, :],
                                    zv[t_rows - 1:t_rows, :], jnp.zeros((SUBLANE - 3, lanes), F32)],
                                   axis=0)
    prep = {"c": 0, "phase": 0}

    def prep_tile_a():
        w2_hi, w2_lo = _split_bf16(w2_ref[...])
        th_hi = th_hi_scr[...]
        w_pre = (pva(_PV_W0) + jnp.dot(th_hi, w2_hi, preferred_element_type=F32)
                 + jnp.dot(th_lo_scr[...], w2_hi, preferred_element_type=F32)
                 + jnp.dot(th_hi, w2_lo, preferred_element_type=F32))
        t = -w_pre
        softplus = jnp.maximum(t, 0.0) + jnp.log1p(jnp.exp(-jnp.abs(t)))
        prep["log_decay"] = -jnp.exp(-softplus - 0.5)
        prep["a"] = jax.nn.sigmoid(pva(_PV_A0) + _mm(xw_scr[...], a2_ref[...]))
        g_scr[s13_w] = _mm(sg_scr[...], g2_ref[...])

    def prep_tile_b():
        a = prep["a"]
        kk = k * pva(_PV_KK)
        kk = kk / jnp.maximum(jnp.sqrt(_mm(kk * kk, e_head)), 1e-12)
        kn = k * (1.0 + (a - 1.0) * pva(_PV_KA))
        prep["kk"], prep["kn"], prep["bv"] = kk, kn, kk * a
        bonus_scr[s13_w] = _mm(r * kn * pva(_PV_RK), e_head) * v
        lw_hi, lw_lo = _split_bf16(prep["log_decay"])
        prep["cm"] = [jnp.dot(tri, lw_hi[rw], preferred_element_type=F32)
                      + jnp.dot(tri, lw_lo[rw], preferred_element_type=F32) for rw in rows]

    def prep_chunk():
        c = prep["c"]
        if c >= n_chunks:
            return
        prep["c"] = c + 1
        rw = rows[c]
        cm = prep["cm"][c]
        cm_last = cm[L - 1:L, :]
        kc, bvc = prep["kn"][rw], prep["bv"][rw]
        r_t = r[rw] * jnp.exp(cm)
        a_st = stack(-prep["kk"][rw] * jnp.exp(cm - prep["log_decay"][rw])).astype(BF16)
        e_neg = jnp.exp(-cm)
        e_rem = jnp.exp(cm_last - cm)
        ar_scr[sa, c] = jnp.concatenate([a_st, stack(r_t).astype(BF16)], axis=0)
        kbh_scr[sa, c] = jnp.concatenate([(kc * e_neg).astype(BF16), (bvc * e_neg).astype(BF16)], axis=0)
        vst_scr[sa, c] = stack(v[rw]).astype(BF16)
        rt_scr[sa, c] = r_t
        kbar_scr[sa, c] = jnp.concatenate([(kc * e_rem).astype(BF16), (bvc * e_rem).astype(BF16)], axis=0)
        vb_scr[sa, c] = v[rw].astype(BF16)
        plp_scr[sa, c] = jnp.broadcast_to(jnp.exp(cm_last), (SUBLANE, lanes))

    def prep_step():
        phase = prep["phase"]
        prep["phase"] = phase + 1
        if phase == 0:
            prep_tile_a()
        elif phase == 1:
            prep_tile_b()
        else:
            for _ in range(-(-n_chunks // (n_sq + 1))):
                prep_chunk()

    chain = {"s": jnp.where(t3 == 0, 0.0, s_scr[p3]), "c": 0, "y": []}

    def chain_step():
        c = chain["c"]
        if c >= n_chunks:
            return
        s0 = chain["s"]
        s0b = s0.astype(BF16)
        chain["y"].append(y_scr[sa, c * L:(c + 1) * L, :] + _mm_nt(q_scr[sa, c], s0b))
        chain["s"] = s0 * pl_scr[sa, c][0:1, :] + _mm(s0b, gm_scr[sa, c]) + cm_scr[sa, c]
        chain["c"] = c + 1

    def fill():
        prep_step()
        chain_step()

    ar_st = [ar_scr[sb, c] for c in cs]
    kb_h = [kbh_scr[sb, c] for c in cs]
    v_st = [vst_scr[sb, c] for c in cs]
    prod = [_mm_nt(ar_st[c], kb_h[c]) for c in cs]
    fill()
    swapped = [pltpu.roll(x, L, axis=1) for x in prod]
    prod_k = [jnp.where(first_half, prod[c], swapped[c]) for c in cs]
    prod_b = [jnp.where(first_half, swapped[c], prod[c]) for c in cs]
    a_ak = [jnp.where(m_strict, x[:SL], 0.0).astype(BF16) for x in prod_k]
    a_rk = [jnp.where(m_incl, x[SL:], 0.0).astype(BF16) for x in prod_k]
    a_ab = [jnp.where(m_strict, x[:SL], 0.0) for x in prod_b]
    a_rb = [jnp.where(m_incl, x[SL:], 0.0).astype(BF16) for x in prod_b]

    xp = [x.astype(BF16) for x in a_ab]
    tinv = [eye + x for x in a_ab]
    def group(fn):
        out = []
        for c in cs:
            out.append(fn(c))
            if n_chunks > n_sq + 3 and c == n_chunks // 2 - 1:
                chain_step()
        return out

    for _ in range(n_sq):
        xp = group(lambda c: _mm(xp[c], xp[c]).astype(BF16))
        tinv = group(lambda c: tinv[c] + _mm(tinv[c], xp[c]))
        fill()
    tinv = [x.astype(BF16) for x in tinv]

    x0 = [_mm(a_ak[c], v_st[c]).astype(BF16) for c in cs]
    fill()
    wu = [_mm(tinv[c], jnp.concatenate([ar_st[c][:SL], x0[c]], axis=1)) for c in cs]
    fill()
    yq = [_mm(a_rb[c], wu[c]) for c in cs]
    y0 = [_mm(a_rk[c], v_st[c]) for c in cs]
    while prep["c"] < n_chunks or chain["c"] < n_chunks:
        fill()
    cmats, gmats = [], []
    for c in cs:
        kbar = kbar_scr[sb, c]
        vu = jnp.concatenate([vb_scr[sb, c], unstack(wu[c][:, lanes:]).astype(BF16)], axis=0)
        cmats.append(jnp.where(same_head, _mm_tn(vu, kbar), 0.0))
        gmats.append(jnp.where(same_head, _mm_tn(unstack(wu[c][:, :lanes]), kbar[L:]), 0.0).astype(BF16))

    s_scr[p3] = chain["s"]
    y = jnp.concatenate(chain["y"], axis=0)
    inv_n = 1.0 / RW_HEAD
    mean = _mm_exact_rhs(y, e_head) * inv_n
    dy = y - mean
    var = _mm(dy * dy, e_head) * inv_n
    yn = dy * lax.rsqrt(var + GN_EPS) * pvb_ref[_PV_LG:_PV_LG + 1, :] + pvb_ref[_PV_LB:_PV_LB + 1, :]
    o_ref[0] = ((yn + bonus_scr[s13_r]) * g_scr[s13_r]).astype(o_ref.dtype)

    for c, rw in enumerate(rows):
        q_scr[sb, c] = (rt_scr[sb, c] + unstack(yq[c][:, :lanes])).astype(BF16)
        y_scr[sb, rw, :] = unstack(yq[c][:, lanes:] + y0[c])
        cm_scr[sb, c] = cmats[c]
        gm_scr[sb, c] = gmats[c]
        pl_scr[sb, c] = plp_scr[sb, c]


def _rwkv(z, z_lora, col_r, col_k, col_v, pvec, mu_l, w2p, a2p, g2p, width, side, side_lead,
          t_rows=1024):
    bsz, s, _ = z.shape
    lora_w = z_lora.shape[2]
    n_p = width // LANE
    L = RW_CHUNK
    n_chunks = t_rows // L
    n_t = s // t_rows
    n_tiles = bsz * n_t * n_p
    n_steps = n_tiles + 2
    assert RW_PACK * L == LANE and lora_w == 3 * LANE

    def tile(i, lag):
        it = jnp.clip(i - lag, 0, n_tiles - 1)
        return it // (n_t * n_p), (it // n_p) % n_t, it % n_p

    def z_spec(col):
        def idx(i):
            b, t, p = tile(i, 0)
            return b, t, col + p
        return pl.BlockSpec((1, t_rows, LANE), idx)

    def zl_idx(i):
        b, t, _ = tile(i, 0)
        return b, t, 0

    side_in, side_out = _slab_specs(side, side_lead, 0, ROW_CHUNK, n_steps, lambda i: i)
    kern = functools.partial(_rwkv_kernel, chunk=L, n_t=n_t, n_p=n_p, n_tiles=n_tiles)

    def per_chunk(rows_, dtype):
        return pltpu.VMEM((2, n_chunks, rows_, LANE), dtype)

    return pl.pallas_call(
        kern,
        grid=(n_steps,),
        in_specs=[z_spec(col_r), z_spec(col_k), z_spec(col_v),
                  pl.BlockSpec((1, t_rows, lora_w), zl_idx),
                  pl.BlockSpec((_PV_ROWS, LANE), lambda i: (0, tile(i, 0)[2])),
                  pl.BlockSpec((_PV_ROWS, LANE), lambda i: (0, tile(i, 2)[2])),
                  pl.BlockSpec((1, lora_w), lambda i: (0, 0)),
                  pl.BlockSpec((LANE, LANE), lambda i: (0, tile(i, 0)[2])),
                  pl.BlockSpec((LANE, LANE), lambda i: (0, tile(i, 0)[2])),
                  pl.BlockSpec((2 * LANE, LANE), lambda i: (0, tile(i, 0)[2])),
                  side_in],
        out_specs=(pl.BlockSpec((1, t_rows, LANE), lambda i: tile(i, 2)), side_out),
        out_shape=(jax.ShapeDtypeStruct((bsz, s, width), BF16),
                   jax.ShapeDtypeStruct(side.shape[1:], BF16)),
        scratch_shapes=[pltpu.VMEM((n_p, LANE, LANE), F32),
                        pltpu.VMEM((n_p, SUBLANE, LANE), F32),
                        pltpu.VMEM((SUBLANE, lora_w), F32),
                        pltpu.VMEM((t_rows, LANE), BF16),
                        pltpu.VMEM((t_rows, LANE), BF16),
                        pltpu.VMEM((t_rows, LANE), BF16),
                        pltpu.VMEM((t_rows, lora_w - LANE), BF16),
                        per_chunk(4 * L, BF16),
                        per_chunk(2 * L, BF16),
                        per_chunk(2 * L, BF16),
                        per_chunk(L, F32),
                        per_chunk(2 * L, BF16),
                        per_chunk(L, BF16),
                        per_chunk(SUBLANE, F32),
                        per_chunk(L, BF16),
                        pltpu.VMEM((2, t_rows, LANE), F32),
                        per_chunk(LANE, BF16),
                        per_chunk(LANE, F32),
                        per_chunk(SUBLANE, F32),
                        pltpu.VMEM((3, t_rows, LANE), F32),
                        pltpu.VMEM((3, t_rows, LANE), F32)],
        compiler_params=_params(("arbitrary",)),
        name="rwkv7",
    )(z, z, z, z_lora, pvec, pvec, mu_l, w2p, a2p, g2p, side)


def _moba_kernel(q_ref, k_ref, v_ref, side_ref, o_ref, side_o_ref, ka_scr, vt_scr, s_scr, p_scr, *,
                 n_heads, heads_per_step):
    side_o_ref[...] = side_ref[...].astype(BF16)
    heads = [_moba_head(q_ref, k_ref, v_ref, o_ref, ka_scr.at[j], vt_scr.at[j], s_scr.at[j],
                        p_scr.at[j], j, pl.program_id(1) * heads_per_step + j, n_heads, heads_per_step)
             for j in range(heads_per_step)]
    nb = q_ref.shape[1] // MOBA_BLOCK
    ahead = MOBA_DEPTH - 1
    pending = {q: [scores(q) for scores, _ in heads] for q in range(min(ahead, nb))}
    for qb in range(nb):
        if qb + ahead < nb:
            pending[qb + ahead] = [scores(qb + ahead) for scores, _ in heads]
        for (_, attend), pend in zip(heads, pending.pop(qb)):
            attend(qb, *pend)


def _moba_head(q_ref, k_ref, v_ref, o_ref, ka_scr, vt_scr, s_scr, p_scr, j, h, n_heads, heads_per_step):
    s_len = q_ref.shape[1]
    dh = q_ref.shape[2] // heads_per_step
    cols = slice(j * dh, (j + 1) * dh)
    blk = MOBA_BLOCK
    nb = s_len // blk
    log2e = 1.4426950408889634
    scale = dh ** -0.5 * log2e
    neg_inf = -jnp.inf

    def slope_row(width):
        return log2e * jnp.exp(jnp.full((1, width), -8.0 / n_heads * 0.6931471805599453, F32)
                               * (h + 1).astype(F32))

    lane = lax.broadcasted_iota(jnp.int32, (blk, dh), 1)
    bias = slope_row(dh) * lax.broadcasted_iota(jnp.int32, (blk, dh), 0).astype(F32)
    extra = jnp.zeros((blk, dh), F32)
    for col in range(3):
        part = bias.astype(BF16).astype(F32)
        extra = jnp.where(lane == col, part, extra)
        bias = bias - part
    extra = extra.astype(BF16)
    ones_cols = jnp.where(lane < 3, 1.0, 0.0).astype(BF16)

    kmean = []
    for j in range(nb):
        rows = slice(j * blk, (j + 1) * blk)
        k_j = k_ref[0, rows, cols]
        kmean.append(jnp.mean(k_j.astype(F32), axis=0, keepdims=True))
        ka_scr[rows, :dh] = k_j
        ka_scr[rows, dh:] = extra
        vt_scr[:dh, rows] = v_ref[0, rows, cols].astype(F32).T.astype(BF16)
    sub = lax.broadcasted_iota(jnp.int32, (vt_scr.shape[0] - dh, s_len), 0)
    vt_scr[dh:, :] = jnp.where(sub == 0, 1.0, 0.0).astype(BF16)
    kmean = jnp.concatenate(kmean, axis=0)
    kmean_parts = []
    for _ in range(3):
        part = kmean.astype(BF16)
        kmean_parts.append(part)
        kmean = kmean - part.astype(F32)

    slope = slope_row(blk)
    causal = (lax.broadcasted_iota(jnp.int32, (blk, blk), 1)
              >= lax.broadcasted_iota(jnp.int32, (blk, blk), 0))
    blk_id = lax.broadcasted_iota(jnp.int32, (nb, 1), 0)

    def scores(qb):
        q = q_ref[0, qb * blk:(qb + 1) * blk, cols]
        q_aug = jnp.concatenate([(q.astype(F32) * scale).astype(BF16), ones_cols], axis=1)
        gate = sum(lax.dot_general(part, q, NT_DIMS, preferred_element_type=F32)
                   for part in kmean_parts)
        past = blk_id < qb
        offs = []
        m = None
        for n in range(qb + 1):
            t = lax.dot_general(ka_scr[n * blk:(n + 1) * blk, :], q_aug, NT_DIMS,
                                preferred_element_type=F32)
            if n == qb:
                t = jnp.where(causal, t, neg_inf)
                off = jnp.zeros((1, blk), F32)
            else:
                g_n = gate[n:n + 1, :]
                beats = past & ((gate > g_n) | ((gate == g_n) & (blk_id < n)))
                rank = jnp.sum(jnp.where(beats, 1.0, 0.0), axis=0, keepdims=True)
                off = jnp.where(rank < float(MOBA_TOPK), slope * float((n - qb) * blk), neg_inf)
            s_scr[qb % MOBA_DEPTH, n] = t
            offs.append(off)
            cmax = jnp.max(t, axis=0, keepdims=True) + off
            m = cmax if m is None else jnp.maximum(m, cmax)
        return m, offs

    def attend(qb, m, offs):
        for n in range(qb + 1):
            p = jnp.exp2(s_scr[qb % MOBA_DEPTH, n] - (m - offs[n]))
            p_scr[qb % MOBA_DEPTH, n * blk:(n + 1) * blk, :] = p.astype(BF16)
        kk = (qb + 1) * blk
        acc = jnp.dot(vt_scr[:, :kk], p_scr[qb % MOBA_DEPTH, :kk, :], preferred_element_type=F32)
        o_ref[0, qb * blk:(qb + 1) * blk, cols] = (acc[:dh] / acc[dh:dh + 1]).T.astype(o_ref.dtype)

    return scores, attend


def _moba(qkv, n_heads, side, side_lead, heads_per_step=2):
    bsz, s, d3 = qkv.shape
    d = d3 // 3
    dh = d // n_heads
    blk = MOBA_BLOCK
    nb = s // blk
    n_hp = n_heads // heads_per_step
    wdt = heads_per_step * dh
    side_in, side_out = _slab_specs(side, side_lead, 0, ROW_CHUNK, bsz * n_hp,
                                    lambda b, h: b * n_hp + h)
    kern = functools.partial(_moba_kernel, n_heads=n_heads, heads_per_step=heads_per_step)
    return pl.pallas_call(
        kern,
        grid=(bsz, n_hp),
        in_specs=[pl.BlockSpec((1, s, wdt), lambda b, h: (b, 0, h)),
                  pl.BlockSpec((1, s, wdt), lambda b, h: (b, 0, n_hp + h)),
                  pl.BlockSpec((1, s, wdt), lambda b, h: (b, 0, 2 * n_hp + h)),
                  side_in],
        out_specs=(pl.BlockSpec((1, s, wdt), lambda b, h: (b, 0, h)), side_out),
        out_shape=(jax.ShapeDtypeStruct((bsz, s, d), BF16),
                   jax.ShapeDtypeStruct(side.shape[1:], BF16)),
        scratch_shapes=[pltpu.VMEM((heads_per_step, s, 2 * dh), BF16),
                        pltpu.VMEM((heads_per_step, dh + 16, s), BF16),
                        pltpu.VMEM((heads_per_step, MOBA_DEPTH, nb, blk, blk), F32),
                        pltpu.VMEM((heads_per_step, MOBA_DEPTH, s, blk), BF16)],
        compiler_params=_params(("arbitrary", "arbitrary")),
        name="moba",
    )(qkv, qkv, qkv, side)


def _pad_cols(w, n):
    return jnp.pad(w, ((0, 0), (0, n - w.shape[1])))


def _pad_rows(w, n, before=0):
    return jnp.pad(w, ((before, n - before - w.shape[0]), (0, 0)))


def kernel(x, c, w_ada, b_ada, g_pre_mix, g_post_mix, g_pre_ffn, g_post_ffn, w_ffn_in, w_ffn_out,
           w_in_ab, w_out_ab, a_v_gain, a_v_bias, a_w_s, a_b_s, b_mu, b_w0, b_w2, b_a0, b_a2, b_g2,
           b_k_k, b_k_a, b_r_k, b_lnx_gain, b_lnx_bias, w_qkv, w_o):
    bsz, s, d = x.shape
    depth = w_ada.shape[0]
    a_width = a_v_gain.shape[1]
    b_width = b_w0.shape[1]
    n_lw = b_w2.shape[1]
    n_la = b_a2.shape[1]
    n_lg = b_g2.shape[1]
    n_heads = d // ATT_HEAD
    assert s % MOBA_BLOCK == 0 and s % 1024 == 0
    assert n_lw + n_la <= LANE and n_lg <= 2 * LANE

    w_in_t = jnp.swapaxes(w_in_ab, 1, 2)
    mod, w_in_h0 = _ada_mod(c, w_ada, b_ada, w_in_t, 0)

    w_qkv_h = None
    for layer in range(depth):
        mod3 = mod[layer].reshape(bsz, 1, 6 * d)
        i = layer // 2
        if layer % 2 == 0:
            lora_w = 3 * LANE
            n_main = 2 * a_width + 3 * b_width
            w_t = w_in_h0 if i == 0 else w_in_t[i]
            w_tail = _pad_rows(w_t[n_main:], lora_w)
            z, z_lora, w_out_h = _norm_mm(x, g_pre_mix[layer], mod3, 1, 0, w_t, BF16, w_out_ab, i,
                                          w_tail=w_tail, transposed=True, tn=n_main // 4,
                                          name="in_proj_ab")
            y_a = _mixer_a(z, a_v_gain[i], a_v_bias[i], a_w_s[i], a_b_s[i], a_width)

            mu = b_mu[i]
            pvec = jnp.stack([mu[0:b_width], mu[b_width:2 * b_width], mu[2 * b_width:3 * b_width],
                              b_w0[i], b_a0[i], b_k_k[i], b_k_a[i], b_r_k[i].reshape(-1),
                              b_lnx_gain[i], b_lnx_bias[i]])
            pvec = _pad_rows(pvec, _PV_ROWS)
            mu_l = _pad_cols(mu[3 * b_width:].reshape(1, -1), lora_w)
            w2p = _pad_rows(b_w2[i], LANE)
            a2p = _pad_rows(b_a2[i], LANE, before=n_lw)
            g2p = _pad_rows(b_g2[i], 2 * LANE)
            cb = 2 * a_width // LANE
            nb_w = b_width // LANE
            y_b, w_ffn_in_h = _rwkv(z, z_lora, cb, cb + nb_w, cb + 2 * nb_w, pvec, mu_l, w2p, a2p,
                                    g2p, b_width, w_ffn_in, layer)
            x, w_ffn_out_h = _out_proj(y_a, y_b, 0, 0, w_out_h, x, mod3, g_post_mix[layer],
                                       w_ffn_out, layer)
        else:
            w_q = w_qkv[i] if w_qkv_h is None else w_qkv_h
            qkv, w_o_h = _norm_mm(x, g_pre_mix[layer], mod3, 1, 0, w_q, BF16, w_o, i,
                                  tn=d, name="qkv_proj")
            o, w_ffn_in_h = _moba(qkv, n_heads, w_ffn_in, layer)
            x, w_ffn_out_h = _out_proj(o, o, 0, 1, w_o_h, x, mod3, g_post_mix[layer],
                                       w_ffn_out, layer)
        if layer + 1 < depth and (layer + 1) % 2 == 1:
            x, w_qkv_h = _ffn(x, g_pre_ffn[layer], g_post_ffn[layer], mod3, w_ffn_in_h, w_ffn_out_h,
                              w_qkv, (layer + 1) // 2)
        else:
            x = _ffn(x, g_pre_ffn[layer], g_post_ffn[layer], mod3, w_ffn_in_h, w_ffn_out_h)
            w_qkv_h = None
    return x
```
